```python
import math
import jax, jax.numpy as jnp
from jax import lax
import numpy as np

D_MODEL = 1024
BATCH = 2
SEQ = 8192
DEPTH = 4

GRID_W = 64
CTX_LEN = 256
HEAD_DIM = 64
NA_HEADS = 4
NA_WIN_R = 8
NA_WIN_C = 16
ML_HEADS = 4
ML_CONV = 3
ML_CHUNK = 128
MLA_HEADS = 8
MLA_NOPE = 64
MLA_ROPE = 32
MLA_V = 64
Q_LORA = 384
KV_LORA = 256
NA_W = NA_HEADS * HEAD_DIM
ML_W = ML_HEADS * HEAD_DIM
MLA_W = MLA_HEADS * MLA_V
D_MIX = NA_W + ML_W + MLA_W
D_FF = -(-8 * D_MODEL // (3 * 256)) * 256
ROPE_BASE = 10000.0
EPS = 1e-6
Q_BLOCK = 128
NA_SCALE = HEAD_DIM ** -0.5
MLA_SCALE = (MLA_NOPE + MLA_ROPE) ** -0.5
IN_SPLITS = (NA_W, NA_W, NA_W, ML_W, ML_W, ML_W, ML_W, 4 * ML_HEADS, Q_LORA, KV_LORA, MLA_ROPE)
IN_OFFSETS = tuple(sum(IN_SPLITS[:i + 1]) for i in range(len(IN_SPLITS) - 1))
D_IN = sum(IN_SPLITS)

kernel_name = 'hybrid_na_mlstm_mla_dit_block'


def rms_norm(x, g):
    xf = x.astype(jnp.float32)
    y = xf * lax.rsqrt(jnp.mean(xf * xf, axis=-1, keepdims=True) + EPS)
    return (y * g).astype(x.dtype)


def modulation(cond, w_mod, b_mod):
    m = jax.nn.silu(cond) @ w_mod + b_mod
    return jnp.split(m, 6, axis=-1)


def modulate(x, g, shift, scale):
    return rms_norm(x, g) * (1.0 + scale[:, None, :]) + shift[:, None, :]


def swiglu(h, w_in, w_out):
    a, b = jnp.split(h @ w_in, 2, axis=-1)
    return (jax.nn.silu(a) * b) @ w_out


def flat_heads(a):
    return a.reshape(a.shape[0], a.shape[1], -1)


def axial_rope(n_tok, dim):
    n_freq = dim // 4
    inv = ROPE_BASE ** (-jnp.arange(n_freq, dtype=jnp.float32) / n_freq)
    t = jnp.arange(n_tok, dtype=jnp.int32)
    row = (t // GRID_W).astype(jnp.float32)
    col = (t % GRID_W).astype(jnp.float32)
    ang = jnp.concatenate([row[:, None] * inv, col[:, None] * inv], axis=-1)
    return jnp.cos(ang), jnp.sin(ang)


def apply_rope(x, cos, sin):
    half = x.shape[-1] // 2
    x1, x2 = x[..., :half], x[..., half:]
    cs = cos[:, None, :].astype(x.dtype)
    sn = sin[:, None, :].astype(x.dtype)
    return jnp.concatenate([x1 * cs - x2 * sn, x1 * sn + x2 * cs], axis=-1)


def attend(q, k, v, scale):
    s = jnp.einsum('bqhd,bkhd->bhqk', q, k).astype(jnp.float32) * scale
    p = jax.nn.softmax(s, axis=-1).astype(v.dtype)
    return jnp.einsum('bhqk,bkhd->bqhd', p, v)


def blocked_attention(q, k, v, scale):
    B, T, H, d = q.shape
    nb = T // Q_BLOCK
    qb = q.reshape(B, nb, Q_BLOCK, H, d).swapaxes(0, 1)
    ob = lax.map(lambda qi: attend(qi, k, v, scale), qb)
    return ob.swapaxes(0, 1).reshape(B, T, H, v.shape[-1])


def na_project(p, qk_gain):
    B, T, _ = p[0].shape
    q = rms_norm(p[0].reshape(B, T, NA_HEADS, HEAD_DIM), qk_gain[0])
    k = rms_norm(p[1].reshape(B, T, NA_HEADS, HEAD_DIM), qk_gain[1])
    v = p[2].reshape(B, T, NA_HEADS, HEAD_DIM)
    return q, k, v


def neighbourhood_attention(q, k, v, k_ctx, v_ctx, rpb):
    B, S, H, d = q.shape
    rows = S // GRID_W
    wr = min(NA_WIN_R, rows)
    qg, kg, vg = (a.reshape(B, rows, GRID_W, H, d) for a in (q, k, v))
    cols = np.arange(GRID_W)
    col_start = np.clip(cols - NA_WIN_C // 2, 0, GRID_W - NA_WIN_C)
    col_idx = col_start[:, None] + np.arange(NA_WIN_C)
    dc = col_idx - cols[:, None] + (NA_WIN_C - 1)
    n_loc = wr * NA_WIN_C

    def row_block(r):
        rs = jnp.clip(r - NA_WIN_R // 2, 0, rows - wr)
        q_r = lax.dynamic_index_in_dim(qg, r, axis=1, keepdims=False)
        k_w = lax.dynamic_slice_in_dim(kg, rs, wr, axis=1)[:, :, col_idx]
        v_w = lax.dynamic_slice_in_dim(vg, rs, wr, axis=1)[:, :, col_idx]
        dr = rs - r + jnp.arange(wr) + (NA_WIN_R - 1)
        bias = rpb[:, dr][:, :, dc].transpose(0, 2, 1, 3)
        s_loc = jnp.einsum('bchd,brcjhd->bhcrj', q_r, k_w).astype(jnp.float32) * NA_SCALE + bias
        s_ctx = jnp.einsum('bchd,bkhd->bhck', q_r, k_ctx).astype(jnp.float32) * NA_SCALE
        s = jnp.concatenate([s_loc.reshape(B, H, GRID_W, n_loc), s_ctx], axis=-1)
        p = jax.nn.softmax(s, axis=-1).astype(v.dtype)
        p_loc = p[..., :n_loc].reshape(B, H, GRID_W, wr, NA_WIN_C)
        p_ctx = p[..., n_loc:]
        return (jnp.einsum('bhcrj,brcjhd->bchd', p_loc, v_w)
                + jnp.einsum('bhck,bkhd->bchd', p_ctx, v_ctx))

    out = lax.map(row_block, jnp.arange(rows, dtype=jnp.int32))
    return out.swapaxes(0, 1).reshape(B, S, H, d)


def short_conv(x, w, b):
    K = w.shape[0]
    T = x.shape[1]
    xp = jnp.pad(x, ((0, 0), (K // 2, K - 1 - K // 2), (0, 0)))
    y = b + xp[:, 0:T] * w[0]
    for j in range(1, K):
        y = y + xp[:, j:j + T] * w[j]
    return jax.nn.silu(y)


def to_heads(a, H):
    B, T, _ = a.shape
    return a.reshape(B, T, H, -1).transpose(0, 2, 1, 3)


def mlstm_project(p, conv_w, conv_b, gate_b):
    qk = short_conv(jnp.concatenate([p[3], p[4]], axis=-1), conv_w, conv_b)
    q, k = jnp.split(qk, 2, axis=-1)
    q = to_heads(q, ML_HEADS).astype(jnp.float32)
    k = to_heads(k, ML_HEADS).astype(jnp.float32) * (HEAD_DIM ** -0.5)
    v = to_heads(p[5], ML_HEADS).astype(jnp.float32)
    gates = (p[7] + gate_b).astype(jnp.float32)
    return q, k, v, gates


def mlstm_chunkwise(q, k, v, log_i, log_f, state):
    B, H, T, d = q.shape
    L = ML_CHUNK
    nc = T // L

    def to_chunks(a):
        return jnp.moveaxis(a.reshape(B, H, nc, L, *a.shape[3:]), 2, 0)

    xs = (to_chunks(q), to_chunks(k), to_chunks(v), to_chunks(log_i), to_chunks(log_f))
    causal = np.tril(np.ones((L, L), dtype=bool))

    def step(carry, inp):
        C, n, m = carry
        qi, ki, vi, li, lf = inp
        b = jnp.cumsum(lf, axis=-1)
        g = b + m[..., None]
        D = jnp.where(causal, b[..., :, None] - b[..., None, :] + li[..., None, :], -jnp.inf)
        m_t = jnp.maximum(g, jnp.max(D, axis=-1))
        w_inter = jnp.exp(g - m_t)
        qk = jnp.einsum('bhtd,bhsd->bhts', qi, ki) * jnp.exp(D - m_t[..., None])
        num = (w_inter[..., None] * jnp.einsum('bhvk,bhtk->bhtv', C, qi)
               + jnp.einsum('bhts,bhsv->bhtv', qk, vi))
        den = w_inter * jnp.einsum('bhk,bhtk->bht', n, qi) + jnp.sum(qk, axis=-1)
        h = num / jnp.maximum(jnp.abs(den), jnp.exp(-m_t))[..., None]
        bL = b[..., -1]
        a = bL[..., None] - b + li
        m_new = jnp.maximum(bL + m, jnp.max(a, axis=-1))
        decay = jnp.exp(bL + m - m_new)
        wk = jnp.exp(a - m_new[..., None])
        C_new = decay[..., None, None] * C + jnp.einsum('bhs,bhsv,bhsk->bhvk', wk, vi, ki)
        n_new = decay[..., None] * n + jnp.einsum('bhs,bhsk->bhk', wk, ki)
        return (C_new, n_new, m_new), h

    state, hc = lax.scan(step, state, xs)
    return jnp.moveaxis(hc, 0, 2).reshape(B, H, T, d), state


def mlstm_bidir(q, k, v, gates, state_f, state_b):
    i_f, f_f, i_b, f_b = (g.transpose(0, 2, 1) for g in jnp.split(gates, 4, axis=-1))
    h_f, st_f = mlstm_chunkwise(q, k, v, i_f, jax.nn.log_sigmoid(f_f), state_f)
    flip = lambda a: jnp.flip(a, axis=2)
    h_b, st_b = mlstm_chunkwise(flip(q), flip(k), flip(v), flip(i_b), flip(jax.nn.log_sigmoid(f_b)), state_b)
    return h_f + flip(h_b), st_f, st_b


def mlstm_output(h, o, head_gain):
    B, H, T, d = h.shape
    hn = rms_norm(h.transpose(0, 2, 1, 3), head_gain)
    out = hn * jax.nn.sigmoid(o.reshape(B, T, H, d).astype(jnp.float32))
    return out.reshape(B, T, H * d).astype(o.dtype)


def mla_project(p, gq, gkv, w_uq, w_ukv, qk_gain, rope):
    B, T, _ = p[8].shape
    q = (rms_norm(p[8], gq) @ w_uq).reshape(B, T, MLA_HEADS, MLA_NOPE + MLA_ROPE)
    kv = (rms_norm(p[9], gkv) @ w_ukv).reshape(B, T, MLA_HEADS, MLA_NOPE + MLA_V)
    k_nope, v = kv[..., :MLA_NOPE], kv[..., MLA_NOPE:]
    k_rope = jnp.broadcast_to(p[10][:, :, None, :], (B, T, MLA_HEADS, MLA_ROPE))
    q = rms_norm(q, qk_gain[0])
    k = rms_norm(jnp.concatenate([k_nope, k_rope], axis=-1), qk_gain[1])
    if rope is not None:
        cos, sin = rope
        q = jnp.concatenate([q[..., :MLA_NOPE], apply_rope(q[..., MLA_NOPE:], cos, sin)], axis=-1)
        k = jnp.concatenate([k[..., :MLA_NOPE], apply_rope(k[..., MLA_NOPE:], cos, sin)], axis=-1)
    return q, k, v


def setup_inputs(seed: int = 0) -> dict:
    key = jax.random.key(seed)
    ks = jax.random.split(key, 24)
    D = D_MODEL

    def nrm(k, shape, s):
        return jax.random.normal(k, shape, jnp.float32) * s

    lin = jnp.linspace(3.0, 6.0, ML_HEADS, dtype=jnp.float32)
    zer = jnp.zeros((ML_HEADS,), jnp.float32)
    gate_base = jnp.concatenate([zer, lin, zer, lin])[None, :]
    return {
        'x': nrm(ks[0], (BATCH, SEQ, D), 1.0),
        'c': nrm(ks[1], (BATCH, D), 1.0),
        'ctx': nrm(ks[2], (BATCH, CTX_LEN, D), 1.0),
        'c_ctx': nrm(ks[3], (D,), 1.0),
        'w_mod': nrm(ks[4], (DEPTH, D, 6 * D), 0.5 * D ** -0.5),
        'b_mod': nrm(ks[5], (DEPTH, 6 * D), 0.02),
        'g_mix': 1.0 + nrm(ks[6], (DEPTH, D), 0.05),
        'g_ffn': 1.0 + nrm(ks[7], (DEPTH, D), 0.05),
        'w_in': nrm(ks[8], (DEPTH, D, D_IN), D ** -0.5),
        'b_gate': gate_base + nrm(ks[9], (DEPTH, 4 * ML_HEADS), 0.1),
        'na_qk_gain': 1.0 + nrm(ks[10], (DEPTH, 2, HEAD_DIM), 0.05),
        'na_rpb': nrm(ks[11], (DEPTH, NA_HEADS, 2 * NA_WIN_R - 1, 2 * NA_WIN_C - 1), 0.2),
        'ml_conv_w': nrm(ks[12], (DEPTH, ML_CONV, 2 * ML_W), ML_CONV ** -0.5),
        'ml_conv_b': nrm(ks[13], (DEPTH, 2 * ML_W), 0.02),
        'ml_head_gain': 1.0 + nrm(ks[14], (DEPTH, ML_HEADS, HEAD_DIM), 0.05),
        'mla_gq': 1.0 + nrm(ks[15], (DEPTH, Q_LORA), 0.05),
        'mla_gkv': 1.0 + nrm(ks[16], (DEPTH, KV_LORA), 0.05),
        'w_uq': nrm(ks[17], (DEPTH, Q_LORA, MLA_HEADS * (MLA_NOPE + MLA_ROPE)), Q_LORA ** -0.5),
        'w_ukv': nrm(ks[18], (DEPTH, KV_LORA, MLA_HEADS * (MLA_NOPE + MLA_V)), KV_LORA ** -0.5),
        'mla_qk_gain': 1.0 + nrm(ks[19], (DEPTH, 2, MLA_NOPE + MLA_ROPE), 0.05),
        'w_out': nrm(ks[20], (DEPTH, D_MIX, D), D_MIX ** -0.5),
        'w_ffn_in': nrm(ks[21], (DEPTH, D, 2 * D_FF), D ** -0.5),
        'w_ffn_out': nrm(ks[22], (DEPTH, D_FF, D), D_FF ** -0.5),
    }


def reference(x, c, ctx, c_ctx, w_mod, b_mod, g_mix, g_ffn, w_in, b_gate, na_qk_gain, na_rpb,
              ml_conv_w, ml_conv_b, ml_head_gain, mla_gq, mla_gkv, w_uq, w_ukv, mla_qk_gain,
              w_out, w_ffn_in, w_ffn_out):
    B, S, _ = x.shape
    rope = axial_rope(S, MLA_ROPE)
    zero_state = (jnp.zeros((B, ML_HEADS, HEAD_DIM, HEAD_DIM), jnp.float32),
                  jnp.zeros((B, ML_HEADS, HEAD_DIM), jnp.float32),
                  jnp.zeros((B, ML_HEADS), jnp.float32))
    xl, xc = x, ctx
    for l in range(DEPTH):
        need_ctx = l < DEPTH - 1
        sh1, sc1, gt1, sh2, sc2, gt2 = modulation(c, w_mod[l], b_mod[l])
        csh1, csc1, cgt1, csh2, csc2, cgt2 = modulation(c_ctx[None, :], w_mod[l], b_mod[l])
        pl = jnp.split(modulate(xl, g_mix[l], sh1, sc1) @ w_in[l], IN_OFFSETS, axis=-1)
        pc = jnp.split(modulate(xc, g_mix[l], csh1, csc1) @ w_in[l], IN_OFFSETS, axis=-1)

        qa_l, ka_l, va_l = na_project(pl, na_qk_gain[l])
        qa_c, ka_c, va_c = na_project(pc, na_qk_gain[l])
        oa_l = neighbourhood_attention(qa_l, ka_l, va_l, ka_c, va_c, na_rpb[l])

        qb_c, kb_c, vb_c, gb_c = mlstm_project(pc, ml_conv_w[l], ml_conv_b[l], b_gate[l])
        qb_l, kb_l, vb_l, gb_l = mlstm_project(pl, ml_conv_w[l], ml_conv_b[l], b_gate[l])
        hb_c, st_f, st_b = mlstm_bidir(qb_c, kb_c, vb_c, gb_c, zero_state, zero_state)
        hb_l, _, _ = mlstm_bidir(qb_l, kb_l, vb_l, gb_l, st_f, st_b)
        ob_l = mlstm_output(hb_l, pl[6], ml_head_gain[l])

        qc_l, kc_l, vc_l = mla_project(pl, mla_gq[l], mla_gkv[l], w_uq[l], w_ukv[l], mla_qk_gain[l], rope)
        qc_c, kc_c, vc_c = mla_project(pc, mla_gq[l], mla_gkv[l], w_uq[l], w_ukv[l], mla_qk_gain[l], None)
        oc_l = blocked_attention(qc_l, jnp.concatenate([kc_l, kc_c], axis=1),
                                 jnp.concatenate([vc_l, vc_c], axis=1), MLA_SCALE)

        mix_l = jnp.concatenate([flat_heads(oa_l), ob_l, flat_heads(oc_l)], axis=-1) @ w_out[l]
        xl = xl + gt1[:, None, :] * mix_l
        xl = xl + gt2[:, None, :] * swiglu(modulate(xl, g_ffn[l], sh2, sc2), w_ffn_in[l], w_ffn_out[l])

        if need_ctx:
            oa_c = attend(qa_c, ka_c, va_c, NA_SCALE)
            ob_c = mlstm_output(hb_c, pc[6], ml_head_gain[l])
            oc_c = attend(qc_c, kc_c, vc_c, MLA_SCALE)
            mix_c = jnp.concatenate([flat_heads(oa_c), ob_c, flat_heads(oc_c)], axis=-1) @ w_out[l]
            xc = xc + cgt1[:, None, :] * mix_c
            xc = xc + cgt2[:, None, :] * swiglu(modulate(xc, g_ffn[l], csh2, csc2), w_ffn_in[l], w_ffn_out[l])
    return xl
```

```python
import functools

import numpy as np
import jax
import jax.numpy as jnp
from jax import lax
from jax.experimental import pallas as pl
from jax.experimental.pallas import tpu as pltpu

GRID_W = 64
HEAD_DIM = 64
NA_HEADS = 4
NA_WIN_R = 8
NA_WIN_C = 16
ML_HEADS = 4
ML_CHUNK = 128
MLA_HEADS = 8
MLA_NOPE = 64
MLA_ROPE = 32
MLA_V = 64
Q_LORA = 384
KV_LORA = 256
ROPE_BASE = 10000.0
EPS = 1e-6
NA_W = NA_HEADS * HEAD_DIM
ML_W = ML_HEADS * HEAD_DIM
MLA_QK = MLA_NOPE + MLA_ROPE
NA_SCALE = HEAD_DIM ** -0.5
MLA_SCALE = MLA_QK ** -0.5

LANES = 128
MXU_DIM = 256
VMEM_LIMIT = 56 * 1024 * 1024

ROW_TILE = 512
ATT_BLOCK = 256
NA_KEY_ROWS = 768
FLASH_TK = 512
NEG = -1e30

C_NAQ, C_NAK, C_NAV = 0, 256, 512
C_MLQK, C_MLV, C_MLO = 768, 1280, 1536
C_QL = 1792
C_KVL = C_QL + Q_LORA
C_MISC = C_KVL + KV_LORA
D_IN_PAD = C_MISC + LANES
MISC_ROPE_LANE = 64

BF16 = jnp.bfloat16
F32 = jnp.float32


def _cparams(sem):
    return pltpu.CompilerParams(dimension_semantics=sem, vmem_limit_bytes=VMEM_LIMIT)


def _dot(a, b):
    return jnp.dot(a, b, preferred_element_type=F32)


def _dot_nt(a, b):
    return lax.dot_general(a, b, (((1,), (1,)), ((), ())), preferred_element_type=F32)


def _dot_tn(a, b):
    return lax.dot_general(a, b, (((0,), (0,)), ((), ())), preferred_element_type=F32)


def _split2(x):
    hi = x.astype(BF16)
    lo = (x - hi.astype(F32)).astype(BF16)
    return hi, lo


def _group_sumsq(x, bd):
    x2 = x * x
    hi, lo = _split2(x2)
    outs = []
    for c in range(x.shape[1] // MXU_DIM):
        sl = slice(c * MXU_DIM, (c + 1) * MXU_DIM)
        outs.append(_dot(hi[:, sl], bd) + _dot(lo[:, sl], bd))
    return outs[0] if len(outs) == 1 else jnp.concatenate(outs, axis=-1)


def _rms_rows(x):
    return x * lax.rsqrt(jnp.mean(x * x, axis=-1, keepdims=True) + EPS)


def _sigmoid(x):
    return 1.0 / (1.0 + jnp.exp(-x))


def _mod_kernel(c_ref, w_ref, b_ref, o_ref):
    c = c_ref[...]
    a = c * _sigmoid(c)
    o_ref[...] = jnp.dot(a, w_ref[...], preferred_element_type=F32,
                         precision=lax.Precision.HIGHEST) + b_ref[...]


def _modulation(cond8, w_mod, b_mod):
    depth, d, d6 = w_mod.shape
    n = d6 // d
    return pl.pallas_call(
        _mod_kernel,
        grid=(depth, n),
        in_specs=[
            pl.BlockSpec((8, d), lambda l, j: (0, 0)),
            pl.BlockSpec((None, d, d), lambda l, j: (l, 0, j)),
            pl.BlockSpec((None, 1, d), lambda l, j: (l, 0, j)),
        ],
        out_specs=pl.BlockSpec((None, 8, d), lambda l, j: (l, 0, j)),
        out_shape=jax.ShapeDtypeStruct((depth, 8, d6), F32),
        compiler_params=_cparams(("arbitrary", "arbitrary")),
        name="modulation",
    )(cond8, w_mod, b_mod.reshape(depth, 1, d6))


def _rope128(x, cos, sa, sb):
    return x * cos + pltpu.roll(x, LANES - 16, 1) * sa + pltpu.roll(x, 16, 1) * sb


def _proj_kernel(x_ref, mod_ref, g_ref, w_in_ref, w_uq_ref, w_uk_ref, w_uv_ref,
                 gq_ref, gkv_ref, na_gq_ref, na_gk_ref, mla_gq_ref, mla_gk_ref,
                 bd64_ref, bd128_ref, cos_ref, sa_ref, sb_ref,
                 naq_ref, nak_ref, nav_ref, mlqk_ref, mlv_ref, mlo_ref, misc_ref,
                 mq_ref, mk_ref, mv_ref):
    x = x_ref[...]
    shift = mod_ref[0:1, :]
    scale = mod_ref[1:2, :]
    h = (_rms_rows(x) * g_ref[...]) * (1.0 + scale) + shift
    hb = h.astype(BF16)

    def proj(c0, width):
        return _dot(hb, w_in_ref[:, c0:c0 + width])

    bd64 = bd64_ref[...]
    bd128 = bd128_ref[...]

    pq = proj(C_NAQ, NA_W)
    naq_ref[...] = (pq * lax.rsqrt(_group_sumsq(pq, bd64) * (1.0 / HEAD_DIM) + EPS)
                    * na_gq_ref[...]).astype(BF16)
    pk = proj(C_NAK, NA_W)
    nak_ref[...] = (pk * lax.rsqrt(_group_sumsq(pk, bd64) * (1.0 / HEAD_DIM) + EPS)
                    * na_gk_ref[...]).astype(BF16)
    nav_ref[...] = proj(C_NAV, NA_W).astype(BF16)

    mlqk_ref[...] = proj(C_MLQK, 2 * ML_W)
    mlv_ref[...] = proj(C_MLV, ML_W).astype(BF16)
    mlo_ref[...] = proj(C_MLO, ML_W)
    misc = proj(C_MISC, LANES)
    misc_ref[...] = misc

    cos = cos_ref[...]
    sa = sa_ref[...]
    sb = sb_ref[...]
    ql = proj(C_QL, Q_LORA)
    qn = (_rms_rows(ql) * gq_ref[...]).astype(BF16)
    qr = _dot(qn, w_uq_ref[...])
    q = qr * lax.rsqrt(_group_sumsq(qr, bd128) * (1.0 / MLA_QK) + EPS) * mla_gq_ref[...]
    kvl = proj(C_KVL, KV_LORA)
    kvn = (_rms_rows(kvl) * gkv_ref[...]).astype(BF16)
    lane = lax.broadcasted_iota(jnp.int32, misc.shape, 1)
    krope = jnp.where((lane >= MISC_ROPE_LANE) & (lane < MISC_ROPE_LANE + MLA_ROPE), misc, 0.0)
    kr = _dot(kvn, w_uk_ref[...]) + jnp.concatenate([krope] * MLA_HEADS, axis=-1)
    k = kr * lax.rsqrt(_group_sumsq(kr, bd128) * (1.0 / MLA_QK) + EPS) * mla_gk_ref[...]
    for g in range(MLA_HEADS):
        sl = slice(g * LANES, (g + 1) * LANES)
        mq_ref[:, sl] = _rope128(q[:, sl], cos, sa, sb).astype(BF16)
        mk_ref[:, sl] = _rope128(k[:, sl], cos, sa, sb).astype(BF16)
    mv_ref[...] = _dot(kvn, w_uv_ref[...]).astype(BF16)


def _proj(l, xs, mods, prm, tabs, dims):
    T, D = xs.shape
    B, S, CTX = dims
    nt = T // ROW_TILE
    lat_tiles = S // ROW_TILE

    def grp(i):
        return jnp.minimum(i // lat_tiles, B)

    def tab_row(i):
        return jnp.where(i < B * lat_tiles, i % lat_tiles, lat_tiles + (i - B * lat_tiles))

    row = lambda w: pl.BlockSpec((ROW_TILE, w), lambda i: (i, 0))
    lay = lambda shape: pl.BlockSpec((None,) + shape, lambda i: (l,) + (0,) * len(shape))
    const = lambda shape: pl.BlockSpec(shape, lambda i: (0,) * len(shape))
    tab = pl.BlockSpec((ROW_TILE, LANES), lambda i: (tab_row(i), 0))
    out_w = [(NA_W, BF16), (NA_W, BF16), (NA_W, BF16), (2 * ML_W, F32), (ML_W, BF16), (ML_W, F32),
             (LANES, F32), (MLA_HEADS * LANES, BF16), (MLA_HEADS * LANES, BF16), (MLA_HEADS * MLA_V, BF16)]
    return pl.pallas_call(
        _proj_kernel,
        grid=(nt,),
        in_specs=[
            row(D),
            pl.BlockSpec((None, None, 6, D), lambda i: (l, grp(i), 0, 0)),
            lay((1, D)),
            lay((D, D_IN_PAD)), lay((Q_LORA, MLA_HEADS * LANES)), lay((KV_LORA, MLA_HEADS * LANES)),
            lay((KV_LORA, MLA_HEADS * MLA_V)),
            lay((1, Q_LORA)), lay((1, KV_LORA)), lay((1, NA_W)), lay((1, NA_W)),
            lay((1, MLA_HEADS * LANES)), lay((1, MLA_HEADS * LANES)),
            const((MXU_DIM, MXU_DIM)), const((MXU_DIM, MXU_DIM)),
            tab, tab, tab,
        ],
        out_specs=[row(w) for w, _ in out_w],
        out_shape=[jax.ShapeDtypeStruct((T, w), dt) for w, dt in out_w],
        compiler_params=_cparams(("arbitrary",)),
        name="proj",
    )(xs, mods, prm["g_mix"], prm["w_in"], prm["w_uq"], prm["w_uk"], prm["w_uv"],
      prm["gq"], prm["gkv"], prm["na_gq"], prm["na_gk"], prm["mla_gq"], prm["mla_gk"],
      tabs["bd64"], tabs["bd128"], tabs["cos"], tabs["sa"], tabs["sb"])


def _conv_kernel(x_ref, prev_ref, next_ref, w_ref, b_ref, q_ref, k_ref, *, lat_tiles, n_lat):
    i = pl.program_id(0)
    is_lat = i < n_lat
    first = jnp.where(is_lat, (i % lat_tiles) == 0, True)
    last = jnp.where(is_lat, (i % lat_tiles) == lat_tiles - 1, True)
    x = x_ref[...]
    n = x.shape[0]
    prev_row = jnp.where(first, 0.0, prev_ref[7:8, :])
    next_row = jnp.where(last, 0.0, next_ref[0:1, :])
    ridx = lax.broadcasted_iota(jnp.int32, x.shape, 0)
    xm1 = jnp.where(ridx == 0, prev_row, pltpu.roll(x, 1, 0))
    xp1 = jnp.where(ridx == n - 1, next_row, pltpu.roll(x, n - 1, 0))
    y = b_ref[...] + xm1 * w_ref[0:1, :] + x * w_ref[1:2, :] + xp1 * w_ref[2:3, :]
    y = y * _sigmoid(y)
    q_ref[...] = y[:, :ML_W].astype(BF16)
    k_ref[...] = (y[:, ML_W:] * (HEAD_DIM ** -0.5)).astype(BF16)


def _conv(l, mlqk, prm, dims):
    T, W = mlqk.shape
    B, S, CTX = dims
    tc = CTX
    nt = T // tc
    sub = tc // 8
    nblk8 = T // 8
    kern = functools.partial(_conv_kernel, lat_tiles=S // tc, n_lat=B * S // tc)
    return pl.pallas_call(
        kern,
        grid=(nt,),
        in_specs=[
            pl.BlockSpec((tc, W), lambda i: (i, 0)),
            pl.BlockSpec((8, W), lambda i: (jnp.maximum(i * sub - 1, 0), 0)),
            pl.BlockSpec((8, W), lambda i: (jnp.minimum((i + 1) * sub, nblk8 - 1), 0)),
            pl.BlockSpec((None, 8, W), lambda i: (l, 0, 0)),
            pl.BlockSpec((None, 1, W), lambda i: (l, 0, 0)),
        ],
        out_specs=[pl.BlockSpec((tc, ML_W), lambda i: (i, 0))] * 2,
        out_shape=[jax.ShapeDtypeStruct((T, ML_W), BF16)] * 2,
        compiler_params=_cparams(("arbitrary",)),
        name="conv",
    )(mlqk, mlqk, mlqk, prm["conv_w"], prm["conv_b"])


def _pair_select(o0, o1):
    lane = lax.broadcasted_iota(jnp.int32, o0.shape, 1)
    return jnp.where(lane < HEAD_DIM, o0, o1)


def _head_q(q, e, packed):
    if packed:
        lane = lax.broadcasted_iota(jnp.int32, q.shape, 1)
        keep = (lane < HEAD_DIM) if e == 0 else (lane >= HEAD_DIM)
        return jnp.where(keep, q, jnp.zeros_like(q))
    return q[:, e * LANES:(e + 1) * LANES]


def _kcols(e, packed):
    return slice(0, LANES) if packed else slice(e * LANES, (e + 1) * LANES)


def _na_kernel(q_ref, k_ref, v_ref, kc_ref, vc_ref, bias_ref, o_ref, *, nb):
    j = pl.program_id(2)
    start = pl.multiple_of(jnp.clip(j - 1, 0, nb - 3) * ATT_BLOCK, ATT_BLOCK)
    q = q_ref[...]
    kw = k_ref[pl.ds(start, NA_KEY_ROWS), :]
    vw = v_ref[pl.ds(start, NA_KEY_ROWS), :]
    kc = kc_ref[...]
    vc = vc_ref[...]
    outs = []
    for e in range(2):
        qh = _head_q(q, e, True)
        s_loc = _dot_nt(qh, kw) + bias_ref[e]
        s_ctx = _dot_nt(qh, kc)
        m = jnp.maximum(jnp.max(s_loc, axis=-1, keepdims=True), jnp.max(s_ctx, axis=-1, keepdims=True))
        p_loc = jnp.exp(s_loc - m)
        p_ctx = jnp.exp(s_ctx - m)
        den = jnp.sum(p_loc, axis=-1, keepdims=True) + jnp.sum(p_ctx, axis=-1, keepdims=True)
        o = _dot(p_loc.astype(BF16), vw) + _dot(p_ctx.astype(BF16), vc)
        outs.append(o / den)
    o_ref[...] = _pair_select(outs[0], outs[1]).astype(o_ref.dtype)


def _na(l, naq, nak, nav, bias, dims):
    T = naq.shape[0]
    B, S, CTX = dims
    nb = S // ATT_BLOCK
    ctx_blk0 = B * S // CTX
    npair = NA_HEADS // 2

    def qrow(b, j):
        return jnp.where(j < nb, b * nb + j, ctx_blk0 + b)

    def variant(j):
        return jnp.where(j == 0, 0, jnp.where(j == nb - 1, 2, jnp.where(j == nb, 3, 1)))

    kern = functools.partial(_na_kernel, nb=nb)
    return pl.pallas_call(
        kern,
        grid=(B, npair, nb + 1),
        in_specs=[
            pl.BlockSpec((ATT_BLOCK, LANES), lambda b, p, j: (qrow(b, j), p)),
            pl.BlockSpec((S, LANES), lambda b, p, j: (b, p)),
            pl.BlockSpec((S, LANES), lambda b, p, j: (b, p)),
            pl.BlockSpec((CTX, LANES), lambda b, p, j: (ctx_blk0 + b, p)),
            pl.BlockSpec((CTX, LANES), lambda b, p, j: (ctx_blk0 + b, p)),
            pl.BlockSpec((None, None, 2, ATT_BLOCK, NA_KEY_ROWS), lambda b, p, j: (l, variant(j), p, 0, 0)),
        ],
        out_specs=pl.BlockSpec((ATT_BLOCK, LANES), lambda b, p, j: (qrow(b, j), p)),
        out_shape=jax.ShapeDtypeStruct((T, NA_W), BF16),
        compiler_params=_cparams(("arbitrary", "arbitrary", "arbitrary")),
        name="na",
    )(naq, nak, nav, nak, nav, bias)


def _flash_kernel(q_ref, km_ref, vm_ref, kc_ref, vc_ref, o_ref, *, nq, n_main):
    qi = pl.program_id(2)
    n_steps = jnp.where(qi < nq, n_main, 0)
    q = q_ref[...]
    tq = q.shape[0]
    vc = vc_ref[...]
    outs = []
    for e in range(2):
        qh = _head_q(q, e, False)
        cols = _kcols(e, False)

        def update(carry, kk, vv):
            m, den, acc = carry
            s = _dot_nt(qh, kk)
            m_new = jnp.maximum(m, jnp.max(s, axis=-1, keepdims=True))
            alpha = jnp.exp(m - m_new)
            p = jnp.exp(s - m_new)
            den = alpha * den + jnp.sum(p, axis=-1, keepdims=True)
            acc = alpha * acc + _dot(p.astype(BF16), vv)
            return m_new, den, acc

        def step(i, carry):
            r0 = pl.multiple_of(i * FLASH_TK, FLASH_TK)
            return update(carry, km_ref[pl.ds(r0, FLASH_TK), cols], vm_ref[pl.ds(r0, FLASH_TK), :])

        init = (jnp.full((tq, 1), NEG, F32), jnp.zeros((tq, 1), F32), jnp.zeros((tq, LANES), F32))
        carry = lax.fori_loop(0, n_steps, step, init)
        m, den, acc = update(carry, kc_ref[:, cols], vc)
        outs.append(acc / den)
    o_ref[...] = _pair_select(outs[0], outs[1]).astype(o_ref.dtype)


def _flash(mq, mk, mv, dims):
    T = mq.shape[0]
    B, S, CTX = dims
    nq = S // ATT_BLOCK
    ctx_blk0 = B * S // CTX
    npair = MLA_HEADS // 2

    def qrow(b, j):
        return jnp.where(j < nq, b * nq + j, ctx_blk0 + b)

    kern = functools.partial(_flash_kernel, nq=nq, n_main=S // FLASH_TK)
    return pl.pallas_call(
        kern,
        grid=(B, npair, nq + 1),
        in_specs=[
            pl.BlockSpec((ATT_BLOCK, 2 * LANES), lambda b, p, j: (qrow(b, j), p)),
            pl.BlockSpec((S, 2 * LANES), lambda b, p, j: (b, p)),
            pl.BlockSpec((S, LANES), lambda b, p, j: (b, p)),
            pl.BlockSpec((CTX, 2 * LANES), lambda b, p, j: (ctx_blk0 + b, p)),
            pl.BlockSpec((CTX, LANES), lambda b, p, j: (ctx_blk0 + b, p)),
        ],
        out_specs=pl.BlockSpec((ATT_BLOCK, LANES), lambda b, p, j: (qrow(b, j), p)),
        out_shape=jax.ShapeDtypeStruct((T, MLA_HEADS * MLA_V), BF16),
        compiler_params=_cparams(("arbitrary", "arbitrary", "arbitrary")),
        name="flash",
    )(mq, mk, mv, mk, mv)


def _mlstm_dir(d, q, k, v, gates, bgate, tri, c_ref, m_ref, h_ref):
    L = q.shape[0]
    gb = gates + bgate
    ls = jnp.minimum(gb, 0.0) - jnp.log1p(jnp.exp(-jnp.abs(gb)))
    lane = lax.broadcasted_iota(jnp.int32, gb.shape, 1)
    is_f = ((lane >= 4) & (lane < 8)) | ((lane >= 12) & (lane < 16))
    xg = jnp.where(lane < 16, jnp.where(is_f, ls, gb), 0.0)
    x1 = xg.astype(BF16)
    r1 = xg - x1.astype(F32)
    x2 = r1.astype(BF16)
    x3 = (r1 - x2.astype(F32)).astype(BF16)
    cum = _dot(tri, x1) + _dot(tri, x2) + _dot(tri, x3)
    xg_t = xg.T
    cum_t = cum.T
    tot = jnp.sum(xg, axis=0, keepdims=True)

    ti = lax.broadcasted_iota(jnp.int32, (L, L), 0)
    si = lax.broadcasted_iota(jnp.int32, (L, L), 1)
    valid = (si <= ti) if d == 0 else (si >= ti)
    lane_l = lax.broadcasted_iota(jnp.int32, (L, LANES), 1)

    for p in range(ML_HEADS // 2):
        qp = q[:, p * LANES:(p + 1) * LANES]
        kp = k[:, p * LANES:(p + 1) * LANES]
        vp = v[:, p * LANES:(p + 1) * LANES]
        hs = []
        for e in range(2):
            hd = 2 * p + e
            ci = 8 * d + hd
            cf = 8 * d + 4 + hd
            in_head = (lane_l < HEAD_DIM) if e == 0 else (lane_l >= HEAD_DIM)
            ones_lane = HEAD_DIM if e == 0 else 0
            km = jnp.where(in_head, kp, jnp.zeros_like(kp))
            one_hot = jnp.where(lane_l == ones_lane, 1.0, 0.0).astype(BF16)
            va = jnp.where(in_head, vp, one_hot)

            b_col = cum[:, cf:cf + 1]
            b_row = cum_t[cf:cf + 1, :]
            li_col = xg[:, ci:ci + 1]
            li_row = xg_t[ci:ci + 1, :]
            m_prev = m_ref[d * ML_HEADS + hd:d * ML_HEADS + hd + 1, 0:1]
            cst = c_ref[d, hd]

            g = b_col + m_prev
            dm = jnp.where(valid, b_col - b_row + li_row, NEG)
            mt = jnp.maximum(g, jnp.max(dm, axis=-1, keepdims=True))
            w_inter = jnp.exp(g - mt)
            qk = _dot_nt(qp, km) * jnp.exp(dm - mt)
            haug = w_inter * _dot_nt(qp, cst.astype(BF16)) + _dot(qk.astype(BF16), va)
            den = haug[:, ones_lane:ones_lane + 1]
            hs.append(haug / jnp.maximum(jnp.abs(den), jnp.exp(-mt)))

            b_tot = tot[:, cf:cf + 1]
            a_col = b_tot - b_col + li_col
            m_new = jnp.maximum(b_tot + m_prev, jnp.max(a_col, axis=0, keepdims=True))
            decay = jnp.exp(b_tot + m_prev - m_new)
            wk = jnp.exp(a_col - m_new)
            wv = (wk * va.astype(F32)).astype(BF16)
            c_ref[d, hd] = decay * cst + _dot_tn(wv, km)
            m_ref[d * ML_HEADS + hd:d * ML_HEADS + hd + 1, :] = jnp.broadcast_to(m_new, (1, LANES))
        h_ref[:, p * LANES:(p + 1) * LANES] = _pair_select(hs[0], hs[1])


def _mlstm_kernel(qf_ref, kf_ref, vf_ref, gf_ref, qb_ref, kb_ref, vb_ref, gb_ref,
                  bgate_ref, tril_ref, triu_ref, hf_ref, hb_ref, c_ref, m_ref):
    @pl.when(pl.program_id(1) == 0)
    def _():
        c_ref[...] = jnp.zeros_like(c_ref)
        m_ref[...] = jnp.zeros_like(m_ref)

    bgate = bgate_ref[...]
    _mlstm_dir(0, qf_ref[...], kf_ref[...], vf_ref[...], gf_ref[...], bgate, tril_ref[...],
               c_ref, m_ref, hf_ref)
    _mlstm_dir(1, qb_ref[...], kb_ref[...], vb_ref[...], gb_ref[...], bgate, triu_ref[...],
               c_ref, m_ref, hb_ref)


def _mlstm(l, cq, ck, mlv, misc, prm, tabs, dims):
    T = cq.shape[0]
    B, S, CTX = dims
    L = ML_CHUNK
    n_ctx = CTX // L
    n_lat = S // L
    ctx0 = B * S // L

    def fwd(b, c):
        return jnp.where(c < n_ctx, ctx0 + b * n_ctx + c, b * n_lat + (c - n_ctx))

    def bwd(b, c):
        return jnp.where(c < n_ctx, ctx0 + b * n_ctx + (n_ctx - 1 - c), b * n_lat + (n_lat - 1 - (c - n_ctx)))

    def specs(fn):
        return [pl.BlockSpec((L, ML_W), lambda b, c: (fn(b, c), 0)),
                pl.BlockSpec((L, ML_W), lambda b, c: (fn(b, c), 0)),
                pl.BlockSpec((L, ML_W), lambda b, c: (fn(b, c), 0)),
                pl.BlockSpec((L, LANES), lambda b, c: (fn(b, c), 0))]

    return pl.pallas_call(
        _mlstm_kernel,
        grid=(B, n_ctx + n_lat),
        in_specs=specs(fwd) + specs(bwd) + [
            pl.BlockSpec((None, 1, LANES), lambda b, c: (l, 0, 0)),
            pl.BlockSpec((L, L), lambda b, c: (0, 0)),
            pl.BlockSpec((L, L), lambda b, c: (0, 0)),
        ],
        out_specs=[pl.BlockSpec((L, ML_W), lambda b, c: (fwd(b, c), 0)),
                   pl.BlockSpec((L, ML_W), lambda b, c: (bwd(b, c), 0))],
        out_shape=[jax.ShapeDtypeStruct((T, ML_W), F32)] * 2,
        scratch_shapes=[pltpu.VMEM((2, ML_HEADS, LANES, LANES), F32), pltpu.VMEM((8, LANES), F32)],
        compiler_params=_cparams(("arbitrary", "arbitrary")),
        name="mlstm",
    )(cq, ck, mlv, misc, cq, ck, mlv, misc, prm["b_gate"], tabs["tril"], tabs["triu"])


def _ffn_chunks(ff):
    step = 2 * MXU_DIM
    return [(c, min(step, ff - c)) for c in range(0, ff, step)]


def _out_kernel(x_ref, mod_ref, ona_ref, omla_ref, hf_ref, hb_ref, mlo_ref, hg_ref, bd64_ref,
                w_out_ref, g_ref, wa_ref, wb_ref, wo_ref, o_ref):
    x = x_ref[...]
    gate1 = mod_ref[2:3, :]
    shift2 = mod_ref[3:4, :]
    scale2 = mod_ref[4:5, :]
    gate2 = mod_ref[5:6, :]

    h = hf_ref[...] + hb_ref[...]
    hn = h * lax.rsqrt(_group_sumsq(h, bd64_ref[...]) * (1.0 / HEAD_DIM) + EPS) * hg_ref[...]
    ob = (hn * _sigmoid(mlo_ref[...])).astype(BF16)
    mix = (_dot(ona_ref[...], w_out_ref[0:NA_W, :])
           + _dot(ob, w_out_ref[NA_W:NA_W + ML_W, :])
           + _dot(omla_ref[...], w_out_ref[NA_W + ML_W:, :]))
    x1 = x + gate1 * mix

    h2 = ((_rms_rows(x1) * g_ref[...]) * (1.0 + scale2) + shift2).astype(BF16)
    acc = jnp.zeros_like(x1)
    for c0, w in _ffn_chunks(wa_ref.shape[1]):
        a = _dot(h2, wa_ref[:, c0:c0 + w])
        b = _dot(h2, wb_ref[:, c0:c0 + w])
        act = (a * _sigmoid(a) * b).astype(BF16)
        acc = acc + _dot(act, wo_ref[c0:c0 + w, :])
    o_ref[...] = x1 + gate2 * acc


def _out(l, xs, mods, ona, omla, hf, hb, mlo, prm, tabs, dims):
    T, D = xs.shape
    B, S, CTX = dims
    nt = T // ROW_TILE
    lat_tiles = S // ROW_TILE
    ff = prm["w_a"].shape[-1]

    def grp(i):
        return jnp.minimum(i // lat_tiles, B)

    row = lambda w: pl.BlockSpec((ROW_TILE, w), lambda i: (i, 0))
    lay = lambda shape: pl.BlockSpec((None,) + shape, lambda i: (l,) + (0,) * len(shape))
    big = lambda shape: pl.BlockSpec((None,) + shape, lambda i: (l,) + (0,) * len(shape),
                                     pipeline_mode=pl.Buffered(1))
    return pl.pallas_call(
        _out_kernel,
        grid=(nt,),
        in_specs=[
            row(D),
            pl.BlockSpec((None, None, 6, D), lambda i: (l, grp(i), 0, 0)),
            row(NA_W), row(MLA_HEADS * MLA_V), row(ML_W), row(ML_W), row(ML_W),
            lay((1, ML_W)),
            pl.BlockSpec((MXU_DIM, MXU_DIM), lambda i: (0, 0)),
            big((NA_W + ML_W + MLA_HEADS * MLA_V, D)),
            lay((1, D)),
            big((D, ff)), big((D, ff)), big((ff, D)),
        ],
        out_specs=row(D),
        out_shape=jax.ShapeDtypeStruct((T, D), F32),
        compiler_params=_cparams(("arbitrary",)),
        name="out_ffn",
    )(xs, mods, ona, omla, hf, hb, mlo, prm["head_gain"], tabs["bd64"],
      prm["w_out"], prm["g_ffn"], prm["w_a"], prm["w_b"], prm["w_fo"])


def _block_diag_ones(group):
    idx = np.arange(MXU_DIM) // group
    return jnp.asarray((idx[:, None] == idx[None, :]).astype(np.float32), dtype=BF16)


def _rope_tables(S, n_ctx_rows):
    n_freq = MLA_ROPE // 4
    inv = ROPE_BASE ** (-jnp.arange(n_freq, dtype=F32) / n_freq)
    t = jnp.arange(S, dtype=jnp.int32)
    row = (t // GRID_W).astype(F32)
    col = (t % GRID_W).astype(F32)
    ang = jnp.concatenate([row[:, None] * inv, col[:, None] * inv], axis=-1)
    ang = jnp.concatenate([ang, jnp.zeros((n_ctx_rows, MLA_ROPE // 2), F32)], axis=0)
    cos, sin = jnp.cos(ang), jnp.sin(ang)
    n = S + n_ctx_rows
    half = MLA_ROPE // 2
    ones = jnp.ones((n, MLA_NOPE), F32)
    z = lambda w: jnp.zeros((n, w), F32)
    tail = LANES - MLA_NOPE - MLA_ROPE
    cos_t = jnp.concatenate([ones, cos, cos, jnp.ones((n, tail), F32)], axis=-1)
    sa_t = jnp.concatenate([z(MLA_NOPE), -sin, z(half), z(tail)], axis=-1)
    sb_t = jnp.concatenate([z(MLA_NOPE), z(half), sin, z(tail)], axis=-1)
    return cos_t, sa_t, sb_t


def _na_bias(rpb, rows):
    depth, H = rpb.shape[:2]
    qr, kr_n = ATT_BLOCK // GRID_W, NA_KEY_ROWS // GRID_W
    nb = rows // qr
    cols = np.arange(GRID_W)
    cs = np.clip(cols - NA_WIN_C // 2, 0, GRID_W - NA_WIN_C)
    col_ok = (cols[None, :] >= cs[:, None]) & (cols[None, :] < cs[:, None] + NA_WIN_C)
    dc = np.clip(cols[None, :] - cols[:, None] + NA_WIN_C - 1, 0, 2 * NA_WIN_C - 2)
    dr_all, ok_all = [], []
    for j in (0, 1, nb - 1):
        ks = int(np.clip(j - 1, 0, nb - 3)) * qr
        r = j * qr + np.arange(qr)
        rs = np.clip(r - NA_WIN_R // 2, 0, rows - NA_WIN_R)
        kr = ks + np.arange(kr_n)
        ok_all.append((kr[None, :] >= rs[:, None]) & (kr[None, :] < rs[:, None] + NA_WIN_R))
        dr_all.append(np.clip(kr[None, :] - r[:, None] + NA_WIN_R - 1, 0, 2 * NA_WIN_R - 2))
    dr_all = np.stack(dr_all)
    ok_all = np.stack(ok_all)
    t1 = jnp.take(rpb, jnp.asarray(dc.reshape(-1)), axis=3).reshape(depth, H, 2 * NA_WIN_R - 1, GRID_W, GRID_W)
    t1 = jnp.where(jnp.asarray(col_ok), t1, NEG)
    t2 = jnp.take(t1, jnp.asarray(dr_all.reshape(-1)), axis=2).reshape(depth, H, 3, qr, kr_n, GRID_W, GRID_W)
    t2 = jnp.where(jnp.asarray(ok_all)[None, None, :, :, :, None, None], t2, NEG)
    bias = t2.transpose(0, 2, 1, 3, 5, 4, 6).reshape(depth, 3, H, ATT_BLOCK, NA_KEY_ROWS)
    none = jnp.full((depth, 1, H, ATT_BLOCK, NA_KEY_ROWS), NEG, F32)
    return jnp.concatenate([bias, none], axis=1)


def _prepare(w_in, w_uq, w_ukv, w_out, w_ffn_in, w_ffn_out, g_mix, g_ffn, b_gate, na_qk_gain,
             ml_conv_w, ml_conv_b, ml_head_gain, mla_gq, mla_gkv, mla_qk_gain):
    depth, D, _ = w_in.shape
    o_gate = C_MLO + ML_W
    o_ql = o_gate + 4 * ML_HEADS
    o_kr = o_ql + Q_LORA + KV_LORA
    zc = lambda w: jnp.zeros((depth, D, w), w_in.dtype)
    w_in_p = jnp.concatenate([
        w_in[..., :o_gate], w_in[..., o_ql:o_kr],
        w_in[..., o_gate:o_ql], zc(MISC_ROPE_LANE - 4 * ML_HEADS),
        w_in[..., o_kr:], zc(LANES - MISC_ROPE_LANE - MLA_ROPE)], axis=-1).astype(BF16)
    pad_h = lambda a, w: jnp.pad(a, [(0, 0)] * (a.ndim - 1) + [(0, LANES - w)])
    w_uq_p = pad_h(w_uq.reshape(depth, Q_LORA, MLA_HEADS, MLA_QK), MLA_QK)
    w_uq_p = w_uq_p.reshape(depth, Q_LORA, MLA_HEADS * LANES).astype(BF16)
    ukv = w_ukv.reshape(depth, KV_LORA, MLA_HEADS, MLA_NOPE + MLA_V)
    w_uk_p = pad_h(ukv[..., :MLA_NOPE], MLA_NOPE).reshape(depth, KV_LORA, MLA_HEADS * LANES).astype(BF16)
    w_uv = ukv[..., MLA_NOPE:].reshape(depth, KV_LORA, MLA_HEADS * MLA_V).astype(BF16)
    ff = w_ffn_out.shape[1]
    mla_g = pad_h(mla_qk_gain, MLA_QK)
    return {
        "w_in": w_in_p, "w_uq": w_uq_p, "w_uk": w_uk_p, "w_uv": w_uv,
        "w_out": w_out.astype(BF16),
        "w_a": w_ffn_in[..., :ff].astype(BF16), "w_b": w_ffn_in[..., ff:].astype(BF16),
        "w_fo": w_ffn_out.astype(BF16),
        "g_mix": g_mix[:, None, :], "g_ffn": g_ffn[:, None, :],
        "gq": mla_gq[:, None, :], "gkv": mla_gkv[:, None, :],
        "na_gq": jnp.tile(na_qk_gain[:, 0:1, :], (1, 1, NA_HEADS)) * NA_SCALE,
        "na_gk": jnp.tile(na_qk_gain[:, 1:2, :], (1, 1, NA_HEADS)),
        "mla_gq": jnp.tile(mla_g[:, 0:1, :], (1, 1, MLA_HEADS)) * MLA_SCALE,
        "mla_gk": jnp.tile(mla_g[:, 1:2, :], (1, 1, MLA_HEADS)),
        "head_gain": ml_head_gain.reshape(depth, 1, ML_W),
        "b_gate": pad_h(b_gate, 4 * ML_HEADS)[:, None, :],
        "conv_w": jnp.pad(ml_conv_w, ((0, 0), (0, 8 - ml_conv_w.shape[1]), (0, 0))),
        "conv_b": ml_conv_b[:, None, :],
    }


def kernel(x, c, ctx, c_ctx, w_mod, b_mod, g_mix, g_ffn, w_in, b_gate, na_qk_gain, na_rpb,
           ml_conv_w, ml_conv_b, ml_head_gain, mla_gq, mla_gkv, w_uq, w_ukv, mla_qk_gain,
           w_out, w_ffn_in, w_ffn_out):
    B, S, D = x.shape
    CTX = ctx.shape[1]
    depth = w_in.shape[0]
    assert CTX == ATT_BLOCK and S % ROW_TILE == 0 and (B * CTX) % ROW_TILE == 0
    assert S % FLASH_TK == 0 and S // ATT_BLOCK >= 3 and B + 1 <= 8
    dims = (B, S, CTX)

    prm = _prepare(w_in, w_uq, w_ukv, w_out, w_ffn_in, w_ffn_out, g_mix, g_ffn, b_gate, na_qk_gain,
                   ml_conv_w, ml_conv_b, ml_head_gain, mla_gq, mla_gkv, mla_qk_gain)
    cos_t, sa_t, sb_t = _rope_tables(S, B * CTX)
    tri = np.tril(np.ones((ML_CHUNK, ML_CHUNK), np.float32))
    tabs = {
        "bd64": _block_diag_ones(HEAD_DIM), "bd128": _block_diag_ones(LANES),
        "cos": cos_t, "sa": sa_t, "sb": sb_t,
        "tril": jnp.asarray(tri, dtype=BF16), "triu": jnp.asarray(tri.T, dtype=BF16),
    }
    bias = _na_bias(na_rpb, S // GRID_W)

    cond8 = jnp.concatenate([c, c_ctx[None, :], jnp.zeros((8 - B - 1, D), F32)], axis=0)
    mods = _modulation(cond8, w_mod, b_mod)
    mods = mods.reshape(depth, 8, 6, D)

    xs = jnp.concatenate([x.reshape(B * S, D), ctx.reshape(B * CTX, D)], axis=0)
    for l in range(depth):
        naq, nak, nav, mlqk, mlv, mlo, misc, mq, mk, mv = _proj(l, xs, mods, prm, tabs, dims)
        cq, ck = _conv(l, mlqk, prm, dims)
        ona = _na(l, naq, nak, nav, bias, dims)
        omla = _flash(mq, mk, mv, dims)
        hf, hb = _mlstm(l, cq, ck, mlv, misc, prm, tabs, dims)
        xs = _out(l, xs, mods, ona, omla, hf, hb, mlo, prm, tabs, dims)
    return xs[:B * S].reshape(B, S, D)
```

```python
import functools

import numpy as np
import jax
import jax.numpy as jnp
from jax import lax
from jax.experimental import pallas as pl
from jax.experimental.pallas import tpu as pltpu

GRID_W = 64
HEAD_DIM = 64
NA_HEADS = 4
NA_WIN_R = 8
NA_WIN_C = 16
ML_HEADS = 4
ML_CHUNK = 128
MLA_HEADS = 8
MLA_NOPE = 64
MLA_ROPE = 32
MLA_V = 64
Q_LORA = 384
KV_LORA = 256
ROPE_BASE = 10000.0
EPS = 1e-6
NA_W = NA_HEADS * HEAD_DIM
ML_W = ML_HEADS * HEAD_DIM
MLA_QK = MLA_NOPE + MLA_ROPE
NA_SCALE = HEAD_DIM ** -0.5
MLA_SCALE = MLA_QK ** -0.5
LOG2E = 1.4426950408889634

LANES = 128
MXU_DIM = 256
VMEM_LIMIT = 56 * 1024 * 1024

ROW_TILE = 512
ATT_BLOCK = 256
NA_KEY_ROWS = 768
FLASH_TK = 512
NEG = -1e30

C_NAQ, C_NAK, C_NAV = 0, 256, 512
C_MLQK, C_MLV, C_MLO = 768, 1280, 1536
C_QL = 1792
C_KVL = C_QL + Q_LORA
C_MISC = C_KVL + KV_LORA
D_IN_PAD = C_MISC + LANES
MISC_ROPE_LANE = 64

BF16 = jnp.bfloat16
F32 = jnp.float32


def _cparams(sem):
    return pltpu.CompilerParams(dimension_semantics=sem, vmem_limit_bytes=VMEM_LIMIT)


def _dot(a, b):
    return jnp.dot(a, b, preferred_element_type=F32)


def _dot_nt(a, b):
    return lax.dot_general(a, b, (((1,), (1,)), ((), ())), preferred_element_type=F32)


def _dot_tn(a, b):
    return lax.dot_general(a, b, (((0,), (0,)), ((), ())), preferred_element_type=F32)


def _split2(x):
    hi = x.astype(BF16)
    lo = (x - hi.astype(F32)).astype(BF16)
    return hi, lo


def _group_sumsq(x, bd):
    x2 = x * x
    hi, lo = _split2(x2)
    outs = []
    for c in range(x.shape[1] // MXU_DIM):
        sl = slice(c * MXU_DIM, (c + 1) * MXU_DIM)
        outs.append(_dot(hi[:, sl], bd) + _dot(lo[:, sl], bd))
    return outs[0] if len(outs) == 1 else jnp.concatenate(outs, axis=-1)


def _rms_rows(x):
    return x * lax.rsqrt(jnp.mean(x * x, axis=-1, keepdims=True) + EPS)


def _sigmoid(x):
    return 1.0 / (1.0 + jnp.exp(-x))


def _mod_kernel(c_ref, w_ref, b_ref, o_ref):
    c = c_ref[...]
    a = c * _sigmoid(c)
    o_ref[...] = jnp.dot(a, w_ref[...], preferred_element_type=F32,
                         precision=lax.Precision.HIGHEST) + b_ref[...]


def _modulation(cond8, w_mod, b_mod):
    depth, d, d6 = w_mod.shape
    n = d6 // d
    return pl.pallas_call(
        _mod_kernel,
        grid=(depth, n),
        in_specs=[
            pl.BlockSpec((8, d), lambda l, j: (0, 0)),
            pl.BlockSpec((None, d, d), lambda l, j: (l, 0, j)),
            pl.BlockSpec((None, 1, d), lambda l, j: (l, 0, j)),
        ],
        out_specs=pl.BlockSpec((None, 8, d), lambda l, j: (l, 0, j)),
        out_shape=jax.ShapeDtypeStruct((depth, 8, d6), F32),
        compiler_params=_cparams(("arbitrary", "arbitrary")),
        name="modulation",
    )(cond8, w_mod, b_mod.reshape(depth, 1, d6))


def _rope128(x, cos, sa, sb):
    return x * cos + pltpu.roll(x, LANES - 16, 1) * sa + pltpu.roll(x, 16, 1) * sb


def _proj_kernel(x_ref, mod_ref, g_ref, w_in_ref, w_uq_ref, w_uk_ref, w_uv_ref,
                 gq_ref, gkv_ref, na_gq_ref, na_gk_ref, mla_gq_ref, mla_gk_ref,
                 bd64_ref, bd128_ref, cos_ref, sa_ref, sb_ref,
                 naq_ref, nak_ref, nav_ref, mlqk_ref, mlv_ref, mlo_ref, misc_ref,
                 mq_ref, mk_ref, mv_ref):
    x = x_ref[...]
    shift = mod_ref[0:1, :]
    scale = mod_ref[1:2, :]
    h = (_rms_rows(x) * g_ref[...]) * (1.0 + scale) + shift
    hb = h.astype(BF16)

    def proj(c0, width):
        return _dot(hb, w_in_ref[:, c0:c0 + width])

    bd64 = bd64_ref[...]
    bd128 = bd128_ref[...]

    pq = proj(C_NAQ, NA_W)
    naq_ref[...] = (pq * lax.rsqrt(_group_sumsq(pq, bd64) * (1.0 / HEAD_DIM) + EPS)
                    * na_gq_ref[...]).astype(BF16)
    pk = proj(C_NAK, NA_W)
    nak_ref[...] = (pk * lax.rsqrt(_group_sumsq(pk, bd64) * (1.0 / HEAD_DIM) + EPS)
                    * na_gk_ref[...]).astype(BF16)
    nav_ref[...] = proj(C_NAV, NA_W).astype(BF16)

    mlqk_ref[...] = proj(C_MLQK, 2 * ML_W)
    mlv_ref[...] = proj(C_MLV, ML_W).astype(BF16)
    mlo_ref[...] = proj(C_MLO, ML_W)
    misc = proj(C_MISC, LANES)
    misc_ref[...] = misc

    cos = cos_ref[...]
    sa = sa_ref[...]
    sb = sb_ref[...]
    ql = proj(C_QL, Q_LORA)
    qn = (_rms_rows(ql) * gq_ref[...]).astype(BF16)
    qr = _dot(qn, w_uq_ref[...])
    q = qr * lax.rsqrt(_group_sumsq(qr, bd128) * (1.0 / MLA_QK) + EPS) * mla_gq_ref[...]
    kvl = proj(C_KVL, KV_LORA)
    kvn = (_rms_rows(kvl) * gkv_ref[...]).astype(BF16)
    lane = lax.broadcasted_iota(jnp.int32, misc.shape, 1)
    krope = jnp.where((lane >= MISC_ROPE_LANE) & (lane < MISC_ROPE_LANE + MLA_ROPE), misc, 0.0)
    kr = _dot(kvn, w_uk_ref[...]) + jnp.concatenate([krope] * MLA_HEADS, axis=-1)
    k = kr * lax.rsqrt(_group_sumsq(kr, bd128) * (1.0 / MLA_QK) + EPS) * mla_gk_ref[...]
    for g in range(MLA_HEADS):
        sl = slice(g * LANES, (g + 1) * LANES)
        mq_ref[:, sl] = _rope128(q[:, sl], cos, sa, sb).astype(BF16)
        mk_ref[:, sl] = _rope128(k[:, sl], cos, sa, sb).astype(BF16)
    mv_ref[...] = _dot_nt(w_uv_ref[...], kvn).astype(BF16)


def _proj(l, xs, mods, prm, tabs, dims):
    T, D = xs.shape
    B, S, CTX = dims
    nt = T // ROW_TILE
    lat_tiles = S // ROW_TILE

    def grp(i):
        return jnp.minimum(i // lat_tiles, B)

    def tab_row(i):
        return jnp.where(i < B * lat_tiles, i % lat_tiles, lat_tiles + (i - B * lat_tiles))

    row = lambda w: pl.BlockSpec((ROW_TILE, w), lambda i: (i, 0))
    lay = lambda shape: pl.BlockSpec((None,) + shape, lambda i: (l,) + (0,) * len(shape))
    const = lambda shape: pl.BlockSpec(shape, lambda i: (0,) * len(shape))
    tab = pl.BlockSpec((ROW_TILE, LANES), lambda i: (tab_row(i), 0))
    out_w = [(NA_W, BF16), (NA_W, BF16), (NA_W, BF16), (2 * ML_W, F32), (ML_W, BF16), (ML_W, F32),
             (LANES, F32), (MLA_HEADS * LANES, BF16), (MLA_HEADS * LANES, BF16)]
    return pl.pallas_call(
        _proj_kernel,
        grid=(nt,),
        in_specs=[
            row(D),
            pl.BlockSpec((None, None, 6, D), lambda i: (l, grp(i), 0, 0)),
            lay((1, D)),
            lay((D, D_IN_PAD)), lay((Q_LORA, MLA_HEADS * LANES)), lay((KV_LORA, MLA_HEADS * LANES)),
            lay((MLA_HEADS * MLA_V, KV_LORA)),
            lay((1, Q_LORA)), lay((1, KV_LORA)), lay((1, NA_W)), lay((1, NA_W)),
            lay((1, MLA_HEADS * LANES)), lay((1, MLA_HEADS * LANES)),
            const((MXU_DIM, MXU_DIM)), const((MXU_DIM, MXU_DIM)),
            tab, tab, tab,
        ],
        out_specs=[row(w) for w, _ in out_w]
        + [pl.BlockSpec((MLA_HEADS * MLA_V, ROW_TILE), lambda i: (0, i))],
        out_shape=[jax.ShapeDtypeStruct((T, w), dt) for w, dt in out_w]
        + [jax.ShapeDtypeStruct((MLA_HEADS * MLA_V, T), BF16)],
        compiler_params=_cparams(("arbitrary",)),
        name="proj",
    )(xs, mods, prm["g_mix"], prm["w_in"], prm["w_uq"], prm["w_uk"], prm["w_uv"],
      prm["gq"], prm["gkv"], prm["na_gq"], prm["na_gk"], prm["mla_gq"], prm["mla_gk"],
      tabs["bd64"], tabs["bd128"], tabs["cos"], tabs["sa"], tabs["sb"])


def _conv_kernel(x_ref, prev_ref, next_ref, w_ref, b_ref, q_ref, k_ref, *, lat_tiles, n_lat):
    i = pl.program_id(0)
    is_lat = i < n_lat
    first = jnp.where(is_lat, (i % lat_tiles) == 0, True)
    last = jnp.where(is_lat, (i % lat_tiles) == lat_tiles - 1, True)
    x = x_ref[...]
    n = x.shape[0]
    prev_row = jnp.where(first, 0.0, prev_ref[7:8, :])
    next_row = jnp.where(last, 0.0, next_ref[0:1, :])
    ridx = lax.broadcasted_iota(jnp.int32, x.shape, 0)
    xm1 = jnp.where(ridx == 0, prev_row, pltpu.roll(x, 1, 0))
    xp1 = jnp.where(ridx == n - 1, next_row, pltpu.roll(x, n - 1, 0))
    y = b_ref[...] + xm1 * w_ref[0:1, :] + x * w_ref[1:2, :] + xp1 * w_ref[2:3, :]
    y = y * _sigmoid(y)
    q_ref[...] = y[:, :ML_W].astype(BF16)
    k_ref[...] = (y[:, ML_W:] * (HEAD_DIM ** -0.5)).astype(BF16)


def _conv(l, mlqk, prm, dims):
    T, W = mlqk.shape
    B, S, CTX = dims
    tc = CTX
    nt = T // tc
    sub = tc // 8
    nblk8 = T // 8
    kern = functools.partial(_conv_kernel, lat_tiles=S // tc, n_lat=B * S // tc)
    return pl.pallas_call(
        kern,
        grid=(nt,),
        in_specs=[
            pl.BlockSpec((tc, W), lambda i: (i, 0)),
            pl.BlockSpec((8, W), lambda i: (jnp.maximum(i * sub - 1, 0), 0)),
            pl.BlockSpec((8, W), lambda i: (jnp.minimum((i + 1) * sub, nblk8 - 1), 0)),
            pl.BlockSpec((None, 8, W), lambda i: (l, 0, 0)),
            pl.BlockSpec((None, 1, W), lambda i: (l, 0, 0)),
        ],
        out_specs=[pl.BlockSpec((tc, ML_W), lambda i: (i, 0))] * 2,
        out_shape=[jax.ShapeDtypeStruct((T, ML_W), BF16)] * 2,
        compiler_params=_cparams(("arbitrary",)),
        name="conv",
    )(mlqk, mlqk, mlqk, prm["conv_w"], prm["conv_b"])


def _pair_select(o0, o1):
    lane = lax.broadcasted_iota(jnp.int32, o0.shape, 1)
    return jnp.where(lane < HEAD_DIM, o0, o1)


def _head_q(q, e, packed):
    if packed:
        lane = lax.broadcasted_iota(jnp.int32, q.shape, 1)
        keep = (lane < HEAD_DIM) if e == 0 else (lane >= HEAD_DIM)
        return jnp.where(keep, q, jnp.zeros_like(q))
    return q[:, e * LANES:(e + 1) * LANES]


def _kcols(e, packed):
    return slice(0, LANES) if packed else slice(e * LANES, (e + 1) * LANES)


def _na_kernel(q_ref, k_ref, v_ref, kc_ref, vc_ref, bias_ref, o_ref, *, nb):
    j = pl.program_id(2)
    start = pl.multiple_of(jnp.clip(j - 1, 0, nb - 3) * ATT_BLOCK, ATT_BLOCK)
    q = q_ref[...]
    kw = k_ref[pl.ds(start, NA_KEY_ROWS), :]
    vw = v_ref[pl.ds(start, NA_KEY_ROWS), :]
    kc = kc_ref[...]
    vc = vc_ref[...]
    outs = []
    for e in range(2):
        qh = _head_q(q, e, True)
        s_loc = _dot_nt(qh, kw) + bias_ref[e]
        s_ctx = _dot_nt(qh, kc)
        m = jnp.maximum(jnp.max(s_loc, axis=-1, keepdims=True), jnp.max(s_ctx, axis=-1, keepdims=True))
        p_loc = jnp.exp(s_loc - m)
        p_ctx = jnp.exp(s_ctx - m)
        den = jnp.sum(p_loc, axis=-1, keepdims=True) + jnp.sum(p_ctx, axis=-1, keepdims=True)
        o = _dot(p_loc.astype(BF16), vw) + _dot(p_ctx.astype(BF16), vc)
        outs.append(o / den)
    o_ref[...] = _pair_select(outs[0], outs[1]).astype(o_ref.dtype)


def _na(l, naq, nak, nav, bias, dims):
    T = naq.shape[0]
    B, S, CTX = dims
    nb = S // ATT_BLOCK
    ctx_blk0 = B * S // CTX
    npair = NA_HEADS // 2

    def qrow(b, j):
        return jnp.where(j < nb, b * nb + j, ctx_blk0 + b)

    def variant(j):
        return jnp.where(j == 0, 0, jnp.where(j == nb - 1, 2, jnp.where(j == nb, 3, 1)))

    kern = functools.partial(_na_kernel, nb=nb)
    return pl.pallas_call(
        kern,
        grid=(B, npair, nb + 1),
        in_specs=[
            pl.BlockSpec((ATT_BLOCK, LANES), lambda b, p, j: (qrow(b, j), p)),
            pl.BlockSpec((S, LANES), lambda b, p, j: (b, p)),
            pl.BlockSpec((S, LANES), lambda b, p, j: (b, p)),
            pl.BlockSpec((CTX, LANES), lambda b, p, j: (ctx_blk0 + b, p)),
            pl.BlockSpec((CTX, LANES), lambda b, p, j: (ctx_blk0 + b, p)),
            pl.BlockSpec((None, None, 2, ATT_BLOCK, NA_KEY_ROWS), lambda b, p, j: (l, variant(j), p, 0, 0)),
        ],
        out_specs=pl.BlockSpec((ATT_BLOCK, LANES), lambda b, p, j: (qrow(b, j), p)),
        out_shape=jax.ShapeDtypeStruct((T, NA_W), BF16),
        compiler_params=_cparams(("arbitrary", "arbitrary", "arbitrary")),
        name="na",
    )(naq, nak, nav, nak, nav, bias)


def _flash_kernel(q_ref, km_ref, vm_ref, kc_ref, vc_ref, o_ref, s_ref, mx_ref, acc_ref, st_ref,
                  *, nq, n_main):
    qi = pl.program_id(2)
    tk = FLASH_TK
    q = q_ref[...]
    qs = [q[:, e * LANES:(e + 1) * LANES] for e in range(2)]
    hrows = lambda e: slice(e * HEAD_DIM, (e + 1) * HEAD_DIM)

    def scores(slot, r0):
        for e in range(2):
            st = _dot_nt(km_ref[pl.ds(r0, tk), e * LANES:(e + 1) * LANES], qs[e])
            s_ref[slot, e] = st
            mx_ref[slot, e] = jnp.max(st, axis=0, keepdims=True)

    def absorb(slot, r0):
        for e in range(2):
            m = st_ref[e, 0]
            m_new = jnp.maximum(m, mx_ref[slot, e])
            alpha = jnp.exp2(m - m_new)
            p = jnp.exp2(s_ref[slot, e] - m_new)
            st_ref[e, 0] = m_new
            st_ref[e, 1] = alpha * st_ref[e, 1] + jnp.sum(p, axis=0, keepdims=True)
            pv = _dot(vm_ref[hrows(e), pl.ds(r0, tk)], p.astype(BF16))
            acc_ref[e] = alpha * acc_ref[e] + pv

    for e in range(2):
        st = _dot_nt(kc_ref[:, e * LANES:(e + 1) * LANES], qs[e])
        m = jnp.max(st, axis=0, keepdims=True)
        p = jnp.exp2(st - m)
        st_ref[e, 0] = m
        st_ref[e, 1] = jnp.sum(p, axis=0, keepdims=True)
        acc_ref[e] = _dot(vc_ref[hrows(e), :], p.astype(BF16))

    @pl.when(qi < nq)
    def _():
        scores(0, 0)

        def body(i2, carry):
            r0 = pl.multiple_of(i2 * (2 * tk), 2 * tk)
            scores(1, r0 + tk)
            absorb(0, r0)
            scores(0, r0 + 2 * tk)
            absorb(1, r0 + tk)
            return carry

        lax.fori_loop(0, n_main // 2 - 1, body, 0)
        r0 = (n_main - 2) * tk
        scores(1, r0 + tk)
        absorb(0, r0)
        absorb(1, r0 + tk)

    ot = jnp.concatenate([acc_ref[e] / st_ref[e, 1] for e in range(2)], axis=0)
    o_ref[...] = ot.T.astype(o_ref.dtype)


def _flash(mq, mk, mvt, dims):
    T = mq.shape[0]
    B, S, CTX = dims
    nq = S // ATT_BLOCK
    ctx_blk0 = B * S // CTX
    npair = MLA_HEADS // 2

    def qrow(b, j):
        return jnp.where(j < nq, b * nq + j, ctx_blk0 + b)

    kern = functools.partial(_flash_kernel, nq=nq, n_main=S // FLASH_TK)
    return pl.pallas_call(
        kern,
        grid=(B, npair, nq + 1),
        in_specs=[
            pl.BlockSpec((ATT_BLOCK, 2 * LANES), lambda b, p, j: (qrow(b, j), p)),
            pl.BlockSpec((S, 2 * LANES), lambda b, p, j: (b, p)),
            pl.BlockSpec((LANES, S), lambda b, p, j: (p, b)),
            pl.BlockSpec((CTX, 2 * LANES), lambda b, p, j: (ctx_blk0 + b, p)),
            pl.BlockSpec((LANES, CTX), lambda b, p, j: (p, ctx_blk0 + b)),
        ],
        out_specs=pl.BlockSpec((ATT_BLOCK, LANES), lambda b, p, j: (qrow(b, j), p)),
        out_shape=jax.ShapeDtypeStruct((T, MLA_HEADS * MLA_V), BF16),
        scratch_shapes=[pltpu.VMEM((2, 2, FLASH_TK, ATT_BLOCK), F32),
                        pltpu.VMEM((2, 2, 1, ATT_BLOCK), F32),
                        pltpu.VMEM((2, HEAD_DIM, ATT_BLOCK), F32),
                        pltpu.VMEM((2, 2, 1, ATT_BLOCK), F32)],
        compiler_params=_cparams(("arbitrary", "arbitrary", "arbitrary")),
        name="flash",
    )(mq, mk, mvt, mk, mvt)


def _mlstm_dir(d, q, k, v, gates, bgate, tri, c_ref, m_ref, h_ref):
    L = q.shape[0]
    gb = gates + bgate
    ls = jnp.minimum(gb, 0.0) - jnp.log1p(jnp.exp(-jnp.abs(gb)))
    lane = lax.broadcasted_iota(jnp.int32, gb.shape, 1)
    is_f = ((lane >= 4) & (lane < 8)) | ((lane >= 12) & (lane < 16))
    xg = jnp.where(lane < 16, jnp.where(is_f, ls, gb), 0.0)
    x1 = xg.astype(BF16)
    r1 = xg - x1.astype(F32)
    x2 = r1.astype(BF16)
    x3 = (r1 - x2.astype(F32)).astype(BF16)
    cum = _dot(tri, x1) + _dot(tri, x2) + _dot(tri, x3)
    xg_t = xg.T
    cum_t = cum.T
    tot = jnp.sum(xg, axis=0, keepdims=True)

    ti = lax.broadcasted_iota(jnp.int32, (L, L), 0)
    si = lax.broadcasted_iota(jnp.int32, (L, L), 1)
    valid = (si <= ti) if d == 0 else (si >= ti)
    lane_l = lax.broadcasted_iota(jnp.int32, (L, LANES), 1)

    for p in range(ML_HEADS // 2):
        qp = q[:, p * LANES:(p + 1) * LANES]
        kp = k[:, p * LANES:(p + 1) * LANES]
        vp = v[:, p * LANES:(p + 1) * LANES]
        hs = []
        for e in range(2):
            hd = 2 * p + e
            ci = 8 * d + hd
            cf = 8 * d + 4 + hd
            in_head = (lane_l < HEAD_DIM) if e == 0 else (lane_l >= HEAD_DIM)
            ones_lane = HEAD_DIM if e == 0 else 0
            km = jnp.where(in_head, kp, jnp.zeros_like(kp))
            one_hot = jnp.where(lane_l == ones_lane, 1.0, 0.0).astype(BF16)
            va = jnp.where(in_head, vp, one_hot)

            b_col = cum[:, cf:cf + 1]
            b_row = cum_t[cf:cf + 1, :]
            li_col = xg[:, ci:ci + 1]
            li_row = xg_t[ci:ci + 1, :]
            m_prev = m_ref[d * ML_HEADS + hd:d * ML_HEADS + hd + 1, 0:1]
            cst = c_ref[d, hd]

            g = b_col + m_prev
            dm = jnp.where(valid, b_col - b_row + li_row, NEG)
            mt = jnp.maximum(g, jnp.max(dm, axis=-1, keepdims=True))
            w_inter = jnp.exp(g - mt)
            qk = _dot_nt(qp, km) * jnp.exp(dm - mt)
            haug = w_inter * _dot_nt(qp, cst.astype(BF16)) + _dot(qk.astype(BF16), va)
            den = haug[:, ones_lane:ones_lane + 1]
            hs.append(haug / jnp.maximum(jnp.abs(den), jnp.exp(-mt)))

            b_tot = tot[:, cf:cf + 1]
            a_col = b_tot - b_col + li_col
            m_new = jnp.maximum(b_tot + m_prev, jnp.max(a_col, axis=0, keepdims=True))
            decay = jnp.exp(b_tot + m_prev - m_new)
            wk = jnp.exp(a_col - m_new)
            wv = (wk * va.astype(F32)).astype(BF16)
            c_ref[d, hd] = decay * cst + _dot_tn(wv, km)
            m_ref[d * ML_HEADS + hd:d * ML_HEADS + hd + 1, :] = jnp.broadcast_to(m_new, (1, LANES))
        h_ref[:, p * LANES:(p + 1) * LANES] = _pair_select(hs[0], hs[1])


def _mlstm_kernel(qf_ref, kf_ref, vf_ref, gf_ref, qb_ref, kb_ref, vb_ref, gb_ref,
                  bgate_ref, tril_ref, triu_ref, hf_ref, hb_ref, c_ref, m_ref):
    @pl.when(pl.program_id(1) == 0)
    def _():
        c_ref[...] = jnp.zeros_like(c_ref)
        m_ref[...] = jnp.zeros_like(m_ref)

    bgate = bgate_ref[...]
    _mlstm_dir(0, qf_ref[...], kf_ref[...], vf_ref[...], gf_ref[...], bgate, tril_ref[...],
               c_ref, m_ref, hf_ref)
    _mlstm_dir(1, qb_ref[...], kb_ref[...], vb_ref[...], gb_ref[...], bgate, triu_ref[...],
               c_ref, m_ref, hb_ref)


def _mlstm(l, cq, ck, mlv, misc, prm, tabs, dims):
    T = cq.shape[0]
    B, S, CTX = dims
    L = ML_CHUNK
    n_ctx = CTX // L
    n_lat = S // L
    ctx0 = B * S // L

    def fwd(b, c):
        return jnp.where(c < n_ctx, ctx0 + b * n_ctx + c, b * n_lat + (c - n_ctx))

    def bwd(b, c):
        return jnp.where(c < n_ctx, ctx0 + b * n_ctx + (n_ctx - 1 - c), b * n_lat + (n_lat - 1 - (c - n_ctx)))

    def specs(fn):
        return [pl.BlockSpec((L, ML_W), lambda b, c: (fn(b, c), 0)),
                pl.BlockSpec((L, ML_W), lambda b, c: (fn(b, c), 0)),
                pl.BlockSpec((L, ML_W), lambda b, c: (fn(b, c), 0)),
                pl.BlockSpec((L, LANES), lambda b, c: (fn(b, c), 0))]

    return pl.pallas_call(
        _mlstm_kernel,
        grid=(B, n_ctx + n_lat),
        in_specs=specs(fwd) + specs(bwd) + [
            pl.BlockSpec((None, 1, LANES), lambda b, c: (l, 0, 0)),
            pl.BlockSpec((L, L), lambda b, c: (0, 0)),
            pl.BlockSpec((L, L), lambda b, c: (0, 0)),
        ],
        out_specs=[pl.BlockSpec((L, ML_W), lambda b, c: (fwd(b, c), 0)),
                   pl.BlockSpec((L, ML_W), lambda b, c: (bwd(b, c), 0))],
        out_shape=[jax.ShapeDtypeStruct((T, ML_W), F32)] * 2,
        scratch_shapes=[pltpu.VMEM((2, ML_HEADS, LANES, LANES), F32), pltpu.VMEM((8, LANES), F32)],
        compiler_params=_cparams(("arbitrary", "arbitrary")),
        name="mlstm",
    )(cq, ck, mlv, misc, cq, ck, mlv, misc, prm["b_gate"], tabs["tril"], tabs["triu"])


def _ffn_chunks(ff):
    step = 2 * MXU_DIM
    return [(c, min(step, ff - c)) for c in range(0, ff, step)]


def _out_kernel(x_ref, mod_ref, ona_ref, omla_ref, hf_ref, hb_ref, mlo_ref, hg_ref, bd64_ref,
                w_out_ref, g_ref, wa_ref, wb_ref, wo_ref, o_ref):
    x = x_ref[...]
    gate1 = mod_ref[2:3, :]
    shift2 = mod_ref[3:4, :]
    scale2 = mod_ref[4:5, :]
    gate2 = mod_ref[5:6, :]

    h = hf_ref[...] + hb_ref[...]
    hn = h * lax.rsqrt(_group_sumsq(h, bd64_ref[...]) * (1.0 / HEAD_DIM) + EPS) * hg_ref[...]
    ob = (hn * _sigmoid(mlo_ref[...])).astype(BF16)
    mix = (_dot(ona_ref[...], w_out_ref[0:NA_W, :])
           + _dot(ob, w_out_ref[NA_W:NA_W + ML_W, :])
           + _dot(omla_ref[...], w_out_ref[NA_W + ML_W:, :]))
    x1 = x + gate1 * mix

    h2 = ((_rms_rows(x1) * g_ref[...]) * (1.0 + scale2) + shift2).astype(BF16)
    acc = jnp.zeros_like(x1)
    for c0, w in _ffn_chunks(wa_ref.shape[1]):
        a = _dot(h2, wa_ref[:, c0:c0 + w])
        b = _dot(h2, wb_ref[:, c0:c0 + w])
        act = (a * _sigmoid(a) * b).astype(BF16)
        acc = acc + _dot(act, wo_ref[c0:c0 + w, :])
    o_ref[...] = x1 + gate2 * acc


def _out(l, xs, mods, ona, omla, hf, hb, mlo, prm, tabs, dims):
    T, D = xs.shape
    B, S, CTX = dims
    nt = T // ROW_TILE
    lat_tiles = S // ROW_TILE
    ff = prm["w_a"].shape[-1]

    def grp(i):
        return jnp.minimum(i // lat_tiles, B)

    row = lambda w: pl.BlockSpec((ROW_TILE, w), lambda i: (i, 0))
    lay = lambda shape: pl.BlockSpec((None,) + shape, lambda i: (l,) + (0,) * len(shape))
    big = lambda shape: pl.BlockSpec((None,) + shape, lambda i: (l,) + (0,) * len(shape),
                                     pipeline_mode=pl.Buffered(1))
    return pl.pallas_call(
        _out_kernel,
        grid=(nt,),
        in_specs=[
            row(D),
            pl.BlockSpec((None, None, 6, D), lambda i: (l, grp(i), 0, 0)),
            row(NA_W), row(MLA_HEADS * MLA_V), row(ML_W), row(ML_W), row(ML_W),
            lay((1, ML_W)),
            pl.BlockSpec((MXU_DIM, MXU_DIM), lambda i: (0, 0)),
            big((NA_W + ML_W + MLA_HEADS * MLA_V, D)),
            lay((1, D)),
            big((D, ff)), big((D, ff)), big((ff, D)),
        ],
        out_specs=row(D),
        out_shape=jax.ShapeDtypeStruct((T, D), F32),
        compiler_params=_cparams(("arbitrary",)),
        name="out_ffn",
    )(xs, mods, ona, omla, hf, hb, mlo, prm["head_gain"], tabs["bd64"],
      prm["w_out"], prm["g_ffn"], prm["w_a"], prm["w_b"], prm["w_fo"])


def _block_diag_ones(group):
    idx = np.arange(MXU_DIM) // group
    return jnp.asarray((idx[:, None] == idx[None, :]).astype(np.float32), dtype=BF16)


def _rope_tables(S, n_ctx_rows):
    n_freq = MLA_ROPE // 4
    inv = ROPE_BASE ** (-jnp.arange(n_freq, dtype=F32) / n_freq)
    t = jnp.arange(S, dtype=jnp.int32)
    row = (t // GRID_W).astype(F32)
    col = (t % GRID_W).astype(F32)
    ang = jnp.concatenate([row[:, None] * inv, col[:, None] * inv], axis=-1)
    ang = jnp.concatenate([ang, jnp.zeros((n_ctx_rows, MLA_ROPE // 2), F32)], axis=0)
    cos, sin = jnp.cos(ang), jnp.sin(ang)
    n = S + n_ctx_rows
    half = MLA_ROPE // 2
    ones = jnp.ones((n, MLA_NOPE), F32)
    z = lambda w: jnp.zeros((n, w), F32)
    tail = LANES - MLA_NOPE - MLA_ROPE
    cos_t = jnp.concatenate([ones, cos, cos, jnp.ones((n, tail), F32)], axis=-1)
    sa_t = jnp.concatenate([z(MLA_NOPE), -sin, z(half), z(tail)], axis=-1)
    sb_t = jnp.concatenate([z(MLA_NOPE), z(half), sin, z(tail)], axis=-1)
    return cos_t, sa_t, sb_t


def _na_bias(rpb, rows):
    depth, H = rpb.shape[:2]
    qr, kr_n = ATT_BLOCK // GRID_W, NA_KEY_ROWS // GRID_W
    nb = rows // qr
    cols = np.arange(GRID_W)
    cs = np.clip(cols - NA_WIN_C // 2, 0, GRID_W - NA_WIN_C)
    col_ok = (cols[None, :] >= cs[:, None]) & (cols[None, :] < cs[:, None] + NA_WIN_C)
    dc = np.clip(cols[None, :] - cols[:, None] + NA_WIN_C - 1, 0, 2 * NA_WIN_C - 2)
    dr_all, ok_all = [], []
    for j in (0, 1, nb - 1):
        ks = int(np.clip(j - 1, 0, nb - 3)) * qr
        r = j * qr + np.arange(qr)
        rs = np.clip(r - NA_WIN_R // 2, 0, rows - NA_WIN_R)
        kr = ks + np.arange(kr_n)
        ok_all.append((kr[None, :] >= rs[:, None]) & (kr[None, :] < rs[:, None] + NA_WIN_R))
        dr_all.append(np.clip(kr[None, :] - r[:, None] + NA_WIN_R - 1, 0, 2 * NA_WIN_R - 2))
    dr_all = np.stack(dr_all)
    ok_all = np.stack(ok_all)
    t1 = jnp.take(rpb, jnp.asarray(dc.reshape(-1)), axis=3).reshape(depth, H, 2 * NA_WIN_R - 1, GRID_W, GRID_W)
    t1 = jnp.where(jnp.asarray(col_ok), t1, NEG)
    t2 = jnp.take(t1, jnp.asarray(dr_all.reshape(-1)), axis=2).reshape(depth, H, 3, qr, kr_n, GRID_W, GRID_W)
    t2 = jnp.where(jnp.asarray(ok_all)[None, None, :, :, :, None, None], t2, NEG)
    bias = t2.transpose(0, 2, 1, 3, 5, 4, 6).reshape(depth, 3, H, ATT_BLOCK, NA_KEY_ROWS)
    none = jnp.full((depth, 1, H, ATT_BLOCK, NA_KEY_ROWS), NEG, F32)
    return jnp.concatenate([bias, none], axis=1)


def _prepare(w_in, w_uq, w_ukv, w_out, w_ffn_in, w_ffn_out, g_mix, g_ffn, b_gate, na_qk_gain,
             ml_conv_w, ml_conv_b, ml_head_gain, mla_gq, mla_gkv, mla_qk_gain):
    depth, D, _ = w_in.shape
    o_gate = C_MLO + ML_W
    o_ql = o_gate + 4 * ML_HEADS
    o_kr = o_ql + Q_LORA + KV_LORA
    zc = lambda w: jnp.zeros((depth, D, w), w_in.dtype)
    w_in_p = jnp.concatenate([
        w_in[..., :o_gate], w_in[..., o_ql:o_kr],
        w_in[..., o_gate:o_ql], zc(MISC_ROPE_LANE - 4 * ML_HEADS),
        w_in[..., o_kr:], zc(LANES - MISC_ROPE_LANE - MLA_ROPE)], axis=-1).astype(BF16)
    pad_h = lambda a, w: jnp.pad(a, [(0, 0)] * (a.ndim - 1) + [(0, LANES - w)])
    w_uq_p = pad_h(w_uq.reshape(depth, Q_LORA, MLA_HEADS, MLA_QK), MLA_QK)
    w_uq_p = w_uq_p.reshape(depth, Q_LORA, MLA_HEADS * LANES).astype(BF16)
    ukv = w_ukv.reshape(depth, KV_LORA, MLA_HEADS, MLA_NOPE + MLA_V)
    w_uk_p = pad_h(ukv[..., :MLA_NOPE], MLA_NOPE).reshape(depth, KV_LORA, MLA_HEADS * LANES).astype(BF16)
    w_uv = ukv[..., MLA_NOPE:].reshape(depth, KV_LORA, MLA_HEADS * MLA_V).transpose(0, 2, 1).astype(BF16)
    ff = w_ffn_out.shape[1]
    mla_g = pad_h(mla_qk_gain, MLA_QK)
    return {
        "w_in": w_in_p, "w_uq": w_uq_p, "w_uk": w_uk_p, "w_uv": w_uv,
        "w_out": w_out.astype(BF16),
        "w_a": w_ffn_in[..., :ff].astype(BF16), "w_b": w_ffn_in[..., ff:].astype(BF16),
        "w_fo": w_ffn_out.astype(BF16),
        "g_mix": g_mix[:, None, :], "g_ffn": g_ffn[:, None, :],
        "gq": mla_gq[:, None, :], "gkv": mla_gkv[:, None, :],
        "na_gq": jnp.tile(na_qk_gain[:, 0:1, :], (1, 1, NA_HEADS)) * NA_SCALE,
        "na_gk": jnp.tile(na_qk_gain[:, 1:2, :], (1, 1, NA_HEADS)),
        "mla_gq": jnp.tile(mla_g[:, 0:1, :], (1, 1, MLA_HEADS)) * (MLA_SCALE * LOG2E),
        "mla_gk": jnp.tile(mla_g[:, 1:2, :], (1, 1, MLA_HEADS)),
        "head_gain": ml_head_gain.reshape(depth, 1, ML_W),
        "b_gate": pad_h(b_gate, 4 * ML_HEADS)[:, None, :],
        "conv_w": jnp.pad(ml_conv_w, ((0, 0), (0, 8 - ml_conv_w.shape[1]), (0, 0))),
        "conv_b": ml_conv_b[:, None, :],
    }


def kernel(x, c, ctx, c_ctx, w_mod, b_mod, g_mix, g_ffn, w_in, b_gate, na_qk_gain, na_rpb,
           ml_conv_w, ml_conv_b, ml_head_gain, mla_gq, mla_gkv, w_uq, w_ukv, mla_qk_gain,
           w_out, w_ffn_in, w_ffn_out):
    B, S, D = x.shape
    CTX = ctx.shape[1]
    depth = w_in.shape[0]
    assert CTX == ATT_BLOCK and S % ROW_TILE == 0 and (B * CTX) % ROW_TILE == 0
    assert S % (2 * FLASH_TK) == 0 and S // ATT_BLOCK >= 3 and B + 1 <= 8
    dims = (B, S, CTX)

    prm = _prepare(w_in, w_uq, w_ukv, w_out, w_ffn_in, w_ffn_out, g_mix, g_ffn, b_gate, na_qk_gain,
                   ml_conv_w, ml_conv_b, ml_head_gain, mla_gq, mla_gkv, mla_qk_gain)
    cos_t, sa_t, sb_t = _rope_tables(S, B * CTX)
    tri = np.tril(np.ones((ML_CHUNK, ML_CHUNK), np.float32))
    tabs = {
        "bd64": _block_diag_ones(HEAD_DIM), "bd128": _block_diag_ones(LANES),
        "cos": cos_t, "sa": sa_t, "sb": sb_t,
        "tril": jnp.asarray(tri, dtype=BF16), "triu": jnp.asarray(tri.T, dtype=BF16),
    }
    bias = _na_bias(na_rpb, S // GRID_W)

    cond8 = jnp.concatenate([c, c_ctx[None, :], jnp.zeros((8 - B - 1, D), F32)], axis=0)
    mods = _modulation(cond8, w_mod, b_mod)
    mods = mods.reshape(depth, 8, 6, D)

    xs = jnp.concatenate([x.reshape(B * S, D), ctx.reshape(B * CTX, D)], axis=0)
    for l in range(depth):
        naq, nak, nav, mlqk, mlv, mlo, misc, mq, mk, mv = _proj(l, xs, mods, prm, tabs, dims)
        cq, ck = _conv(l, mlqk, prm, dims)
        ona = _na(l, naq, nak, nav, bias, dims)
        omla = _flash(mq, mk, mv, dims)
        hf, hb = _mlstm(l, cq, ck, mlv, misc, prm, tabs, dims)
        xs = _out(l, xs, mods, ona, omla, hf, hb, mlo, prm, tabs, dims)
    return xs[:B * S].reshape(B, S, D)
```

```python
import functools

import numpy as np
import jax
import jax.numpy as jnp
from jax import lax
from jax.experimental import pallas as pl
from jax.experimental.pallas import tpu as pltpu

GRID_W = 64
HEAD_DIM = 64
NA_HEADS = 4
NA_WIN_R = 8
NA_WIN_C = 16
ML_HEADS = 4
ML_CHUNK = 128
MLA_HEADS = 8
MLA_NOPE = 64
MLA_ROPE = 32
MLA_V = 64
Q_LORA = 384
KV_LORA = 256
ROPE_BASE = 10000.0
EPS = 1e-6
NA_W = NA_HEADS * HEAD_DIM
ML_W = ML_HEADS * HEAD_DIM
MLA_QK = MLA_NOPE + MLA_ROPE
NA_SCALE = HEAD_DIM ** -0.5
MLA_SCALE = MLA_QK ** -0.5
LOG2E = 1.4426950408889634

LANES = 128
MXU_DIM = 256
VMEM_LIMIT = 56 * 1024 * 1024

ROW_TILE = 512
ATT_BLOCK = 256
NA_KEY_ROWS = 768
FLASH_TK = 512
FLASH_TQ = 512
NEG = -1e30
DEN_ROWS = 16

C_NAQ, C_NAK, C_NAV = 0, 256, 512
C_MLQK, C_MLV, C_MLO = 768, 1280, 1536
C_QL = 1792
C_KVL = C_QL + Q_LORA
C_MISC = C_KVL + KV_LORA
D_IN_PAD = C_MISC + LANES
MISC_ROPE_LANE = 64

BF16 = jnp.bfloat16
F32 = jnp.float32


def _cparams(sem):
    return pltpu.CompilerParams(dimension_semantics=sem, vmem_limit_bytes=VMEM_LIMIT)


def _dot(a, b):
    return jnp.dot(a, b, preferred_element_type=F32)


def _dot_nt(a, b):
    return lax.dot_general(a, b, (((1,), (1,)), ((), ())), preferred_element_type=F32)


def _dot_tn(a, b):
    return lax.dot_general(a, b, (((0,), (0,)), ((), ())), preferred_element_type=F32)


def _split2(x):
    hi = x.astype(BF16)
    lo = (x - hi.astype(F32)).astype(BF16)
    return hi, lo


def _group_sumsq(x, bd):
    x2 = x * x
    hi, lo = _split2(x2)
    outs = []
    for c in range(x.shape[1] // MXU_DIM):
        sl = slice(c * MXU_DIM, (c + 1) * MXU_DIM)
        outs.append(_dot(hi[:, sl], bd) + _dot(lo[:, sl], bd))
    return outs[0] if len(outs) == 1 else jnp.concatenate(outs, axis=-1)


def _rms_rows(x):
    return x * lax.rsqrt(jnp.mean(x * x, axis=-1, keepdims=True) + EPS)


def _sigmoid(x):
    return 1.0 / (1.0 + jnp.exp(-x))


def _mod_kernel(c_ref, w_ref, b_ref, o_ref):
    c = c_ref[...]
    a = c * _sigmoid(c)
    o_ref[...] = jnp.dot(a, w_ref[...], preferred_element_type=F32,
                         precision=lax.Precision.HIGHEST) + b_ref[...]


def _modulation(cond8, w_mod, b_mod):
    depth, d, d6 = w_mod.shape
    n = d6 // d
    return pl.pallas_call(
        _mod_kernel,
        grid=(depth, n),
        in_specs=[
            pl.BlockSpec((8, d), lambda l, j: (0, 0)),
            pl.BlockSpec((None, d, d), lambda l, j: (l, 0, j)),
            pl.BlockSpec((None, 1, d), lambda l, j: (l, 0, j)),
        ],
        out_specs=pl.BlockSpec((None, 8, d), lambda l, j: (l, 0, j)),
        out_shape=jax.ShapeDtypeStruct((depth, 8, d6), F32),
        compiler_params=_cparams(("arbitrary", "arbitrary")),
        name="modulation",
    )(cond8, w_mod, b_mod.reshape(depth, 1, d6))


def _rope128(x, cos, sa, sb):
    return x * cos + pltpu.roll(x, LANES - 16, 1) * sa + pltpu.roll(x, 16, 1) * sb


def _proj_kernel(x_ref, mod_ref, g_ref, w_in_ref, w_uq_ref, w_uk_ref, w_uv_ref,
                 gq_ref, gkv_ref, na_gq_ref, na_gk_ref, mla_gq_ref, mla_gk_ref,
                 bd64_ref, bd128_ref, cos_ref, sa_ref, sb_ref,
                 naq_ref, nak_ref, nav_ref, mlqk_ref, mlv_ref, mlo_ref, misc_ref,
                 mq_ref, mk_ref, mv_ref):
    x = x_ref[...]
    shift = mod_ref[0:1, :]
    scale = mod_ref[1:2, :]
    h = (_rms_rows(x) * g_ref[...]) * (1.0 + scale) + shift
    hb = h.astype(BF16)

    def proj(c0, width):
        return _dot(hb, w_in_ref[:, c0:c0 + width])

    bd64 = bd64_ref[...]
    bd128 = bd128_ref[...]

    pq = proj(C_NAQ, NA_W)
    naq_ref[...] = (pq * lax.rsqrt(_group_sumsq(pq, bd64) * (1.0 / HEAD_DIM) + EPS)
                    * na_gq_ref[...]).astype(BF16)
    pk = proj(C_NAK, NA_W)
    nak_ref[...] = (pk * lax.rsqrt(_group_sumsq(pk, bd64) * (1.0 / HEAD_DIM) + EPS)
                    * na_gk_ref[...]).astype(BF16)
    nav_ref[...] = proj(C_NAV, NA_W).astype(BF16)

    mlqk_ref[...] = proj(C_MLQK, 2 * ML_W)
    mlv_ref[...] = proj(C_MLV, ML_W).astype(BF16)
    mlo_ref[...] = proj(C_MLO, ML_W)
    misc = proj(C_MISC, LANES)
    misc_ref[...] = misc

    cos = cos_ref[...]
    sa = sa_ref[...]
    sb = sb_ref[...]
    ql = proj(C_QL, Q_LORA)
    qn = (_rms_rows(ql) * gq_ref[...]).astype(BF16)
    qr = _dot(qn, w_uq_ref[...])
    q = qr * lax.rsqrt(_group_sumsq(qr, bd128) * (1.0 / MLA_QK) + EPS) * mla_gq_ref[...]
    kvl = proj(C_KVL, KV_LORA)
    kvn = (_rms_rows(kvl) * gkv_ref[...]).astype(BF16)
    lane = lax.broadcasted_iota(jnp.int32, misc.shape, 1)
    krope = jnp.where((lane >= MISC_ROPE_LANE) & (lane < MISC_ROPE_LANE + MLA_ROPE), misc, 0.0)
    kr = _dot(kvn, w_uk_ref[...]) + jnp.concatenate([krope] * MLA_HEADS, axis=-1)
    k = kr * lax.rsqrt(_group_sumsq(kr, bd128) * (1.0 / MLA_QK) + EPS) * mla_gk_ref[...]
    for g in range(MLA_HEADS):
        sl = slice(g * LANES, (g + 1) * LANES)
        mq_ref[:, sl] = _rope128(q[:, sl], cos, sa, sb).astype(BF16)
        mk_ref[:, sl] = _rope128(k[:, sl], cos, sa, sb).astype(BF16)
    mv_ref[...] = _dot_nt(w_uv_ref[...], kvn).astype(BF16)


def _proj(l, xs, mods, prm, tabs, dims):
    T, D = xs.shape
    B, S, CTX = dims
    nt = T // ROW_TILE
    lat_tiles = S // ROW_TILE

    def grp(i):
        return jnp.minimum(i // lat_tiles, B)

    def tab_row(i):
        return jnp.where(i < B * lat_tiles, i % lat_tiles, lat_tiles + (i - B * lat_tiles))

    row = lambda w: pl.BlockSpec((ROW_TILE, w), lambda i: (i, 0))
    lay = lambda shape: pl.BlockSpec((None,) + shape, lambda i: (l,) + (0,) * len(shape))
    const = lambda shape: pl.BlockSpec(shape, lambda i: (0,) * len(shape))
    tab = pl.BlockSpec((ROW_TILE, LANES), lambda i: (tab_row(i), 0))
    out_w = [(NA_W, BF16), (NA_W, BF16), (NA_W, BF16), (2 * ML_W, F32), (ML_W, BF16), (ML_W, F32),
             (LANES, F32), (MLA_HEADS * LANES, BF16), (MLA_HEADS * LANES, BF16)]
    return pl.pallas_call(
        _proj_kernel,
        grid=(nt,),
        in_specs=[
            row(D),
            pl.BlockSpec((None, None, 6, D), lambda i: (l, grp(i), 0, 0)),
            lay((1, D)),
            lay((D, D_IN_PAD)), lay((Q_LORA, MLA_HEADS * LANES)), lay((KV_LORA, MLA_HEADS * LANES)),
            lay((MLA_HEADS * MLA_V, KV_LORA)),
            lay((1, Q_LORA)), lay((1, KV_LORA)), lay((1, NA_W)), lay((1, NA_W)),
            lay((1, MLA_HEADS * LANES)), lay((1, MLA_HEADS * LANES)),
            const((MXU_DIM, MXU_DIM)), const((MXU_DIM, MXU_DIM)),
            tab, tab, tab,
        ],
        out_specs=[row(w) for w, _ in out_w]
        + [pl.BlockSpec((MLA_HEADS * MLA_V, ROW_TILE), lambda i: (0, i))],
        out_shape=[jax.ShapeDtypeStruct((T, w), dt) for w, dt in out_w]
        + [jax.ShapeDtypeStruct((MLA_HEADS * MLA_V, T), BF16)],
        compiler_params=_cparams(("arbitrary",)),
        name="proj",
    )(xs, mods, prm["g_mix"], prm["w_in"], prm["w_uq"], prm["w_uk"], prm["w_uv"],
      prm["gq"], prm["gkv"], prm["na_gq"], prm["na_gk"], prm["mla_gq"], prm["mla_gk"],
      tabs["bd64"], tabs["bd128"], tabs["cos"], tabs["sa"], tabs["sb"])


def _conv_kernel(x_ref, prev_ref, next_ref, w_ref, b_ref, q_ref, k_ref, *, lat_tiles, n_lat):
    i = pl.program_id(0)
    is_lat = i < n_lat
    first = jnp.where(is_lat, (i % lat_tiles) == 0, True)
    last = jnp.where(is_lat, (i % lat_tiles) == lat_tiles - 1, True)
    x = x_ref[...]
    n = x.shape[0]
    prev_row = jnp.where(first, 0.0, prev_ref[7:8, :])
    next_row = jnp.where(last, 0.0, next_ref[0:1, :])
    ridx = lax.broadcasted_iota(jnp.int32, x.shape, 0)
    xm1 = jnp.where(ridx == 0, prev_row, pltpu.roll(x, 1, 0))
    xp1 = jnp.where(ridx == n - 1, next_row, pltpu.roll(x, n - 1, 0))
    y = b_ref[...] + xm1 * w_ref[0:1, :] + x * w_ref[1:2, :] + xp1 * w_ref[2:3, :]
    y = y * _sigmoid(y)
    q_ref[...] = y[:, :ML_W].astype(BF16)
    k_ref[...] = (y[:, ML_W:] * (HEAD_DIM ** -0.5)).astype(BF16)


def _conv(l, mlqk, prm, dims):
    T, W = mlqk.shape
    B, S, CTX = dims
    tc = CTX
    nt = T // tc
    sub = tc // 8
    nblk8 = T // 8
    kern = functools.partial(_conv_kernel, lat_tiles=S // tc, n_lat=B * S // tc)
    return pl.pallas_call(
        kern,
        grid=(nt,),
        in_specs=[
            pl.BlockSpec((tc, W), lambda i: (i, 0)),
            pl.BlockSpec((8, W), lambda i: (jnp.maximum(i * sub - 1, 0), 0)),
            pl.BlockSpec((8, W), lambda i: (jnp.minimum((i + 1) * sub, nblk8 - 1), 0)),
            pl.BlockSpec((None, 8, W), lambda i: (l, 0, 0)),
            pl.BlockSpec((None, 1, W), lambda i: (l, 0, 0)),
        ],
        out_specs=[pl.BlockSpec((tc, ML_W), lambda i: (i, 0))] * 2,
        out_shape=[jax.ShapeDtypeStruct((T, ML_W), BF16)] * 2,
        compiler_params=_cparams(("arbitrary",)),
        name="conv",
    )(mlqk, mlqk, mlqk, prm["conv_w"], prm["conv_b"])


def _pair_select(o0, o1):
    lane = lax.broadcasted_iota(jnp.int32, o0.shape, 1)
    return jnp.where(lane < HEAD_DIM, o0, o1)


def _head_q(q, e, packed):
    if packed:
        lane = lax.broadcasted_iota(jnp.int32, q.shape, 1)
        keep = (lane < HEAD_DIM) if e == 0 else (lane >= HEAD_DIM)
        return jnp.where(keep, q, jnp.zeros_like(q))
    return q[:, e * LANES:(e + 1) * LANES]


def _kcols(e, packed):
    return slice(0, LANES) if packed else slice(e * LANES, (e + 1) * LANES)


def _na_kernel(q_ref, k_ref, v_ref, kc_ref, vc_ref, bias_ref, o_ref, *, nb):
    j = pl.program_id(2)
    start = pl.multiple_of(jnp.clip(j - 1, 0, nb - 3) * ATT_BLOCK, ATT_BLOCK)
    q = q_ref[...]
    kw = k_ref[pl.ds(start, NA_KEY_ROWS), :]
    vw = v_ref[pl.ds(start, NA_KEY_ROWS), :]
    kc = kc_ref[...]
    vc = vc_ref[...]
    outs = []
    for e in range(2):
        qh = _head_q(q, e, True)
        s_loc = _dot_nt(qh, kw) + bias_ref[e]
        s_ctx = _dot_nt(qh, kc)
        m = jnp.maximum(jnp.max(s_loc, axis=-1, keepdims=True), jnp.max(s_ctx, axis=-1, keepdims=True))
        p_loc = jnp.exp(s_loc - m)
        p_ctx = jnp.exp(s_ctx - m)
        den = jnp.sum(p_loc, axis=-1, keepdims=True) + jnp.sum(p_ctx, axis=-1, keepdims=True)
        o = _dot(p_loc.astype(BF16), vw) + _dot(p_ctx.astype(BF16), vc)
        outs.append(o / den)
    o_ref[...] = _pair_select(outs[0], outs[1]).astype(o_ref.dtype)


def _na(l, naq, nak, nav, bias, dims):
    T = naq.shape[0]
    B, S, CTX = dims
    nb = S // ATT_BLOCK
    ctx_blk0 = B * S // CTX
    npair = NA_HEADS // 2

    def qrow(b, j):
        return jnp.where(j < nb, b * nb + j, ctx_blk0 + b)

    def variant(j):
        return jnp.where(j == 0, 0, jnp.where(j == nb - 1, 2, jnp.where(j == nb, 3, 1)))

    kern = functools.partial(_na_kernel, nb=nb)
    return pl.pallas_call(
        kern,
        grid=(B, npair, nb + 1),
        in_specs=[
            pl.BlockSpec((ATT_BLOCK, LANES), lambda b, p, j: (qrow(b, j), p)),
            pl.BlockSpec((S, LANES), lambda b, p, j: (b, p)),
            pl.BlockSpec((S, LANES), lambda b, p, j: (b, p)),
            pl.BlockSpec((CTX, LANES), lambda b, p, j: (ctx_blk0 + b, p)),
            pl.BlockSpec((CTX, LANES), lambda b, p, j: (ctx_blk0 + b, p)),
            pl.BlockSpec((None, None, 2, ATT_BLOCK, NA_KEY_ROWS), lambda b, p, j: (l, variant(j), p, 0, 0)),
        ],
        out_specs=pl.BlockSpec((ATT_BLOCK, LANES), lambda b, p, j: (qrow(b, j), p)),
        out_shape=jax.ShapeDtypeStruct((T, NA_W), BF16),
        compiler_params=_cparams(("arbitrary", "arbitrary", "arbitrary")),
        name="na",
    )(naq, nak, nav, nak, nav, bias)


def _flash_values(v):
    return jnp.concatenate([v, jnp.ones((DEN_ROWS, v.shape[1]), BF16)], axis=0)


def _flash_queries(q_ref):
    q = q_ref[...]
    return [q[:, e * LANES:(e + 1) * LANES].astype(F32).T.astype(BF16) for e in range(2)]


def _flash_ctx_init(qs, kc_ref, vc_ref, acc_ref, st_ref):
    for e in range(2):
        st = _dot(kc_ref[:, e * LANES:(e + 1) * LANES], qs[e])
        m = jnp.max(st, axis=0, keepdims=True)
        p = jnp.exp2(st - m)
        st_ref[e] = m
        acc_ref[e] = _dot(_flash_values(vc_ref[e * HEAD_DIM:(e + 1) * HEAD_DIM, :]), p.astype(BF16))


def _flash_finish(acc_ref, o_ref):
    ot = jnp.concatenate([acc_ref[e, :HEAD_DIM, :] / acc_ref[e, HEAD_DIM:HEAD_DIM + 1, :]
                          for e in range(2)], axis=0)
    o_ref[...] = ot.T.astype(o_ref.dtype)


def _flash_kernel(q_ref, km_ref, vm_ref, kc_ref, vc_ref, o_ref, s_ref, mx_ref, acc_ref, st_ref,
                  *, n_main):
    tk = FLASH_TK
    qs = _flash_queries(q_ref)

    def scores(slot, r0):
        for e in range(2):
            st = _dot(km_ref[pl.ds(r0, tk), e * LANES:(e + 1) * LANES], qs[e])
            s_ref[slot, e] = st
            mx_ref[slot, e] = jnp.max(st, axis=0, keepdims=True)

    def absorb(slot, r0):
        for e in range(2):
            m = st_ref[e]
            m_new = jnp.maximum(m, mx_ref[slot, e])
            alpha = jnp.exp2(m - m_new)
            p = jnp.exp2(s_ref[slot, e] - m_new)
            st_ref[e] = m_new
            v = vm_ref[e * HEAD_DIM:(e + 1) * HEAD_DIM, pl.ds(r0, tk)]
            acc_ref[e] = alpha * acc_ref[e] + _dot(_flash_values(v), p.astype(BF16))

    scores(0, 0)
    _flash_ctx_init(qs, kc_ref, vc_ref, acc_ref, st_ref)

    def body(i2, carry):
        r0 = pl.multiple_of(i2 * (2 * tk), 2 * tk)
        scores(1, r0 + tk)
        absorb(0, r0)
        scores(0, r0 + 2 * tk)
        absorb(1, r0 + tk)
        return carry

    lax.fori_loop(0, n_main // 2 - 1, body, 0)
    r0 = (n_main - 2) * tk
    scores(1, r0 + tk)
    absorb(0, r0)
    absorb(1, r0 + tk)
    _flash_finish(acc_ref, o_ref)


def _flash_ctx_kernel(o_in_ref, q_ref, kc_ref, vc_ref, o_ref, acc_ref, st_ref):
    del o_in_ref
    _flash_ctx_init(_flash_queries(q_ref), kc_ref, vc_ref, acc_ref, st_ref)
    _flash_finish(acc_ref, o_ref)


def _flash(mq, mk, mvt, dims):
    T = mq.shape[0]
    B, S, CTX = dims
    tq = FLASH_TQ
    nq = S // tq
    ctx_blk0 = B * S // CTX
    npair = MLA_HEADS // 2
    acc = lambda n: [pltpu.VMEM((2, HEAD_DIM + DEN_ROWS, n), F32), pltpu.VMEM((2, 1, n), F32)]
    out_shape = jax.ShapeDtypeStruct((T, MLA_HEADS * MLA_V), BF16)
    kc_spec = pl.BlockSpec((CTX, 2 * LANES), lambda b, p, *_: (ctx_blk0 + b, p))
    vc_spec = pl.BlockSpec((LANES, CTX), lambda b, p, *_: (p, ctx_blk0 + b))

    o = pl.pallas_call(
        functools.partial(_flash_kernel, n_main=S // FLASH_TK),
        grid=(B, npair, nq),
        in_specs=[
            pl.BlockSpec((tq, 2 * LANES), lambda b, p, j: (b * nq + j, p)),
            pl.BlockSpec((S, 2 * LANES), lambda b, p, j: (b, p)),
            pl.BlockSpec((LANES, S), lambda b, p, j: (p, b)),
            kc_spec, vc_spec,
        ],
        out_specs=pl.BlockSpec((tq, LANES), lambda b, p, j: (b * nq + j, p)),
        out_shape=out_shape,
        scratch_shapes=[pltpu.VMEM((2, 2, FLASH_TK, tq), F32), pltpu.VMEM((2, 2, 1, tq), F32)] + acc(tq),
        compiler_params=_cparams(("arbitrary", "arbitrary", "arbitrary")),
        name="flash",
    )(mq, mk, mvt, mk, mvt)
    return pl.pallas_call(
        _flash_ctx_kernel,
        grid=(B, npair),
        in_specs=[
            pl.BlockSpec(memory_space=pl.ANY),
            pl.BlockSpec((CTX, 2 * LANES), lambda b, p: (ctx_blk0 + b, p)),
            kc_spec, vc_spec,
        ],
        out_specs=pl.BlockSpec((CTX, LANES), lambda b, p: (ctx_blk0 + b, p)),
        out_shape=out_shape,
        scratch_shapes=acc(CTX),
        input_output_aliases={0: 0},
        compiler_params=_cparams(("arbitrary", "arbitrary")),
        name="flash_ctx",
    )(o, mq, mk, mvt)


def _mlstm_dir(d, q, k, v, gates, bgate, tri, c_ref, m_ref, h_ref):
    L = q.shape[0]
    gb = gates + bgate
    ls = jnp.minimum(gb, 0.0) - jnp.log1p(jnp.exp(-jnp.abs(gb)))
    lane = lax.broadcasted_iota(jnp.int32, gb.shape, 1)
    is_f = ((lane >= 4) & (lane < 8)) | ((lane >= 12) & (lane < 16))
    xg = jnp.where(lane < 16, jnp.where(is_f, ls, gb), 0.0)
    x1 = xg.astype(BF16)
    r1 = xg - x1.astype(F32)
    x2 = r1.astype(BF16)
    x3 = (r1 - x2.astype(F32)).astype(BF16)
    cum = _dot(tri, x1) + _dot(tri, x2) + _dot(tri, x3)
    xg_t = xg.T
    cum_t = cum.T
    tot = jnp.sum(xg, axis=0, keepdims=True)

    ti = lax.broadcasted_iota(jnp.int32, (L, L), 0)
    si = lax.broadcasted_iota(jnp.int32, (L, L), 1)
    valid = (si <= ti) if d == 0 else (si >= ti)
    lane_l = lax.broadcasted_iota(jnp.int32, (L, LANES), 1)

    for p in range(ML_HEADS // 2):
        qp = q[:, p * LANES:(p + 1) * LANES]
        kp = k[:, p * LANES:(p + 1) * LANES]
        vp = v[:, p * LANES:(p + 1) * LANES]
        hs = []
        for e in range(2):
            hd = 2 * p + e
            ci = 8 * d + hd
            cf = 8 * d + 4 + hd
            in_head = (lane_l < HEAD_DIM) if e == 0 else (lane_l >= HEAD_DIM)
            ones_lane = HEAD_DIM if e == 0 else 0
            km = jnp.where(in_head, kp, jnp.zeros_like(kp))
            one_hot = jnp.where(lane_l == ones_lane, 1.0, 0.0).astype(BF16)
            va = jnp.where(in_head, vp, one_hot)

            b_col = cum[:, cf:cf + 1]
            b_row = cum_t[cf:cf + 1, :]
            li_col = xg[:, ci:ci + 1]
            li_row = xg_t[ci:ci + 1, :]
            m_prev = m_ref[d * ML_HEADS + hd:d * ML_HEADS + hd + 1, 0:1]
            cst = c_ref[d, hd]

            g = b_col + m_prev
            dm = jnp.where(valid, b_col - b_row + li_row, NEG)
            mt = jnp.maximum(g, jnp.max(dm, axis=-1, keepdims=True))
            w_inter = jnp.exp(g - mt)
            qk = _dot_nt(qp, km) * jnp.exp(dm - mt)
            haug = w_inter * _dot_nt(qp, cst.astype(BF16)) + _dot(qk.astype(BF16), va)
            den = haug[:, ones_lane:ones_lane + 1]
            hs.append(haug / jnp.maximum(jnp.abs(den), jnp.exp(-mt)))

            b_tot = tot[:, cf:cf + 1]
            a_col = b_tot - b_col + li_col
            m_new = jnp.maximum(b_tot + m_prev, jnp.max(a_col, axis=0, keepdims=True))
            decay = jnp.exp(b_tot + m_prev - m_new)
            wk = jnp.exp(a_col - m_new)
            wv = (wk * va.astype(F32)).astype(BF16)
            c_ref[d, hd] = decay * cst + _dot_tn(wv, km)
            m_ref[d * ML_HEADS + hd:d * ML_HEADS + hd + 1, :] = jnp.broadcast_to(m_new, (1, LANES))
        h_ref[:, p * LANES:(p + 1) * LANES] = _pair_select(hs[0], hs[1])


def _mlstm_kernel(qf_ref, kf_ref, vf_ref, gf_ref, qb_ref, kb_ref, vb_ref, gb_ref,
                  bgate_ref, tril_ref, triu_ref, hf_ref, hb_ref, c_ref, m_ref):
    @pl.when(pl.program_id(1) == 0)
    def _():
        c_ref[...] = jnp.zeros_like(c_ref)
        m_ref[...] = jnp.zeros_like(m_ref)

    bgate = bgate_ref[...]
    _mlstm_dir(0, qf_ref[...], kf_ref[...], vf_ref[...], gf_ref[...], bgate, tril_ref[...],
               c_ref, m_ref, hf_ref)
    _mlstm_dir(1, qb_ref[...], kb_ref[...], vb_ref[...], gb_ref[...], bgate, triu_ref[...],
               c_ref, m_ref, hb_ref)


def _mlstm(l, cq, ck, mlv, misc, prm, tabs, dims):
    T = cq.shape[0]
    B, S, CTX = dims
    L = ML_CHUNK
    n_ctx = CTX // L
    n_lat = S // L
    ctx0 = B * S // L

    def fwd(b, c):
        return jnp.where(c < n_ctx, ctx0 + b * n_ctx + c, b * n_lat + (c - n_ctx))

    def bwd(b, c):
        return jnp.where(c < n_ctx, ctx0 + b * n_ctx + (n_ctx - 1 - c), b * n_lat + (n_lat - 1 - (c - n_ctx)))

    def specs(fn):
        return [pl.BlockSpec((L, ML_W), lambda b, c: (fn(b, c), 0)),
                pl.BlockSpec((L, ML_W), lambda b, c: (fn(b, c), 0)),
                pl.BlockSpec((L, ML_W), lambda b, c: (fn(b, c), 0)),
                pl.BlockSpec((L, LANES), lambda b, c: (fn(b, c), 0))]

    return pl.pallas_call(
        _mlstm_kernel,
        grid=(B, n_ctx + n_lat),
        in_specs=specs(fwd) + specs(bwd) + [
            pl.BlockSpec((None, 1, LANES), lambda b, c: (l, 0, 0)),
            pl.BlockSpec((L, L), lambda b, c: (0, 0)),
            pl.BlockSpec((L, L), lambda b, c: (0, 0)),
        ],
        out_specs=[pl.BlockSpec((L, ML_W), lambda b, c: (fwd(b, c), 0)),
                   pl.BlockSpec((L, ML_W), lambda b, c: (bwd(b, c), 0))],
        out_shape=[jax.ShapeDtypeStruct((T, ML_W), F32)] * 2,
        scratch_shapes=[pltpu.VMEM((2, ML_HEADS, LANES, LANES), F32), pltpu.VMEM((8, LANES), F32)],
        compiler_params=_cparams(("arbitrary", "arbitrary")),
        name="mlstm",
    )(cq, ck, mlv, misc, cq, ck, mlv, misc, prm["b_gate"], tabs["tril"], tabs["triu"])


def _ffn_chunks(ff):
    step = 2 * MXU_DIM
    return [(c, min(step, ff - c)) for c in range(0, ff, step)]


def _out_kernel(x_ref, mod_ref, ona_ref, omla_ref, hf_ref, hb_ref, mlo_ref, hg_ref, bd64_ref,
                w_out_ref, g_ref, wa_ref, wb_ref, wo_ref, o_ref):
    x = x_ref[...]
    gate1 = mod_ref[2:3, :]
    shift2 = mod_ref[3:4, :]
    scale2 = mod_ref[4:5, :]
    gate2 = mod_ref[5:6, :]

    h = hf_ref[...] + hb_ref[...]
    hn = h * lax.rsqrt(_group_sumsq(h, bd64_ref[...]) * (1.0 / HEAD_DIM) + EPS) * hg_ref[...]
    ob = (hn * _sigmoid(mlo_ref[...])).astype(BF16)
    mix = (_dot(ona_ref[...], w_out_ref[0:NA_W, :])
           + _dot(ob, w_out_ref[NA_W:NA_W + ML_W, :])
           + _dot(omla_ref[...], w_out_ref[NA_W + ML_W:, :]))
    x1 = x + gate1 * mix

    h2 = ((_rms_rows(x1) * g_ref[...]) * (1.0 + scale2) + shift2).astype(BF16)
    acc = jnp.zeros_like(x1)
    for c0, w in _ffn_chunks(wa_ref.shape[1]):
        a = _dot(h2, wa_ref[:, c0:c0 + w])
        b = _dot(h2, wb_ref[:, c0:c0 + w])
        act = (a * _sigmoid(a) * b).astype(BF16)
        acc = acc + _dot(act, wo_ref[c0:c0 + w, :])
    o_ref[...] = x1 + gate2 * acc


def _out(l, xs, mods, ona, omla, hf, hb, mlo, prm, tabs, dims):
    T, D = xs.shape
    B, S, CTX = dims
    nt = T // ROW_TILE
    lat_tiles = S // ROW_TILE
    ff = prm["w_a"].shape[-1]

    def grp(i):
        return jnp.minimum(i // lat_tiles, B)

    row = lambda w: pl.BlockSpec((ROW_TILE, w), lambda i: (i, 0))
    lay = lambda shape: pl.BlockSpec((None,) + shape, lambda i: (l,) + (0,) * len(shape))
    big = lambda shape: pl.BlockSpec((None,) + shape, lambda i: (l,) + (0,) * len(shape),
                                     pipeline_mode=pl.Buffered(1))
    return pl.pallas_call(
        _out_kernel,
        grid=(nt,),
        in_specs=[
            row(D),
            pl.BlockSpec((None, None, 6, D), lambda i: (l, grp(i), 0, 0)),
            row(NA_W), row(MLA_HEADS * MLA_V), row(ML_W), row(ML_W), row(ML_W),
            lay((1, ML_W)),
            pl.BlockSpec((MXU_DIM, MXU_DIM), lambda i: (0, 0)),
            big((NA_W + ML_W + MLA_HEADS * MLA_V, D)),
            lay((1, D)),
            big((D, ff)), big((D, ff)), big((ff, D)),
        ],
        out_specs=row(D),
        out_shape=jax.ShapeDtypeStruct((T, D), F32),
        compiler_params=_cparams(("arbitrary",)),
        name="out_ffn",
    )(xs, mods, ona, omla, hf, hb, mlo, prm["head_gain"], tabs["bd64"],
      prm["w_out"], prm["g_ffn"], prm["w_a"], prm["w_b"], prm["w_fo"])


def _block_diag_ones(group):
    idx = np.arange(MXU_DIM) // group
    return jnp.asarray((idx[:, None] == idx[None, :]).astype(np.float32), dtype=BF16)


def _rope_tables(S, n_ctx_rows):
    n_freq = MLA_ROPE // 4
    inv = ROPE_BASE ** (-jnp.arange(n_freq, dtype=F32) / n_freq)
    t = jnp.arange(S, dtype=jnp.int32)
    row = (t // GRID_W).astype(F32)
    col = (t % GRID_W).astype(F32)
    ang = jnp.concatenate([row[:, None] * inv, col[:, None] * inv], axis=-1)
    ang = jnp.concatenate([ang, jnp.zeros((n_ctx_rows, MLA_ROPE // 2), F32)], axis=0)
    cos, sin = jnp.cos(ang), jnp.sin(ang)
    n = S + n_ctx_rows
    half = MLA_ROPE // 2
    ones = jnp.ones((n, MLA_NOPE), F32)
    z = lambda w: jnp.zeros((n, w), F32)
    tail = LANES - MLA_NOPE - MLA_ROPE
    cos_t = jnp.concatenate([ones, cos, cos, jnp.ones((n, tail), F32)], axis=-1)
    sa_t = jnp.concatenate([z(MLA_NOPE), -sin, z(half), z(tail)], axis=-1)
    sb_t = jnp.concatenate([z(MLA_NOPE), z(half), sin, z(tail)], axis=-1)
    return cos_t, sa_t, sb_t


def _na_bias(rpb, rows):
    depth, H = rpb.shape[:2]
    qr, kr_n = ATT_BLOCK // GRID_W, NA_KEY_ROWS // GRID_W
    nb = rows // qr
    cols = np.arange(GRID_W)
    cs = np.clip(cols - NA_WIN_C // 2, 0, GRID_W - NA_WIN_C)
    col_ok = (cols[None, :] >= cs[:, None]) & (cols[None, :] < cs[:, None] + NA_WIN_C)
    dc = np.clip(cols[None, :] - cols[:, None] + NA_WIN_C - 1, 0, 2 * NA_WIN_C - 2)
    dr_all, ok_all = [], []
    for j in (0, 1, nb - 1):
        ks = int(np.clip(j - 1, 0, nb - 3)) * qr
        r = j * qr + np.arange(qr)
        rs = np.clip(r - NA_WIN_R // 2, 0, rows - NA_WIN_R)
        kr = ks + np.arange(kr_n)
        ok_all.append((kr[None, :] >= rs[:, None]) & (kr[None, :] < rs[:, None] + NA_WIN_R))
        dr_all.append(np.clip(kr[None, :] - r[:, None] + NA_WIN_R - 1, 0, 2 * NA_WIN_R - 2))
    dr_all = np.stack(dr_all)
    ok_all = np.stack(ok_all)
    t1 = jnp.take(rpb, jnp.asarray(dc.reshape(-1)), axis=3).reshape(depth, H, 2 * NA_WIN_R - 1, GRID_W, GRID_W)
    t1 = jnp.where(jnp.asarray(col_ok), t1, NEG)
    t2 = jnp.take(t1, jnp.asarray(dr_all.reshape(-1)), axis=2).reshape(depth, H, 3, qr, kr_n, GRID_W, GRID_W)
    t2 = jnp.where(jnp.asarray(ok_all)[None, None, :, :, :, None, None], t2, NEG)
    bias = t2.transpose(0, 2, 1, 3, 5, 4, 6).reshape(depth, 3, H, ATT_BLOCK, NA_KEY_ROWS)
    none = jnp.full((depth, 1, H, ATT_BLOCK, NA_KEY_ROWS), NEG, F32)
    return jnp.concatenate([bias, none], axis=1)


def _prepare(w_in, w_uq, w_ukv, w_out, w_ffn_in, w_ffn_out, g_mix, g_ffn, b_gate, na_qk_gain,
             ml_conv_w, ml_conv_b, ml_head_gain, mla_gq, mla_gkv, mla_qk_gain):
    depth, D, _ = w_in.shape
    o_gate = C_MLO + ML_W
    o_ql = o_gate + 4 * ML_HEADS
    o_kr = o_ql + Q_LORA + KV_LORA
    zc = lambda w: jnp.zeros((depth, D, w), w_in.dtype)
    w_in_p = jnp.concatenate([
        w_in[..., :o_gate], w_in[..., o_ql:o_kr],
        w_in[..., o_gate:o_ql], zc(MISC_ROPE_LANE - 4 * ML_HEADS),
        w_in[..., o_kr:], zc(LANES - MISC_ROPE_LANE - MLA_ROPE)], axis=-1).astype(BF16)
    pad_h = lambda a, w: jnp.pad(a, [(0, 0)] * (a.ndim - 1) + [(0, LANES - w)])
    w_uq_p = pad_h(w_uq.reshape(depth, Q_LORA, MLA_HEADS, MLA_QK), MLA_QK)
    w_uq_p = w_uq_p.reshape(depth, Q_LORA, MLA_HEADS * LANES).astype(BF16)
    ukv = w_ukv.reshape(depth, KV_LORA, MLA_HEADS, MLA_NOPE + MLA_V)
    w_uk_p = pad_h(ukv[..., :MLA_NOPE], MLA_NOPE).reshape(depth, KV_LORA, MLA_HEADS * LANES).astype(BF16)
    w_uv = ukv[..., MLA_NOPE:].reshape(depth, KV_LORA, MLA_HEADS * MLA_V).transpose(0, 2, 1).astype(BF16)
    ff = w_ffn_out.shape[1]
    mla_g = pad_h(mla_qk_gain, MLA_QK)
    return {
        "w_in": w_in_p, "w_uq": w_uq_p, "w_uk": w_uk_p, "w_uv": w_uv,
        "w_out": w_out.astype(BF16),
        "w_a": w_ffn_in[..., :ff].astype(BF16), "w_b": w_ffn_in[..., ff:].astype(BF16),
        "w_fo": w_ffn_out.astype(BF16),
        "g_mix": g_mix[:, None, :], "g_ffn": g_ffn[:, None, :],
        "gq": mla_gq[:, None, :], "gkv": mla_gkv[:, None, :],
        "na_gq": jnp.tile(na_qk_gain[:, 0:1, :], (1, 1, NA_HEADS)) * NA_SCALE,
        "na_gk": jnp.tile(na_qk_gain[:, 1:2, :], (1, 1, NA_HEADS)),
        "mla_gq": jnp.tile(mla_g[:, 0:1, :], (1, 1, MLA_HEADS)) * (MLA_SCALE * LOG2E),
        "mla_gk": jnp.tile(mla_g[:, 1:2, :], (1, 1, MLA_HEADS)),
        "head_gain": ml_head_gain.reshape(depth, 1, ML_W),
        "b_gate": pad_h(b_gate, 4 * ML_HEADS)[:, None, :],
        "conv_w": jnp.pad(ml_conv_w, ((0, 0), (0, 8 - ml_conv_w.shape[1]), (0, 0))),
        "conv_b": ml_conv_b[:, None, :],
    }


def kernel(x, c, ctx, c_ctx, w_mod, b_mod, g_mix, g_ffn, w_in, b_gate, na_qk_gain, na_rpb,
           ml_conv_w, ml_conv_b, ml_head_gain, mla_gq, mla_gkv, w_uq, w_ukv, mla_qk_gain,
           w_out, w_ffn_in, w_ffn_out):
    B, S, D = x.shape
    CTX = ctx.shape[1]
    depth = w_in.shape[0]
    assert CTX == ATT_BLOCK and S % ROW_TILE == 0 and (B * CTX) % ROW_TILE == 0
    assert S % (2 * FLASH_TK) == 0 and S % FLASH_TQ == 0 and S // ATT_BLOCK >= 3 and B + 1 <= 8
    dims = (B, S, CTX)

    prm = _prepare(w_in, w_uq, w_ukv, w_out, w_ffn_in, w_ffn_out, g_mix, g_ffn, b_gate, na_qk_gain,
                   ml_conv_w, ml_conv_b, ml_head_gain, mla_gq, mla_gkv, mla_qk_gain)
    cos_t, sa_t, sb_t = _rope_tables(S, B * CTX)
    tri = np.tril(np.ones((ML_CHUNK, ML_CHUNK), np.float32))
    tabs = {
        "bd64": _block_diag_ones(HEAD_DIM), "bd128": _block_diag_ones(LANES),
        "cos": cos_t, "sa": sa_t, "sb": sb_t,
        "tril": jnp.asarray(tri, dtype=BF16), "triu": jnp.asarray(tri.T, dtype=BF16),
    }
    bias = _na_bias(na_rpb, S // GRID_W)

    cond8 = jnp.concatenate([c, c_ctx[None, :], jnp.zeros((8 - B - 1, D), F32)], axis=0)
    mods = _modulation(cond8, w_mod, b_mod)
    mods = mods.reshape(depth, 8, 6, D)

    xs = jnp.concatenate([x.reshape(B * S, D), ctx.reshape(B * CTX, D)], axis=0)
    for l in range(depth):
        naq, nak, nav, mlqk, mlv, mlo, misc, mq, mk, mv = _proj(l, xs, mods, prm, tabs, dims)
        cq, ck = _conv(l, mlqk, prm, dims)
        ona = _na(l, naq, nak, nav, bias, dims)
        omla = _flash(mq, mk, mv, dims)
        hf, hb = _mlstm(l, cq, ck, mlv, misc, prm, tabs, dims)
        xs = _out(l, xs, mods, ona, omla, hf, hb, mlo, prm, tabs, dims)
    return xs[:B * S].reshape(B, S, D)
```

```python
import functools

import numpy as np
import jax
import jax.numpy as jnp
from jax import lax
from jax.experimental import pallas as pl
from jax.experimental.pallas import tpu as pltpu

GRID_W = 64
HEAD_DIM = 64
NA_HEADS = 4
NA_WIN_R = 8
NA_WIN_C = 16
ML_HEADS = 4
ML_CHUNK = 128
MLA_HEADS = 8
MLA_NOPE = 64
MLA_ROPE = 32
MLA_V = 64
Q_LORA = 384
KV_LORA = 256
ROPE_BASE = 10000.0
EPS = 1e-6
NA_W = NA_HEADS * HEAD_DIM
ML_W = ML_HEADS * HEAD_DIM
MLA_QK = MLA_NOPE + MLA_ROPE
NA_SCALE = HEAD_DIM ** -0.5
MLA_SCALE = MLA_QK ** -0.5
LOG2E = 1.4426950408889634

LANES = 128
MXU_DIM = 256
VMEM_LIMIT = 56 * 1024 * 1024

ROW_TILE = 512
ATT_BLOCK = 256
NA_KEY_ROWS = 768
FLASH_TK = 512
FLASH_TQ = 512
NEG = -1e30
DEN_ROWS = 16

C_NAQ, C_NAK, C_NAV = 0, 256, 512
C_MLQK, C_MLV, C_MLO = 768, 1280, 1536
C_QL = 1792
C_KVL = C_QL + Q_LORA
C_MISC = C_KVL + KV_LORA
D_IN_PAD = C_MISC + LANES
MISC_ROPE_LANE = 64

BF16 = jnp.bfloat16
F32 = jnp.float32


def _cparams(sem):
    return pltpu.CompilerParams(dimension_semantics=sem, vmem_limit_bytes=VMEM_LIMIT)


def _dot(a, b):
    return jnp.dot(a, b, preferred_element_type=F32)


def _dot_nt(a, b):
    return lax.dot_general(a, b, (((1,), (1,)), ((), ())), preferred_element_type=F32)


def _dot_tn(a, b):
    return lax.dot_general(a, b, (((0,), (0,)), ((), ())), preferred_element_type=F32)


def _split2(x):
    hi = x.astype(BF16)
    lo = (x - hi.astype(F32)).astype(BF16)
    return hi, lo


def _group_sumsq(x, bd):
    x2 = x * x
    hi, lo = _split2(x2)
    outs = []
    for c in range(x.shape[1] // MXU_DIM):
        sl = slice(c * MXU_DIM, (c + 1) * MXU_DIM)
        outs.append(_dot(hi[:, sl], bd) + _dot(lo[:, sl], bd))
    return outs[0] if len(outs) == 1 else jnp.concatenate(outs, axis=-1)


def _rms_rows(x):
    return x * lax.rsqrt(jnp.mean(x * x, axis=-1, keepdims=True) + EPS)


def _sigmoid(x):
    return 1.0 / (1.0 + jnp.exp(-x))


def _mod_kernel(c_ref, w_ref, b_ref, o_ref):
    c = c_ref[...]
    a = c * _sigmoid(c)
    o_ref[...] = jnp.dot(a, w_ref[...], preferred_element_type=F32,
                         precision=lax.Precision.HIGHEST) + b_ref[...]


def _modulation(cond8, w_mod, b_mod):
    depth, d, d6 = w_mod.shape
    n = d6 // d
    return pl.pallas_call(
        _mod_kernel,
        grid=(depth, n),
        in_specs=[
            pl.BlockSpec((8, d), lambda l, j: (0, 0)),
            pl.BlockSpec((None, d, d), lambda l, j: (l, 0, j)),
            pl.BlockSpec((None, 1, d), lambda l, j: (l, 0, j)),
        ],
        out_specs=pl.BlockSpec((None, 8, d), lambda l, j: (l, 0, j)),
        out_shape=jax.ShapeDtypeStruct((depth, 8, d6), F32),
        compiler_params=_cparams(("arbitrary", "arbitrary")),
        name="modulation",
    )(cond8, w_mod, b_mod.reshape(depth, 1, d6))


def _rope128(x, cos, sa, sb):
    return x * cos + pltpu.roll(x, LANES - 16, 1) * sa + pltpu.roll(x, 16, 1) * sb


def _proj_kernel(x_ref, mod_ref, g_ref, w_in_ref, w_uq_ref, w_uk_ref, w_uv_ref,
                 gq_ref, gkv_ref, na_gq_ref, na_gk_ref, mla_gq_ref, mla_gk_ref,
                 bd64_ref, bd128_ref, cos_ref, sa_ref, sb_ref,
                 naq_ref, nak_ref, nav_ref, mlqk_ref, mlv_ref, mlo_ref, misc_ref,
                 mq_ref, mk_ref, mv_ref):
    x = x_ref[...]
    shift = mod_ref[0:1, :]
    scale = mod_ref[1:2, :]
    h = (_rms_rows(x) * g_ref[...]) * (1.0 + scale) + shift
    hb = h.astype(BF16)

    def proj(c0, width):
        return _dot(hb, w_in_ref[:, c0:c0 + width])

    bd64 = bd64_ref[...]
    bd128 = bd128_ref[...]

    pq = proj(C_NAQ, NA_W)
    naq_ref[...] = (pq * lax.rsqrt(_group_sumsq(pq, bd64) * (1.0 / HEAD_DIM) + EPS)
                    * na_gq_ref[...]).astype(BF16)
    pk = proj(C_NAK, NA_W)
    nak_ref[...] = (pk * lax.rsqrt(_group_sumsq(pk, bd64) * (1.0 / HEAD_DIM) + EPS)
                    * na_gk_ref[...]).astype(BF16)
    nav_ref[...] = proj(C_NAV, NA_W).astype(BF16)

    mlqk_ref[...] = proj(C_MLQK, 2 * ML_W)
    mlv_ref[...] = proj(C_MLV, ML_W).astype(BF16)
    mlo_ref[...] = proj(C_MLO, ML_W)
    misc = proj(C_MISC, LANES)
    misc_ref[...] = misc

    cos = cos_ref[...]
    sa = sa_ref[...]
    sb = sb_ref[...]
    ql = proj(C_QL, Q_LORA)
    qn = (_rms_rows(ql) * gq_ref[...]).astype(BF16)
    qr = _dot(qn, w_uq_ref[...])
    q = qr * lax.rsqrt(_group_sumsq(qr, bd128) * (1.0 / MLA_QK) + EPS) * mla_gq_ref[...]
    kvl = proj(C_KVL, KV_LORA)
    kvn = (_rms_rows(kvl) * gkv_ref[...]).astype(BF16)
    lane = lax.broadcasted_iota(jnp.int32, misc.shape, 1)
    krope = jnp.where((lane >= MISC_ROPE_LANE) & (lane < MISC_ROPE_LANE + MLA_ROPE), misc, 0.0)
    kr = _dot(kvn, w_uk_ref[...]) + jnp.concatenate([krope] * MLA_HEADS, axis=-1)
    k = kr * lax.rsqrt(_group_sumsq(kr, bd128) * (1.0 / MLA_QK) + EPS) * mla_gk_ref[...]
    for g in range(MLA_HEADS):
        sl = slice(g * LANES, (g + 1) * LANES)
        mq_ref[:, sl] = _rope128(q[:, sl], cos, sa, sb).astype(BF16)
        mk_ref[:, sl] = _rope128(k[:, sl], cos, sa, sb).astype(BF16)
    mv_ref[...] = _dot_nt(w_uv_ref[...], kvn).astype(BF16)


def _proj(l, xs, mods, prm, tabs, dims):
    T, D = xs.shape
    B, S, CTX = dims
    nt = T // ROW_TILE
    lat_tiles = S // ROW_TILE

    def grp(i):
        return jnp.minimum(i // lat_tiles, B)

    def tab_row(i):
        return jnp.where(i < B * lat_tiles, i % lat_tiles, lat_tiles + (i - B * lat_tiles))

    row = lambda w: pl.BlockSpec((ROW_TILE, w), lambda i: (i, 0))
    lay = lambda shape: pl.BlockSpec((None,) + shape, lambda i: (l,) + (0,) * len(shape))
    const = lambda shape: pl.BlockSpec(shape, lambda i: (0,) * len(shape))
    tab = pl.BlockSpec((ROW_TILE, LANES), lambda i: (tab_row(i), 0))
    out_w = [(NA_W, BF16), (NA_W, BF16), (NA_W, BF16), (2 * ML_W, F32), (ML_W, BF16), (ML_W, F32),
             (LANES, F32), (MLA_HEADS * LANES, BF16), (MLA_HEADS * LANES, BF16)]
    return pl.pallas_call(
        _proj_kernel,
        grid=(nt,),
        in_specs=[
            row(D),
            pl.BlockSpec((None, None, 6, D), lambda i: (l, grp(i), 0, 0)),
            lay((1, D)),
            lay((D, D_IN_PAD)), lay((Q_LORA, MLA_HEADS * LANES)), lay((KV_LORA, MLA_HEADS * LANES)),
            lay((MLA_HEADS * MLA_V, KV_LORA)),
            lay((1, Q_LORA)), lay((1, KV_LORA)), lay((1, NA_W)), lay((1, NA_W)),
            lay((1, MLA_HEADS * LANES)), lay((1, MLA_HEADS * LANES)),
            const((MXU_DIM, MXU_DIM)), const((MXU_DIM, MXU_DIM)),
            tab, tab, tab,
        ],
        out_specs=[row(w) for w, _ in out_w]
        + [pl.BlockSpec((MLA_HEADS * MLA_V, ROW_TILE), lambda i: (0, i))],
        out_shape=[jax.ShapeDtypeStruct((T, w), dt) for w, dt in out_w]
        + [jax.ShapeDtypeStruct((MLA_HEADS * MLA_V, T), BF16)],
        compiler_params=_cparams(("arbitrary",)),
        name="proj",
    )(xs, mods, prm["g_mix"], prm["w_in"], prm["w_uq"], prm["w_uk"], prm["w_uv"],
      prm["gq"], prm["gkv"], prm["na_gq"], prm["na_gk"], prm["mla_gq"], prm["mla_gk"],
      tabs["bd64"], tabs["bd128"], tabs["cos"], tabs["sa"], tabs["sb"])


def _conv_kernel(x_ref, prev_ref, next_ref, w_ref, b_ref, q_ref, k_ref, *, lat_tiles, n_lat):
    i = pl.program_id(0)
    is_lat = i < n_lat
    first = jnp.where(is_lat, (i % lat_tiles) == 0, True)
    last = jnp.where(is_lat, (i % lat_tiles) == lat_tiles - 1, True)
    x = x_ref[...]
    n = x.shape[0]
    prev_row = jnp.where(first, 0.0, prev_ref[7:8, :])
    next_row = jnp.where(last, 0.0, next_ref[0:1, :])
    ridx = lax.broadcasted_iota(jnp.int32, x.shape, 0)
    xm1 = jnp.where(ridx == 0, prev_row, pltpu.roll(x, 1, 0))
    xp1 = jnp.where(ridx == n - 1, next_row, pltpu.roll(x, n - 1, 0))
    y = b_ref[...] + xm1 * w_ref[0:1, :] + x * w_ref[1:2, :] + xp1 * w_ref[2:3, :]
    y = y * _sigmoid(y)
    q_ref[...] = y[:, :ML_W].astype(BF16)
    k_ref[...] = (y[:, ML_W:] * (HEAD_DIM ** -0.5)).astype(BF16)


def _conv(l, mlqk, prm, dims):
    T, W = mlqk.shape
    B, S, CTX = dims
    tc = CTX
    nt = T // tc
    sub = tc // 8
    nblk8 = T // 8
    kern = functools.partial(_conv_kernel, lat_tiles=S // tc, n_lat=B * S // tc)
    return pl.pallas_call(
        kern,
        grid=(nt,),
        in_specs=[
            pl.BlockSpec((tc, W), lambda i: (i, 0)),
            pl.BlockSpec((8, W), lambda i: (jnp.maximum(i * sub - 1, 0), 0)),
            pl.BlockSpec((8, W), lambda i: (jnp.minimum((i + 1) * sub, nblk8 - 1), 0)),
            pl.BlockSpec((None, 8, W), lambda i: (l, 0, 0)),
            pl.BlockSpec((None, 1, W), lambda i: (l, 0, 0)),
        ],
        out_specs=[pl.BlockSpec((tc, ML_W), lambda i: (i, 0))] * 2,
        out_shape=[jax.ShapeDtypeStruct((T, ML_W), BF16)] * 2,
        compiler_params=_cparams(("arbitrary",)),
        name="conv",
    )(mlqk, mlqk, mlqk, prm["conv_w"], prm["conv_b"])


def _pair_select(o0, o1):
    lane = lax.broadcasted_iota(jnp.int32, o0.shape, 1)
    return jnp.where(lane < HEAD_DIM, o0, o1)


def _head_q(q, e, packed):
    if packed:
        lane = lax.broadcasted_iota(jnp.int32, q.shape, 1)
        keep = (lane < HEAD_DIM) if e == 0 else (lane >= HEAD_DIM)
        return jnp.where(keep, q, jnp.zeros_like(q))
    return q[:, e * LANES:(e + 1) * LANES]


def _kcols(e, packed):
    return slice(0, LANES) if packed else slice(e * LANES, (e + 1) * LANES)


def _na_kernel(q_ref, k_ref, v_ref, kc_ref, vc_ref, bias_ref, o_ref, *, nb):
    j = pl.program_id(2)
    start = pl.multiple_of(jnp.clip(j - 1, 0, nb - 3) * ATT_BLOCK, ATT_BLOCK)
    q = q_ref[...]
    kw = k_ref[pl.ds(start, NA_KEY_ROWS), :]
    vw = v_ref[pl.ds(start, NA_KEY_ROWS), :]
    kc = kc_ref[...]
    vc = vc_ref[...]
    outs = []
    for e in range(2):
        qh = _head_q(q, e, True)
        s_loc = _dot_nt(qh, kw) + bias_ref[e]
        s_ctx = _dot_nt(qh, kc)
        m = jnp.maximum(jnp.max(s_loc, axis=-1, keepdims=True), jnp.max(s_ctx, axis=-1, keepdims=True))
        p_loc = jnp.exp(s_loc - m)
        p_ctx = jnp.exp(s_ctx - m)
        den = jnp.sum(p_loc, axis=-1, keepdims=True) + jnp.sum(p_ctx, axis=-1, keepdims=True)
        o = _dot(p_loc.astype(BF16), vw) + _dot(p_ctx.astype(BF16), vc)
        outs.append(o / den)
    o_ref[...] = _pair_select(outs[0], outs[1]).astype(o_ref.dtype)


def _na(l, naq, nak, nav, bias, dims):
    T = naq.shape[0]
    B, S, CTX = dims
    nb = S // ATT_BLOCK
    ctx_blk0 = B * S // CTX
    npair = NA_HEADS // 2

    def qrow(b, j):
        return jnp.where(j < nb, b * nb + j, ctx_blk0 + b)

    def variant(j):
        return jnp.where(j == 0, 0, jnp.where(j == nb - 1, 2, jnp.where(j == nb, 3, 1)))

    kern = functools.partial(_na_kernel, nb=nb)
    return pl.pallas_call(
        kern,
        grid=(B, npair, nb + 1),
        in_specs=[
            pl.BlockSpec((ATT_BLOCK, LANES), lambda b, p, j: (qrow(b, j), p)),
            pl.BlockSpec((S, LANES), lambda b, p, j: (b, p)),
            pl.BlockSpec((S, LANES), lambda b, p, j: (b, p)),
            pl.BlockSpec((CTX, LANES), lambda b, p, j: (ctx_blk0 + b, p)),
            pl.BlockSpec((CTX, LANES), lambda b, p, j: (ctx_blk0 + b, p)),
            pl.BlockSpec((None, None, 2, ATT_BLOCK, NA_KEY_ROWS), lambda b, p, j: (l, variant(j), p, 0, 0)),
        ],
        out_specs=pl.BlockSpec((ATT_BLOCK, LANES), lambda b, p, j: (qrow(b, j), p)),
        out_shape=jax.ShapeDtypeStruct((T, NA_W), BF16),
        compiler_params=_cparams(("arbitrary", "arbitrary", "arbitrary")),
        name="na",
    )(naq, nak, nav, nak, nav, bias)


def _flash_values(v):
    return jnp.concatenate([v, jnp.ones((DEN_ROWS, v.shape[1]), BF16)], axis=0)


def _flash_queries(q_ref):
    q = q_ref[...]
    return [q[:, e * LANES:(e + 1) * LANES].astype(F32).T.astype(BF16) for e in range(2)]


def _flash_ctx_init(qs, kc_ref, vc_ref, acc_ref, st_ref):
    for e in range(2):
        st = _dot(kc_ref[:, e * LANES:(e + 1) * LANES], qs[e])
        m = jnp.max(st, axis=0, keepdims=True)
        p = jnp.exp2(st - m)
        st_ref[e] = m
        acc_ref[e] = _dot(_flash_values(vc_ref[e * HEAD_DIM:(e + 1) * HEAD_DIM, :]), p.astype(BF16))


def _flash_finish(acc_ref, o_ref):
    ot = jnp.concatenate([acc_ref[e, :HEAD_DIM, :] / acc_ref[e, HEAD_DIM:HEAD_DIM + 1, :]
                          for e in range(2)], axis=0)
    o_ref[...] = ot.T.astype(o_ref.dtype)


def _flash_kernel(q_ref, km_ref, vm_ref, kc_ref, vc_ref, o_ref, s_ref, mx_ref, acc_ref, st_ref,
                  *, n_main):
    tk = FLASH_TK
    qs = _flash_queries(q_ref)

    def scores(slot, r0):
        for e in range(2):
            st = _dot(km_ref[pl.ds(r0, tk), e * LANES:(e + 1) * LANES], qs[e])
            s_ref[slot, e] = st
            mx_ref[slot, e] = jnp.max(st, axis=0, keepdims=True)

    def absorb(slot, r0):
        for e in range(2):
            m = st_ref[e]
            m_new = jnp.maximum(m, mx_ref[slot, e])
            alpha = jnp.exp2(m - m_new)
            p = jnp.exp2(s_ref[slot, e] - m_new)
            st_ref[e] = m_new
            v = vm_ref[e * HEAD_DIM:(e + 1) * HEAD_DIM, pl.ds(r0, tk)]
            acc_ref[e] = alpha * acc_ref[e] + _dot(_flash_values(v), p.astype(BF16))

    scores(0, 0)
    _flash_ctx_init(qs, kc_ref, vc_ref, acc_ref, st_ref)

    def body(i2, carry):
        r0 = pl.multiple_of(i2 * (2 * tk), 2 * tk)
        scores(1, r0 + tk)
        absorb(0, r0)
        scores(0, r0 + 2 * tk)
        absorb(1, r0 + tk)
        return carry

    lax.fori_loop(0, n_main // 2 - 1, body, 0)
    r0 = (n_main - 2) * tk
    scores(1, r0 + tk)
    absorb(0, r0)
    absorb(1, r0 + tk)
    _flash_finish(acc_ref, o_ref)


def _flash_ctx_kernel(o_in_ref, q_ref, kc_ref, vc_ref, o_ref, acc_ref, st_ref):
    del o_in_ref
    _flash_ctx_init(_flash_queries(q_ref), kc_ref, vc_ref, acc_ref, st_ref)
    _flash_finish(acc_ref, o_ref)


def _flash(mq, mk, mvt, dims):
    T = mq.shape[0]
    B, S, CTX = dims
    tq = FLASH_TQ
    nq = S // tq
    ctx_blk0 = B * S // CTX
    npair = MLA_HEADS // 2
    acc = lambda n: [pltpu.VMEM((2, HEAD_DIM + DEN_ROWS, n), F32), pltpu.VMEM((2, 1, n), F32)]
    out_shape = jax.ShapeDtypeStruct((T, MLA_HEADS * MLA_V), BF16)
    kc_spec = pl.BlockSpec((CTX, 2 * LANES), lambda b, p, *_: (ctx_blk0 + b, p))
    vc_spec = pl.BlockSpec((LANES, CTX), lambda b, p, *_: (p, ctx_blk0 + b))

    o = pl.pallas_call(
        functools.partial(_flash_kernel, n_main=S // FLASH_TK),
        grid=(B, npair, nq),
        in_specs=[
            pl.BlockSpec((tq, 2 * LANES), lambda b, p, j: (b * nq + j, p)),
            pl.BlockSpec((S, 2 * LANES), lambda b, p, j: (b, p)),
            pl.BlockSpec((LANES, S), lambda b, p, j: (p, b)),
            kc_spec, vc_spec,
        ],
        out_specs=pl.BlockSpec((tq, LANES), lambda b, p, j: (b * nq + j, p)),
        out_shape=out_shape,
        scratch_shapes=[pltpu.VMEM((2, 2, FLASH_TK, tq), F32), pltpu.VMEM((2, 2, 1, tq), F32)] + acc(tq),
        compiler_params=_cparams(("arbitrary", "arbitrary", "arbitrary")),
        name="flash",
    )(mq, mk, mvt, mk, mvt)
    return pl.pallas_call(
        _flash_ctx_kernel,
        grid=(B, npair),
        in_specs=[
            pl.BlockSpec(memory_space=pl.ANY),
            pl.BlockSpec((CTX, 2 * LANES), lambda b, p: (ctx_blk0 + b, p)),
            kc_spec, vc_spec,
        ],
        out_specs=pl.BlockSpec((CTX, LANES), lambda b, p: (ctx_blk0 + b, p)),
        out_shape=out_shape,
        scratch_shapes=acc(CTX),
        input_output_aliases={0: 0},
        compiler_params=_cparams(("arbitrary", "arbitrary")),
        name="flash_ctx",
    )(o, mq, mk, mvt)


def _mlstm_dir(d, q, k, v, gates, bgate, tri, c_ref, m_ref, h_ref):
    L = q.shape[0]
    gb = gates + bgate
    ls = jnp.minimum(gb, 0.0) - jnp.log1p(jnp.exp(-jnp.abs(gb)))
    lane = lax.broadcasted_iota(jnp.int32, gb.shape, 1)
    is_f = ((lane >= 4) & (lane < 8)) | ((lane >= 12) & (lane < 16))
    xg = jnp.where(lane < 16, jnp.where(is_f, ls, gb), 0.0)
    x1 = xg.astype(BF16)
    r1 = xg - x1.astype(F32)
    x2 = r1.astype(BF16)
    x3 = (r1 - x2.astype(F32)).astype(BF16)
    cum = _dot(tri, x1) + _dot(tri, x2) + _dot(tri, x3)
    xg_t = xg.T
    cum_t = cum.T
    tot = jnp.sum(xg, axis=0, keepdims=True)

    si = lax.broadcasted_iota(jnp.int32, (L, L), 0)
    ti = lax.broadcasted_iota(jnp.int32, (L, L), 1)
    valid = (si <= ti) if d == 0 else (si >= ti)
    lane_l = lax.broadcasted_iota(jnp.int32, (L, LANES), 1)
    row_l = lax.broadcasted_iota(jnp.int32, (LANES, L), 0)

    for p in range(ML_HEADS // 2):
        qp = q[:, p * LANES:(p + 1) * LANES]
        kp = k[:, p * LANES:(p + 1) * LANES]
        vt = v[:, p * LANES:(p + 1) * LANES].astype(F32).T
        hs = []
        for e in range(2):
            hd = 2 * p + e
            ci = 8 * d + hd
            cf = 8 * d + 4 + hd
            in_head = (lane_l < HEAD_DIM) if e == 0 else (lane_l >= HEAD_DIM)
            in_rows = (row_l < HEAD_DIM) if e == 0 else (row_l >= HEAD_DIM)
            ones_row = HEAD_DIM if e == 0 else 0
            km = jnp.where(in_head, kp, jnp.zeros_like(kp))
            vat = jnp.where(in_rows, vt, jnp.where(row_l == ones_row, 1.0, 0.0))

            b_row = cum_t[cf:cf + 1, :]
            li_row = xg_t[ci:ci + 1, :]
            c_col = xg[:, ci:ci + 1] - cum[:, cf:cf + 1]
            m_prev = m_ref[d * ML_HEADS + hd:d * ML_HEADS + hd + 1, 0:1]
            cst = c_ref[d, hd]

            g = b_row + m_prev
            dm = jnp.where(valid, b_row + c_col, NEG)
            mt = jnp.maximum(g, jnp.max(dm, axis=0, keepdims=True))
            w_inter = jnp.exp(g - mt)
            qk = (_dot_nt(km, qp) * jnp.exp(dm - mt)).astype(BF16)
            haug = w_inter * _dot_nt(cst.astype(BF16), qp) + _dot(vat.astype(BF16), qk)
            den = haug[ones_row:ones_row + 1, :]
            hs.append(haug / jnp.maximum(jnp.abs(den), jnp.exp(-mt)))

            b_tot = tot[:, cf:cf + 1]
            a_row = b_tot - b_row + li_row
            m_new = jnp.maximum(b_tot + m_prev, jnp.max(a_row, axis=1, keepdims=True))
            decay = jnp.exp(b_tot + m_prev - m_new)
            wv = (vat * jnp.exp(a_row - m_new)).astype(BF16)
            c_ref[d, hd] = decay * cst + _dot(wv, km)
            m_ref[d * ML_HEADS + hd:d * ML_HEADS + hd + 1, :] = jnp.broadcast_to(m_new, (1, LANES))
        h_ref[:, p * LANES:(p + 1) * LANES] = jnp.where(row_l < HEAD_DIM, hs[0], hs[1]).T


def _mlstm_kernel(qf_ref, kf_ref, vf_ref, gf_ref, qb_ref, kb_ref, vb_ref, gb_ref,
                  bgate_ref, tril_ref, triu_ref, hf_ref, hb_ref, c_ref, m_ref):
    @pl.when(pl.program_id(1) == 0)
    def _():
        c_ref[...] = jnp.zeros_like(c_ref)
        m_ref[...] = jnp.zeros_like(m_ref)

    bgate = bgate_ref[...]
    _mlstm_dir(0, qf_ref[...], kf_ref[...], vf_ref[...], gf_ref[...], bgate, tril_ref[...],
               c_ref, m_ref, hf_ref)
    _mlstm_dir(1, qb_ref[...], kb_ref[...], vb_ref[...], gb_ref[...], bgate, triu_ref[...],
               c_ref, m_ref, hb_ref)


def _mlstm(l, cq, ck, mlv, misc, prm, tabs, dims):
    T = cq.shape[0]
    B, S, CTX = dims
    L = ML_CHUNK
    n_ctx = CTX // L
    n_lat = S // L
    ctx0 = B * S // L

    def fwd(b, c):
        return jnp.where(c < n_ctx, ctx0 + b * n_ctx + c, b * n_lat + (c - n_ctx))

    def bwd(b, c):
        return jnp.where(c < n_ctx, ctx0 + b * n_ctx + (n_ctx - 1 - c), b * n_lat + (n_lat - 1 - (c - n_ctx)))

    def specs(fn):
        return [pl.BlockSpec((L, ML_W), lambda b, c: (fn(b, c), 0)),
                pl.BlockSpec((L, ML_W), lambda b, c: (fn(b, c), 0)),
                pl.BlockSpec((L, ML_W), lambda b, c: (fn(b, c), 0)),
                pl.BlockSpec((L, LANES), lambda b, c: (fn(b, c), 0))]

    return pl.pallas_call(
        _mlstm_kernel,
        grid=(B, n_ctx + n_lat),
        in_specs=specs(fwd) + specs(bwd) + [
            pl.BlockSpec((None, 1, LANES), lambda b, c: (l, 0, 0)),
            pl.BlockSpec((L, L), lambda b, c: (0, 0)),
            pl.BlockSpec((L, L), lambda b, c: (0, 0)),
        ],
        out_specs=[pl.BlockSpec((L, ML_W), lambda b, c: (fwd(b, c), 0)),
                   pl.BlockSpec((L, ML_W), lambda b, c: (bwd(b, c), 0))],
        out_shape=[jax.ShapeDtypeStruct((T, ML_W), F32)] * 2,
        scratch_shapes=[pltpu.VMEM((2, ML_HEADS, LANES, LANES), F32), pltpu.VMEM((8, LANES), F32)],
        compiler_params=_cparams(("arbitrary", "arbitrary")),
        name="mlstm",
    )(cq, ck, mlv, misc, cq, ck, mlv, misc, prm["b_gate"], tabs["tril"], tabs["triu"])


def _ffn_chunks(ff):
    step = 2 * MXU_DIM
    return [(c, min(step, ff - c)) for c in range(0, ff, step)]


def _out_kernel(x_ref, mod_ref, ona_ref, omla_ref, hf_ref, hb_ref, mlo_ref, hg_ref, bd64_ref,
                w_out_ref, g_ref, wa_ref, wb_ref, wo_ref, o_ref):
    x = x_ref[...]
    gate1 = mod_ref[2:3, :]
    shift2 = mod_ref[3:4, :]
    scale2 = mod_ref[4:5, :]
    gate2 = mod_ref[5:6, :]

    h = hf_ref[...] + hb_ref[...]
    hn = h * lax.rsqrt(_group_sumsq(h, bd64_ref[...]) * (1.0 / HEAD_DIM) + EPS) * hg_ref[...]
    ob = (hn * _sigmoid(mlo_ref[...])).astype(BF16)
    mix = (_dot(ona_ref[...], w_out_ref[0:NA_W, :])
           + _dot(ob, w_out_ref[NA_W:NA_W + ML_W, :])
           + _dot(omla_ref[...], w_out_ref[NA_W + ML_W:, :]))
    x1 = x + gate1 * mix

    h2 = ((_rms_rows(x1) * g_ref[...]) * (1.0 + scale2) + shift2).astype(BF16)
    acc = jnp.zeros_like(x1)
    for c0, w in _ffn_chunks(wa_ref.shape[1]):
        a = _dot(h2, wa_ref[:, c0:c0 + w])
        b = _dot(h2, wb_ref[:, c0:c0 + w])
        act = (a * _sigmoid(a) * b).astype(BF16)
        acc = acc + _dot(act, wo_ref[c0:c0 + w, :])
    o_ref[...] = x1 + gate2 * acc


def _out(l, xs, mods, ona, omla, hf, hb, mlo, prm, tabs, dims):
    T, D = xs.shape
    B, S, CTX = dims
    nt = T // ROW_TILE
    lat_tiles = S // ROW_TILE
    ff = prm["w_a"].shape[-1]

    def grp(i):
        return jnp.minimum(i // lat_tiles, B)

    row = lambda w: pl.BlockSpec((ROW_TILE, w), lambda i: (i, 0))
    lay = lambda shape: pl.BlockSpec((None,) + shape, lambda i: (l,) + (0,) * len(shape))
    big = lambda shape: pl.BlockSpec((None,) + shape, lambda i: (l,) + (0,) * len(shape),
                                     pipeline_mode=pl.Buffered(1))
    return pl.pallas_call(
        _out_kernel,
        grid=(nt,),
        in_specs=[
            row(D),
            pl.BlockSpec((None, None, 6, D), lambda i: (l, grp(i), 0, 0)),
            row(NA_W), row(MLA_HEADS * MLA_V), row(ML_W), row(ML_W), row(ML_W),
            lay((1, ML_W)),
            pl.BlockSpec((MXU_DIM, MXU_DIM), lambda i: (0, 0)),
            big((NA_W + ML_W + MLA_HEADS * MLA_V, D)),
            lay((1, D)),
            big((D, ff)), big((D, ff)), big((ff, D)),
        ],
        out_specs=row(D),
        out_shape=jax.ShapeDtypeStruct((T, D), F32),
        compiler_params=_cparams(("arbitrary",)),
        name="out_ffn",
    )(xs, mods, ona, omla, hf, hb, mlo, prm["head_gain"], tabs["bd64"],
      prm["w_out"], prm["g_ffn"], prm["w_a"], prm["w_b"], prm["w_fo"])


def _block_diag_ones(group):
    idx = np.arange(MXU_DIM) // group
    return jnp.asarray((idx[:, None] == idx[None, :]).astype(np.float32), dtype=BF16)


def _rope_tables(S, n_ctx_rows):
    n_freq = MLA_ROPE // 4
    inv = ROPE_BASE ** (-jnp.arange(n_freq, dtype=F32) / n_freq)
    t = jnp.arange(S, dtype=jnp.int32)
    row = (t // GRID_W).astype(F32)
    col = (t % GRID_W).astype(F32)
    ang = jnp.concatenate([row[:, None] * inv, col[:, None] * inv], axis=-1)
    ang = jnp.concatenate([ang, jnp.zeros((n_ctx_rows, MLA_ROPE // 2), F32)], axis=0)
    cos, sin = jnp.cos(ang), jnp.sin(ang)
    n = S + n_ctx_rows
    half = MLA_ROPE // 2
    ones = jnp.ones((n, MLA_NOPE), F32)
    z = lambda w: jnp.zeros((n, w), F32)
    tail = LANES - MLA_NOPE - MLA_ROPE
    cos_t = jnp.concatenate([ones, cos, cos, jnp.ones((n, tail), F32)], axis=-1)
    sa_t = jnp.concatenate([z(MLA_NOPE), -sin, z(half), z(tail)], axis=-1)
    sb_t = jnp.concatenate([z(MLA_NOPE), z(half), sin, z(tail)], axis=-1)
    return cos_t, sa_t, sb_t


def _na_bias(rpb, rows):
    depth, H = rpb.shape[:2]
    qr, kr_n = ATT_BLOCK // GRID_W, NA_KEY_ROWS // GRID_W
    nb = rows // qr
    cols = np.arange(GRID_W)
    cs = np.clip(cols - NA_WIN_C // 2, 0, GRID_W - NA_WIN_C)
    col_ok = (cols[None, :] >= cs[:, None]) & (cols[None, :] < cs[:, None] + NA_WIN_C)
    dc = np.clip(cols[None, :] - cols[:, None] + NA_WIN_C - 1, 0, 2 * NA_WIN_C - 2)
    dr_all, ok_all = [], []
    for j in (0, 1, nb - 1):
        ks = int(np.clip(j - 1, 0, nb - 3)) * qr
        r = j * qr + np.arange(qr)
        rs = np.clip(r - NA_WIN_R // 2, 0, rows - NA_WIN_R)
        kr = ks + np.arange(kr_n)
        ok_all.append((kr[None, :] >= rs[:, None]) & (kr[None, :] < rs[:, None] + NA_WIN_R))
        dr_all.append(np.clip(kr[None, :] - r[:, None] + NA_WIN_R - 1, 0, 2 * NA_WIN_R - 2))
    dr_all = np.stack(dr_all)
    ok_all = np.stack(ok_all)
    t1 = jnp.take(rpb, jnp.asarray(dc.reshape(-1)), axis=3).reshape(depth, H, 2 * NA_WIN_R - 1, GRID_W, GRID_W)
    t1 = jnp.where(jnp.asarray(col_ok), t1, NEG)
    t2 = jnp.take(t1, jnp.asarray(dr_all.reshape(-1)), axis=2).reshape(depth, H, 3, qr, kr_n, GRID_W, GRID_W)
    t2 = jnp.where(jnp.asarray(ok_all)[None, None, :, :, :, None, None], t2, NEG)
    bias = t2.transpose(0, 2, 1, 3, 5, 4, 6).reshape(depth, 3, H, ATT_BLOCK, NA_KEY_ROWS)
    none = jnp.full((depth, 1, H, ATT_BLOCK, NA_KEY_ROWS), NEG, F32)
    return jnp.concatenate([bias, none], axis=1)


def _prepare(w_in, w_uq, w_ukv, w_out, w_ffn_in, w_ffn_out, g_mix, g_ffn, b_gate, na_qk_gain,
             ml_conv_w, ml_conv_b, ml_head_gain, mla_gq, mla_gkv, mla_qk_gain):
    depth, D, _ = w_in.shape
    o_gate = C_MLO + ML_W
    o_ql = o_gate + 4 * ML_HEADS
    o_kr = o_ql + Q_LORA + KV_LORA
    zc = lambda w: jnp.zeros((depth, D, w), w_in.dtype)
    w_in_p = jnp.concatenate([
        w_in[..., :o_gate], w_in[..., o_ql:o_kr],
        w_in[..., o_gate:o_ql], zc(MISC_ROPE_LANE - 4 * ML_HEADS),
        w_in[..., o_kr:], zc(LANES - MISC_ROPE_LANE - MLA_ROPE)], axis=-1).astype(BF16)
    pad_h = lambda a, w: jnp.pad(a, [(0, 0)] * (a.ndim - 1) + [(0, LANES - w)])
    w_uq_p = pad_h(w_uq.reshape(depth, Q_LORA, MLA_HEADS, MLA_QK), MLA_QK)
    w_uq_p = w_uq_p.reshape(depth, Q_LORA, MLA_HEADS * LANES).astype(BF16)
    ukv = w_ukv.reshape(depth, KV_LORA, MLA_HEADS, MLA_NOPE + MLA_V)
    w_uk_p = pad_h(ukv[..., :MLA_NOPE], MLA_NOPE).reshape(depth, KV_LORA, MLA_HEADS * LANES).astype(BF16)
    w_uv = ukv[..., MLA_NOPE:].reshape(depth, KV_LORA, MLA_HEADS * MLA_V).transpose(0, 2, 1).astype(BF16)
    ff = w_ffn_out.shape[1]
    mla_g = pad_h(mla_qk_gain, MLA_QK)
    return {
        "w_in": w_in_p, "w_uq": w_uq_p, "w_uk": w_uk_p, "w_uv": w_uv,
        "w_out": w_out.astype(BF16),
        "w_a": w_ffn_in[..., :ff].astype(BF16), "w_b": w_ffn_in[..., ff:].astype(BF16),
        "w_fo": w_ffn_out.astype(BF16),
        "g_mix": g_mix[:, None, :], "g_ffn": g_ffn[:, None, :],
        "gq": mla_gq[:, None, :], "gkv": mla_gkv[:, None, :],
        "na_gq": jnp.tile(na_qk_gain[:, 0:1, :], (1, 1, NA_HEADS)) * NA_SCALE,
        "na_gk": jnp.tile(na_qk_gain[:, 1:2, :], (1, 1, NA_HEADS)),
        "mla_gq": jnp.tile(mla_g[:, 0:1, :], (1, 1, MLA_HEADS)) * (MLA_SCALE * LOG2E),
        "mla_gk": jnp.tile(mla_g[:, 1:2, :], (1, 1, MLA_HEADS)),
        "head_gain": ml_head_gain.reshape(depth, 1, ML_W),
        "b_gate": pad_h(b_gate, 4 * ML_HEADS)[:, None, :],
        "conv_w": jnp.pad(ml_conv_w, ((0, 0), (0, 8 - ml_conv_w.shape[1]), (0, 0))),
        "conv_b": ml_conv_b[:, None, :],
    }


def kernel(x, c, ctx, c_ctx, w_mod, b_mod, g_mix, g_ffn, w_in, b_gate, na_qk_gain, na_rpb,
           ml_conv_w, ml_conv_b, ml_head_gain, mla_gq, mla_gkv, w_uq, w_ukv, mla_qk_gain,
           w_out, w_ffn_in, w_ffn_out):
    B, S, D = x.shape
    CTX = ctx.shape[1]
    depth = w_in.shape[0]
    assert CTX == ATT_BLOCK and S % ROW_TILE == 0 and (B * CTX) % ROW_TILE == 0
    assert S % (2 * FLASH_TK) == 0 and S % FLASH_TQ == 0 and S // ATT_BLOCK >= 3 and B + 1 <= 8
    dims = (B, S, CTX)

    prm = _prepare(w_in, w_uq, w_ukv, w_out, w_ffn_in, w_ffn_out, g_mix, g_ffn, b_gate, na_qk_gain,
                   ml_conv_w, ml_conv_b, ml_head_gain, mla_gq, mla_gkv, mla_qk_gain)
    cos_t, sa_t, sb_t = _rope_tables(S, B * CTX)
    tri = np.tril(np.ones((ML_CHUNK, ML_CHUNK), np.float32))
    tabs = {
        "bd64": _block_diag_ones(HEAD_DIM), "bd128": _block_diag_ones(LANES),
        "cos": cos_t, "sa": sa_t, "sb": sb_t,
        "tril": jnp.asarray(tri, dtype=BF16), "triu": jnp.asarray(tri.T, dtype=BF16),
    }
    bias = _na_bias(na_rpb, S // GRID_W)

    cond8 = jnp.concatenate([c, c_ctx[None, :], jnp.zeros((8 - B - 1, D), F32)], axis=0)
    mods = _modulation(cond8, w_mod, b_mod)
    mods = mods.reshape(depth, 8, 6, D)

    xs = jnp.concatenate([x.reshape(B * S, D), ctx.reshape(B * CTX, D)], axis=0)
    for l in range(depth):
        naq, nak, nav, mlqk, mlv, mlo, misc, mq, mk, mv = _proj(l, xs, mods, prm, tabs, dims)
        cq, ck = _conv(l, mlqk, prm, dims)
        ona = _na(l, naq, nak, nav, bias, dims)
        omla = _flash(mq, mk, mv, dims)
        hf, hb = _mlstm(l, cq, ck, mlv, misc, prm, tabs, dims)
        xs = _out(l, xs, mods, ona, omla, hf, hb, mlo, prm, tabs, dims)
    return xs[:B * S].reshape(B, S, D)
```

```python
import functools

import numpy as np
import jax
import jax.numpy as jnp
from jax import lax
from jax.experimental import pallas as pl
from jax.experimental.pallas import tpu as pltpu

GRID_W = 64
HEAD_DIM = 64
NA_HEADS = 4
NA_WIN_R = 8
NA_WIN_C = 16
ML_HEADS = 4
ML_CHUNK = 128
MLA_HEADS = 8
MLA_NOPE = 64
MLA_ROPE = 32
MLA_V = 64
Q_LORA = 384
KV_LORA = 256
ROPE_BASE = 10000.0
EPS = 1e-6
NA_W = NA_HEADS * HEAD_DIM
ML_W = ML_HEADS * HEAD_DIM
MLA_QK = MLA_NOPE + MLA_ROPE
NA_SCALE = HEAD_DIM ** -0.5
MLA_SCALE = MLA_QK ** -0.5
LOG2E = 1.4426950408889634

LANES = 128
MXU_DIM = 256
VMEM_LIMIT = 56 * 1024 * 1024

ROW_TILE = 512
ATT_BLOCK = 256
NA_KEY_ROWS = 768
FLASH_TK = 512
FLASH_TQ = 512
FLASH_SLOTS = 2
NEG = -1e30
DEN_ROWS = 16

C_NAQ, C_NAK, C_NAV = 0, 256, 512
C_MLQK, C_MLV, C_MLO = 768, 1280, 1536
C_QL = 1792
C_KVL = C_QL + Q_LORA
C_MISC = C_KVL + KV_LORA
D_IN_PAD = C_MISC + LANES
MISC_ROPE_LANE = 64

BF16 = jnp.bfloat16
F32 = jnp.float32


def _cparams(sem):
    return pltpu.CompilerParams(dimension_semantics=sem, vmem_limit_bytes=VMEM_LIMIT)


def _dot(a, b):
    return jnp.dot(a, b, preferred_element_type=F32)


def _dot_nt(a, b):
    return lax.dot_general(a, b, (((1,), (1,)), ((), ())), preferred_element_type=F32)


def _dot_tn(a, b):
    return lax.dot_general(a, b, (((0,), (0,)), ((), ())), preferred_element_type=F32)


def _split2(x):
    hi = x.astype(BF16)
    lo = (x - hi.astype(F32)).astype(BF16)
    return hi, lo


def _group_sumsq(x, bd):
    x2 = x * x
    hi, lo = _split2(x2)
    outs = []
    for c in range(x.shape[1] // MXU_DIM):
        sl = slice(c * MXU_DIM, (c + 1) * MXU_DIM)
        outs.append(_dot(hi[:, sl], bd) + _dot(lo[:, sl], bd))
    return outs[0] if len(outs) == 1 else jnp.concatenate(outs, axis=-1)


def _rms_rows(x):
    return x * lax.rsqrt(jnp.mean(x * x, axis=-1, keepdims=True) + EPS)


def _sigmoid(x):
    return 1.0 / (1.0 + jnp.exp(-x))


def _mod_kernel(c_ref, w_ref, b_ref, o_ref):
    c = c_ref[...]
    a = c * _sigmoid(c)
    o_ref[...] = jnp.dot(a, w_ref[...], preferred_element_type=F32,
                         precision=lax.Precision.HIGHEST) + b_ref[...]


def _modulation(cond8, w_mod, b_mod):
    depth, d, d6 = w_mod.shape
    n = d6 // d
    return pl.pallas_call(
        _mod_kernel,
        grid=(depth, n),
        in_specs=[
            pl.BlockSpec((8, d), lambda l, j: (0, 0)),
            pl.BlockSpec((None, d, d), lambda l, j: (l, 0, j)),
            pl.BlockSpec((None, 1, d), lambda l, j: (l, 0, j)),
        ],
        out_specs=pl.BlockSpec((None, 8, d), lambda l, j: (l, 0, j)),
        out_shape=jax.ShapeDtypeStruct((depth, 8, d6), F32),
        compiler_params=_cparams(("arbitrary", "arbitrary")),
        name="modulation",
    )(cond8, w_mod, b_mod.reshape(depth, 1, d6))


def _rope128(x, cos, sa, sb):
    return x * cos + pltpu.roll(x, LANES - 16, 1) * sa + pltpu.roll(x, 16, 1) * sb


def _proj_kernel(x_ref, mod_ref, g_ref, w_in_ref, w_uq_ref, w_uk_ref, w_uv_ref, w_nav_ref,
                 gq_ref, gkv_ref, na_gq_ref, na_gk_ref, mla_gq_ref, mla_gk_ref,
                 bd64_ref, bd128_ref, cos_ref, sa_ref, sb_ref,
                 naq_ref, nak_ref, mlqk_ref, mlv_ref, mlo_ref, misc_ref,
                 mq_ref, mk_ref, nav_ref, mv_ref):
    x = x_ref[...]
    shift = mod_ref[0:1, :]
    scale = mod_ref[1:2, :]
    h = (_rms_rows(x) * g_ref[...]) * (1.0 + scale) + shift
    hb = h.astype(BF16)

    def proj(c0, width):
        return _dot(hb, w_in_ref[:, c0:c0 + width])

    bd64 = bd64_ref[...]
    bd128 = bd128_ref[...]

    pq = proj(C_NAQ, NA_W)
    naq_ref[...] = (pq * lax.rsqrt(_group_sumsq(pq, bd64) * (1.0 / HEAD_DIM) + EPS)
                    * na_gq_ref[...]).astype(BF16)
    pk = proj(C_NAK, NA_W)
    nak_ref[...] = (pk * lax.rsqrt(_group_sumsq(pk, bd64) * (1.0 / HEAD_DIM) + EPS)
                    * na_gk_ref[...]).astype(BF16)
    nav_ref[...] = _dot_nt(w_nav_ref[...], hb).astype(BF16)

    mlqk_ref[...] = proj(C_MLQK, 2 * ML_W)
    mlv_ref[...] = proj(C_MLV, ML_W).astype(BF16)
    mlo_ref[...] = proj(C_MLO, ML_W)
    misc = proj(C_MISC, LANES)
    misc_ref[...] = misc

    cos = cos_ref[...]
    sa = sa_ref[...]
    sb = sb_ref[...]
    ql = proj(C_QL, Q_LORA)
    qn = (_rms_rows(ql) * gq_ref[...]).astype(BF16)
    qr = _dot(qn, w_uq_ref[...])
    q = qr * lax.rsqrt(_group_sumsq(qr, bd128) * (1.0 / MLA_QK) + EPS) * mla_gq_ref[...]
    kvl = proj(C_KVL, KV_LORA)
    kvn = (_rms_rows(kvl) * gkv_ref[...]).astype(BF16)
    lane = lax.broadcasted_iota(jnp.int32, misc.shape, 1)
    krope = jnp.where((lane >= MISC_ROPE_LANE) & (lane < MISC_ROPE_LANE + MLA_ROPE), misc, 0.0)
    kr = _dot(kvn, w_uk_ref[...]) + jnp.concatenate([krope] * MLA_HEADS, axis=-1)
    k = kr * lax.rsqrt(_group_sumsq(kr, bd128) * (1.0 / MLA_QK) + EPS) * mla_gk_ref[...]
    for g in range(MLA_HEADS):
        sl = slice(g * LANES, (g + 1) * LANES)
        mq_ref[:, sl] = _rope128(q[:, sl], cos, sa, sb).astype(BF16)
        mk_ref[:, sl] = _rope128(k[:, sl], cos, sa, sb).astype(BF16)
    mv_ref[...] = _dot_nt(w_uv_ref[...], kvn).astype(BF16)


def _proj(l, xs, mods, prm, tabs, dims):
    T, D = xs.shape
    B, S, CTX = dims
    nt = T // ROW_TILE
    lat_tiles = S // ROW_TILE

    def grp(i):
        return jnp.minimum(i // lat_tiles, B)

    def tab_row(i):
        return jnp.where(i < B * lat_tiles, i % lat_tiles, lat_tiles + (i - B * lat_tiles))

    row = lambda w: pl.BlockSpec((ROW_TILE, w), lambda i: (i, 0))
    lay = lambda shape: pl.BlockSpec((None,) + shape, lambda i: (l,) + (0,) * len(shape))
    const = lambda shape: pl.BlockSpec(shape, lambda i: (0,) * len(shape))
    tab = pl.BlockSpec((ROW_TILE, LANES), lambda i: (tab_row(i), 0))
    out_w = [(NA_W, BF16), (NA_W, BF16), (2 * ML_W, F32), (ML_W, BF16), (ML_W, F32),
             (LANES, F32), (MLA_HEADS * LANES, BF16), (MLA_HEADS * LANES, BF16)]
    out_t = [NA_W, MLA_HEADS * MLA_V]
    return pl.pallas_call(
        _proj_kernel,
        grid=(nt,),
        in_specs=[
            row(D),
            pl.BlockSpec((None, None, 6, D), lambda i: (l, grp(i), 0, 0)),
            lay((1, D)),
            lay((D, D_IN_PAD)), lay((Q_LORA, MLA_HEADS * LANES)), lay((KV_LORA, MLA_HEADS * LANES)),
            lay((MLA_HEADS * MLA_V, KV_LORA)), lay((NA_W, D)),
            lay((1, Q_LORA)), lay((1, KV_LORA)), lay((1, NA_W)), lay((1, NA_W)),
            lay((1, MLA_HEADS * LANES)), lay((1, MLA_HEADS * LANES)),
            const((MXU_DIM, MXU_DIM)), const((MXU_DIM, MXU_DIM)),
            tab, tab, tab,
        ],
        out_specs=[row(w) for w, _ in out_w]
        + [pl.BlockSpec((w, ROW_TILE), lambda i: (0, i)) for w in out_t],
        out_shape=[jax.ShapeDtypeStruct((T, w), dt) for w, dt in out_w]
        + [jax.ShapeDtypeStruct((w, T), BF16) for w in out_t],
        compiler_params=_cparams(("arbitrary",)),
        name="proj",
    )(xs, mods, prm["g_mix"], prm["w_in"], prm["w_uq"], prm["w_uk"], prm["w_uv"], prm["w_nav"],
      prm["gq"], prm["gkv"], prm["na_gq"], prm["na_gk"], prm["mla_gq"], prm["mla_gk"],
      tabs["bd64"], tabs["bd128"], tabs["cos"], tabs["sa"], tabs["sb"])


def _conv_kernel(x_ref, prev_ref, next_ref, w_ref, b_ref, q_ref, k_ref, *, lat_tiles, n_lat):
    i = pl.program_id(0)
    is_lat = i < n_lat
    first = jnp.where(is_lat, (i % lat_tiles) == 0, True)
    last = jnp.where(is_lat, (i % lat_tiles) == lat_tiles - 1, True)
    x = x_ref[...]
    n = x.shape[0]
    prev_row = jnp.where(first, 0.0, prev_ref[7:8, :])
    next_row = jnp.where(last, 0.0, next_ref[0:1, :])
    ridx = lax.broadcasted_iota(jnp.int32, x.shape, 0)
    xm1 = jnp.where(ridx == 0, prev_row, pltpu.roll(x, 1, 0))
    xp1 = jnp.where(ridx == n - 1, next_row, pltpu.roll(x, n - 1, 0))
    y = b_ref[...] + xm1 * w_ref[0:1, :] + x * w_ref[1:2, :] + xp1 * w_ref[2:3, :]
    y = y * _sigmoid(y)
    q_ref[...] = y[:, :ML_W].astype(BF16)
    k_ref[...] = (y[:, ML_W:] * (HEAD_DIM ** -0.5)).astype(BF16)


def _conv(l, mlqk, prm, dims):
    T, W = mlqk.shape
    B, S, CTX = dims
    tc = CTX
    nt = T // tc
    sub = tc // 8
    nblk8 = T // 8
    kern = functools.partial(_conv_kernel, lat_tiles=S // tc, n_lat=B * S // tc)
    return pl.pallas_call(
        kern,
        grid=(nt,),
        in_specs=[
            pl.BlockSpec((tc, W), lambda i: (i, 0)),
            pl.BlockSpec((8, W), lambda i: (jnp.maximum(i * sub - 1, 0), 0)),
            pl.BlockSpec((8, W), lambda i: (jnp.minimum((i + 1) * sub, nblk8 - 1), 0)),
            pl.BlockSpec((None, 8, W), lambda i: (l, 0, 0)),
            pl.BlockSpec((None, 1, W), lambda i: (l, 0, 0)),
        ],
        out_specs=[pl.BlockSpec((tc, ML_W), lambda i: (i, 0))] * 2,
        out_shape=[jax.ShapeDtypeStruct((T, ML_W), BF16)] * 2,
        compiler_params=_cparams(("arbitrary",)),
        name="conv",
    )(mlqk, mlqk, mlqk, prm["conv_w"], prm["conv_b"])


def _pair_select(o0, o1):
    lane = lax.broadcasted_iota(jnp.int32, o0.shape, 1)
    return jnp.where(lane < HEAD_DIM, o0, o1)


def _head_q(q, e, packed):
    if packed:
        lane = lax.broadcasted_iota(jnp.int32, q.shape, 1)
        keep = (lane < HEAD_DIM) if e == 0 else (lane >= HEAD_DIM)
        return jnp.where(keep, q, jnp.zeros_like(q))
    return q[:, e * LANES:(e + 1) * LANES]


def _kcols(e, packed):
    return slice(0, LANES) if packed else slice(e * LANES, (e + 1) * LANES)


def _na_kernel(q_ref, k_ref, vt_ref, kc_ref, vct_ref, bias_ref, o_ref, *, nb):
    j = pl.program_id(1)
    start = pl.multiple_of(jnp.clip(j - 1, 0, nb - 3) * ATT_BLOCK, ATT_BLOCK)
    q = q_ref[...]
    lane = lax.broadcasted_iota(jnp.int32, (ATT_BLOCK, LANES), 1)
    scores = []
    for h in range(NA_HEADS):
        p, e = divmod(h, 2)
        cols = slice(p * LANES, (p + 1) * LANES)
        qp = q[:, cols]
        keep = (lane < HEAD_DIM) if e == 0 else (lane >= HEAD_DIM)
        qt = jnp.where(keep, qp, jnp.zeros_like(qp)).astype(F32).T.astype(BF16)
        s_loc = _dot(k_ref[pl.ds(start, NA_KEY_ROWS), cols], qt) + bias_ref[h]
        s_ctx = _dot(kc_ref[:, cols], qt)
        scores.append((s_loc, s_ctx))
    probs = []
    for s_loc, s_ctx in scores:
        m = jnp.maximum(jnp.max(s_loc, axis=0, keepdims=True), jnp.max(s_ctx, axis=0, keepdims=True))
        probs.append((jnp.exp(s_loc - m).astype(BF16), jnp.exp(s_ctx - m).astype(BF16)))
    outs = []
    for h, (p_loc, p_ctx) in enumerate(probs):
        rows = slice(h * HEAD_DIM, (h + 1) * HEAD_DIM)
        ot = (_dot(_flash_values(vt_ref[rows, pl.ds(start, NA_KEY_ROWS)]), p_loc)
              + _dot(_flash_values(vct_ref[rows, :]), p_ctx))
        outs.append(ot[:HEAD_DIM, :] / ot[HEAD_DIM:HEAD_DIM + 1, :])
    o_ref[...] = jnp.concatenate(outs, axis=0).T.astype(o_ref.dtype)


def _na(l, naq, nak, navt, bias, dims, with_ctx):
    B, S, CTX = dims
    nb = S // ATT_BLOCK
    ctx_blk0 = B * S // CTX
    n_steps = nb + 1 if with_ctx else nb
    rows_out = naq.shape[0] if with_ctx else B * S

    def qrow(b, j):
        return jnp.where(j < nb, b * nb + j, ctx_blk0 + b)

    def variant(j):
        return jnp.where(j == 0, 0, jnp.where(j == nb - 1, 2, jnp.where(j == nb, 3, 1)))

    kern = functools.partial(_na_kernel, nb=nb)
    return pl.pallas_call(
        kern,
        grid=(B, n_steps),
        in_specs=[
            pl.BlockSpec((ATT_BLOCK, NA_W), lambda b, j: (qrow(b, j), 0)),
            pl.BlockSpec((S, NA_W), lambda b, j: (b, 0)),
            pl.BlockSpec((NA_W, S), lambda b, j: (0, b)),
            pl.BlockSpec((CTX, NA_W), lambda b, j: (ctx_blk0 + b, 0)),
            pl.BlockSpec((NA_W, CTX), lambda b, j: (0, ctx_blk0 + b)),
            pl.BlockSpec((None, None, NA_HEADS, NA_KEY_ROWS, ATT_BLOCK), lambda b, j: (l, variant(j), 0, 0, 0)),
        ],
        out_specs=pl.BlockSpec((ATT_BLOCK, NA_W), lambda b, j: (qrow(b, j), 0)),
        out_shape=jax.ShapeDtypeStruct((rows_out, NA_W), BF16),
        compiler_params=_cparams(("arbitrary", "arbitrary")),
        name="na",
    )(naq, nak, navt, nak, navt, bias)


def _flash_values(v):
    return jnp.concatenate([v, jnp.ones((DEN_ROWS, v.shape[1]), BF16)], axis=0)


def _flash_queries(q_ref):
    q = q_ref[...]
    return [q[:, e * LANES:(e + 1) * LANES].astype(F32).T.astype(BF16) for e in range(2)]


def _flash_ctx_init(qs, kc_ref, vc_ref, acc_ref, st_ref):
    for e in range(2):
        st = _dot(kc_ref[:, e * LANES:(e + 1) * LANES], qs[e])
        m = jnp.max(st, axis=0, keepdims=True)
        p = jnp.exp2(st - m)
        st_ref[e] = m
        acc_ref[e] = _dot(_flash_values(vc_ref[e * HEAD_DIM:(e + 1) * HEAD_DIM, :]), p.astype(BF16))


def _flash_finish(acc_ref, o_ref):
    ot = jnp.concatenate([acc_ref[e, :HEAD_DIM, :] / acc_ref[e, HEAD_DIM:HEAD_DIM + 1, :]
                          for e in range(2)], axis=0)
    o_ref[...] = ot.T.astype(o_ref.dtype)


def _flash_kernel(q_ref, km_ref, vm_ref, kc_ref, vc_ref, o_ref, *scratch, n_main):
    ns = FLASH_SLOTS
    s_refs = [scratch[2 * i:2 * i + 2] for i in range(ns)]
    mx_refs = [scratch[2 * ns + 2 * i:2 * ns + 2 * i + 2] for i in range(ns)]
    acc_ref, st_ref = scratch[4 * ns:4 * ns + 2]
    tk = FLASH_TK
    qs = _flash_queries(q_ref)

    def scores(slot, r0):
        for e in range(2):
            st = _dot(km_ref[pl.ds(r0, tk), e * LANES:(e + 1) * LANES], qs[e])
            s_refs[slot][e][...] = st
            mx_refs[slot][e][...] = jnp.max(st, axis=0, keepdims=True)

    def absorb(slot, r0):
        staged = []
        for e in range(2):
            m = st_ref[e]
            m_new = jnp.maximum(m, mx_refs[slot][e][...])
            st_ref[e] = m_new
            staged.append((jnp.exp2(m - m_new), jnp.exp2(s_refs[slot][e][...] - m_new).astype(BF16)))
        for e, (alpha, p) in enumerate(staged):
            v = vm_ref[e * HEAD_DIM:(e + 1) * HEAD_DIM, pl.ds(r0, tk)]
            acc_ref[e] = alpha * acc_ref[e] + _dot(_flash_values(v), p)

    scores(0, 0)
    _flash_ctx_init(qs, kc_ref, vc_ref, acc_ref, st_ref)

    def advance(c, r0):
        scores((c + 1) % ns, r0 + tk)
        absorb(c % ns, r0)

    def body(i, carry):
        r0 = pl.multiple_of(i * (ns * tk), ns * tk)
        for c in range(ns):
            advance(c, r0 + c * tk)
        return carry

    n_loop = (n_main - 1) // ns
    lax.fori_loop(0, n_loop, body, 0)
    for c in range(n_loop * ns, n_main - 1):
        advance(c, c * tk)
    absorb((n_main - 1) % ns, (n_main - 1) * tk)
    _flash_finish(acc_ref, o_ref)


def _flash_ctx_kernel(o_in_ref, q_ref, kc_ref, vc_ref, o_ref, acc_ref, st_ref):
    del o_in_ref
    _flash_ctx_init(_flash_queries(q_ref), kc_ref, vc_ref, acc_ref, st_ref)
    _flash_finish(acc_ref, o_ref)


def _flash(mq, mk, mvt, dims, with_ctx):
    B, S, CTX = dims
    T = mq.shape[0] if with_ctx else B * S
    tq = FLASH_TQ
    nq = S // tq
    ctx_blk0 = B * S // CTX
    npair = MLA_HEADS // 2
    acc = lambda n: [pltpu.VMEM((2, HEAD_DIM + DEN_ROWS, n), F32), pltpu.VMEM((2, 1, n), F32)]
    out_shape = jax.ShapeDtypeStruct((T, MLA_HEADS * MLA_V), BF16)
    kc_spec = pl.BlockSpec((CTX, 2 * LANES), lambda b, p, *_: (ctx_blk0 + b, p))
    vc_spec = pl.BlockSpec((LANES, CTX), lambda b, p, *_: (p, ctx_blk0 + b))

    o = pl.pallas_call(
        functools.partial(_flash_kernel, n_main=S // FLASH_TK),
        grid=(B, npair, nq),
        in_specs=[
            pl.BlockSpec((tq, 2 * LANES), lambda b, p, j: (b * nq + j, p)),
            pl.BlockSpec((S, 2 * LANES), lambda b, p, j: (b, p)),
            pl.BlockSpec((LANES, S), lambda b, p, j: (p, b)),
            kc_spec, vc_spec,
        ],
        out_specs=pl.BlockSpec((tq, LANES), lambda b, p, j: (b * nq + j, p)),
        out_shape=out_shape,
        scratch_shapes=[pltpu.VMEM((FLASH_TK, tq), F32)] * (2 * FLASH_SLOTS)
        + [pltpu.VMEM((1, tq), F32)] * (2 * FLASH_SLOTS) + acc(tq),
        compiler_params=_cparams(("arbitrary", "arbitrary", "arbitrary")),
        name="flash",
    )(mq, mk, mvt, mk, mvt)
    if not with_ctx:
        return o
    return pl.pallas_call(
        _flash_ctx_kernel,
        grid=(B, npair),
        in_specs=[
            pl.BlockSpec(memory_space=pl.ANY),
            pl.BlockSpec((CTX, 2 * LANES), lambda b, p: (ctx_blk0 + b, p)),
            kc_spec, vc_spec,
        ],
        out_specs=pl.BlockSpec((CTX, LANES), lambda b, p: (ctx_blk0 + b, p)),
        out_shape=out_shape,
        scratch_shapes=acc(CTX),
        input_output_aliases={0: 0},
        compiler_params=_cparams(("arbitrary", "arbitrary")),
        name="flash_ctx",
    )(o, mq, mk, mvt)


def _mlstm_gates(d, gates, bgate, tri):
    gb = gates + bgate
    ls = jnp.minimum(gb, 0.0) - jnp.log1p(jnp.exp(-jnp.abs(gb)))
    lane = lax.broadcasted_iota(jnp.int32, gb.shape, 1)
    is_f = ((lane >= 4) & (lane < 8)) | ((lane >= 12) & (lane < 16))
    xg = jnp.where(lane < 16, jnp.where(is_f, ls, gb), 0.0)
    x1 = xg.astype(BF16)
    r1 = xg - x1.astype(F32)
    x2 = r1.astype(BF16)
    x3 = (r1 - x2.astype(F32)).astype(BF16)
    cum = _dot(tri, x1) + _dot(tri, x2) + _dot(tri, x3)
    return xg, cum, xg.T, cum.T, jnp.sum(xg, axis=0, keepdims=True)


def _mlstm_kernel(qf_ref, kf_ref, vf_ref, gf_ref, qb_ref, kb_ref, vb_ref, gb_ref,
                  bgate_ref, tril_ref, triu_ref, hf_ref, hb_ref, c_ref, m_ref):
    @pl.when(pl.program_id(1) == 0)
    def _():
        c_ref[...] = jnp.zeros_like(c_ref)
        m_ref[...] = jnp.zeros_like(m_ref)

    L = ML_CHUNK
    bgate = bgate_ref[...]
    q = (qf_ref[...], qb_ref[...])
    k = (kf_ref[...], kb_ref[...])
    v = (vf_ref[...], vb_ref[...])
    gat = (_mlstm_gates(0, gf_ref[...], bgate, tril_ref[...]),
           _mlstm_gates(1, gb_ref[...], bgate, triu_ref[...]))
    si = lax.broadcasted_iota(jnp.int32, (L, L), 0)
    ti = lax.broadcasted_iota(jnp.int32, (L, L), 1)
    valid = (si <= ti, si >= ti)
    lane_l = lax.broadcasted_iota(jnp.int32, (L, LANES), 1)
    row_l = lax.broadcasted_iota(jnp.int32, (LANES, L), 0)
    vts = [[v[d][:, p * LANES:(p + 1) * LANES].astype(F32).T for p in range(ML_HEADS // 2)]
           for d in range(2)]
    chains = [(d, hd) for d in range(2) for hd in range(ML_HEADS)]

    st1 = []
    for d, hd in chains:
        p, e = divmod(hd, 2)
        xg, cum, xg_t, cum_t, tot = gat[d]
        ci = 8 * d + hd
        cf = 8 * d + 4 + hd
        in_head = (lane_l < HEAD_DIM) if e == 0 else (lane_l >= HEAD_DIM)
        in_rows = (row_l < HEAD_DIM) if e == 0 else (row_l >= HEAD_DIM)
        ones_row = HEAD_DIM if e == 0 else 0
        qp = q[d][:, p * LANES:(p + 1) * LANES]
        kp = k[d][:, p * LANES:(p + 1) * LANES]
        km = jnp.where(in_head, kp, jnp.zeros_like(kp))
        vat = jnp.where(in_rows, vts[d][p], jnp.where(row_l == ones_row, 1.0, 0.0))
        b_row = cum_t[cf:cf + 1, :]
        li_row = xg_t[ci:ci + 1, :]
        c_col = xg[:, ci:ci + 1] - cum[:, cf:cf + 1]
        m_prev = m_ref[d * ML_HEADS + hd:d * ML_HEADS + hd + 1, 0:1]
        g = b_row + m_prev
        dm = jnp.where(valid[d], b_row + c_col, NEG)
        mt = jnp.maximum(g, jnp.max(dm, axis=0, keepdims=True))
        b_tot = tot[:, cf:cf + 1]
        a_row = b_tot - b_row + li_row
        m_new = jnp.maximum(b_tot + m_prev, jnp.max(a_row, axis=1, keepdims=True))
        st1.append(dict(qp=qp, km=km, vat=vat, ones_row=ones_row, g=g, dm=dm, mt=mt,
                        m_new=m_new, decay=jnp.exp(b_tot + m_prev - m_new), a_row=a_row))

    for c in st1:
        c["qk"] = (_dot_nt(c["km"], c["qp"]) * jnp.exp(c["dm"] - c["mt"])).astype(BF16)

    hs = []
    for (d, hd), c in zip(chains, st1):
        cst = c_ref[d, hd]
        haug = (jnp.exp(c["g"] - c["mt"]) * _dot_nt(cst.astype(BF16), c["qp"])
                + _dot(c["vat"].astype(BF16), c["qk"]))
        den = haug[c["ones_row"]:c["ones_row"] + 1, :]
        hs.append(haug / jnp.maximum(jnp.abs(den), jnp.exp(-c["mt"])))
        wv = (c["vat"] * jnp.exp(c["a_row"] - c["m_new"])).astype(BF16)
        c_ref[d, hd] = c["decay"] * cst + _dot(wv, c["km"])
        r = d * ML_HEADS + hd
        m_ref[r:r + 1, :] = jnp.broadcast_to(c["m_new"], (1, LANES))

    for d, h_ref in enumerate((hf_ref, hb_ref)):
        for p in range(ML_HEADS // 2):
            pair = hs[d * ML_HEADS + 2 * p:d * ML_HEADS + 2 * p + 2]
            h_ref[:, p * LANES:(p + 1) * LANES] = jnp.where(row_l < HEAD_DIM, pair[0], pair[1]).T


def _mlstm(l, cq, ck, mlv, misc, prm, tabs, dims):
    T = cq.shape[0]
    B, S, CTX = dims
    L = ML_CHUNK
    n_ctx = CTX // L
    n_lat = S // L
    ctx0 = B * S // L

    def fwd(b, c):
        return jnp.where(c < n_ctx, ctx0 + b * n_ctx + c, b * n_lat + (c - n_ctx))

    def bwd(b, c):
        return jnp.where(c < n_ctx, ctx0 + b * n_ctx + (n_ctx - 1 - c), b * n_lat + (n_lat - 1 - (c - n_ctx)))

    def specs(fn):
        return [pl.BlockSpec((L, ML_W), lambda b, c: (fn(b, c), 0)),
                pl.BlockSpec((L, ML_W), lambda b, c: (fn(b, c), 0)),
                pl.BlockSpec((L, ML_W), lambda b, c: (fn(b, c), 0)),
                pl.BlockSpec((L, LANES), lambda b, c: (fn(b, c), 0))]

    return pl.pallas_call(
        _mlstm_kernel,
        grid=(B, n_ctx + n_lat),
        in_specs=specs(fwd) + specs(bwd) + [
            pl.BlockSpec((None, 1, LANES), lambda b, c: (l, 0, 0)),
            pl.BlockSpec((L, L), lambda b, c: (0, 0)),
            pl.BlockSpec((L, L), lambda b, c: (0, 0)),
        ],
        out_specs=[pl.BlockSpec((L, ML_W), lambda b, c: (fwd(b, c), 0)),
                   pl.BlockSpec((L, ML_W), lambda b, c: (bwd(b, c), 0))],
        out_shape=[jax.ShapeDtypeStruct((T, ML_W), F32)] * 2,
        scratch_shapes=[pltpu.VMEM((2, ML_HEADS, LANES, LANES), F32), pltpu.VMEM((8, LANES), F32)],
        compiler_params=_cparams(("arbitrary", "arbitrary")),
        name="mlstm",
    )(cq, ck, mlv, misc, cq, ck, mlv, misc, prm["b_gate"], tabs["tril"], tabs["triu"])


def _ffn_chunks(ff):
    step = 2 * MXU_DIM
    return [(c, min(step, ff - c)) for c in range(0, ff, step)]


def _out_kernel(x_ref, mod_ref, ona_ref, omla_ref, hf_ref, hb_ref, mlo_ref, hg_ref, bd64_ref,
                w_out_ref, g_ref, wa_ref, wb_ref, wo_ref, o_ref):
    x = x_ref[...]
    gate1 = mod_ref[2:3, :]
    shift2 = mod_ref[3:4, :]
    scale2 = mod_ref[4:5, :]
    gate2 = mod_ref[5:6, :]

    h = hf_ref[...] + hb_ref[...]
    hn = h * lax.rsqrt(_group_sumsq(h, bd64_ref[...]) * (1.0 / HEAD_DIM) + EPS) * hg_ref[...]
    ob = (hn * _sigmoid(mlo_ref[...])).astype(BF16)
    mix = (_dot(ona_ref[...], w_out_ref[0:NA_W, :])
           + _dot(ob, w_out_ref[NA_W:NA_W + ML_W, :])
           + _dot(omla_ref[...], w_out_ref[NA_W + ML_W:, :]))
    x1 = x + gate1 * mix

    h2 = ((_rms_rows(x1) * g_ref[...]) * (1.0 + scale2) + shift2).astype(BF16)
    acc = jnp.zeros_like(x1)
    for c0, w in _ffn_chunks(wa_ref.shape[1]):
        a = _dot(h2, wa_ref[:, c0:c0 + w])
        b = _dot(h2, wb_ref[:, c0:c0 + w])
        act = (a * _sigmoid(a) * b).astype(BF16)
        acc = acc + _dot(act, wo_ref[c0:c0 + w, :])
    o_ref[...] = x1 + gate2 * acc


def _out(l, xs, mods, ona, omla, hf, hb, mlo, prm, tabs, dims, with_ctx):
    D = xs.shape[1]
    B, S, CTX = dims
    T = xs.shape[0] if with_ctx else B * S
    nt = T // ROW_TILE
    lat_tiles = S // ROW_TILE
    ff = prm["w_a"].shape[-1]

    def grp(i):
        return jnp.minimum(i // lat_tiles, B)

    row = lambda w: pl.BlockSpec((ROW_TILE, w), lambda i: (i, 0))
    lay = lambda shape: pl.BlockSpec((None,) + shape, lambda i: (l,) + (0,) * len(shape))
    big = lambda shape: pl.BlockSpec((None,) + shape, lambda i: (l,) + (0,) * len(shape),
                                     pipeline_mode=pl.Buffered(1))
    return pl.pallas_call(
        _out_kernel,
        grid=(nt,),
        in_specs=[
            row(D),
            pl.BlockSpec((None, None, 6, D), lambda i: (l, grp(i), 0, 0)),
            row(NA_W), row(MLA_HEADS * MLA_V), row(ML_W), row(ML_W), row(ML_W),
            lay((1, ML_W)),
            pl.BlockSpec((MXU_DIM, MXU_DIM), lambda i: (0, 0)),
            big((NA_W + ML_W + MLA_HEADS * MLA_V, D)),
            lay((1, D)),
            big((D, ff)), big((D, ff)), big((ff, D)),
        ],
        out_specs=row(D),
        out_shape=jax.ShapeDtypeStruct((T, D), F32),
        compiler_params=_cparams(("arbitrary",)),
        name="out_ffn",
    )(xs, mods, ona, omla, hf, hb, mlo, prm["head_gain"], tabs["bd64"],
      prm["w_out"], prm["g_ffn"], prm["w_a"], prm["w_b"], prm["w_fo"])


def _block_diag_ones(group):
    idx = np.arange(MXU_DIM) // group
    return jnp.asarray((idx[:, None] == idx[None, :]).astype(np.float32), dtype=BF16)


def _rope_tables(S, n_ctx_rows):
    n_freq = MLA_ROPE // 4
    inv = ROPE_BASE ** (-jnp.arange(n_freq, dtype=F32) / n_freq)
    t = jnp.arange(S, dtype=jnp.int32)
    row = (t // GRID_W).astype(F32)
    col = (t % GRID_W).astype(F32)
    ang = jnp.concatenate([row[:, None] * inv, col[:, None] * inv], axis=-1)
    ang = jnp.concatenate([ang, jnp.zeros((n_ctx_rows, MLA_ROPE // 2), F32)], axis=0)
    cos, sin = jnp.cos(ang), jnp.sin(ang)
    n = S + n_ctx_rows
    half = MLA_ROPE // 2
    ones = jnp.ones((n, MLA_NOPE), F32)
    z = lambda w: jnp.zeros((n, w), F32)
    tail = LANES - MLA_NOPE - MLA_ROPE
    cos_t = jnp.concatenate([ones, cos, cos, jnp.ones((n, tail), F32)], axis=-1)
    sa_t = jnp.concatenate([z(MLA_NOPE), -sin, z(half), z(tail)], axis=-1)
    sb_t = jnp.concatenate([z(MLA_NOPE), z(half), sin, z(tail)], axis=-1)
    return cos_t, sa_t, sb_t


def _na_bias(rpb, rows):
    depth, H = rpb.shape[:2]
    qr, kr_n = ATT_BLOCK // GRID_W, NA_KEY_ROWS // GRID_W
    nb = rows // qr
    cols = np.arange(GRID_W)
    cs = np.clip(cols - NA_WIN_C // 2, 0, GRID_W - NA_WIN_C)
    col_ok = (cols[None, :] >= cs[:, None]) & (cols[None, :] < cs[:, None] + NA_WIN_C)
    dc = np.clip(cols[None, :] - cols[:, None] + NA_WIN_C - 1, 0, 2 * NA_WIN_C - 2)
    dr_all, ok_all = [], []
    for j in (0, 1, nb - 1):
        ks = int(np.clip(j - 1, 0, nb - 3)) * qr
        r = j * qr + np.arange(qr)
        rs = np.clip(r - NA_WIN_R // 2, 0, rows - NA_WIN_R)
        kr = ks + np.arange(kr_n)
        ok_all.append((kr[None, :] >= rs[:, None]) & (kr[None, :] < rs[:, None] + NA_WIN_R))
        dr_all.append(np.clip(kr[None, :] - r[:, None] + NA_WIN_R - 1, 0, 2 * NA_WIN_R - 2))
    dr_all = np.stack(dr_all)
    ok_all = np.stack(ok_all)
    t1 = jnp.take(rpb, jnp.asarray(dc.reshape(-1)), axis=3).reshape(depth, H, 2 * NA_WIN_R - 1, GRID_W, GRID_W)
    t1 = jnp.where(jnp.asarray(col_ok), t1, NEG)
    t2 = jnp.take(t1, jnp.asarray(dr_all.reshape(-1)), axis=2).reshape(depth, H, 3, qr, kr_n, GRID_W, GRID_W)
    t2 = jnp.where(jnp.asarray(ok_all)[None, None, :, :, :, None, None], t2, NEG)
    bias = t2.transpose(0, 2, 1, 4, 6, 3, 5).reshape(depth, 3, H, NA_KEY_ROWS, ATT_BLOCK)
    none = jnp.full((depth, 1, H, NA_KEY_ROWS, ATT_BLOCK), NEG, F32)
    return jnp.concatenate([bias, none], axis=1)


def _prepare(w_in, w_uq, w_ukv, w_out, w_ffn_in, w_ffn_out, g_mix, g_ffn, b_gate, na_qk_gain,
             ml_conv_w, ml_conv_b, ml_head_gain, mla_gq, mla_gkv, mla_qk_gain):
    depth, D, _ = w_in.shape
    o_gate = C_MLO + ML_W
    o_ql = o_gate + 4 * ML_HEADS
    o_kr = o_ql + Q_LORA + KV_LORA
    zc = lambda w: jnp.zeros((depth, D, w), w_in.dtype)
    w_in_p = jnp.concatenate([
        w_in[..., :o_gate], w_in[..., o_ql:o_kr],
        w_in[..., o_gate:o_ql], zc(MISC_ROPE_LANE - 4 * ML_HEADS),
        w_in[..., o_kr:], zc(LANES - MISC_ROPE_LANE - MLA_ROPE)], axis=-1).astype(BF16)
    pad_h = lambda a, w: jnp.pad(a, [(0, 0)] * (a.ndim - 1) + [(0, LANES - w)])
    w_uq_p = pad_h(w_uq.reshape(depth, Q_LORA, MLA_HEADS, MLA_QK), MLA_QK)
    w_uq_p = w_uq_p.reshape(depth, Q_LORA, MLA_HEADS * LANES).astype(BF16)
    ukv = w_ukv.reshape(depth, KV_LORA, MLA_HEADS, MLA_NOPE + MLA_V)
    w_uk_p = pad_h(ukv[..., :MLA_NOPE], MLA_NOPE).reshape(depth, KV_LORA, MLA_HEADS * LANES).astype(BF16)
    w_uv = ukv[..., MLA_NOPE:].reshape(depth, KV_LORA, MLA_HEADS * MLA_V).transpose(0, 2, 1).astype(BF16)
    ff = w_ffn_out.shape[1]
    mla_g = pad_h(mla_qk_gain, MLA_QK)
    return {
        "w_in": w_in_p, "w_uq": w_uq_p, "w_uk": w_uk_p, "w_uv": w_uv,
        "w_nav": w_in[..., C_NAV:C_NAV + NA_W].transpose(0, 2, 1).astype(BF16),
        "w_out": w_out.astype(BF16),
        "w_a": w_ffn_in[..., :ff].astype(BF16), "w_b": w_ffn_in[..., ff:].astype(BF16),
        "w_fo": w_ffn_out.astype(BF16),
        "g_mix": g_mix[:, None, :], "g_ffn": g_ffn[:, None, :],
        "gq": mla_gq[:, None, :], "gkv": mla_gkv[:, None, :],
        "na_gq": jnp.tile(na_qk_gain[:, 0:1, :], (1, 1, NA_HEADS)) * NA_SCALE,
        "na_gk": jnp.tile(na_qk_gain[:, 1:2, :], (1, 1, NA_HEADS)),
        "mla_gq": jnp.tile(mla_g[:, 0:1, :], (1, 1, MLA_HEADS)) * (MLA_SCALE * LOG2E),
        "mla_gk": jnp.tile(mla_g[:, 1:2, :], (1, 1, MLA_HEADS)),
        "head_gain": ml_head_gain.reshape(depth, 1, ML_W),
        "b_gate": pad_h(b_gate, 4 * ML_HEADS)[:, None, :],
        "conv_w": jnp.pad(ml_conv_w, ((0, 0), (0, 8 - ml_conv_w.shape[1]), (0, 0))),
        "conv_b": ml_conv_b[:, None, :],
    }


def kernel(x, c, ctx, c_ctx, w_mod, b_mod, g_mix, g_ffn, w_in, b_gate, na_qk_gain, na_rpb,
           ml_conv_w, ml_conv_b, ml_head_gain, mla_gq, mla_gkv, w_uq, w_ukv, mla_qk_gain,
           w_out, w_ffn_in, w_ffn_out):
    B, S, D = x.shape
    CTX = ctx.shape[1]
    depth = w_in.shape[0]
    assert CTX == ATT_BLOCK and S % ROW_TILE == 0 and (B * CTX) % ROW_TILE == 0
    assert S % (2 * FLASH_TK) == 0 and S % FLASH_TQ == 0 and S // ATT_BLOCK >= 3 and B + 1 <= 8
    dims = (B, S, CTX)

    prm = _prepare(w_in, w_uq, w_ukv, w_out, w_ffn_in, w_ffn_out, g_mix, g_ffn, b_gate, na_qk_gain,
                   ml_conv_w, ml_conv_b, ml_head_gain, mla_gq, mla_gkv, mla_qk_gain)
    cos_t, sa_t, sb_t = _rope_tables(S, B * CTX)
    tri = np.tril(np.ones((ML_CHUNK, ML_CHUNK), np.float32))
    tabs = {
        "bd64": _block_diag_ones(HEAD_DIM), "bd128": _block_diag_ones(LANES),
        "cos": cos_t, "sa": sa_t, "sb": sb_t,
        "tril": jnp.asarray(tri, dtype=BF16), "triu": jnp.asarray(tri.T, dtype=BF16),
    }
    bias = _na_bias(na_rpb, S // GRID_W)

    cond8 = jnp.concatenate([c, c_ctx[None, :], jnp.zeros((8 - B - 1, D), F32)], axis=0)
    mods = _modulation(cond8, w_mod, b_mod)
    mods = mods.reshape(depth, 8, 6, D)

    xs = jnp.concatenate([x.reshape(B * S, D), ctx.reshape(B * CTX, D)], axis=0)
    for l in range(depth):
        with_ctx = l < depth - 1
        naq, nak, mlqk, mlv, mlo, misc, mq, mk, navt, mvt = _proj(l, xs, mods, prm, tabs, dims)
        cq, ck = _conv(l, mlqk, prm, dims)
        ona = _na(l, naq, nak, navt, bias, dims, with_ctx)
        omla = _flash(mq, mk, mvt, dims, with_ctx)
        hf, hb = _mlstm(l, cq, ck, mlv, misc, prm, tabs, dims)
        xs = _out(l, xs, mods, ona, omla, hf, hb, mlo, prm, tabs, dims, with_ctx)
    return xs.reshape(B, S, D)
```

```python
import functools

import numpy as np
import jax
import jax.numpy as jnp
from jax import lax
from jax.experimental import pallas as pl
from jax.experimental.pallas import tpu as pltpu

GRID_W = 64
HEAD_DIM = 64
NA_HEADS = 4
NA_WIN_R = 8
NA_WIN_C = 16
ML_HEADS = 4
ML_CHUNK = 128
MLA_HEADS = 8
MLA_NOPE = 64
MLA_ROPE = 32
MLA_V = 64
Q_LORA = 384
KV_LORA = 256
ROPE_BASE = 10000.0
EPS = 1e-6
NA_W = NA_HEADS * HEAD_DIM
ML_W = ML_HEADS * HEAD_DIM
MLA_QK = MLA_NOPE + MLA_ROPE
NA_SCALE = HEAD_DIM ** -0.5
MLA_SCALE = MLA_QK ** -0.5
LOG2E = 1.4426950408889634

LANES = 128
MXU_DIM = 256
VMEM_LIMIT = 56 * 1024 * 1024

ROW_TILE = 512
ATT_BLOCK = 256
NA_KEY_ROWS = 768
FLASH_TK = 512
FLASH_TQ = 1024
FLASH_SLOTS = 2
FLASH_UNROLL = 4
NEG = -1e30
DEN_ROWS = 16

C_NAQ, C_NAK, C_NAV = 0, 256, 512
C_MLQK, C_MLV, C_MLO = 768, 1280, 1536
C_QL = 1792
C_KVL = C_QL + Q_LORA
C_MISC = C_KVL + KV_LORA
D_IN_PAD = C_MISC + LANES
MISC_ROPE_LANE = 64

BF16 = jnp.bfloat16
F32 = jnp.float32


def _cparams(sem):
    return pltpu.CompilerParams(dimension_semantics=sem, vmem_limit_bytes=VMEM_LIMIT)


def _dot(a, b):
    return jnp.dot(a, b, preferred_element_type=F32)


def _dot_nt(a, b):
    return lax.dot_general(a, b, (((1,), (1,)), ((), ())), preferred_element_type=F32)


def _dot_tn(a, b):
    return lax.dot_general(a, b, (((0,), (0,)), ((), ())), preferred_element_type=F32)


def _split2(x):
    hi = x.astype(BF16)
    lo = (x - hi.astype(F32)).astype(BF16)
    return hi, lo


def _group_sumsq(x, bd):
    x2 = x * x
    hi, lo = _split2(x2)
    outs = []
    for c in range(x.shape[1] // MXU_DIM):
        sl = slice(c * MXU_DIM, (c + 1) * MXU_DIM)
        outs.append(_dot(hi[:, sl], bd) + _dot(lo[:, sl], bd))
    return outs[0] if len(outs) == 1 else jnp.concatenate(outs, axis=-1)


def _rms_rows(x):
    return x * lax.rsqrt(jnp.mean(x * x, axis=-1, keepdims=True) + EPS)


def _sigmoid(x):
    return 1.0 / (1.0 + jnp.exp(-x))


def _mod_kernel(c_ref, w_ref, b_ref, o_ref):
    c = c_ref[...]
    a = c * _sigmoid(c)
    o_ref[...] = jnp.dot(a, w_ref[...], preferred_element_type=F32,
                         precision=lax.Precision.HIGHEST) + b_ref[...]


def _modulation(cond8, w_mod, b_mod):
    depth, d, d6 = w_mod.shape
    n = d6 // d
    return pl.pallas_call(
        _mod_kernel,
        grid=(depth, n),
        in_specs=[
            pl.BlockSpec((8, d), lambda l, j: (0, 0)),
            pl.BlockSpec((None, d, d), lambda l, j: (l, 0, j)),
            pl.BlockSpec((None, 1, d), lambda l, j: (l, 0, j)),
        ],
        out_specs=pl.BlockSpec((None, 8, d), lambda l, j: (l, 0, j)),
        out_shape=jax.ShapeDtypeStruct((depth, 8, d6), F32),
        compiler_params=_cparams(("arbitrary", "arbitrary")),
        name="modulation",
    )(cond8, w_mod, b_mod.reshape(depth, 1, d6))


def _rope128(x, cos, sa, sb):
    return x * cos + pltpu.roll(x, LANES - 16, 1) * sa + pltpu.roll(x, 16, 1) * sb


def _proj_kernel(x_ref, mod_ref, g_ref, w_in_ref, w_uq_ref, w_uk_ref, w_uv_ref, w_nav_ref,
                 gq_ref, gkv_ref, na_gq_ref, na_gk_ref, mla_gq_ref, mla_gk_ref,
                 bd64_ref, bd128_ref, cos_ref, sa_ref, sb_ref,
                 naq_ref, nak_ref, mlqk_ref, mlv_ref, mlo_ref, misc_ref,
                 mq_ref, mk_ref, nav_ref, mv_ref):
    x = x_ref[...]
    shift = mod_ref[0:1, :]
    scale = mod_ref[1:2, :]
    h = (_rms_rows(x) * g_ref[...]) * (1.0 + scale) + shift
    hb = h.astype(BF16)

    def proj(c0, width):
        return _dot(hb, w_in_ref[:, c0:c0 + width])

    bd64 = bd64_ref[...]
    bd128 = bd128_ref[...]

    pq = proj(C_NAQ, NA_W)
    naq_ref[...] = (pq * lax.rsqrt(_group_sumsq(pq, bd64) * (1.0 / HEAD_DIM) + EPS)
                    * na_gq_ref[...]).astype(BF16)
    pk = proj(C_NAK, NA_W)
    nak_ref[...] = (pk * lax.rsqrt(_group_sumsq(pk, bd64) * (1.0 / HEAD_DIM) + EPS)
                    * na_gk_ref[...]).astype(BF16)
    nav_ref[...] = _dot_nt(w_nav_ref[...], hb).astype(BF16)

    mlqk_ref[...] = proj(C_MLQK, 2 * ML_W)
    mlv_ref[...] = proj(C_MLV, ML_W).astype(BF16)
    mlo_ref[...] = proj(C_MLO, ML_W)
    misc = proj(C_MISC, LANES)
    misc_ref[...] = misc

    cos = cos_ref[...]
    sa = sa_ref[...]
    sb = sb_ref[...]
    ql = proj(C_QL, Q_LORA)
    qn = (_rms_rows(ql) * gq_ref[...]).astype(BF16)
    qr = _dot(qn, w_uq_ref[...])
    q = qr * lax.rsqrt(_group_sumsq(qr, bd128) * (1.0 / MLA_QK) + EPS) * mla_gq_ref[...]
    kvl = proj(C_KVL, KV_LORA)
    kvn = (_rms_rows(kvl) * gkv_ref[...]).astype(BF16)
    lane = lax.broadcasted_iota(jnp.int32, misc.shape, 1)
    krope = jnp.where((lane >= MISC_ROPE_LANE) & (lane < MISC_ROPE_LANE + MLA_ROPE), misc, 0.0)
    kr = _dot(kvn, w_uk_ref[...]) + jnp.concatenate([krope] * MLA_HEADS, axis=-1)
    k = kr * lax.rsqrt(_group_sumsq(kr, bd128) * (1.0 / MLA_QK) + EPS) * mla_gk_ref[...]
    for g in range(MLA_HEADS):
        sl = slice(g * LANES, (g + 1) * LANES)
        mq_ref[:, sl] = _rope128(q[:, sl], cos, sa, sb).astype(BF16)
        mk_ref[:, sl] = _rope128(k[:, sl], cos, sa, sb).astype(BF16)
    mv_ref[...] = _dot_nt(w_uv_ref[...], kvn).astype(BF16)


def _proj(l, xs, mods, prm, tabs, dims):
    T, D = xs.shape
    B, S, CTX = dims
    nt = T // ROW_TILE
    lat_tiles = S // ROW_TILE

    def grp(i):
        return jnp.minimum(i // lat_tiles, B)

    def tab_row(i):
        return jnp.where(i < B * lat_tiles, i % lat_tiles, lat_tiles + (i - B * lat_tiles))

    row = lambda w: pl.BlockSpec((ROW_TILE, w), lambda i: (i, 0))
    lay = lambda shape: pl.BlockSpec((None,) + shape, lambda i: (l,) + (0,) * len(shape))
    const = lambda shape: pl.BlockSpec(shape, lambda i: (0,) * len(shape))
    tab = pl.BlockSpec((ROW_TILE, LANES), lambda i: (tab_row(i), 0))
    out_w = [(NA_W, BF16), (NA_W, BF16), (2 * ML_W, F32), (ML_W, BF16), (ML_W, F32),
             (LANES, F32), (MLA_HEADS * LANES, BF16), (MLA_HEADS * LANES, BF16)]
    out_t = [NA_W, MLA_HEADS * MLA_V]
    return pl.pallas_call(
        _proj_kernel,
        grid=(nt,),
        in_specs=[
            row(D),
            pl.BlockSpec((None, None, 6, D), lambda i: (l, grp(i), 0, 0)),
            lay((1, D)),
            lay((D, D_IN_PAD)), lay((Q_LORA, MLA_HEADS * LANES)), lay((KV_LORA, MLA_HEADS * LANES)),
            lay((MLA_HEADS * MLA_V, KV_LORA)), lay((NA_W, D)),
            lay((1, Q_LORA)), lay((1, KV_LORA)), lay((1, NA_W)), lay((1, NA_W)),
            lay((1, MLA_HEADS * LANES)), lay((1, MLA_HEADS * LANES)),
            const((MXU_DIM, MXU_DIM)), const((MXU_DIM, MXU_DIM)),
            tab, tab, tab,
        ],
        out_specs=[row(w) for w, _ in out_w]
        + [pl.BlockSpec((w, ROW_TILE), lambda i: (0, i)) for w in out_t],
        out_shape=[jax.ShapeDtypeStruct((T, w), dt) for w, dt in out_w]
        + [jax.ShapeDtypeStruct((w, T), BF16) for w in out_t],
        compiler_params=_cparams(("arbitrary",)),
        name="proj",
    )(xs, mods, prm["g_mix"], prm["w_in"], prm["w_uq"], prm["w_uk"], prm["w_uv"], prm["w_nav"],
      prm["gq"], prm["gkv"], prm["na_gq"], prm["na_gk"], prm["mla_gq"], prm["mla_gk"],
      tabs["bd64"], tabs["bd128"], tabs["cos"], tabs["sa"], tabs["sb"])


def _conv_kernel(x_ref, prev_ref, next_ref, w_ref, b_ref, q_ref, k_ref, *, lat_tiles, n_lat):
    i = pl.program_id(0)
    is_lat = i < n_lat
    first = jnp.where(is_lat, (i % lat_tiles) == 0, True)
    last = jnp.where(is_lat, (i % lat_tiles) == lat_tiles - 1, True)
    x = x_ref[...]
    n = x.shape[0]
    prev_row = jnp.where(first, 0.0, prev_ref[7:8, :])
    next_row = jnp.where(last, 0.0, next_ref[0:1, :])
    ridx = lax.broadcasted_iota(jnp.int32, x.shape, 0)
    xm1 = jnp.where(ridx == 0, prev_row, pltpu.roll(x, 1, 0))
    xp1 = jnp.where(ridx == n - 1, next_row, pltpu.roll(x, n - 1, 0))
    y = b_ref[...] + xm1 * w_ref[0:1, :] + x * w_ref[1:2, :] + xp1 * w_ref[2:3, :]
    y = y * _sigmoid(y)
    q_ref[...] = y[:, :ML_W].astype(BF16)
    k_ref[...] = (y[:, ML_W:] * (HEAD_DIM ** -0.5)).astype(BF16)


def _conv(l, mlqk, prm, dims):
    T, W = mlqk.shape
    B, S, CTX = dims
    tc = CTX
    nt = T // tc
    sub = tc // 8
    nblk8 = T // 8
    kern = functools.partial(_conv_kernel, lat_tiles=S // tc, n_lat=B * S // tc)
    return pl.pallas_call(
        kern,
        grid=(nt,),
        in_specs=[
            pl.BlockSpec((tc, W), lambda i: (i, 0)),
            pl.BlockSpec((8, W), lambda i: (jnp.maximum(i * sub - 1, 0), 0)),
            pl.BlockSpec((8, W), lambda i: (jnp.minimum((i + 1) * sub, nblk8 - 1), 0)),
            pl.BlockSpec((None, 8, W), lambda i: (l, 0, 0)),
            pl.BlockSpec((None, 1, W), lambda i: (l, 0, 0)),
        ],
        out_specs=[pl.BlockSpec((tc, ML_W), lambda i: (i, 0))] * 2,
        out_shape=[jax.ShapeDtypeStruct((T, ML_W), BF16)] * 2,
        compiler_params=_cparams(("arbitrary",)),
        name="conv",
    )(mlqk, mlqk, mlqk, prm["conv_w"], prm["conv_b"])


def _pair_select(o0, o1):
    lane = lax.broadcasted_iota(jnp.int32, o0.shape, 1)
    return jnp.where(lane < HEAD_DIM, o0, o1)


def _head_q(q, e, packed):
    if packed:
        lane = lax.broadcasted_iota(jnp.int32, q.shape, 1)
        keep = (lane < HEAD_DIM) if e == 0 else (lane >= HEAD_DIM)
        return jnp.where(keep, q, jnp.zeros_like(q))
    return q[:, e * LANES:(e + 1) * LANES]


def _kcols(e, packed):
    return slice(0, LANES) if packed else slice(e * LANES, (e + 1) * LANES)


def _na_kernel(q_ref, k_ref, vt_ref, kc_ref, vct_ref, bias_ref, o_ref, *, nb):
    j = pl.program_id(1)
    start = pl.multiple_of(jnp.clip(j - 1, 0, nb - 3) * ATT_BLOCK, ATT_BLOCK)
    q = q_ref[...]
    lane = lax.broadcasted_iota(jnp.int32, (ATT_BLOCK, LANES), 1)
    scores = []
    for h in range(NA_HEADS):
        p, e = divmod(h, 2)
        cols = slice(p * LANES, (p + 1) * LANES)
        qp = q[:, cols]
        keep = (lane < HEAD_DIM) if e == 0 else (lane >= HEAD_DIM)
        qt = jnp.where(keep, qp, jnp.zeros_like(qp)).astype(F32).T.astype(BF16)
        s_loc = _dot(k_ref[pl.ds(start, NA_KEY_ROWS), cols], qt) + bias_ref[h]
        s_ctx = _dot(kc_ref[:, cols], qt)
        scores.append((s_loc, s_ctx))
    probs = []
    for s_loc, s_ctx in scores:
        m = jnp.maximum(jnp.max(s_loc, axis=0, keepdims=True), jnp.max(s_ctx, axis=0, keepdims=True))
        probs.append((jnp.exp(s_loc - m).astype(BF16), jnp.exp(s_ctx - m).astype(BF16)))
    outs = []
    for h, (p_loc, p_ctx) in enumerate(probs):
        rows = slice(h * HEAD_DIM, (h + 1) * HEAD_DIM)
        ot = (_dot(_flash_values(vt_ref[rows, pl.ds(start, NA_KEY_ROWS)]), p_loc)
              + _dot(_flash_values(vct_ref[rows, :]), p_ctx))
        outs.append(ot[:HEAD_DIM, :] / ot[HEAD_DIM:HEAD_DIM + 1, :])
    o_ref[...] = jnp.concatenate(outs, axis=0).T.astype(o_ref.dtype)


def _na(l, naq, nak, navt, bias, dims, with_ctx):
    B, S, CTX = dims
    nb = S // ATT_BLOCK
    ctx_blk0 = B * S // CTX
    n_steps = nb + 1 if with_ctx else nb
    rows_out = naq.shape[0] if with_ctx else B * S

    def qrow(b, j):
        return jnp.where(j < nb, b * nb + j, ctx_blk0 + b)

    def variant(j):
        return jnp.where(j == 0, 0, jnp.where(j == nb - 1, 2, jnp.where(j == nb, 3, 1)))

    kern = functools.partial(_na_kernel, nb=nb)
    return pl.pallas_call(
        kern,
        grid=(B, n_steps),
        in_specs=[
            pl.BlockSpec((ATT_BLOCK, NA_W), lambda b, j: (qrow(b, j), 0)),
            pl.BlockSpec((S, NA_W), lambda b, j: (b, 0)),
            pl.BlockSpec((NA_W, S), lambda b, j: (0, b)),
            pl.BlockSpec((CTX, NA_W), lambda b, j: (ctx_blk0 + b, 0)),
            pl.BlockSpec((NA_W, CTX), lambda b, j: (0, ctx_blk0 + b)),
            pl.BlockSpec((None, None, NA_HEADS, NA_KEY_ROWS, ATT_BLOCK), lambda b, j: (l, variant(j), 0, 0, 0)),
        ],
        out_specs=pl.BlockSpec((ATT_BLOCK, NA_W), lambda b, j: (qrow(b, j), 0)),
        out_shape=jax.ShapeDtypeStruct((rows_out, NA_W), BF16),
        compiler_params=_cparams(("arbitrary", "arbitrary")),
        name="na",
    )(naq, nak, navt, nak, navt, bias)


def _flash_values(v):
    return jnp.concatenate([v, jnp.ones((DEN_ROWS, v.shape[1]), BF16)], axis=0)


def _flash_queries(q_ref):
    q = q_ref[...]
    return [q[:, e * LANES:(e + 1) * LANES].astype(F32).T.astype(BF16) for e in range(2)]


def _flash_ctx_scores(qs, kc_ref):
    return [_dot(kc_ref[:, e * LANES:(e + 1) * LANES], qs[e]) for e in range(2)]


def _flash_ctx_init(sts, vc_ref, acc_ref, st_ref):
    ps = []
    for e, st in enumerate(sts):
        m = jnp.max(st, axis=0, keepdims=True)
        st_ref[e] = m
        ps.append(jnp.exp2(st - m).astype(BF16))
    for e, p in enumerate(ps):
        acc_ref[e] = _dot(_flash_values(vc_ref[e * HEAD_DIM:(e + 1) * HEAD_DIM, :]), p)


def _flash_finish(acc_ref, o_ref):
    ot = jnp.concatenate([acc_ref[e, :HEAD_DIM, :] / acc_ref[e, HEAD_DIM:HEAD_DIM + 1, :]
                          for e in range(2)], axis=0)
    o_ref[...] = ot.T.astype(o_ref.dtype)


def _flash_kernel(q_ref, km_ref, vm_ref, kc_ref, vc_ref, o_ref, *scratch, n_main):
    ns = FLASH_SLOTS
    s_refs = [scratch[2 * i:2 * i + 2] for i in range(ns)]
    mx_refs = [scratch[2 * ns + 2 * i:2 * ns + 2 * i + 2] for i in range(ns)]
    acc_ref, st_ref = scratch[4 * ns:4 * ns + 2]
    tk = FLASH_TK
    qs = _flash_queries(q_ref)

    def scores1(slot, r0, e):
        st = _dot(km_ref[pl.ds(r0, tk), e * LANES:(e + 1) * LANES], qs[e])
        s_refs[slot][e][...] = st
        mx_refs[slot][e][...] = jnp.max(st, axis=0, keepdims=True)

    def absorb1(slot, r0, e):
        m = st_ref[e]
        m_new = jnp.maximum(m, mx_refs[slot][e][...])
        st_ref[e] = m_new
        alpha = jnp.exp2(m - m_new)
        p = jnp.exp2(s_refs[slot][e][...] - m_new).astype(BF16)
        v = vm_ref[e * HEAD_DIM:(e + 1) * HEAD_DIM, pl.ds(r0, tk)]
        acc_ref[e] = alpha * acc_ref[e] + _dot(_flash_values(v), p)

    def scores(slot, r0):
        for e in range(2):
            scores1(slot, r0, e)

    def absorb(slot, r0):
        for e in range(2):
            absorb1(slot, r0, e)

    ctx_scores = _flash_ctx_scores(qs, kc_ref)
    scores(0, 0)
    _flash_ctx_init(ctx_scores, vc_ref, acc_ref, st_ref)

    def advance(c, r0):
        scores1((c + 1) % ns, r0 + tk, 0)
        absorb1(c % ns, r0, 1)
        scores1((c + 1) % ns, r0 + tk, 1)
        absorb1(c % ns, r0, 0)

    nu = FLASH_UNROLL

    def body(i, carry):
        r0 = pl.multiple_of(i * (nu * tk), nu * tk)
        for c in range(nu):
            advance(c, r0 + c * tk)
        return carry

    n_loop = (n_main - 1) // nu
    lax.fori_loop(0, n_loop, body, 0)
    for c in range(n_loop * nu, n_main - 1):
        advance(c, c * tk)
    absorb((n_main - 1) % ns, (n_main - 1) * tk)
    _flash_finish(acc_ref, o_ref)


def _flash_ctx_kernel(o_in_ref, q_ref, kc_ref, vc_ref, o_ref, acc_ref, st_ref):
    del o_in_ref
    _flash_ctx_init(_flash_ctx_scores(_flash_queries(q_ref), kc_ref), vc_ref, acc_ref, st_ref)
    _flash_finish(acc_ref, o_ref)


def _flash(mq, mk, mvt, dims, with_ctx):
    B, S, CTX = dims
    T = mq.shape[0] if with_ctx else B * S
    tq = FLASH_TQ
    nq = S // tq
    ctx_blk0 = B * S // CTX
    npair = MLA_HEADS // 2
    acc = lambda n: [pltpu.VMEM((2, HEAD_DIM + DEN_ROWS, n), F32), pltpu.VMEM((2, 1, n), F32)]
    out_shape = jax.ShapeDtypeStruct((T, MLA_HEADS * MLA_V), BF16)
    kc_spec = pl.BlockSpec((CTX, 2 * LANES), lambda b, p, *_: (ctx_blk0 + b, p))
    vc_spec = pl.BlockSpec((LANES, CTX), lambda b, p, *_: (p, ctx_blk0 + b))

    o = pl.pallas_call(
        functools.partial(_flash_kernel, n_main=S // FLASH_TK),
        grid=(B, npair, nq),
        in_specs=[
            pl.BlockSpec((tq, 2 * LANES), lambda b, p, j: (b * nq + j, p)),
            pl.BlockSpec((S, 2 * LANES), lambda b, p, j: (b, p)),
            pl.BlockSpec((LANES, S), lambda b, p, j: (p, b)),
            kc_spec, vc_spec,
        ],
        out_specs=pl.BlockSpec((tq, LANES), lambda b, p, j: (b * nq + j, p)),
        out_shape=out_shape,
        scratch_shapes=[pltpu.VMEM((FLASH_TK, tq), F32)] * (2 * FLASH_SLOTS)
        + [pltpu.VMEM((1, tq), F32)] * (2 * FLASH_SLOTS) + acc(tq),
        compiler_params=_cparams(("arbitrary", "arbitrary", "arbitrary")),
        name="flash",
    )(mq, mk, mvt, mk, mvt)
    if not with_ctx:
        return o
    return pl.pallas_call(
        _flash_ctx_kernel,
        grid=(B, npair),
        in_specs=[
            pl.BlockSpec(memory_space=pl.ANY),
            pl.BlockSpec((CTX, 2 * LANES), lambda b, p: (ctx_blk0 + b, p)),
            kc_spec, vc_spec,
        ],
        out_specs=pl.BlockSpec((CTX, LANES), lambda b, p: (ctx_blk0 + b, p)),
        out_shape=out_shape,
        scratch_shapes=acc(CTX),
        input_output_aliases={0: 0},
        compiler_params=_cparams(("arbitrary", "arbitrary")),
        name="flash_ctx",
    )(o, mq, mk, mvt)


def _mlstm_gates(d, gates, bgate, tri):
    gb = gates + bgate
    ls = jnp.minimum(gb, 0.0) - jnp.log1p(jnp.exp(-jnp.abs(gb)))
    lane = lax.broadcasted_iota(jnp.int32, gb.shape, 1)
    is_f = ((lane >= 4) & (lane < 8)) | ((lane >= 12) & (lane < 16))
    xg = jnp.where(lane < 16, jnp.where(is_f, ls, gb), 0.0)
    x1 = xg.astype(BF16)
    r1 = xg - x1.astype(F32)
    x2 = r1.astype(BF16)
    x3 = (r1 - x2.astype(F32)).astype(BF16)
    cum = _dot(tri, x1) + _dot(tri, x2) + _dot(tri, x3)
    return xg, cum, xg.T, cum.T, jnp.sum(xg, axis=0, keepdims=True)


def _mlstm_kernel(qf_ref, kf_ref, vf_ref, gf_ref, qb_ref, kb_ref, vb_ref, gb_ref,
                  bgate_ref, tril_ref, triu_ref, hf_ref, hb_ref, c_ref, m_ref):
    @pl.when(pl.program_id(1) == 0)
    def _():
        c_ref[...] = jnp.zeros_like(c_ref)
        m_ref[...] = jnp.zeros_like(m_ref)

    L = ML_CHUNK
    bgate = bgate_ref[...]
    q = (qf_ref[...], qb_ref[...])
    k = (kf_ref[...], kb_ref[...])
    v = (vf_ref[...], vb_ref[...])
    gat = (_mlstm_gates(0, gf_ref[...], bgate, tril_ref[...]),
           _mlstm_gates(1, gb_ref[...], bgate, triu_ref[...]))
    si = lax.broadcasted_iota(jnp.int32, (L, L), 0)
    ti = lax.broadcasted_iota(jnp.int32, (L, L), 1)
    valid = (si <= ti, si >= ti)
    lane_l = lax.broadcasted_iota(jnp.int32, (L, LANES), 1)
    row_l = lax.broadcasted_iota(jnp.int32, (LANES, L), 0)
    vts = [[v[d][:, p * LANES:(p + 1) * LANES].astype(F32).T for p in range(ML_HEADS // 2)]
           for d in range(2)]
    chains = [(d, hd) for d in range(2) for hd in range(ML_HEADS)]

    st1 = []
    for d, hd in chains:
        p, e = divmod(hd, 2)
        xg, cum, xg_t, cum_t, tot = gat[d]
        ci = 8 * d + hd
        cf = 8 * d + 4 + hd
        in_head = (lane_l < HEAD_DIM) if e == 0 else (lane_l >= HEAD_DIM)
        in_rows = (row_l < HEAD_DIM) if e == 0 else (row_l >= HEAD_DIM)
        ones_row = HEAD_DIM if e == 0 else 0
        qp = q[d][:, p * LANES:(p + 1) * LANES]
        kp = k[d][:, p * LANES:(p + 1) * LANES]
        km = jnp.where(in_head, kp, jnp.zeros_like(kp))
        vat = jnp.where(in_rows, vts[d][p], jnp.where(row_l == ones_row, 1.0, 0.0))
        b_row = cum_t[cf:cf + 1, :]
        li_row = xg_t[ci:ci + 1, :]
        c_col = xg[:, ci:ci + 1] - cum[:, cf:cf + 1]
        m_prev = m_ref[d * ML_HEADS + hd:d * ML_HEADS + hd + 1, 0:1]
        g = b_row + m_prev
        dm = jnp.where(valid[d], b_row + c_col, NEG)
        mt = jnp.maximum(g, jnp.max(dm, axis=0, keepdims=True))
        b_tot = tot[:, cf:cf + 1]
        a_row = b_tot - b_row + li_row
        m_new = jnp.maximum(b_tot + m_prev, jnp.max(a_row, axis=1, keepdims=True))
        st1.append(dict(qp=qp, km=km, vat=vat, ones_row=ones_row, g=g, dm=dm, mt=mt,
                        m_new=m_new, decay=jnp.exp(b_tot + m_prev - m_new), a_row=a_row))

    for c in st1:
        c["qk"] = (_dot_nt(c["km"], c["qp"]) * jnp.exp(c["dm"] - c["mt"])).astype(BF16)

    hs = []
    for (d, hd), c in zip(chains, st1):
        cst = c_ref[d, hd]
        haug = (jnp.exp(c["g"] - c["mt"]) * _dot_nt(cst.astype(BF16), c["qp"])
                + _dot(c["vat"].astype(BF16), c["qk"]))
        den = haug[c["ones_row"]:c["ones_row"] + 1, :]
        hs.append(haug / jnp.maximum(jnp.abs(den), jnp.exp(-c["mt"])))
        wv = (c["vat"] * jnp.exp(c["a_row"] - c["m_new"])).astype(BF16)
        c_ref[d, hd] = c["decay"] * cst + _dot(wv, c["km"])
        r = d * ML_HEADS + hd
        m_ref[r:r + 1, :] = jnp.broadcast_to(c["m_new"], (1, LANES))

    for d, h_ref in enumerate((hf_ref, hb_ref)):
        for p in range(ML_HEADS // 2):
            pair = hs[d * ML_HEADS + 2 * p:d * ML_HEADS + 2 * p + 2]
            h_ref[:, p * LANES:(p + 1) * LANES] = jnp.where(row_l < HEAD_DIM, pair[0], pair[1]).T


def _mlstm(l, cq, ck, mlv, misc, prm, tabs, dims):
    T = cq.shape[0]
    B, S, CTX = dims
    L = ML_CHUNK
    n_ctx = CTX // L
    n_lat = S // L
    ctx0 = B * S // L

    def fwd(b, c):
        return jnp.where(c < n_ctx, ctx0 + b * n_ctx + c, b * n_lat + (c - n_ctx))

    def bwd(b, c):
        return jnp.where(c < n_ctx, ctx0 + b * n_ctx + (n_ctx - 1 - c), b * n_lat + (n_lat - 1 - (c - n_ctx)))

    def specs(fn):
        return [pl.BlockSpec((L, ML_W), lambda b, c: (fn(b, c), 0)),
                pl.BlockSpec((L, ML_W), lambda b, c: (fn(b, c), 0)),
                pl.BlockSpec((L, ML_W), lambda b, c: (fn(b, c), 0)),
                pl.BlockSpec((L, LANES), lambda b, c: (fn(b, c), 0))]

    return pl.pallas_call(
        _mlstm_kernel,
        grid=(B, n_ctx + n_lat),
        in_specs=specs(fwd) + specs(bwd) + [
            pl.BlockSpec((None, 1, LANES), lambda b, c: (l, 0, 0)),
            pl.BlockSpec((L, L), lambda b, c: (0, 0)),
            pl.BlockSpec((L, L), lambda b, c: (0, 0)),
        ],
        out_specs=[pl.BlockSpec((L, ML_W), lambda b, c: (fwd(b, c), 0)),
                   pl.BlockSpec((L, ML_W), lambda b, c: (bwd(b, c), 0))],
        out_shape=[jax.ShapeDtypeStruct((T, ML_W), F32)] * 2,
        scratch_shapes=[pltpu.VMEM((2, ML_HEADS, LANES, LANES), F32), pltpu.VMEM((8, LANES), F32)],
        compiler_params=_cparams(("arbitrary", "arbitrary")),
        name="mlstm",
    )(cq, ck, mlv, misc, cq, ck, mlv, misc, prm["b_gate"], tabs["tril"], tabs["triu"])


def _ffn_chunks(ff):
    step = 2 * MXU_DIM
    return [(c, min(step, ff - c)) for c in range(0, ff, step)]


def _out_kernel(x_ref, mod_ref, ona_ref, omla_ref, hf_ref, hb_ref, mlo_ref, hg_ref, bd64_ref,
                w_out_ref, g_ref, wa_ref, wb_ref, wo_ref, o_ref):
    x = x_ref[...]
    gate1 = mod_ref[2:3, :]
    shift2 = mod_ref[3:4, :]
    scale2 = mod_ref[4:5, :]
    gate2 = mod_ref[5:6, :]

    h = hf_ref[...] + hb_ref[...]
    hn = h * lax.rsqrt(_group_sumsq(h, bd64_ref[...]) * (1.0 / HEAD_DIM) + EPS) * hg_ref[...]
    ob = (hn * _sigmoid(mlo_ref[...])).astype(BF16)
    mix = (_dot(ona_ref[...], w_out_ref[0:NA_W, :])
           + _dot(ob, w_out_ref[NA_W:NA_W + ML_W, :])
           + _dot(omla_ref[...], w_out_ref[NA_W + ML_W:, :]))
    x1 = x + gate1 * mix

    h2 = ((_rms_rows(x1) * g_ref[...]) * (1.0 + scale2) + shift2).astype(BF16)
    acc = jnp.zeros_like(x1)
    for c0, w in _ffn_chunks(wa_ref.shape[1]):
        a = _dot(h2, wa_ref[:, c0:c0 + w])
        b = _dot(h2, wb_ref[:, c0:c0 + w])
        act = (a * _sigmoid(a) * b).astype(BF16)
        acc = acc + _dot(act, wo_ref[c0:c0 + w, :])
    o_ref[...] = x1 + gate2 * acc


def _out(l, xs, mods, ona, omla, hf, hb, mlo, prm, tabs, dims, with_ctx):
    D = xs.shape[1]
    B, S, CTX = dims
    T = xs.shape[0] if with_ctx else B * S
    nt = T // ROW_TILE
    lat_tiles = S // ROW_TILE
    ff = prm["w_a"].shape[-1]

    def grp(i):
        return jnp.minimum(i // lat_tiles, B)

    row = lambda w: pl.BlockSpec((ROW_TILE, w), lambda i: (i, 0))
    lay = lambda shape: pl.BlockSpec((None,) + shape, lambda i: (l,) + (0,) * len(shape))
    big = lambda shape: pl.BlockSpec((None,) + shape, lambda i: (l,) + (0,) * len(shape),
                                     pipeline_mode=pl.Buffered(1))
    return pl.pallas_call(
        _out_kernel,
        grid=(nt,),
        in_specs=[
            row(D),
            pl.BlockSpec((None, None, 6, D), lambda i: (l, grp(i), 0, 0)),
            row(NA_W), row(MLA_HEADS * MLA_V), row(ML_W), row(ML_W), row(ML_W),
            lay((1, ML_W)),
            pl.BlockSpec((MXU_DIM, MXU_DIM), lambda i: (0, 0)),
            big((NA_W + ML_W + MLA_HEADS * MLA_V, D)),
            lay((1, D)),
            big((D, ff)), big((D, ff)), big((ff, D)),
        ],
        out_specs=row(D),
        out_shape=jax.ShapeDtypeStruct((T, D), F32),
        compiler_params=_cparams(("arbitrary",)),
        name="out_ffn",
    )(xs, mods, ona, omla, hf, hb, mlo, prm["head_gain"], tabs["bd64"],
      prm["w_out"], prm["g_ffn"], prm["w_a"], prm["w_b"], prm["w_fo"])


def _block_diag_ones(group):
    idx = np.arange(MXU_DIM) // group
    return jnp.asarray((idx[:, None] == idx[None, :]).astype(np.float32), dtype=BF16)


def _rope_tables(S, n_ctx_rows):
    n_freq = MLA_ROPE // 4
    inv = ROPE_BASE ** (-jnp.arange(n_freq, dtype=F32) / n_freq)
    t = jnp.arange(S, dtype=jnp.int32)
    row = (t // GRID_W).astype(F32)
    col = (t % GRID_W).astype(F32)
    ang = jnp.concatenate([row[:, None] * inv, col[:, None] * inv], axis=-1)
    ang = jnp.concatenate([ang, jnp.zeros((n_ctx_rows, MLA_ROPE // 2), F32)], axis=0)
    cos, sin = jnp.cos(ang), jnp.sin(ang)
    n = S + n_ctx_rows
    half = MLA_ROPE // 2
    ones = jnp.ones((n, MLA_NOPE), F32)
    z = lambda w: jnp.zeros((n, w), F32)
    tail = LANES - MLA_NOPE - MLA_ROPE
    cos_t = jnp.concatenate([ones, cos, cos, jnp.ones((n, tail), F32)], axis=-1)
    sa_t = jnp.concatenate([z(MLA_NOPE), -sin, z(half), z(tail)], axis=-1)
    sb_t = jnp.concatenate([z(MLA_NOPE), z(half), sin, z(tail)], axis=-1)
    return cos_t, sa_t, sb_t


def _na_bias(rpb, rows):
    depth, H = rpb.shape[:2]
    qr, kr_n = ATT_BLOCK // GRID_W, NA_KEY_ROWS // GRID_W
    nb = rows // qr
    cols = np.arange(GRID_W)
    cs = np.clip(cols - NA_WIN_C // 2, 0, GRID_W - NA_WIN_C)
    col_ok = (cols[None, :] >= cs[:, None]) & (cols[None, :] < cs[:, None] + NA_WIN_C)
    dc = np.clip(cols[None, :] - cols[:, None] + NA_WIN_C - 1, 0, 2 * NA_WIN_C - 2)
    dr_all, ok_all = [], []
    for j in (0, 1, nb - 1):
        ks = int(np.clip(j - 1, 0, nb - 3)) * qr
        r = j * qr + np.arange(qr)
        rs = np.clip(r - NA_WIN_R // 2, 0, rows - NA_WIN_R)
        kr = ks + np.arange(kr_n)
        ok_all.append((kr[None, :] >= rs[:, None]) & (kr[None, :] < rs[:, None] + NA_WIN_R))
        dr_all.append(np.clip(kr[None, :] - r[:, None] + NA_WIN_R - 1, 0, 2 * NA_WIN_R - 2))
    dr_all = np.stack(dr_all)
    ok_all = np.stack(ok_all)
    t1 = jnp.take(rpb, jnp.asarray(dc.reshape(-1)), axis=3).reshape(depth, H, 2 * NA_WIN_R - 1, GRID_W, GRID_W)
    t1 = jnp.where(jnp.asarray(col_ok), t1, NEG)
    t2 = jnp.take(t1, jnp.asarray(dr_all.reshape(-1)), axis=2).reshape(depth, H, 3, qr, kr_n, GRID_W, GRID_W)
    t2 = jnp.where(jnp.asarray(ok_all)[None, None, :, :, :, None, None], t2, NEG)
    bias = t2.transpose(0, 2, 1, 4, 6, 3, 5).reshape(depth, 3, H, NA_KEY_ROWS, ATT_BLOCK)
    none = jnp.full((depth, 1, H, NA_KEY_ROWS, ATT_BLOCK), NEG, F32)
    return jnp.concatenate([bias, none], axis=1)


def _prepare(w_in, w_uq, w_ukv, w_out, w_ffn_in, w_ffn_out, g_mix, g_ffn, b_gate, na_qk_gain,
             ml_conv_w, ml_conv_b, ml_head_gain, mla_gq, mla_gkv, mla_qk_gain):
    depth, D, _ = w_in.shape
    o_gate = C_MLO + ML_W
    o_ql = o_gate + 4 * ML_HEADS
    o_kr = o_ql + Q_LORA + KV_LORA
    zc = lambda w: jnp.zeros((depth, D, w), w_in.dtype)
    w_in_p = jnp.concatenate([
        w_in[..., :o_gate], w_in[..., o_ql:o_kr],
        w_in[..., o_gate:o_ql], zc(MISC_ROPE_LANE - 4 * ML_HEADS),
        w_in[..., o_kr:], zc(LANES - MISC_ROPE_LANE - MLA_ROPE)], axis=-1).astype(BF16)
    pad_h = lambda a, w: jnp.pad(a, [(0, 0)] * (a.ndim - 1) + [(0, LANES - w)])
    w_uq_p = pad_h(w_uq.reshape(depth, Q_LORA, MLA_HEADS, MLA_QK), MLA_QK)
    w_uq_p = w_uq_p.reshape(depth, Q_LORA, MLA_HEADS * LANES).astype(BF16)
    ukv = w_ukv.reshape(depth, KV_LORA, MLA_HEADS, MLA_NOPE + MLA_V)
    w_uk_p = pad_h(ukv[..., :MLA_NOPE], MLA_NOPE).reshape(depth, KV_LORA, MLA_HEADS * LANES).astype(BF16)
    w_uv = ukv[..., MLA_NOPE:].reshape(depth, KV_LORA, MLA_HEADS * MLA_V).transpose(0, 2, 1).astype(BF16)
    ff = w_ffn_out.shape[1]
    mla_g = pad_h(mla_qk_gain, MLA_QK)
    return {
        "w_in": w_in_p, "w_uq": w_uq_p, "w_uk": w_uk_p, "w_uv": w_uv,
        "w_nav": w_in[..., C_NAV:C_NAV + NA_W].transpose(0, 2, 1).astype(BF16),
        "w_out": w_out.astype(BF16),
        "w_a": w_ffn_in[..., :ff].astype(BF16), "w_b": w_ffn_in[..., ff:].astype(BF16),
        "w_fo": w_ffn_out.astype(BF16),
        "g_mix": g_mix[:, None, :], "g_ffn": g_ffn[:, None, :],
        "gq": mla_gq[:, None, :], "gkv": mla_gkv[:, None, :],
        "na_gq": jnp.tile(na_qk_gain[:, 0:1, :], (1, 1, NA_HEADS)) * NA_SCALE,
        "na_gk": jnp.tile(na_qk_gain[:, 1:2, :], (1, 1, NA_HEADS)),
        "mla_gq": jnp.tile(mla_g[:, 0:1, :], (1, 1, MLA_HEADS)) * (MLA_SCALE * LOG2E),
        "mla_gk": jnp.tile(mla_g[:, 1:2, :], (1, 1, MLA_HEADS)),
        "head_gain": ml_head_gain.reshape(depth, 1, ML_W),
        "b_gate": pad_h(b_gate, 4 * ML_HEADS)[:, None, :],
        "conv_w": jnp.pad(ml_conv_w, ((0, 0), (0, 8 - ml_conv_w.shape[1]), (0, 0))),
        "conv_b": ml_conv_b[:, None, :],
    }


def kernel(x, c, ctx, c_ctx, w_mod, b_mod, g_mix, g_ffn, w_in, b_gate, na_qk_gain, na_rpb,
           ml_conv_w, ml_conv_b, ml_head_gain, mla_gq, mla_gkv, w_uq, w_ukv, mla_qk_gain,
           w_out, w_ffn_in, w_ffn_out):
    B, S, D = x.shape
    CTX = ctx.shape[1]
    depth = w_in.shape[0]
    assert CTX == ATT_BLOCK and S % ROW_TILE == 0 and (B * CTX) % ROW_TILE == 0
    assert S % (2 * FLASH_TK) == 0 and S % FLASH_TQ == 0 and S // ATT_BLOCK >= 3 and B + 1 <= 8
    dims = (B, S, CTX)

    prm = _prepare(w_in, w_uq, w_ukv, w_out, w_ffn_in, w_ffn_out, g_mix, g_ffn, b_gate, na_qk_gain,
                   ml_conv_w, ml_conv_b, ml_head_gain, mla_gq, mla_gkv, mla_qk_gain)
    cos_t, sa_t, sb_t = _rope_tables(S, B * CTX)
    tri = np.tril(np.ones((ML_CHUNK, ML_CHUNK), np.float32))
    tabs = {
        "bd64": _block_diag_ones(HEAD_DIM), "bd128": _block_diag_ones(LANES),
        "cos": cos_t, "sa": sa_t, "sb": sb_t,
        "tril": jnp.asarray(tri, dtype=BF16), "triu": jnp.asarray(tri.T, dtype=BF16),
    }
    bias = _na_bias(na_rpb, S // GRID_W)

    cond8 = jnp.concatenate([c, c_ctx[None, :], jnp.zeros((8 - B - 1, D), F32)], axis=0)
    mods = _modulation(cond8, w_mod, b_mod)
    mods = mods.reshape(depth, 8, 6, D)

    xs = jnp.concatenate([x.reshape(B * S, D), ctx.reshape(B * CTX, D)], axis=0)
    for l in range(depth):
        with_ctx = l < depth - 1
        naq, nak, mlqk, mlv, mlo, misc, mq, mk, navt, mvt = _proj(l, xs, mods, prm, tabs, dims)
        cq, ck = _conv(l, mlqk, prm, dims)
        ona = _na(l, naq, nak, navt, bias, dims, with_ctx)
        omla = _flash(mq, mk, mvt, dims, with_ctx)
        hf, hb = _mlstm(l, cq, ck, mlv, misc, prm, tabs, dims)
        xs = _out(l, xs, mods, ona, omla, hf, hb, mlo, prm, tabs, dims, with_ctx)
    return xs.reshape(B, S, D)
```

```python
import functools

import numpy as np
import jax
import jax.numpy as jnp
from jax import lax
from jax.experimental import pallas as pl
from jax.experimental.pallas import tpu as pltpu

GRID_W = 64
HEAD_DIM = 64
NA_HEADS = 4
NA_WIN_R = 8
NA_WIN_C = 16
ML_HEADS = 4
ML_CHUNK = 128
MLA_HEADS = 8
MLA_NOPE = 64
MLA_ROPE = 32
MLA_V = 64
Q_LORA = 384
KV_LORA = 256
ROPE_BASE = 10000.0
EPS = 1e-6
NA_W = NA_HEADS * HEAD_DIM
ML_W = ML_HEADS * HEAD_DIM
MLA_QK = MLA_NOPE + MLA_ROPE
NA_SCALE = HEAD_DIM ** -0.5
MLA_SCALE = MLA_QK ** -0.5
LOG2E = 1.4426950408889634

LANES = 128
MXU_DIM = 256
VMEM_LIMIT = 56 * 1024 * 1024

ROW_TILE = 512
ATT_BLOCK = 256
NA_KEY_ROWS = 768
FLASH_TK = 512
FLASH_TQ = 1024
FLASH_SLOTS = 2
FLASH_UNROLL = 4
NEG = -1e30
DEN_ROWS = 16

C_NAQ, C_NAK, C_NAV = 0, 256, 512
C_MLQK, C_MLV, C_MLO = 768, 1280, 1536
C_QL = 1792
C_KVL = C_QL + Q_LORA
C_MISC = C_KVL + KV_LORA
C_MISC2 = C_MISC + LANES
D_IN_PAD = C_MISC2 + LANES
MISC_ROPE_LANE = 64

BF16 = jnp.bfloat16
F32 = jnp.float32


def _cparams(sem):
    return pltpu.CompilerParams(dimension_semantics=sem, vmem_limit_bytes=VMEM_LIMIT)


def _dot(a, b):
    return jnp.dot(a, b, preferred_element_type=F32)


def _dot_nt(a, b):
    return lax.dot_general(a, b, (((1,), (1,)), ((), ())), preferred_element_type=F32)


def _dot_tn(a, b):
    return lax.dot_general(a, b, (((0,), (0,)), ((), ())), preferred_element_type=F32)


def _split2(x):
    hi = x.astype(BF16)
    lo = (x - hi.astype(F32)).astype(BF16)
    return hi, lo


def _group_sumsq(x, bd):
    x2 = x * x
    hi, lo = _split2(x2)
    outs = []
    for c in range(x.shape[1] // MXU_DIM):
        sl = slice(c * MXU_DIM, (c + 1) * MXU_DIM)
        outs.append(_dot(hi[:, sl], bd) + _dot(lo[:, sl], bd))
    return outs[0] if len(outs) == 1 else jnp.concatenate(outs, axis=-1)


def _rms_rows(x):
    return x * lax.rsqrt(jnp.mean(x * x, axis=-1, keepdims=True) + EPS)


def _sigmoid(x):
    return 1.0 / (1.0 + jnp.exp(-x))


def _mod_kernel(c_ref, w_ref, b_ref, o_ref):
    c = c_ref[...]
    a = c * _sigmoid(c)
    o_ref[...] = jnp.dot(a, w_ref[...], preferred_element_type=F32,
                         precision=lax.Precision.HIGHEST) + b_ref[...]


def _modulation(cond8, w_mod, b_mod):
    depth, d, d6 = w_mod.shape
    n = d6 // d
    return pl.pallas_call(
        _mod_kernel,
        grid=(depth, n),
        in_specs=[
            pl.BlockSpec((8, d), lambda l, j: (0, 0)),
            pl.BlockSpec((None, d, d), lambda l, j: (l, 0, j)),
            pl.BlockSpec((None, 1, d), lambda l, j: (l, 0, j)),
        ],
        out_specs=pl.BlockSpec((None, 8, d), lambda l, j: (l, 0, j)),
        out_shape=jax.ShapeDtypeStruct((depth, 8, d6), F32),
        compiler_params=_cparams(("arbitrary", "arbitrary")),
        name="modulation",
    )(cond8, w_mod, b_mod.reshape(depth, 1, d6))


def _proj_kernel(x_ref, mod_ref, g_ref, w_in_ref, w_uq_ref, w_uqs_ref, w_uk_ref, w_uv_ref,
                 gq_ref, gkv_ref, na_gq_ref, na_gk_ref, mla_gq_ref, mla_gk_ref, mla_gqs_ref, mla_gks_ref,
                 bd64_ref, bd128_ref, cos_ref, sin_ref,
                 naq_ref, nak_ref, mlqk_ref, mlv_ref, mlo_ref, misc_ref,
                 mq_ref, mk_ref, nav_ref, mv_ref):
    x = x_ref[...]
    shift = mod_ref[0:1, :]
    scale = mod_ref[1:2, :]
    h = (_rms_rows(x) * g_ref[...]) * (1.0 + scale) + shift
    hb = h.astype(BF16)

    def proj(c0, width):
        return _dot(hb, w_in_ref[:, c0:c0 + width])

    bd64 = bd64_ref[...]
    bd128 = bd128_ref[...]

    ql = proj(C_QL, Q_LORA)
    kvl = proj(C_KVL, KV_LORA)
    misc = proj(C_MISC, LANES)
    misc_ref[...] = misc
    qn = (_rms_rows(ql) * gq_ref[...]).astype(BF16)
    kvn = (_rms_rows(kvl) * gkv_ref[...]).astype(BF16)

    pq = proj(C_NAQ, NA_W)
    pk = proj(C_NAK, NA_W)
    nav_ref[...] = proj(C_NAV, NA_W).T.astype(BF16)

    qr = _dot(qn, w_uq_ref[...])
    qx = _dot(qn, w_uqs_ref[...])
    lane = lax.broadcasted_iota(jnp.int32, misc.shape, 1)
    krope = jnp.where((lane >= MISC_ROPE_LANE) & (lane < MISC_ROPE_LANE + MLA_ROPE), misc, 0.0)
    kr = _dot(kvn, w_uk_ref[...]) + jnp.concatenate([krope] * MLA_HEADS, axis=-1)
    kx = proj(C_MISC2, LANES)
    mv_ref[...] = _dot(kvn, w_uv_ref[...]).T.astype(BF16)

    mlqk_ref[...] = proj(C_MLQK, 2 * ML_W)
    mlv_ref[...] = proj(C_MLV, ML_W).astype(BF16)
    mlo_ref[...] = proj(C_MLO, ML_W)

    naq_ref[...] = (pq * lax.rsqrt(_group_sumsq(pq, bd64) * (1.0 / HEAD_DIM) + EPS)
                    * na_gq_ref[...]).astype(BF16)
    nak_ref[...] = (pk * lax.rsqrt(_group_sumsq(pk, bd64) * (1.0 / HEAD_DIM) + EPS)
                    * na_gk_ref[...]).astype(BF16)

    cos = cos_ref[...]
    sin = sin_ref[...]
    rq = lax.rsqrt(_group_sumsq(qr, bd128) * (1.0 / MLA_QK) + EPS)
    rk = lax.rsqrt(_group_sumsq(kr, bd128) * (1.0 / MLA_QK) + EPS)
    kxs = kx * sin
    for g in range(MLA_HEADS):
        sl = slice(g * LANES, (g + 1) * LANES)
        mq_ref[:, sl] = (rq[:, sl] * (qr[:, sl] * mla_gq_ref[:, sl] * cos
                                      + qx[:, sl] * mla_gqs_ref[:, sl] * sin)).astype(BF16)
        mk_ref[:, sl] = (rk[:, sl] * (kr[:, sl] * mla_gk_ref[:, sl] * cos
                                      + kxs * mla_gks_ref[:, sl])).astype(BF16)


def _proj(l, xs, mods, prm, tabs, dims):
    T, D = xs.shape
    B, S, CTX = dims
    nt = T // ROW_TILE
    lat_tiles = S // ROW_TILE

    def grp(i):
        return jnp.minimum(i // lat_tiles, B)

    def tab_row(i):
        return jnp.where(i < B * lat_tiles, i % lat_tiles, lat_tiles + (i - B * lat_tiles))

    row = lambda w: pl.BlockSpec((ROW_TILE, w), lambda i: (i, 0))
    lay = lambda shape: pl.BlockSpec((None,) + shape, lambda i: (l,) + (0,) * len(shape))
    const = lambda shape: pl.BlockSpec(shape, lambda i: (0,) * len(shape))
    tab = pl.BlockSpec((ROW_TILE, LANES), lambda i: (tab_row(i), 0))
    out_w = [(NA_W, BF16), (NA_W, BF16), (2 * ML_W, F32), (ML_W, BF16), (ML_W, F32),
             (LANES, F32), (MLA_HEADS * LANES, BF16), (MLA_HEADS * LANES, BF16)]
    out_t = [NA_W, MLA_HEADS * MLA_V]
    return pl.pallas_call(
        _proj_kernel,
        grid=(nt,),
        in_specs=[
            row(D),
            pl.BlockSpec((None, None, 6, D), lambda i: (l, grp(i), 0, 0)),
            lay((1, D)),
            lay((D, D_IN_PAD)), lay((Q_LORA, MLA_HEADS * LANES)), lay((Q_LORA, MLA_HEADS * LANES)),
            lay((KV_LORA, MLA_HEADS * LANES)), lay((KV_LORA, MLA_HEADS * MLA_V)),
            lay((1, Q_LORA)), lay((1, KV_LORA)), lay((1, NA_W)), lay((1, NA_W)),
            lay((1, MLA_HEADS * LANES)), lay((1, MLA_HEADS * LANES)),
            lay((1, MLA_HEADS * LANES)), lay((1, MLA_HEADS * LANES)),
            const((MXU_DIM, MXU_DIM)), const((MXU_DIM, MXU_DIM)),
            tab, tab,
        ],
        out_specs=[row(w) for w, _ in out_w]
        + [pl.BlockSpec((w, ROW_TILE), lambda i: (0, i)) for w in out_t],
        out_shape=[jax.ShapeDtypeStruct((T, w), dt) for w, dt in out_w]
        + [jax.ShapeDtypeStruct((w, T), BF16) for w in out_t],
        compiler_params=_cparams(("arbitrary",)),
        name="proj",
    )(xs, mods, prm["g_mix"], prm["w_in"], prm["w_uq"], prm["w_uqs"], prm["w_uk"], prm["w_uv"],
      prm["gq"], prm["gkv"], prm["na_gq"], prm["na_gk"],
      prm["mla_gq"], prm["mla_gk"], prm["mla_gqs"], prm["mla_gks"],
      tabs["bd64"], tabs["bd128"], tabs["cos"], tabs["sin"])


def _short_conv(x, prev_blk, next_blk, first, last, w_ref, b_ref):
    n = x.shape[0]
    prev_row = jnp.where(first, 0.0, prev_blk[7:8, :])
    next_row = jnp.where(last, 0.0, next_blk[0:1, :])
    ridx = lax.broadcasted_iota(jnp.int32, x.shape, 0)
    xm1 = jnp.where(ridx == 0, prev_row, pltpu.roll(x, 1, 0))
    xp1 = jnp.where(ridx == n - 1, next_row, pltpu.roll(x, n - 1, 0))
    y = b_ref[...] + xm1 * w_ref[0:1, :] + x * w_ref[1:2, :] + xp1 * w_ref[2:3, :]
    y = y * _sigmoid(y)
    return y[:, :ML_W].astype(BF16), (y[:, ML_W:] * (HEAD_DIM ** -0.5)).astype(BF16)


def _pair_select(o0, o1):
    lane = lax.broadcasted_iota(jnp.int32, o0.shape, 1)
    return jnp.where(lane < HEAD_DIM, o0, o1)


def _head_q(q, e, packed):
    if packed:
        lane = lax.broadcasted_iota(jnp.int32, q.shape, 1)
        keep = (lane < HEAD_DIM) if e == 0 else (lane >= HEAD_DIM)
        return jnp.where(keep, q, jnp.zeros_like(q))
    return q[:, e * LANES:(e + 1) * LANES]


def _kcols(e, packed):
    return slice(0, LANES) if packed else slice(e * LANES, (e + 1) * LANES)


def _na_kernel(q_ref, k_ref, vt_ref, kc_ref, vct_ref, bias_ref, o_ref, *, nb):
    j = pl.program_id(1)
    start = pl.multiple_of(jnp.clip(j - 1, 0, nb - 3) * ATT_BLOCK, ATT_BLOCK)
    q = q_ref[...]
    lane = lax.broadcasted_iota(jnp.int32, (ATT_BLOCK, LANES), 1)
    scores = []
    for h in range(NA_HEADS):
        p, e = divmod(h, 2)
        cols = slice(p * LANES, (p + 1) * LANES)
        qp = q[:, cols]
        keep = (lane < HEAD_DIM) if e == 0 else (lane >= HEAD_DIM)
        qt = jnp.where(keep, qp, jnp.zeros_like(qp)).astype(F32).T.astype(BF16)
        s_loc = _dot(k_ref[pl.ds(start, NA_KEY_ROWS), cols], qt) + bias_ref[h]
        s_ctx = _dot(kc_ref[:, cols], qt)
        scores.append((s_loc, s_ctx))
    probs = []
    for s_loc, s_ctx in scores:
        m = jnp.maximum(jnp.max(s_loc, axis=0, keepdims=True), jnp.max(s_ctx, axis=0, keepdims=True))
        probs.append((jnp.exp(s_loc - m).astype(BF16), jnp.exp(s_ctx - m).astype(BF16)))
    outs = []
    for h, (p_loc, p_ctx) in enumerate(probs):
        rows = slice(h * HEAD_DIM, (h + 1) * HEAD_DIM)
        ot = (_dot(_flash_values(vt_ref[rows, pl.ds(start, NA_KEY_ROWS)]), p_loc)
              + _dot(_flash_values(vct_ref[rows, :]), p_ctx))
        outs.append(ot[:HEAD_DIM, :] / ot[HEAD_DIM:HEAD_DIM + 1, :])
    o_ref[...] = jnp.concatenate(outs, axis=0).T.astype(o_ref.dtype)


def _na(l, naq, nak, navt, bias, dims, with_ctx):
    B, S, CTX = dims
    nb = S // ATT_BLOCK
    ctx_blk0 = B * S // CTX
    n_steps = nb + 1 if with_ctx else nb
    rows_out = naq.shape[0] if with_ctx else B * S

    def qrow(b, j):
        return jnp.where(j < nb, b * nb + j, ctx_blk0 + b)

    def variant(j):
        return jnp.where(j == 0, 0, jnp.where(j == nb - 1, 2, jnp.where(j == nb, 3, 1)))

    kern = functools.partial(_na_kernel, nb=nb)
    return pl.pallas_call(
        kern,
        grid=(B, n_steps),
        in_specs=[
            pl.BlockSpec((ATT_BLOCK, NA_W), lambda b, j: (qrow(b, j), 0)),
            pl.BlockSpec((S, NA_W), lambda b, j: (b, 0)),
            pl.BlockSpec((NA_W, S), lambda b, j: (0, b)),
            pl.BlockSpec((CTX, NA_W), lambda b, j: (ctx_blk0 + b, 0)),
            pl.BlockSpec((NA_W, CTX), lambda b, j: (0, ctx_blk0 + b)),
            pl.BlockSpec((None, None, NA_HEADS, NA_KEY_ROWS, ATT_BLOCK), lambda b, j: (l, variant(j), 0, 0, 0)),
        ],
        out_specs=pl.BlockSpec((ATT_BLOCK, NA_W), lambda b, j: (qrow(b, j), 0)),
        out_shape=jax.ShapeDtypeStruct((rows_out, NA_W), BF16),
        compiler_params=_cparams(("arbitrary", "arbitrary")),
        name="na",
    )(naq, nak, navt, nak, navt, bias)


def _flash_values(v):
    return jnp.concatenate([v, jnp.ones((DEN_ROWS, v.shape[1]), BF16)], axis=0)


def _flash_queries(q_ref):
    q = q_ref[...]
    return [q[:, e * LANES:(e + 1) * LANES].astype(F32).T.astype(BF16) for e in range(2)]


def _flash_ctx_scores(qs, kc_ref):
    return [_dot(kc_ref[:, e * LANES:(e + 1) * LANES], qs[e]) for e in range(2)]


def _flash_ctx_init(sts, vc_ref, acc_ref, st_ref):
    ps = []
    for e, st in enumerate(sts):
        m = jnp.max(st, axis=0, keepdims=True)
        st_ref[e] = m
        ps.append(jnp.exp2(st - m).astype(BF16))
    for e, p in enumerate(ps):
        acc_ref[e] = _dot(_flash_values(vc_ref[e * HEAD_DIM:(e + 1) * HEAD_DIM, :]), p)


def _flash_finish(acc_ref, o_ref):
    ot = jnp.concatenate([acc_ref[e, :HEAD_DIM, :] / acc_ref[e, HEAD_DIM:HEAD_DIM + 1, :]
                          for e in range(2)], axis=0)
    o_ref[...] = ot.T.astype(o_ref.dtype)


def _flash_kernel(q_ref, km_ref, vm_ref, kc_ref, vc_ref, o_ref, *scratch, n_main):
    ns = FLASH_SLOTS
    s_refs = [scratch[2 * i:2 * i + 2] for i in range(ns)]
    mx_refs = [scratch[2 * ns + 2 * i:2 * ns + 2 * i + 2] for i in range(ns)]
    acc_ref, st_ref = scratch[4 * ns:4 * ns + 2]
    tk = FLASH_TK
    qs = _flash_queries(q_ref)

    def scores1(slot, r0, e):
        st = _dot(km_ref[pl.ds(r0, tk), e * LANES:(e + 1) * LANES], qs[e])
        s_refs[slot][e][...] = st
        mx_refs[slot][e][...] = jnp.max(st, axis=0, keepdims=True)

    def absorb1(slot, r0, e):
        m = st_ref[e]
        m_new = jnp.maximum(m, mx_refs[slot][e][...])
        st_ref[e] = m_new
        alpha = jnp.exp2(m - m_new)
        p = jnp.exp2(s_refs[slot][e][...] - m_new).astype(BF16)
        v = vm_ref[e * HEAD_DIM:(e + 1) * HEAD_DIM, pl.ds(r0, tk)]
        acc_ref[e] = alpha * acc_ref[e] + _dot(_flash_values(v), p)

    def scores(slot, r0):
        for e in range(2):
            scores1(slot, r0, e)

    def absorb(slot, r0):
        for e in range(2):
            absorb1(slot, r0, e)

    ctx_scores = _flash_ctx_scores(qs, kc_ref)
    scores(0, 0)
    _flash_ctx_init(ctx_scores, vc_ref, acc_ref, st_ref)

    def advance(c, r0):
        scores1((c + 1) % ns, r0 + tk, 0)
        absorb1(c % ns, r0, 1)
        scores1((c + 1) % ns, r0 + tk, 1)
        absorb1(c % ns, r0, 0)

    nu = FLASH_UNROLL

    def body(i, carry):
        r0 = pl.multiple_of(i * (nu * tk), nu * tk)
        for c in range(nu):
            advance(c, r0 + c * tk)
        return carry

    n_loop = (n_main - 1) // nu
    lax.fori_loop(0, n_loop, body, 0)
    for c in range(n_loop * nu, n_main - 1):
        advance(c, c * tk)
    absorb((n_main - 1) % ns, (n_main - 1) * tk)
    _flash_finish(acc_ref, o_ref)


def _flash_ctx_kernel(o_in_ref, q_ref, kc_ref, vc_ref, o_ref, acc_ref, st_ref):
    del o_in_ref
    _flash_ctx_init(_flash_ctx_scores(_flash_queries(q_ref), kc_ref), vc_ref, acc_ref, st_ref)
    _flash_finish(acc_ref, o_ref)


def _flash(mq, mk, mvt, dims, with_ctx):
    B, S, CTX = dims
    T = mq.shape[0] if with_ctx else B * S
    tq = FLASH_TQ
    nq = S // tq
    ctx_blk0 = B * S // CTX
    npair = MLA_HEADS // 2
    acc = lambda n: [pltpu.VMEM((2, HEAD_DIM + DEN_ROWS, n), F32), pltpu.VMEM((2, 1, n), F32)]
    out_shape = jax.ShapeDtypeStruct((T, MLA_HEADS * MLA_V), BF16)
    kc_spec = pl.BlockSpec((CTX, 2 * LANES), lambda b, p, *_: (ctx_blk0 + b, p))
    vc_spec = pl.BlockSpec((LANES, CTX), lambda b, p, *_: (p, ctx_blk0 + b))

    o = pl.pallas_call(
        functools.partial(_flash_kernel, n_main=S // FLASH_TK),
        grid=(B, npair, nq),
        in_specs=[
            pl.BlockSpec((tq, 2 * LANES), lambda b, p, j: (b * nq + j, p)),
            pl.BlockSpec((S, 2 * LANES), lambda b, p, j: (b, p)),
            pl.BlockSpec((LANES, S), lambda b, p, j: (p, b)),
            kc_spec, vc_spec,
        ],
        out_specs=pl.BlockSpec((tq, LANES), lambda b, p, j: (b * nq + j, p)),
        out_shape=out_shape,
        scratch_shapes=[pltpu.VMEM((FLASH_TK, tq), F32)] * (2 * FLASH_SLOTS)
        + [pltpu.VMEM((1, tq), F32)] * (2 * FLASH_SLOTS) + acc(tq),
        compiler_params=_cparams(("arbitrary", "arbitrary", "arbitrary")),
        name="flash",
    )(mq, mk, mvt, mk, mvt)
    if not with_ctx:
        return o
    return pl.pallas_call(
        _flash_ctx_kernel,
        grid=(B, npair),
        in_specs=[
            pl.BlockSpec(memory_space=pl.ANY),
            pl.BlockSpec((CTX, 2 * LANES), lambda b, p: (ctx_blk0 + b, p)),
            kc_spec, vc_spec,
        ],
        out_specs=pl.BlockSpec((CTX, LANES), lambda b, p: (ctx_blk0 + b, p)),
        out_shape=out_shape,
        scratch_shapes=acc(CTX),
        input_output_aliases={0: 0},
        compiler_params=_cparams(("arbitrary", "arbitrary")),
        name="flash_ctx",
    )(o, mq, mk, mvt)


def _mlstm_gates(d, gates, bgate, tri):
    gb = gates + bgate
    ls = jnp.minimum(gb, 0.0) - jnp.log1p(jnp.exp(-jnp.abs(gb)))
    lane = lax.broadcasted_iota(jnp.int32, gb.shape, 1)
    is_f = ((lane >= 4) & (lane < 8)) | ((lane >= 12) & (lane < 16))
    xg = jnp.where(lane < 16, jnp.where(is_f, ls, gb), 0.0)
    x1 = xg.astype(BF16)
    r1 = xg - x1.astype(F32)
    x2 = r1.astype(BF16)
    x3 = (r1 - x2.astype(F32)).astype(BF16)
    cum = _dot(tri, x1) + _dot(tri, x2) + _dot(tri, x3)
    return xg, cum, xg.T, cum.T, jnp.sum(xg, axis=0, keepdims=True)


def _mlstm_kernel(xf_ref, xpf_ref, xnf_ref, vf_ref, gf_ref, xb_ref, xpb_ref, xnb_ref, vb_ref, gb_ref,
                  cw_ref, cb_ref, bgate_ref, tril_ref, triu_ref, hf_ref, hb_ref, c_ref, m_ref,
                  *, n_ctx, n_lat):
    @pl.when(pl.program_id(1) == 0)
    def _():
        c_ref[...] = jnp.zeros_like(c_ref)
        m_ref[...] = jnp.zeros_like(m_ref)

    L = ML_CHUNK
    bgate = bgate_ref[...]
    step = pl.program_id(1)
    in_ctx = step < n_ctx
    pos = jnp.where(in_ctx, step, step - n_ctx)
    starts = pos == 0
    ends = pos == jnp.where(in_ctx, n_ctx, n_lat) - 1
    qk_f = _short_conv(xf_ref[...], xpf_ref[...], xnf_ref[...], starts, ends, cw_ref, cb_ref)
    qk_b = _short_conv(xb_ref[...], xpb_ref[...], xnb_ref[...], ends, starts, cw_ref, cb_ref)
    q = (qk_f[0], qk_b[0])
    k = (qk_f[1], qk_b[1])
    v = (vf_ref[...], vb_ref[...])
    gat = (_mlstm_gates(0, gf_ref[...], bgate, tril_ref[...]),
           _mlstm_gates(1, gb_ref[...], bgate, triu_ref[...]))
    si = lax.broadcasted_iota(jnp.int32, (L, L), 0)
    ti = lax.broadcasted_iota(jnp.int32, (L, L), 1)
    valid = (si <= ti, si >= ti)
    lane_l = lax.broadcasted_iota(jnp.int32, (L, LANES), 1)
    row_l = lax.broadcasted_iota(jnp.int32, (LANES, L), 0)
    vts = [[v[d][:, p * LANES:(p + 1) * LANES].astype(F32).T for p in range(ML_HEADS // 2)]
           for d in range(2)]
    chains = [(d, hd) for d in range(2) for hd in range(ML_HEADS)]

    st1 = []
    for d, hd in chains:
        p, e = divmod(hd, 2)
        xg, cum, xg_t, cum_t, tot = gat[d]
        ci = 8 * d + hd
        cf = 8 * d + 4 + hd
        in_head = (lane_l < HEAD_DIM) if e == 0 else (lane_l >= HEAD_DIM)
        in_rows = (row_l < HEAD_DIM) if e == 0 else (row_l >= HEAD_DIM)
        ones_row = HEAD_DIM if e == 0 else 0
        qp = q[d][:, p * LANES:(p + 1) * LANES]
        kp = k[d][:, p * LANES:(p + 1) * LANES]
        km = jnp.where(in_head, kp, jnp.zeros_like(kp))
        vat = jnp.where(in_rows, vts[d][p], jnp.where(row_l == ones_row, 1.0, 0.0))
        b_row = cum_t[cf:cf + 1, :]
        li_row = xg_t[ci:ci + 1, :]
        c_col = xg[:, ci:ci + 1] - cum[:, cf:cf + 1]
        m_prev = m_ref[d * ML_HEADS + hd:d * ML_HEADS + hd + 1, 0:1]
        g = b_row + m_prev
        dm = jnp.where(valid[d], b_row + c_col, NEG)
        mt = jnp.maximum(g, jnp.max(dm, axis=0, keepdims=True))
        b_tot = tot[:, cf:cf + 1]
        a_row = b_tot - b_row + li_row
        m_new = jnp.maximum(b_tot + m_prev, jnp.max(a_row, axis=1, keepdims=True))
        st1.append(dict(qp=qp, km=km, vat=vat, ones_row=ones_row, g=g, dm=dm, mt=mt,
                        m_new=m_new, decay=jnp.exp(b_tot + m_prev - m_new), a_row=a_row))

    for c in st1:
        c["qk"] = (_dot_nt(c["km"], c["qp"]) * jnp.exp(c["dm"] - c["mt"])).astype(BF16)

    hs = []
    for (d, hd), c in zip(chains, st1):
        cst = c_ref[d, hd]
        haug = (jnp.exp(c["g"] - c["mt"]) * _dot_nt(cst.astype(BF16), c["qp"])
                + _dot(c["vat"].astype(BF16), c["qk"]))
        den = haug[c["ones_row"]:c["ones_row"] + 1, :]
        hs.append(haug / jnp.maximum(jnp.abs(den), jnp.exp(-c["mt"])))
        wv = (c["vat"] * jnp.exp(c["a_row"] - c["m_new"])).astype(BF16)
        c_ref[d, hd] = c["decay"] * cst + _dot(wv, c["km"])
        r = d * ML_HEADS + hd
        m_ref[r:r + 1, :] = jnp.broadcast_to(c["m_new"], (1, LANES))

    for d, h_ref in enumerate((hf_ref, hb_ref)):
        for p in range(ML_HEADS // 2):
            pair = hs[d * ML_HEADS + 2 * p:d * ML_HEADS + 2 * p + 2]
            h_ref[:, p * LANES:(p + 1) * LANES] = jnp.where(row_l < HEAD_DIM, pair[0], pair[1]).T


def _mlstm(l, mlqk, mlv, misc, prm, tabs, dims):
    T, W = mlqk.shape
    B, S, CTX = dims
    L = ML_CHUNK
    n_ctx = CTX // L
    n_lat = S // L
    ctx0 = B * S // L

    def fwd(b, c):
        return jnp.where(c < n_ctx, ctx0 + b * n_ctx + c, b * n_lat + (c - n_ctx))

    def bwd(b, c):
        return jnp.where(c < n_ctx, ctx0 + b * n_ctx + (n_ctx - 1 - c), b * n_lat + (n_lat - 1 - (c - n_ctx)))

    sub = L // 8
    last8 = T // 8 - 1

    def specs(fn):
        return [pl.BlockSpec((L, W), lambda b, c: (fn(b, c), 0)),
                pl.BlockSpec((8, W), lambda b, c: (jnp.maximum(fn(b, c) * sub - 1, 0), 0)),
                pl.BlockSpec((8, W), lambda b, c: (jnp.minimum((fn(b, c) + 1) * sub, last8), 0)),
                pl.BlockSpec((L, ML_W), lambda b, c: (fn(b, c), 0)),
                pl.BlockSpec((L, LANES), lambda b, c: (fn(b, c), 0))]

    return pl.pallas_call(
        functools.partial(_mlstm_kernel, n_ctx=n_ctx, n_lat=n_lat),
        grid=(B, n_ctx + n_lat),
        in_specs=specs(fwd) + specs(bwd) + [
            pl.BlockSpec((None, 8, W), lambda b, c: (l, 0, 0)),
            pl.BlockSpec((None, 1, W), lambda b, c: (l, 0, 0)),
            pl.BlockSpec((None, 1, LANES), lambda b, c: (l, 0, 0)),
            pl.BlockSpec((L, L), lambda b, c: (0, 0)),
            pl.BlockSpec((L, L), lambda b, c: (0, 0)),
        ],
        out_specs=[pl.BlockSpec((L, ML_W), lambda b, c: (fwd(b, c), 0)),
                   pl.BlockSpec((L, ML_W), lambda b, c: (bwd(b, c), 0))],
        out_shape=[jax.ShapeDtypeStruct((T, ML_W), F32)] * 2,
        scratch_shapes=[pltpu.VMEM((2, ML_HEADS, LANES, LANES), F32), pltpu.VMEM((8, LANES), F32)],
        compiler_params=_cparams(("arbitrary", "arbitrary")),
        name="mlstm",
    )(mlqk, mlqk, mlqk, mlv, misc, mlqk, mlqk, mlqk, mlv, misc,
      prm["conv_w"], prm["conv_b"], prm["b_gate"], tabs["tril"], tabs["triu"])


def _ffn_chunks(ff):
    step = 2 * MXU_DIM
    return [(c, min(step, ff - c)) for c in range(0, ff, step)]


def _out_kernel(x_ref, mod_ref, ona_ref, omla_ref, hf_ref, hb_ref, mlo_ref, hg_ref, bd64_ref,
                w_out_ref, g_ref, wa_ref, wb_ref, wo_ref, o_ref):
    x = x_ref[...]
    gate1 = mod_ref[2:3, :]
    shift2 = mod_ref[3:4, :]
    scale2 = mod_ref[4:5, :]
    gate2 = mod_ref[5:6, :]

    h = hf_ref[...] + hb_ref[...]
    hn = h * lax.rsqrt(_group_sumsq(h, bd64_ref[...]) * (1.0 / HEAD_DIM) + EPS) * hg_ref[...]
    ob = (hn * _sigmoid(mlo_ref[...])).astype(BF16)
    mix = (_dot(ona_ref[...], w_out_ref[0:NA_W, :])
           + _dot(ob, w_out_ref[NA_W:NA_W + ML_W, :])
           + _dot(omla_ref[...], w_out_ref[NA_W + ML_W:, :]))
    x1 = x + gate1 * mix

    h2 = ((_rms_rows(x1) * g_ref[...]) * (1.0 + scale2) + shift2).astype(BF16)
    acc = jnp.zeros_like(x1)
    for c0, w in _ffn_chunks(wa_ref.shape[1]):
        a = _dot(h2, wa_ref[:, c0:c0 + w])
        b = _dot(h2, wb_ref[:, c0:c0 + w])
        act = (a * _sigmoid(a) * b).astype(BF16)
        acc = acc + _dot(act, wo_ref[c0:c0 + w, :])
    o_ref[...] = x1 + gate2 * acc


def _out(l, xs, mods, ona, omla, hf, hb, mlo, prm, tabs, dims, with_ctx):
    D = xs.shape[1]
    B, S, CTX = dims
    T = xs.shape[0] if with_ctx else B * S
    nt = T // ROW_TILE
    lat_tiles = S // ROW_TILE
    ff = prm["w_a"].shape[-1]

    def grp(i):
        return jnp.minimum(i // lat_tiles, B)

    row = lambda w: pl.BlockSpec((ROW_TILE, w), lambda i: (i, 0))
    lay = lambda shape: pl.BlockSpec((None,) + shape, lambda i: (l,) + (0,) * len(shape))
    big = lambda shape: pl.BlockSpec((None,) + shape, lambda i: (l,) + (0,) * len(shape),
                                     pipeline_mode=pl.Buffered(1))
    return pl.pallas_call(
        _out_kernel,
        grid=(nt,),
        in_specs=[
            row(D),
            pl.BlockSpec((None, None, 6, D), lambda i: (l, grp(i), 0, 0)),
            row(NA_W), row(MLA_HEADS * MLA_V), row(ML_W), row(ML_W), row(ML_W),
            lay((1, ML_W)),
            pl.BlockSpec((MXU_DIM, MXU_DIM), lambda i: (0, 0)),
            big((NA_W + ML_W + MLA_HEADS * MLA_V, D)),
            lay((1, D)),
            big((D, ff)), big((D, ff)), big((ff, D)),
        ],
        out_specs=row(D),
        out_shape=jax.ShapeDtypeStruct((T, D), F32),
        compiler_params=_cparams(("arbitrary",)),
        name="out_ffn",
    )(xs, mods, ona, omla, hf, hb, mlo, prm["head_gain"], tabs["bd64"],
      prm["w_out"], prm["g_ffn"], prm["w_a"], prm["w_b"], prm["w_fo"])


def _block_diag_ones(group):
    idx = np.arange(MXU_DIM) // group
    return jnp.asarray((idx[:, None] == idx[None, :]).astype(np.float32), dtype=BF16)


def _rope_tables(S, n_ctx_rows):
    n_freq = MLA_ROPE // 4
    inv = ROPE_BASE ** (-jnp.arange(n_freq, dtype=F32) / n_freq)
    t = jnp.arange(S, dtype=jnp.int32)
    row = (t // GRID_W).astype(F32)
    col = (t % GRID_W).astype(F32)
    ang = jnp.concatenate([row[:, None] * inv, col[:, None] * inv], axis=-1)
    ang = jnp.concatenate([ang, jnp.zeros((n_ctx_rows, MLA_ROPE // 2), F32)], axis=0)
    cos, sin = jnp.cos(ang), jnp.sin(ang)
    n = S + n_ctx_rows
    half = MLA_ROPE // 2
    ones = jnp.ones((n, MLA_NOPE), F32)
    z = lambda w: jnp.zeros((n, w), F32)
    tail = LANES - MLA_NOPE - MLA_ROPE
    del half
    cos_t = jnp.concatenate([ones, cos, cos, jnp.ones((n, tail), F32)], axis=-1)
    sin_t = jnp.concatenate([z(MLA_NOPE), -sin, sin, z(tail)], axis=-1)
    return cos_t, sin_t


def _na_bias(rpb, rows):
    depth, H = rpb.shape[:2]
    qr, kr_n = ATT_BLOCK // GRID_W, NA_KEY_ROWS // GRID_W
    nb = rows // qr
    cols = np.arange(GRID_W)
    cs = np.clip(cols - NA_WIN_C // 2, 0, GRID_W - NA_WIN_C)
    col_ok = (cols[None, :] >= cs[:, None]) & (cols[None, :] < cs[:, None] + NA_WIN_C)
    dc = np.clip(cols[None, :] - cols[:, None] + NA_WIN_C - 1, 0, 2 * NA_WIN_C - 2)
    dr_all, ok_all = [], []
    for j in (0, 1, nb - 1):
        ks = int(np.clip(j - 1, 0, nb - 3)) * qr
        r = j * qr + np.arange(qr)
        rs = np.clip(r - NA_WIN_R // 2, 0, rows - NA_WIN_R)
        kr = ks + np.arange(kr_n)
        ok_all.append((kr[None, :] >= rs[:, None]) & (kr[None, :] < rs[:, None] + NA_WIN_R))
        dr_all.append(np.clip(kr[None, :] - r[:, None] + NA_WIN_R - 1, 0, 2 * NA_WIN_R - 2))
    dr_all = np.stack(dr_all)
    ok_all = np.stack(ok_all)
    t1 = jnp.take(rpb, jnp.asarray(dc.reshape(-1)), axis=3).reshape(depth, H, 2 * NA_WIN_R - 1, GRID_W, GRID_W)
    t1 = jnp.where(jnp.asarray(col_ok), t1, NEG)
    t2 = jnp.take(t1, jnp.asarray(dr_all.reshape(-1)), axis=2).reshape(depth, H, 3, qr, kr_n, GRID_W, GRID_W)
    t2 = jnp.where(jnp.asarray(ok_all)[None, None, :, :, :, None, None], t2, NEG)
    bias = t2.transpose(0, 2, 1, 4, 6, 3, 5).reshape(depth, 3, H, NA_KEY_ROWS, ATT_BLOCK)
    none = jnp.full((depth, 1, H, NA_KEY_ROWS, ATT_BLOCK), NEG, F32)
    return jnp.concatenate([bias, none], axis=1)


def _prepare(w_in, w_uq, w_ukv, w_out, w_ffn_in, w_ffn_out, g_mix, g_ffn, b_gate, na_qk_gain,
             ml_conv_w, ml_conv_b, ml_head_gain, mla_gq, mla_gkv, mla_qk_gain):
    depth, D, _ = w_in.shape
    o_gate = C_MLO + ML_W
    o_ql = o_gate + 4 * ML_HEADS
    o_kr = o_ql + Q_LORA + KV_LORA
    half = MLA_ROPE // 2
    zc = lambda w: jnp.zeros((depth, D, w), w_in.dtype)
    w_in_p = jnp.concatenate([
        w_in[..., :o_gate], w_in[..., o_ql:o_kr],
        w_in[..., o_gate:o_ql], zc(MISC_ROPE_LANE - 4 * ML_HEADS),
        w_in[..., o_kr:], zc(LANES - MISC_ROPE_LANE - MLA_ROPE),
        zc(MISC_ROPE_LANE), w_in[..., o_kr + half:], w_in[..., o_kr:o_kr + half],
        zc(LANES - MISC_ROPE_LANE - MLA_ROPE)], axis=-1).astype(BF16)
    pad_h = lambda a, w: jnp.pad(a, [(0, 0)] * (a.ndim - 1) + [(0, LANES - w)])

    def partner(a):
        lo, hi = a[..., MLA_NOPE:MLA_NOPE + half], a[..., MLA_NOPE + half:MLA_QK]
        return jnp.concatenate([jnp.zeros_like(a[..., :MLA_NOPE]), hi, lo,
                                jnp.zeros_like(a[..., MLA_QK:])], axis=-1)

    w_uq_h = pad_h(w_uq.reshape(depth, Q_LORA, MLA_HEADS, MLA_QK), MLA_QK)
    w_uq_p = w_uq_h.reshape(depth, Q_LORA, MLA_HEADS * LANES).astype(BF16)
    w_uq_s = partner(w_uq_h).reshape(depth, Q_LORA, MLA_HEADS * LANES).astype(BF16)
    ukv = w_ukv.reshape(depth, KV_LORA, MLA_HEADS, MLA_NOPE + MLA_V)
    w_uk_p = pad_h(ukv[..., :MLA_NOPE], MLA_NOPE).reshape(depth, KV_LORA, MLA_HEADS * LANES).astype(BF16)
    w_uv = ukv[..., MLA_NOPE:].reshape(depth, KV_LORA, MLA_HEADS * MLA_V).astype(BF16)
    ff = w_ffn_out.shape[1]
    mla_g = pad_h(mla_qk_gain, MLA_QK)
    return {
        "w_in": w_in_p, "w_uq": w_uq_p, "w_uqs": w_uq_s, "w_uk": w_uk_p, "w_uv": w_uv,
        "w_out": w_out.astype(BF16),
        "w_a": w_ffn_in[..., :ff].astype(BF16), "w_b": w_ffn_in[..., ff:].astype(BF16),
        "w_fo": w_ffn_out.astype(BF16),
        "g_mix": g_mix[:, None, :], "g_ffn": g_ffn[:, None, :],
        "gq": mla_gq[:, None, :], "gkv": mla_gkv[:, None, :],
        "na_gq": jnp.tile(na_qk_gain[:, 0:1, :], (1, 1, NA_HEADS)) * NA_SCALE,
        "na_gk": jnp.tile(na_qk_gain[:, 1:2, :], (1, 1, NA_HEADS)),
        "mla_gq": jnp.tile(mla_g[:, 0:1, :], (1, 1, MLA_HEADS)) * (MLA_SCALE * LOG2E),
        "mla_gk": jnp.tile(mla_g[:, 1:2, :], (1, 1, MLA_HEADS)),
        "mla_gqs": jnp.tile(partner(mla_g[:, 0:1, :]), (1, 1, MLA_HEADS)) * (MLA_SCALE * LOG2E),
        "mla_gks": jnp.tile(partner(mla_g[:, 1:2, :]), (1, 1, MLA_HEADS)),
        "head_gain": ml_head_gain.reshape(depth, 1, ML_W),
        "b_gate": pad_h(b_gate, 4 * ML_HEADS)[:, None, :],
        "conv_w": jnp.pad(ml_conv_w, ((0, 0), (0, 8 - ml_conv_w.shape[1]), (0, 0))),
        "conv_b": ml_conv_b[:, None, :],
    }


def kernel(x, c, ctx, c_ctx, w_mod, b_mod, g_mix, g_ffn, w_in, b_gate, na_qk_gain, na_rpb,
           ml_conv_w, ml_conv_b, ml_head_gain, mla_gq, mla_gkv, w_uq, w_ukv, mla_qk_gain,
           w_out, w_ffn_in, w_ffn_out):
    B, S, D = x.shape
    CTX = ctx.shape[1]
    depth = w_in.shape[0]
    assert CTX == ATT_BLOCK and S % ROW_TILE == 0 and (B * CTX) % ROW_TILE == 0
    assert S % (2 * FLASH_TK) == 0 and S % FLASH_TQ == 0 and S // ATT_BLOCK >= 3 and B + 1 <= 8
    dims = (B, S, CTX)

    prm = _prepare(w_in, w_uq, w_ukv, w_out, w_ffn_in, w_ffn_out, g_mix, g_ffn, b_gate, na_qk_gain,
                   ml_conv_w, ml_conv_b, ml_head_gain, mla_gq, mla_gkv, mla_qk_gain)
    cos_t, sin_t = _rope_tables(S, B * CTX)
    tri = np.tril(np.ones((ML_CHUNK, ML_CHUNK), np.float32))
    tabs = {
        "bd64": _block_diag_ones(HEAD_DIM), "bd128": _block_diag_ones(LANES),
        "cos": cos_t, "sin": sin_t,
        "tril": jnp.asarray(tri, dtype=BF16), "triu": jnp.asarray(tri.T, dtype=BF16),
    }
    bias = _na_bias(na_rpb, S // GRID_W)

    cond8 = jnp.concatenate([c, c_ctx[None, :], jnp.zeros((8 - B - 1, D), F32)], axis=0)
    mods = _modulation(cond8, w_mod, b_mod)
    mods = mods.reshape(depth, 8, 6, D)

    xs = jnp.concatenate([x.reshape(B * S, D), ctx.reshape(B * CTX, D)], axis=0)
    for l in range(depth):
        with_ctx = l < depth - 1
        naq, nak, mlqk, mlv, mlo, misc, mq, mk, navt, mvt = _proj(l, xs, mods, prm, tabs, dims)
        ona = _na(l, naq, nak, navt, bias, dims, with_ctx)
        omla = _flash(mq, mk, mvt, dims, with_ctx)
        hf, hb = _mlstm(l, mlqk, mlv, misc, prm, tabs, dims)
        xs = _out(l, xs, mods, ona, omla, hf, hb, mlo, prm, tabs, dims, with_ctx)
    return xs.reshape(B, S, D)
```

```python
import functools

import numpy as np
import jax
import jax.numpy as jnp
from jax import lax
from jax.experimental import pallas as pl
from jax.experimental.pallas import tpu as pltpu

GRID_W = 64
HEAD_DIM = 64
NA_HEADS = 4
NA_WIN_R = 8
NA_WIN_C = 16
ML_HEADS = 4
ML_CHUNK = 128
MLA_HEADS = 8
MLA_NOPE = 64
MLA_ROPE = 32
MLA_V = 64
Q_LORA = 384
KV_LORA = 256
ROPE_BASE = 10000.0
EPS = 1e-6
NA_W = NA_HEADS * HEAD_DIM
ML_W = ML_HEADS * HEAD_DIM
MLA_QK = MLA_NOPE + MLA_ROPE
NA_SCALE = HEAD_DIM ** -0.5
MLA_SCALE = MLA_QK ** -0.5
LOG2E = 1.4426950408889634

LANES = 128
MXU_DIM = 256
VMEM_LIMIT = 56 * 1024 * 1024

ROW_TILE = 512
ATT_BLOCK = 256
NA_KEY_ROWS = 768
FLASH_TK = 512
FLASH_TQ = 1024
FLASH_SLOTS = 2
FLASH_UNROLL = 4
NEG = -1e30
DEN_ROWS = 16

C_NAQ, C_NAK, C_NAV = 0, 256, 512
C_MLQK, C_MLV, C_MLO = 768, 1280, 1536
C_QL = 1792
C_KVL = C_QL + Q_LORA
C_MISC = C_KVL + KV_LORA
C_MISC2 = C_MISC + LANES
D_IN_PAD = C_MISC2 + LANES
MISC_ROPE_LANE = 64

BF16 = jnp.bfloat16
F32 = jnp.float32


def _cparams(sem):
    return pltpu.CompilerParams(dimension_semantics=sem, vmem_limit_bytes=VMEM_LIMIT)


def _dot(a, b):
    return jnp.dot(a, b, preferred_element_type=F32)


def _dot_nt(a, b):
    return lax.dot_general(a, b, (((1,), (1,)), ((), ())), preferred_element_type=F32)


def _dot_tn(a, b):
    return lax.dot_general(a, b, (((0,), (0,)), ((), ())), preferred_element_type=F32)


def _split2(x):
    hi = x.astype(BF16)
    lo = (x - hi.astype(F32)).astype(BF16)
    return hi, lo


def _group_sumsq(x, bd):
    x2 = x * x
    hi, lo = _split2(x2)
    outs = []
    for c in range(x.shape[1] // MXU_DIM):
        sl = slice(c * MXU_DIM, (c + 1) * MXU_DIM)
        outs.append(_dot(hi[:, sl], bd) + _dot(lo[:, sl], bd))
    return outs[0] if len(outs) == 1 else jnp.concatenate(outs, axis=-1)


def _rms_rows(x):
    return x * lax.rsqrt(jnp.mean(x * x, axis=-1, keepdims=True) + EPS)


def _sigmoid(x):
    return 1.0 / (1.0 + jnp.exp(-x))


def _mod_kernel(c_ref, w_ref, b_ref, o_ref):
    c = c_ref[...]
    a = c * _sigmoid(c)
    o_ref[...] = jnp.dot(a, w_ref[...], preferred_element_type=F32,
                         precision=lax.Precision.HIGHEST) + b_ref[...]


def _modulation(cond8, w_mod, b_mod):
    depth, d, d6 = w_mod.shape
    n = d6 // d
    return pl.pallas_call(
        _mod_kernel,
        grid=(depth, n),
        in_specs=[
            pl.BlockSpec((8, d), lambda l, j: (0, 0)),
            pl.BlockSpec((None, d, d), lambda l, j: (l, 0, j)),
            pl.BlockSpec((None, 1, d), lambda l, j: (l, 0, j)),
        ],
        out_specs=pl.BlockSpec((None, 8, d), lambda l, j: (l, 0, j)),
        out_shape=jax.ShapeDtypeStruct((depth, 8, d6), F32),
        compiler_params=_cparams(("arbitrary", "arbitrary")),
        name="modulation",
    )(cond8, w_mod, b_mod.reshape(depth, 1, d6))


def _proj_kernel(x_ref, mod_ref, g_ref, w_in_ref, w_uq_ref, w_uqs_ref, w_uk_ref, w_uv_ref,
                 gq_ref, gkv_ref, na_gq_ref, na_gk_ref, mla_gq_ref, mla_gk_ref, mla_gqs_ref, mla_gks_ref,
                 bd64_ref, bd128_ref, cos_ref, sin_ref,
                 naq_ref, nak_ref, mlqk_ref, mlv_ref, mlo_ref, misc_ref,
                 mq_ref, mk_ref, nav_ref, mv_ref):
    x = x_ref[...]
    shift = mod_ref[0:1, :]
    scale = mod_ref[1:2, :]
    h = (_rms_rows(x) * g_ref[...]) * (1.0 + scale) + shift
    hb = h.astype(BF16)

    def proj(c0, width):
        return _dot(hb, w_in_ref[:, c0:c0 + width])

    bd64 = bd64_ref[...]
    bd128 = bd128_ref[...]

    lat = proj(C_QL, D_IN_PAD - C_QL)
    ql = lat[:, :Q_LORA]
    kvl = lat[:, C_KVL - C_QL:C_MISC - C_QL]
    misc = lat[:, C_MISC - C_QL:C_MISC2 - C_QL]
    kx = lat[:, C_MISC2 - C_QL:]
    misc_ref[...] = misc
    qn = (_rms_rows(ql) * gq_ref[...]).astype(BF16)
    kvn = (_rms_rows(kvl) * gkv_ref[...]).astype(BF16)

    na = proj(C_NAQ, 3 * NA_W)
    pq = na[:, :NA_W]
    pk = na[:, NA_W:2 * NA_W]
    nav_ref[...] = na[:, 2 * NA_W:].T.astype(BF16)

    qr = _dot(qn, w_uq_ref[...])
    qx = _dot(qn, w_uqs_ref[...])
    lane = lax.broadcasted_iota(jnp.int32, misc.shape, 1)
    krope = jnp.where((lane >= MISC_ROPE_LANE) & (lane < MISC_ROPE_LANE + MLA_ROPE), misc, 0.0)
    kr = _dot(kvn, w_uk_ref[...]) + jnp.concatenate([krope] * MLA_HEADS, axis=-1)
    mv_ref[...] = _dot(kvn, w_uv_ref[...]).T.astype(BF16)

    ml = proj(C_MLQK, 4 * ML_W)
    mlqk_ref[...] = ml[:, :2 * ML_W]
    mlv_ref[...] = ml[:, 2 * ML_W:3 * ML_W].astype(BF16)
    mlo_ref[...] = ml[:, 3 * ML_W:]

    naq_ref[...] = (pq * lax.rsqrt(_group_sumsq(pq, bd64) * (1.0 / HEAD_DIM) + EPS)
                    * na_gq_ref[...]).astype(BF16)
    nak_ref[...] = (pk * lax.rsqrt(_group_sumsq(pk, bd64) * (1.0 / HEAD_DIM) + EPS)
                    * na_gk_ref[...]).astype(BF16)

    cos = cos_ref[...]
    sin = sin_ref[...]
    rq = lax.rsqrt(_group_sumsq(qr, bd128) * (1.0 / MLA_QK) + EPS)
    rk = lax.rsqrt(_group_sumsq(kr, bd128) * (1.0 / MLA_QK) + EPS)
    kxs = kx * sin
    for g in range(MLA_HEADS):
        sl = slice(g * LANES, (g + 1) * LANES)
        mq_ref[:, sl] = (rq[:, sl] * (qr[:, sl] * mla_gq_ref[:, sl] * cos
                                      + qx[:, sl] * mla_gqs_ref[:, sl] * sin)).astype(BF16)
        mk_ref[:, sl] = (rk[:, sl] * (kr[:, sl] * mla_gk_ref[:, sl] * cos
                                      + kxs * mla_gks_ref[:, sl])).astype(BF16)


def _proj(l, xs, mods, prm, tabs, dims):
    T, D = xs.shape
    B, S, CTX = dims
    nt = T // ROW_TILE
    lat_tiles = S // ROW_TILE

    def grp(i):
        return jnp.minimum(i // lat_tiles, B)

    def tab_row(i):
        return jnp.where(i < B * lat_tiles, i % lat_tiles, lat_tiles + (i - B * lat_tiles))

    row = lambda w: pl.BlockSpec((ROW_TILE, w), lambda i: (i, 0))
    lay = lambda shape: pl.BlockSpec((None,) + shape, lambda i: (l,) + (0,) * len(shape))
    const = lambda shape: pl.BlockSpec(shape, lambda i: (0,) * len(shape))
    tab = pl.BlockSpec((ROW_TILE, LANES), lambda i: (tab_row(i), 0))
    out_w = [(NA_W, BF16), (NA_W, BF16), (2 * ML_W, F32), (ML_W, BF16), (ML_W, F32),
             (LANES, F32), (MLA_HEADS * LANES, BF16), (MLA_HEADS * LANES, BF16)]
    out_t = [NA_W, MLA_HEADS * MLA_V]
    return pl.pallas_call(
        _proj_kernel,
        grid=(nt,),
        in_specs=[
            row(D),
            pl.BlockSpec((None, None, 6, D), lambda i: (l, grp(i), 0, 0)),
            lay((1, D)),
            lay((D, D_IN_PAD)), lay((Q_LORA, MLA_HEADS * LANES)), lay((Q_LORA, MLA_HEADS * LANES)),
            lay((KV_LORA, MLA_HEADS * LANES)), lay((KV_LORA, MLA_HEADS * MLA_V)),
            lay((1, Q_LORA)), lay((1, KV_LORA)), lay((1, NA_W)), lay((1, NA_W)),
            lay((1, MLA_HEADS * LANES)), lay((1, MLA_HEADS * LANES)),
            lay((1, MLA_HEADS * LANES)), lay((1, MLA_HEADS * LANES)),
            const((MXU_DIM, MXU_DIM)), const((MXU_DIM, MXU_DIM)),
            tab, tab,
        ],
        out_specs=[row(w) for w, _ in out_w]
        + [pl.BlockSpec((w, ROW_TILE), lambda i: (0, i)) for w in out_t],
        out_shape=[jax.ShapeDtypeStruct((T, w), dt) for w, dt in out_w]
        + [jax.ShapeDtypeStruct((w, T), BF16) for w in out_t],
        compiler_params=_cparams(("arbitrary",)),
        name="proj",
    )(xs, mods, prm["g_mix"], prm["w_in"], prm["w_uq"], prm["w_uqs"], prm["w_uk"], prm["w_uv"],
      prm["gq"], prm["gkv"], prm["na_gq"], prm["na_gk"],
      prm["mla_gq"], prm["mla_gk"], prm["mla_gqs"], prm["mla_gks"],
      tabs["bd64"], tabs["bd128"], tabs["cos"], tabs["sin"])


def _short_conv(x, prev_blk, next_blk, first, last, w_ref, b_ref):
    n = x.shape[0]
    prev_row = jnp.where(first, 0.0, prev_blk[7:8, :])
    next_row = jnp.where(last, 0.0, next_blk[0:1, :])
    ridx = lax.broadcasted_iota(jnp.int32, x.shape, 0)
    xm1 = jnp.where(ridx == 0, prev_row, pltpu.roll(x, 1, 0))
    xp1 = jnp.where(ridx == n - 1, next_row, pltpu.roll(x, n - 1, 0))
    y = b_ref[...] + xm1 * w_ref[0:1, :] + x * w_ref[1:2, :] + xp1 * w_ref[2:3, :]
    y = y * _sigmoid(y)
    return y[:, :ML_W].astype(BF16), (y[:, ML_W:] * (HEAD_DIM ** -0.5)).astype(BF16)


def _pair_select(o0, o1):
    lane = lax.broadcasted_iota(jnp.int32, o0.shape, 1)
    return jnp.where(lane < HEAD_DIM, o0, o1)


def _head_q(q, e, packed):
    if packed:
        lane = lax.broadcasted_iota(jnp.int32, q.shape, 1)
        keep = (lane < HEAD_DIM) if e == 0 else (lane >= HEAD_DIM)
        return jnp.where(keep, q, jnp.zeros_like(q))
    return q[:, e * LANES:(e + 1) * LANES]


def _kcols(e, packed):
    return slice(0, LANES) if packed else slice(e * LANES, (e + 1) * LANES)


def _na_kernel(q_ref, k_ref, vt_ref, kc_ref, vct_ref, bias_ref, o_ref, *, nb):
    j = pl.program_id(1)
    start = pl.multiple_of(jnp.clip(j - 1, 0, nb - 3) * ATT_BLOCK, ATT_BLOCK)
    q = q_ref[...]
    lane = lax.broadcasted_iota(jnp.int32, (ATT_BLOCK, LANES), 1)
    scores = []
    for h in range(NA_HEADS):
        p, e = divmod(h, 2)
        cols = slice(p * LANES, (p + 1) * LANES)
        qp = q[:, cols]
        keep = (lane < HEAD_DIM) if e == 0 else (lane >= HEAD_DIM)
        qt = jnp.where(keep, qp, jnp.zeros_like(qp)).astype(F32).T.astype(BF16)
        s_loc = _dot(k_ref[pl.ds(start, NA_KEY_ROWS), cols], qt) + bias_ref[h].astype(F32)
        s_ctx = _dot(kc_ref[:, cols], qt)
        scores.append((s_loc, s_ctx))
    probs = []
    for s_loc, s_ctx in scores:
        m = jnp.maximum(jnp.max(s_loc, axis=0, keepdims=True), jnp.max(s_ctx, axis=0, keepdims=True))
        probs.append((jnp.exp(s_loc - m).astype(BF16), jnp.exp(s_ctx - m).astype(BF16)))
    outs = []
    for h, (p_loc, p_ctx) in enumerate(probs):
        rows = slice(h * HEAD_DIM, (h + 1) * HEAD_DIM)
        ot = (_dot(_flash_values(vt_ref[rows, pl.ds(start, NA_KEY_ROWS)]), p_loc)
              + _dot(_flash_values(vct_ref[rows, :]), p_ctx))
        outs.append(ot[:HEAD_DIM, :] / ot[HEAD_DIM:HEAD_DIM + 1, :])
    o_ref[...] = jnp.concatenate(outs, axis=0).T.astype(o_ref.dtype)


def _na(l, naq, nak, navt, bias, dims, with_ctx):
    B, S, CTX = dims
    nb = S // ATT_BLOCK
    ctx_blk0 = B * S // CTX
    n_steps = nb + 1 if with_ctx else nb
    rows_out = naq.shape[0] if with_ctx else B * S

    def qrow(b, j):
        return jnp.where(j < nb, b * nb + j, ctx_blk0 + b)

    def variant(j):
        return jnp.where(j == 0, 0, jnp.where(j == nb - 1, 2, jnp.where(j == nb, 3, 1)))

    kern = functools.partial(_na_kernel, nb=nb)
    return pl.pallas_call(
        kern,
        grid=(B, n_steps),
        in_specs=[
            pl.BlockSpec((ATT_BLOCK, NA_W), lambda b, j: (qrow(b, j), 0)),
            pl.BlockSpec((S, NA_W), lambda b, j: (b, 0)),
            pl.BlockSpec((NA_W, S), lambda b, j: (0, b)),
            pl.BlockSpec((CTX, NA_W), lambda b, j: (ctx_blk0 + b, 0)),
            pl.BlockSpec((NA_W, CTX), lambda b, j: (0, ctx_blk0 + b)),
            pl.BlockSpec((None, None, NA_HEADS, NA_KEY_ROWS, ATT_BLOCK), lambda b, j: (l, variant(j), 0, 0, 0)),
        ],
        out_specs=pl.BlockSpec((ATT_BLOCK, NA_W), lambda b, j: (qrow(b, j), 0)),
        out_shape=jax.ShapeDtypeStruct((rows_out, NA_W), BF16),
        compiler_params=_cparams(("arbitrary", "arbitrary")),
        name="na",
    )(naq, nak, navt, nak, navt, bias)


def _flash_values(v):
    return jnp.concatenate([v, jnp.ones((DEN_ROWS, v.shape[1]), BF16)], axis=0)


def _flash_queries(q_ref):
    q = q_ref[...]
    return [q[:, e * LANES:(e + 1) * LANES].astype(F32).T.astype(BF16) for e in range(2)]


def _flash_ctx_scores(qs, kc_ref):
    return [_dot(kc_ref[:, e * LANES:(e + 1) * LANES], qs[e]) for e in range(2)]


def _flash_ctx_init(sts, vc_ref, acc_ref, st_ref):
    ps = []
    for e, st in enumerate(sts):
        m = jnp.max(st, axis=0, keepdims=True)
        st_ref[e] = m
        ps.append(jnp.exp2(st - m).astype(BF16))
    for e, p in enumerate(ps):
        acc_ref[e] = _dot(_flash_values(vc_ref[e * HEAD_DIM:(e + 1) * HEAD_DIM, :]), p)


def _flash_finish(acc_ref, o_ref):
    ot = jnp.concatenate([acc_ref[e, :HEAD_DIM, :] / acc_ref[e, HEAD_DIM:HEAD_DIM + 1, :]
                          for e in range(2)], axis=0)
    o_ref[...] = ot.T.astype(o_ref.dtype)


def _flash_kernel(q_ref, km_ref, vm_ref, kc_ref, vc_ref, o_ref, *scratch, n_main):
    ns = FLASH_SLOTS
    s_refs = [scratch[2 * i:2 * i + 2] for i in range(ns)]
    mx_refs = [scratch[2 * ns + 2 * i:2 * ns + 2 * i + 2] for i in range(ns)]
    acc_ref, st_ref = scratch[4 * ns:4 * ns + 2]
    tk = FLASH_TK
    qs = _flash_queries(q_ref)

    def scores1(slot, r0, e):
        st = _dot(km_ref[pl.ds(r0, tk), e * LANES:(e + 1) * LANES], qs[e])
        s_refs[slot][e][...] = st
        mx_refs[slot][e][...] = jnp.max(st, axis=0, keepdims=True)

    def absorb1(slot, r0, e):
        m = st_ref[e]
        m_new = jnp.maximum(m, mx_refs[slot][e][...])
        st_ref[e] = m_new
        alpha = jnp.exp2(m - m_new)
        p = jnp.exp2(s_refs[slot][e][...] - m_new).astype(BF16)
        v = vm_ref[e * HEAD_DIM:(e + 1) * HEAD_DIM, pl.ds(r0, tk)]
        acc_ref[e] = alpha * acc_ref[e] + _dot(_flash_values(v), p)

    def scores(slot, r0):
        for e in range(2):
            scores1(slot, r0, e)

    def absorb(slot, r0):
        for e in range(2):
            absorb1(slot, r0, e)

    ctx_scores = _flash_ctx_scores(qs, kc_ref)
    scores(0, 0)
    _flash_ctx_init(ctx_scores, vc_ref, acc_ref, st_ref)

    def advance(c, r0):
        scores1((c + 1) % ns, r0 + tk, 0)
        absorb1(c % ns, r0, 1)
        scores1((c + 1) % ns, r0 + tk, 1)
        absorb1(c % ns, r0, 0)

    nu = FLASH_UNROLL

    def body(i, carry):
        r0 = pl.multiple_of(i * (nu * tk), nu * tk)
        for c in range(nu):
            advance(c, r0 + c * tk)
        return carry

    n_loop = (n_main - 1) // nu
    lax.fori_loop(0, n_loop, body, 0)
    for c in range(n_loop * nu, n_main - 1):
        advance(c, c * tk)
    absorb((n_main - 1) % ns, (n_main - 1) * tk)
    _flash_finish(acc_ref, o_ref)


def _flash_ctx_kernel(o_in_ref, q_ref, kc_ref, vc_ref, o_ref, acc_ref, st_ref):
    del o_in_ref
    _flash_ctx_init(_flash_ctx_scores(_flash_queries(q_ref), kc_ref), vc_ref, acc_ref, st_ref)
    _flash_finish(acc_ref, o_ref)


def _flash(mq, mk, mvt, dims, with_ctx):
    B, S, CTX = dims
    T = mq.shape[0] if with_ctx else B * S
    tq = FLASH_TQ
    nq = S // tq
    ctx_blk0 = B * S // CTX
    npair = MLA_HEADS // 2
    acc = lambda n: [pltpu.VMEM((2, HEAD_DIM + DEN_ROWS, n), F32), pltpu.VMEM((2, 1, n), F32)]
    out_shape = jax.ShapeDtypeStruct((T, MLA_HEADS * MLA_V), BF16)
    kc_spec = pl.BlockSpec((CTX, 2 * LANES), lambda b, p, *_: (ctx_blk0 + b, p))
    vc_spec = pl.BlockSpec((LANES, CTX), lambda b, p, *_: (p, ctx_blk0 + b))

    o = pl.pallas_call(
        functools.partial(_flash_kernel, n_main=S // FLASH_TK),
        grid=(B, npair, nq),
        in_specs=[
            pl.BlockSpec((tq, 2 * LANES), lambda b, p, j: (b * nq + j, p)),
            pl.BlockSpec((S, 2 * LANES), lambda b, p, j: (b, p)),
            pl.BlockSpec((LANES, S), lambda b, p, j: (p, b)),
            kc_spec, vc_spec,
        ],
        out_specs=pl.BlockSpec((tq, LANES), lambda b, p, j: (b * nq + j, p)),
        out_shape=out_shape,
        scratch_shapes=[pltpu.VMEM((FLASH_TK, tq), F32)] * (2 * FLASH_SLOTS)
        + [pltpu.VMEM((1, tq), F32)] * (2 * FLASH_SLOTS) + acc(tq),
        compiler_params=_cparams(("arbitrary", "arbitrary", "arbitrary")),
        name="flash",
    )(mq, mk, mvt, mk, mvt)
    if not with_ctx:
        return o
    return pl.pallas_call(
        _flash_ctx_kernel,
        grid=(B, npair),
        in_specs=[
            pl.BlockSpec(memory_space=pl.ANY),
            pl.BlockSpec((CTX, 2 * LANES), lambda b, p: (ctx_blk0 + b, p)),
            kc_spec, vc_spec,
        ],
        out_specs=pl.BlockSpec((CTX, LANES), lambda b, p: (ctx_blk0 + b, p)),
        out_shape=out_shape,
        scratch_shapes=acc(CTX),
        input_output_aliases={0: 0},
        compiler_params=_cparams(("arbitrary", "arbitrary")),
        name="flash_ctx",
    )(o, mq, mk, mvt)


def _mlstm_gates(d, gates, bgate, tri):
    gb = gates + bgate
    ls = jnp.minimum(gb, 0.0) - jnp.log1p(jnp.exp(-jnp.abs(gb)))
    lane = lax.broadcasted_iota(jnp.int32, gb.shape, 1)
    is_f = ((lane >= 4) & (lane < 8)) | ((lane >= 12) & (lane < 16))
    xg = jnp.where(lane < 16, jnp.where(is_f, ls, gb), 0.0)
    x1 = xg.astype(BF16)
    r1 = xg - x1.astype(F32)
    x2 = r1.astype(BF16)
    x3 = (r1 - x2.astype(F32)).astype(BF16)
    cum = _dot(tri, x1) + _dot(tri, x2) + _dot(tri, x3)
    return xg, cum, xg.T, cum.T, jnp.sum(xg, axis=0, keepdims=True)


def _mlstm_kernel(xf_ref, xpf_ref, xnf_ref, vf_ref, gf_ref, xb_ref, xpb_ref, xnb_ref, vb_ref, gb_ref,
                  cw_ref, cb_ref, bgate_ref, tril_ref, triu_ref, hf_ref, hb_ref, c_ref, m_ref,
                  *, n_ctx, n_lat):
    @pl.when(pl.program_id(1) == 0)
    def _():
        c_ref[...] = jnp.zeros_like(c_ref)
        m_ref[...] = jnp.zeros_like(m_ref)

    L = ML_CHUNK
    bgate = bgate_ref[...]
    step = pl.program_id(1)
    in_ctx = step < n_ctx
    pos = jnp.where(in_ctx, step, step - n_ctx)
    starts = pos == 0
    ends = pos == jnp.where(in_ctx, n_ctx, n_lat) - 1
    qk_f = _short_conv(xf_ref[...], xpf_ref[...], xnf_ref[...], starts, ends, cw_ref, cb_ref)
    qk_b = _short_conv(xb_ref[...], xpb_ref[...], xnb_ref[...], ends, starts, cw_ref, cb_ref)
    q = (qk_f[0], qk_b[0])
    k = (qk_f[1], qk_b[1])
    v = (vf_ref[...], vb_ref[...])
    gat = (_mlstm_gates(0, gf_ref[...], bgate, tril_ref[...]),
           _mlstm_gates(1, gb_ref[...], bgate, triu_ref[...]))
    si = lax.broadcasted_iota(jnp.int32, (L, L), 0)
    ti = lax.broadcasted_iota(jnp.int32, (L, L), 1)
    valid = (si <= ti, si >= ti)
    lane_l = lax.broadcasted_iota(jnp.int32, (L, LANES), 1)
    row_l = lax.broadcasted_iota(jnp.int32, (LANES, L), 0)
    vts = [[v[d][:, p * LANES:(p + 1) * LANES].astype(F32).T for p in range(ML_HEADS // 2)]
           for d in range(2)]
    chains = [(d, hd) for d in range(2) for hd in range(ML_HEADS)]

    st1 = []
    for d, hd in chains:
        p, e = divmod(hd, 2)
        xg, cum, xg_t, cum_t, tot = gat[d]
        ci = 8 * d + hd
        cf = 8 * d + 4 + hd
        in_head = (lane_l < HEAD_DIM) if e == 0 else (lane_l >= HEAD_DIM)
        in_rows = (row_l < HEAD_DIM) if e == 0 else (row_l >= HEAD_DIM)
        ones_row = HEAD_DIM if e == 0 else 0
        qp = q[d][:, p * LANES:(p + 1) * LANES]
        kp = k[d][:, p * LANES:(p + 1) * LANES]
        km = jnp.where(in_head, kp, jnp.zeros_like(kp))
        vat = jnp.where(in_rows, vts[d][p], jnp.where(row_l == ones_row, 1.0, 0.0))
        b_row = cum_t[cf:cf + 1, :]
        li_row = xg_t[ci:ci + 1, :]
        c_col = xg[:, ci:ci + 1] - cum[:, cf:cf + 1]
        m_prev = m_ref[d * ML_HEADS + hd:d * ML_HEADS + hd + 1, 0:1]
        g = b_row + m_prev
        dm = jnp.where(valid[d], b_row + c_col, NEG)
        mt = jnp.maximum(g, jnp.max(dm, axis=0, keepdims=True))
        b_tot = tot[:, cf:cf + 1]
        a_row = b_tot - b_row + li_row
        m_new = jnp.maximum(b_tot + m_prev, jnp.max(a_row, axis=1, keepdims=True))
        st1.append(dict(qp=qp, km=km, vat=vat, ones_row=ones_row, g=g, dm=dm, mt=mt,
                        m_new=m_new, decay=jnp.exp(b_tot + m_prev - m_new), a_row=a_row))

    for c in st1:
        c["qk"] = (_dot_nt(c["km"], c["qp"]) * jnp.exp(c["dm"] - c["mt"])).astype(BF16)

    hs = []
    for (d, hd), c in zip(chains, st1):
        cst = c_ref[d, hd]
        haug = (jnp.exp(c["g"] - c["mt"]) * _dot_nt(cst.astype(BF16), c["qp"])
                + _dot(c["vat"].astype(BF16), c["qk"]))
        den = haug[c["ones_row"]:c["ones_row"] + 1, :]
        hs.append(haug / jnp.maximum(jnp.abs(den), jnp.exp(-c["mt"])))
        wv = (c["vat"] * jnp.exp(c["a_row"] - c["m_new"])).astype(BF16)
        c_ref[d, hd] = c["decay"] * cst + _dot(wv, c["km"])
        r = d * ML_HEADS + hd
        m_ref[r:r + 1, :] = jnp.broadcast_to(c["m_new"], (1, LANES))

    for d, h_ref in enumerate((hf_ref, hb_ref)):
        for p in range(ML_HEADS // 2):
            pair = hs[d * ML_HEADS + 2 * p:d * ML_HEADS + 2 * p + 2]
            h_ref[:, p * LANES:(p + 1) * LANES] = jnp.where(row_l < HEAD_DIM, pair[0], pair[1]).T


def _mlstm(l, mlqk, mlv, misc, prm, tabs, dims):
    T, W = mlqk.shape
    B, S, CTX = dims
    L = ML_CHUNK
    n_ctx = CTX // L
    n_lat = S // L
    ctx0 = B * S // L

    def fwd(b, c):
        return jnp.where(c < n_ctx, ctx0 + b * n_ctx + c, b * n_lat + (c - n_ctx))

    def bwd(b, c):
        return jnp.where(c < n_ctx, ctx0 + b * n_ctx + (n_ctx - 1 - c), b * n_lat + (n_lat - 1 - (c - n_ctx)))

    sub = L // 8
    last8 = T // 8 - 1

    def specs(fn):
        return [pl.BlockSpec((L, W), lambda b, c: (fn(b, c), 0)),
                pl.BlockSpec((8, W), lambda b, c: (jnp.maximum(fn(b, c) * sub - 1, 0), 0)),
                pl.BlockSpec((8, W), lambda b, c: (jnp.minimum((fn(b, c) + 1) * sub, last8), 0)),
                pl.BlockSpec((L, ML_W), lambda b, c: (fn(b, c), 0)),
                pl.BlockSpec((L, LANES), lambda b, c: (fn(b, c), 0))]

    return pl.pallas_call(
        functools.partial(_mlstm_kernel, n_ctx=n_ctx, n_lat=n_lat),
        grid=(B, n_ctx + n_lat),
        in_specs=specs(fwd) + specs(bwd) + [
            pl.BlockSpec((None, 8, W), lambda b, c: (l, 0, 0)),
            pl.BlockSpec((None, 1, W), lambda b, c: (l, 0, 0)),
            pl.BlockSpec((None, 1, LANES), lambda b, c: (l, 0, 0)),
            pl.BlockSpec((L, L), lambda b, c: (0, 0)),
            pl.BlockSpec((L, L), lambda b, c: (0, 0)),
        ],
        out_specs=[pl.BlockSpec((L, ML_W), lambda b, c: (fwd(b, c), 0)),
                   pl.BlockSpec((L, ML_W), lambda b, c: (bwd(b, c), 0))],
        out_shape=[jax.ShapeDtypeStruct((T, ML_W), F32)] * 2,
        scratch_shapes=[pltpu.VMEM((2, ML_HEADS, LANES, LANES), F32), pltpu.VMEM((8, LANES), F32)],
        compiler_params=_cparams(("arbitrary", "arbitrary")),
        name="mlstm",
    )(mlqk, mlqk, mlqk, mlv, misc, mlqk, mlqk, mlqk, mlv, misc,
      prm["conv_w"], prm["conv_b"], prm["b_gate"], tabs["tril"], tabs["triu"])


def _ffn_chunks(ff):
    step = 2 * MXU_DIM
    starts = list(range(0, ff - ff % step, step)) or [0]
    return [(c, (ff - c) if c == starts[-1] else step) for c in starts]


def _out_kernel(x_ref, mod_ref, ona_ref, omla_ref, hf_ref, hb_ref, mlo_ref, hg_ref, bd64_ref,
                w_out_ref, g_ref, wa_ref, wb_ref, wo_ref, o_ref):
    x = x_ref[...]
    gate1 = mod_ref[2:3, :]
    shift2 = mod_ref[3:4, :]
    scale2 = mod_ref[4:5, :]
    gate2 = mod_ref[5:6, :]

    h = hf_ref[...] + hb_ref[...]
    hn = h * lax.rsqrt(_group_sumsq(h, bd64_ref[...]) * (1.0 / HEAD_DIM) + EPS) * hg_ref[...]
    ob = (hn * _sigmoid(mlo_ref[...])).astype(BF16)
    mix = _dot(jnp.concatenate([ona_ref[...], ob, omla_ref[...]], axis=-1), w_out_ref[...])
    x1 = x + gate1 * mix

    h2 = ((_rms_rows(x1) * g_ref[...]) * (1.0 + scale2) + shift2).astype(BF16)
    acc = jnp.zeros_like(x1)
    for c0, w in _ffn_chunks(wa_ref.shape[1]):
        a = _dot(h2, wa_ref[:, c0:c0 + w])
        b = _dot(h2, wb_ref[:, c0:c0 + w])
        act = (a * _sigmoid(a) * b).astype(BF16)
        acc = acc + _dot(act, wo_ref[c0:c0 + w, :])
    o_ref[...] = x1 + gate2 * acc


def _out(l, xs, mods, ona, omla, hf, hb, mlo, prm, tabs, dims, with_ctx):
    D = xs.shape[1]
    B, S, CTX = dims
    T = xs.shape[0] if with_ctx else B * S
    nt = T // ROW_TILE
    lat_tiles = S // ROW_TILE
    ff = prm["w_a"].shape[-1]

    def grp(i):
        return jnp.minimum(i // lat_tiles, B)

    row = lambda w: pl.BlockSpec((ROW_TILE, w), lambda i: (i, 0))
    lay = lambda shape: pl.BlockSpec((None,) + shape, lambda i: (l,) + (0,) * len(shape))
    big = lambda shape: pl.BlockSpec((None,) + shape, lambda i: (l,) + (0,) * len(shape),
                                     pipeline_mode=pl.Buffered(1))
    return pl.pallas_call(
        _out_kernel,
        grid=(nt,),
        in_specs=[
            row(D),
            pl.BlockSpec((None, None, 6, D), lambda i: (l, grp(i), 0, 0)),
            row(NA_W), row(MLA_HEADS * MLA_V), row(ML_W), row(ML_W), row(ML_W),
            lay((1, ML_W)),
            pl.BlockSpec((MXU_DIM, MXU_DIM), lambda i: (0, 0)),
            big((NA_W + ML_W + MLA_HEADS * MLA_V, D)),
            lay((1, D)),
            big((D, ff)), big((D, ff)), big((ff, D)),
        ],
        out_specs=row(D),
        out_shape=jax.ShapeDtypeStruct((T, D), F32),
        compiler_params=_cparams(("arbitrary",)),
        name="out_ffn",
    )(xs, mods, ona, omla, hf, hb, mlo, prm["head_gain"], tabs["bd64"],
      prm["w_out"], prm["g_ffn"], prm["w_a"], prm["w_b"], prm["w_fo"])


def _block_diag_ones(group):
    idx = np.arange(MXU_DIM) // group
    return jnp.asarray((idx[:, None] == idx[None, :]).astype(np.float32), dtype=BF16)


def _rope_tables(S, n_ctx_rows):
    n_freq = MLA_ROPE // 4
    inv = ROPE_BASE ** (-jnp.arange(n_freq, dtype=F32) / n_freq)
    t = jnp.arange(S, dtype=jnp.int32)
    row = (t // GRID_W).astype(F32)
    col = (t % GRID_W).astype(F32)
    ang = jnp.concatenate([row[:, None] * inv, col[:, None] * inv], axis=-1)
    ang = jnp.concatenate([ang, jnp.zeros((n_ctx_rows, MLA_ROPE // 2), F32)], axis=0)
    cos, sin = jnp.cos(ang), jnp.sin(ang)
    n = S + n_ctx_rows
    half = MLA_ROPE // 2
    ones = jnp.ones((n, MLA_NOPE), F32)
    z = lambda w: jnp.zeros((n, w), F32)
    tail = LANES - MLA_NOPE - MLA_ROPE
    del half
    cos_t = jnp.concatenate([ones, cos, cos, jnp.ones((n, tail), F32)], axis=-1)
    sin_t = jnp.concatenate([z(MLA_NOPE), -sin, sin, z(tail)], axis=-1)
    return cos_t, sin_t


def _na_bias(rpb, rows):
    depth, H = rpb.shape[:2]
    qr, kr_n = ATT_BLOCK // GRID_W, NA_KEY_ROWS // GRID_W
    nb = rows // qr
    cols = np.arange(GRID_W)
    cs = np.clip(cols - NA_WIN_C // 2, 0, GRID_W - NA_WIN_C)
    col_ok = (cols[None, :] >= cs[:, None]) & (cols[None, :] < cs[:, None] + NA_WIN_C)
    dc = np.clip(cols[None, :] - cols[:, None] + NA_WIN_C - 1, 0, 2 * NA_WIN_C - 2)
    dr_all, ok_all = [], []
    for j in (0, 1, nb - 1):
        ks = int(np.clip(j - 1, 0, nb - 3)) * qr
        r = j * qr + np.arange(qr)
        rs = np.clip(r - NA_WIN_R // 2, 0, rows - NA_WIN_R)
        kr = ks + np.arange(kr_n)
        ok_all.append((kr[None, :] >= rs[:, None]) & (kr[None, :] < rs[:, None] + NA_WIN_R))
        dr_all.append(np.clip(kr[None, :] - r[:, None] + NA_WIN_R - 1, 0, 2 * NA_WIN_R - 2))
    dr_all = np.stack(dr_all)
    ok_all = np.stack(ok_all)
    t1 = jnp.take(rpb, jnp.asarray(dc.reshape(-1)), axis=3).reshape(depth, H, 2 * NA_WIN_R - 1, GRID_W, GRID_W)
    t1 = jnp.where(jnp.asarray(col_ok), t1, NEG).astype(BF16)
    t2 = jnp.take(t1, jnp.asarray(dr_all.reshape(-1)), axis=2).reshape(depth, H, 3, qr, kr_n, GRID_W, GRID_W)
    t2 = jnp.where(jnp.asarray(ok_all)[None, None, :, :, :, None, None], t2, NEG)
    bias = t2.transpose(0, 2, 1, 4, 6, 3, 5).reshape(depth, 3, H, NA_KEY_ROWS, ATT_BLOCK)
    none = jnp.full((depth, 1, H, NA_KEY_ROWS, ATT_BLOCK), NEG, BF16)
    return jnp.concatenate([bias, none], axis=1)


def _prepare(w_in, w_uq, w_ukv, w_out, w_ffn_in, w_ffn_out, g_mix, g_ffn, b_gate, na_qk_gain,
             ml_conv_w, ml_conv_b, ml_head_gain, mla_gq, mla_gkv, mla_qk_gain):
    depth, D, _ = w_in.shape
    o_gate = C_MLO + ML_W
    o_ql = o_gate + 4 * ML_HEADS
    o_kr = o_ql + Q_LORA + KV_LORA
    half = MLA_ROPE // 2
    zc = lambda w: jnp.zeros((depth, D, w), w_in.dtype)
    w_in_p = jnp.concatenate([
        w_in[..., :o_gate], w_in[..., o_ql:o_kr],
        w_in[..., o_gate:o_ql], zc(MISC_ROPE_LANE - 4 * ML_HEADS),
        w_in[..., o_kr:], zc(LANES - MISC_ROPE_LANE - MLA_ROPE),
        zc(MISC_ROPE_LANE), w_in[..., o_kr + half:], w_in[..., o_kr:o_kr + half],
        zc(LANES - MISC_ROPE_LANE - MLA_ROPE)], axis=-1).astype(BF16)
    pad_h = lambda a, w: jnp.pad(a, [(0, 0)] * (a.ndim - 1) + [(0, LANES - w)])

    def partner(a):
        lo, hi = a[..., MLA_NOPE:MLA_NOPE + half], a[..., MLA_NOPE + half:MLA_QK]
        return jnp.concatenate([jnp.zeros_like(a[..., :MLA_NOPE]), hi, lo,
                                jnp.zeros_like(a[..., MLA_QK:])], axis=-1)

    w_uq_h = pad_h(w_uq.reshape(depth, Q_LORA, MLA_HEADS, MLA_QK), MLA_QK)
    w_uq_p = w_uq_h.reshape(depth, Q_LORA, MLA_HEADS * LANES).astype(BF16)
    w_uq_s = partner(w_uq_h).reshape(depth, Q_LORA, MLA_HEADS * LANES).astype(BF16)
    ukv = w_ukv.reshape(depth, KV_LORA, MLA_HEADS, MLA_NOPE + MLA_V)
    w_uk_p = pad_h(ukv[..., :MLA_NOPE], MLA_NOPE).reshape(depth, KV_LORA, MLA_HEADS * LANES).astype(BF16)
    w_uv = ukv[..., MLA_NOPE:].reshape(depth, KV_LORA, MLA_HEADS * MLA_V).astype(BF16)
    ff = w_ffn_out.shape[1]
    mla_g = pad_h(mla_qk_gain, MLA_QK)
    return {
        "w_in": w_in_p, "w_uq": w_uq_p, "w_uqs": w_uq_s, "w_uk": w_uk_p, "w_uv": w_uv,
        "w_out": w_out.astype(BF16),
        "w_a": w_ffn_in[..., :ff].astype(BF16), "w_b": w_ffn_in[..., ff:].astype(BF16),
        "w_fo": w_ffn_out.astype(BF16),
        "g_mix": g_mix[:, None, :], "g_ffn": g_ffn[:, None, :],
        "gq": mla_gq[:, None, :], "gkv": mla_gkv[:, None, :],
        "na_gq": jnp.tile(na_qk_gain[:, 0:1, :], (1, 1, NA_HEADS)) * NA_SCALE,
        "na_gk": jnp.tile(na_qk_gain[:, 1:2, :], (1, 1, NA_HEADS)),
        "mla_gq": jnp.tile(mla_g[:, 0:1, :], (1, 1, MLA_HEADS)) * (MLA_SCALE * LOG2E),
        "mla_gk": jnp.tile(mla_g[:, 1:2, :], (1, 1, MLA_HEADS)),
        "mla_gqs": jnp.tile(partner(mla_g[:, 0:1, :]), (1, 1, MLA_HEADS)) * (MLA_SCALE * LOG2E),
        "mla_gks": jnp.tile(partner(mla_g[:, 1:2, :]), (1, 1, MLA_HEADS)),
        "head_gain": ml_head_gain.reshape(depth, 1, ML_W),
        "b_gate": pad_h(b_gate, 4 * ML_HEADS)[:, None, :],
        "conv_w": jnp.pad(ml_conv_w, ((0, 0), (0, 8 - ml_conv_w.shape[1]), (0, 0))),
        "conv_b": ml_conv_b[:, None, :],
    }


def kernel(x, c, ctx, c_ctx, w_mod, b_mod, g_mix, g_ffn, w_in, b_gate, na_qk_gain, na_rpb,
           ml_conv_w, ml_conv_b, ml_head_gain, mla_gq, mla_gkv, w_uq, w_ukv, mla_qk_gain,
           w_out, w_ffn_in, w_ffn_out):
    B, S, D = x.shape
    CTX = ctx.shape[1]
    depth = w_in.shape[0]
    assert CTX == ATT_BLOCK and S % ROW_TILE == 0 and (B * CTX) % ROW_TILE == 0
    assert S % (2 * FLASH_TK) == 0 and S % FLASH_TQ == 0 and S // ATT_BLOCK >= 3 and B + 1 <= 8
    dims = (B, S, CTX)

    prm = _prepare(w_in, w_uq, w_ukv, w_out, w_ffn_in, w_ffn_out, g_mix, g_ffn, b_gate, na_qk_gain,
                   ml_conv_w, ml_conv_b, ml_head_gain, mla_gq, mla_gkv, mla_qk_gain)
    cos_t, sin_t = _rope_tables(S, B * CTX)
    tri = np.tril(np.ones((ML_CHUNK, ML_CHUNK), np.float32))
    tabs = {
        "bd64": _block_diag_ones(HEAD_DIM), "bd128": _block_diag_ones(LANES),
        "cos": cos_t, "sin": sin_t,
        "tril": jnp.asarray(tri, dtype=BF16), "triu": jnp.asarray(tri.T, dtype=BF16),
    }
    bias = _na_bias(na_rpb, S // GRID_W)

    cond8 = jnp.concatenate([c, c_ctx[None, :], jnp.zeros((8 - B - 1, D), F32)], axis=0)
    mods = _modulation(cond8, w_mod, b_mod)
    mods = mods.reshape(depth, 8, 6, D)

    xs = jnp.concatenate([x.reshape(B * S, D), ctx.reshape(B * CTX, D)], axis=0)
    for l in range(depth):
        with_ctx = l < depth - 1
        naq, nak, mlqk, mlv, mlo, misc, mq, mk, navt, mvt = _proj(l, xs, mods, prm, tabs, dims)
        ona = _na(l, naq, nak, navt, bias, dims, with_ctx)
        omla = _flash(mq, mk, mvt, dims, with_ctx)
        hf, hb = _mlstm(l, mlqk, mlv, misc, prm, tabs, dims)
        xs = _out(l, xs, mods, ona, omla, hf, hb, mlo, prm, tabs, dims, with_ctx)
    return xs.reshape(B, S, D)
```

```python
import functools

import numpy as np
import jax
import jax.numpy as jnp
from jax import lax
from jax.experimental import pallas as pl
from jax.experimental.pallas import tpu as pltpu

GRID_W = 64
HEAD_DIM = 64
NA_HEADS = 4
NA_WIN_R = 8
NA_WIN_C = 16
ML_HEADS = 4
ML_CHUNK = 128
ML_STEP_CHUNKS = 2
MLA_HEADS = 8
MLA_NOPE = 64
MLA_ROPE = 32
MLA_V = 64
Q_LORA = 384
KV_LORA = 256
ROPE_BASE = 10000.0
EPS = 1e-6
NA_W = NA_HEADS * HEAD_DIM
ML_W = ML_HEADS * HEAD_DIM
MLA_QK = MLA_NOPE + MLA_ROPE
NA_SCALE = HEAD_DIM ** -0.5
MLA_SCALE = MLA_QK ** -0.5
LOG2E = 1.4426950408889634

LANES = 128
MXU_DIM = 256
VMEM_LIMIT = 56 * 1024 * 1024

ROW_TILE = 512
ATT_BLOCK = 256
NA_KEY_ROWS = 768
FLASH_TK = 512
FLASH_TQ = 1024
FLASH_SLOTS = 2
FLASH_UNROLL = 4
NEG = -1e30
DEN_ROWS = 16

C_NAQ, C_NAK, C_NAV = 0, 256, 512
C_MLQK, C_MLV, C_MLO = 768, 1280, 1536
C_QL = 1792
C_KVL = C_QL + Q_LORA
C_MISC = C_KVL + KV_LORA
C_MISC2 = C_MISC + LANES
D_IN_PAD = C_MISC2 + LANES
MISC_ROPE_LANE = 64

BF16 = jnp.bfloat16
F32 = jnp.float32


def _cparams(sem):
    return pltpu.CompilerParams(dimension_semantics=sem, vmem_limit_bytes=VMEM_LIMIT)


def _dot(a, b):
    return jnp.dot(a, b, preferred_element_type=F32)


def _dot_nt(a, b):
    return lax.dot_general(a, b, (((1,), (1,)), ((), ())), preferred_element_type=F32)


def _dot_tn(a, b):
    return lax.dot_general(a, b, (((0,), (0,)), ((), ())), preferred_element_type=F32)


def _split2(x):
    hi = x.astype(BF16)
    lo = (x - hi.astype(F32)).astype(BF16)
    return hi, lo


def _group_sumsq(x, bd):
    x2 = x * x
    hi, lo = _split2(x2)
    outs = []
    for c in range(x.shape[1] // MXU_DIM):
        sl = slice(c * MXU_DIM, (c + 1) * MXU_DIM)
        outs.append(_dot(hi[:, sl], bd) + _dot(lo[:, sl], bd))
    return outs[0] if len(outs) == 1 else jnp.concatenate(outs, axis=-1)


def _rms_rows(x):
    return x * lax.rsqrt(jnp.mean(x * x, axis=-1, keepdims=True) + EPS)


def _sigmoid(x):
    return 1.0 / (1.0 + jnp.exp(-x))


def _mod_kernel(c_ref, w_ref, b_ref, o_ref):
    c = c_ref[...]
    a = c * _sigmoid(c)
    o_ref[...] = jnp.dot(a, w_ref[...], preferred_element_type=F32,
                         precision=lax.Precision.HIGHEST) + b_ref[...]


def _modulation(cond8, w_mod, b_mod):
    depth, d, d6 = w_mod.shape
    n = d6 // d
    return pl.pallas_call(
        _mod_kernel,
        grid=(depth, n),
        in_specs=[
            pl.BlockSpec((8, d), lambda l, j: (0, 0)),
            pl.BlockSpec((None, d, d), lambda l, j: (l, 0, j)),
            pl.BlockSpec((None, 1, d), lambda l, j: (l, 0, j)),
        ],
        out_specs=pl.BlockSpec((None, 8, d), lambda l, j: (l, 0, j)),
        out_shape=jax.ShapeDtypeStruct((depth, 8, d6), F32),
        compiler_params=_cparams(("arbitrary", "arbitrary")),
        name="modulation",
    )(cond8, w_mod, b_mod.reshape(depth, 1, d6))


def _proj_kernel(x_ref, mod_ref, g_ref, w_in_ref, w_uq_ref, w_uqs_ref, w_uk_ref, w_uv_ref,
                 gq_ref, gkv_ref, na_gq_ref, na_gk_ref, mla_gq_ref, mla_gk_ref, mla_gqs_ref, mla_gks_ref,
                 bd64_ref, bd128_ref, cos_ref, sin_ref,
                 naq_ref, nak_ref, mlqk_ref, mlv_ref, mlo_ref, misc_ref,
                 mq_ref, mk_ref, nav_ref, mv_ref):
    x = x_ref[...]
    shift = mod_ref[0:1, :]
    scale = mod_ref[1:2, :]
    h = (_rms_rows(x) * g_ref[...]) * (1.0 + scale) + shift
    hb = h.astype(BF16)

    def proj(c0, width):
        return _dot(hb, w_in_ref[:, c0:c0 + width])

    bd64 = bd64_ref[...]
    bd128 = bd128_ref[...]

    lat = proj(C_QL, D_IN_PAD - C_QL)
    ql = lat[:, :Q_LORA]
    kvl = lat[:, C_KVL - C_QL:C_MISC - C_QL]
    misc = lat[:, C_MISC - C_QL:C_MISC2 - C_QL]
    kx = lat[:, C_MISC2 - C_QL:]
    misc_ref[...] = misc
    qn = (_rms_rows(ql) * gq_ref[...]).astype(BF16)
    kvn = (_rms_rows(kvl) * gkv_ref[...]).astype(BF16)

    na = proj(C_NAQ, 3 * NA_W)
    pq = na[:, :NA_W]
    pk = na[:, NA_W:2 * NA_W]
    nav_ref[...] = na[:, 2 * NA_W:].T.astype(BF16)

    qr = _dot(qn, w_uq_ref[...])
    qx = _dot(qn, w_uqs_ref[...])
    lane = lax.broadcasted_iota(jnp.int32, misc.shape, 1)
    krope = jnp.where((lane >= MISC_ROPE_LANE) & (lane < MISC_ROPE_LANE + MLA_ROPE), misc, 0.0)
    kr = _dot(kvn, w_uk_ref[...]) + jnp.concatenate([krope] * MLA_HEADS, axis=-1)
    mv_ref[...] = _dot(kvn, w_uv_ref[...]).T.astype(BF16)

    ml = proj(C_MLQK, 4 * ML_W)
    mlqk_ref[...] = ml[:, :2 * ML_W]
    mlv_ref[...] = ml[:, 2 * ML_W:3 * ML_W].astype(BF16)
    mlo_ref[...] = ml[:, 3 * ML_W:]

    naq_ref[...] = (pq * lax.rsqrt(_group_sumsq(pq, bd64) * (1.0 / HEAD_DIM) + EPS)
                    * na_gq_ref[...]).astype(BF16)
    nak_ref[...] = (pk * lax.rsqrt(_group_sumsq(pk, bd64) * (1.0 / HEAD_DIM) + EPS)
                    * na_gk_ref[...]).astype(BF16)

    cos = cos_ref[...]
    sin = sin_ref[...]
    rq = lax.rsqrt(_group_sumsq(qr, bd128) * (1.0 / MLA_QK) + EPS)
    rk = lax.rsqrt(_group_sumsq(kr, bd128) * (1.0 / MLA_QK) + EPS)
    kxs = kx * sin
    for g in range(MLA_HEADS):
        sl = slice(g * LANES, (g + 1) * LANES)
        mq_ref[:, sl] = (rq[:, sl] * (qr[:, sl] * mla_gq_ref[:, sl] * cos
                                      + qx[:, sl] * mla_gqs_ref[:, sl] * sin)).astype(BF16)
        mk_ref[:, sl] = (rk[:, sl] * (kr[:, sl] * mla_gk_ref[:, sl] * cos
                                      + kxs * mla_gks_ref[:, sl])).astype(BF16)


def _proj(l, xs, mods, prm, tabs, dims):
    T, D = xs.shape
    B, S, CTX = dims
    nt = T // ROW_TILE
    lat_tiles = S // ROW_TILE

    def grp(i):
        return jnp.minimum(i // lat_tiles, B)

    def tab_row(i):
        return jnp.where(i < B * lat_tiles, i % lat_tiles, lat_tiles + (i - B * lat_tiles))

    row = lambda w: pl.BlockSpec((ROW_TILE, w), lambda i: (i, 0))
    lay = lambda shape: pl.BlockSpec((None,) + shape, lambda i: (l,) + (0,) * len(shape))
    const = lambda shape: pl.BlockSpec(shape, lambda i: (0,) * len(shape))
    tab = pl.BlockSpec((ROW_TILE, LANES), lambda i: (tab_row(i), 0))
    out_w = [(NA_W, BF16), (NA_W, BF16), (2 * ML_W, F32), (ML_W, BF16), (ML_W, F32),
             (LANES, F32), (MLA_HEADS * LANES, BF16), (MLA_HEADS * LANES, BF16)]
    out_t = [NA_W, MLA_HEADS * MLA_V]
    return pl.pallas_call(
        _proj_kernel,
        grid=(nt,),
        in_specs=[
            row(D),
            pl.BlockSpec((None, None, 6, D), lambda i: (l, grp(i), 0, 0)),
            lay((1, D)),
            lay((D, D_IN_PAD)), lay((Q_LORA, MLA_HEADS * LANES)), lay((Q_LORA, MLA_HEADS * LANES)),
            lay((KV_LORA, MLA_HEADS * LANES)), lay((KV_LORA, MLA_HEADS * MLA_V)),
            lay((1, Q_LORA)), lay((1, KV_LORA)), lay((1, NA_W)), lay((1, NA_W)),
            lay((1, MLA_HEADS * LANES)), lay((1, MLA_HEADS * LANES)),
            lay((1, MLA_HEADS * LANES)), lay((1, MLA_HEADS * LANES)),
            const((MXU_DIM, MXU_DIM)), const((MXU_DIM, MXU_DIM)),
            tab, tab,
        ],
        out_specs=[row(w) for w, _ in out_w]
        + [pl.BlockSpec((w, ROW_TILE), lambda i: (0, i)) for w in out_t],
        out_shape=[jax.ShapeDtypeStruct((T, w), dt) for w, dt in out_w]
        + [jax.ShapeDtypeStruct((w, T), BF16) for w in out_t],
        compiler_params=_cparams(("arbitrary",)),
        name="proj",
    )(xs, mods, prm["g_mix"], prm["w_in"], prm["w_uq"], prm["w_uqs"], prm["w_uk"], prm["w_uv"],
      prm["gq"], prm["gkv"], prm["na_gq"], prm["na_gk"],
      prm["mla_gq"], prm["mla_gk"], prm["mla_gqs"], prm["mla_gks"],
      tabs["bd64"], tabs["bd128"], tabs["cos"], tabs["sin"])


def _short_conv(x, prev_blk, next_blk, first, last, w_ref, b_ref):
    n = x.shape[0]
    prev_row = jnp.where(first, 0.0, prev_blk[7:8, :])
    next_row = jnp.where(last, 0.0, next_blk[0:1, :])
    ridx = lax.broadcasted_iota(jnp.int32, x.shape, 0)
    xm1 = jnp.where(ridx == 0, prev_row, pltpu.roll(x, 1, 0))
    xp1 = jnp.where(ridx == n - 1, next_row, pltpu.roll(x, n - 1, 0))
    y = b_ref[...] + xm1 * w_ref[0:1, :] + x * w_ref[1:2, :] + xp1 * w_ref[2:3, :]
    y = y * _sigmoid(y)
    return y[:, :ML_W].astype(BF16), (y[:, ML_W:] * (HEAD_DIM ** -0.5)).astype(BF16)


def _pair_select(o0, o1):
    lane = lax.broadcasted_iota(jnp.int32, o0.shape, 1)
    return jnp.where(lane < HEAD_DIM, o0, o1)


def _head_q(q, e, packed):
    if packed:
        lane = lax.broadcasted_iota(jnp.int32, q.shape, 1)
        keep = (lane < HEAD_DIM) if e == 0 else (lane >= HEAD_DIM)
        return jnp.where(keep, q, jnp.zeros_like(q))
    return q[:, e * LANES:(e + 1) * LANES]


def _kcols(e, packed):
    return slice(0, LANES) if packed else slice(e * LANES, (e + 1) * LANES)


def _na_kernel(q_ref, k_ref, vt_ref, kc_ref, vct_ref, bias_ref, o_ref, *, nb):
    j = pl.program_id(1)
    start = pl.multiple_of(jnp.clip(j - 1, 0, nb - 3) * ATT_BLOCK, ATT_BLOCK)
    q = q_ref[...]
    lane = lax.broadcasted_iota(jnp.int32, (ATT_BLOCK, LANES), 1)
    scores = []
    for h in range(NA_HEADS):
        p, e = divmod(h, 2)
        cols = slice(p * LANES, (p + 1) * LANES)
        qp = q[:, cols]
        keep = (lane < HEAD_DIM) if e == 0 else (lane >= HEAD_DIM)
        qt = jnp.where(keep, qp, jnp.zeros_like(qp)).astype(F32).T.astype(BF16)
        s_loc = _dot(k_ref[pl.ds(start, NA_KEY_ROWS), cols], qt) + bias_ref[h].astype(F32)
        s_ctx = _dot(kc_ref[:, cols], qt)
        scores.append((s_loc, s_ctx))
    probs = []
    for s_loc, s_ctx in scores:
        m = jnp.maximum(jnp.max(s_loc, axis=0, keepdims=True), jnp.max(s_ctx, axis=0, keepdims=True))
        probs.append((jnp.exp(s_loc - m).astype(BF16), jnp.exp(s_ctx - m).astype(BF16)))
    outs = []
    for h, (p_loc, p_ctx) in enumerate(probs):
        rows = slice(h * HEAD_DIM, (h + 1) * HEAD_DIM)
        ot = (_dot(_flash_values(vt_ref[rows, pl.ds(start, NA_KEY_ROWS)]), p_loc)
              + _dot(_flash_values(vct_ref[rows, :]), p_ctx))
        outs.append(ot[:HEAD_DIM, :] / ot[HEAD_DIM:HEAD_DIM + 1, :])
    o_ref[...] = jnp.concatenate(outs, axis=0).T.astype(o_ref.dtype)


def _na(l, naq, nak, navt, bias, dims, with_ctx):
    B, S, CTX = dims
    nb = S // ATT_BLOCK
    ctx_blk0 = B * S // CTX
    n_steps = nb + 1 if with_ctx else nb
    rows_out = naq.shape[0] if with_ctx else B * S

    def qrow(b, j):
        return jnp.where(j < nb, b * nb + j, ctx_blk0 + b)

    def variant(j):
        return jnp.where(j == 0, 0, jnp.where(j == nb - 1, 2, jnp.where(j == nb, 3, 1)))

    kern = functools.partial(_na_kernel, nb=nb)
    return pl.pallas_call(
        kern,
        grid=(B, n_steps),
        in_specs=[
            pl.BlockSpec((ATT_BLOCK, NA_W), lambda b, j: (qrow(b, j), 0)),
            pl.BlockSpec((S, NA_W), lambda b, j: (b, 0)),
            pl.BlockSpec((NA_W, S), lambda b, j: (0, b)),
            pl.BlockSpec((CTX, NA_W), lambda b, j: (ctx_blk0 + b, 0)),
            pl.BlockSpec((NA_W, CTX), lambda b, j: (0, ctx_blk0 + b)),
            pl.BlockSpec((None, None, NA_HEADS, NA_KEY_ROWS, ATT_BLOCK), lambda b, j: (l, variant(j), 0, 0, 0)),
        ],
        out_specs=pl.BlockSpec((ATT_BLOCK, NA_W), lambda b, j: (qrow(b, j), 0)),
        out_shape=jax.ShapeDtypeStruct((rows_out, NA_W), BF16),
        compiler_params=_cparams(("arbitrary", "arbitrary")),
        name="na",
    )(naq, nak, navt, nak, navt, bias)


def _flash_values(v):
    return jnp.concatenate([v, jnp.ones((DEN_ROWS, v.shape[1]), BF16)], axis=0)


def _flash_queries(q_ref):
    q = q_ref[...]
    return [q[:, e * LANES:(e + 1) * LANES].astype(F32).T.astype(BF16) for e in range(2)]


def _flash_ctx_scores(qs, kc_ref):
    return [_dot(kc_ref[:, e * LANES:(e + 1) * LANES], qs[e]) for e in range(2)]


def _flash_ctx_init(sts, vc_ref, acc_ref, st_ref):
    ps = []
    for e, st in enumerate(sts):
        m = jnp.max(st, axis=0, keepdims=True)
        st_ref[e] = m
        ps.append(jnp.exp2(st - m).astype(BF16))
    for e, p in enumerate(ps):
        acc_ref[e] = _dot(_flash_values(vc_ref[e * HEAD_DIM:(e + 1) * HEAD_DIM, :]), p)


def _flash_finish(acc_ref, o_ref):
    ot = jnp.concatenate([acc_ref[e, :HEAD_DIM, :] / acc_ref[e, HEAD_DIM:HEAD_DIM + 1, :]
                          for e in range(2)], axis=0)
    o_ref[...] = ot.T.astype(o_ref.dtype)


def _flash_kernel(q_ref, km_ref, vm_ref, kc_ref, vc_ref, o_ref, *scratch, n_main):
    ns = FLASH_SLOTS
    s_refs = [scratch[2 * i:2 * i + 2] for i in range(ns)]
    mx_refs = [scratch[2 * ns + 2 * i:2 * ns + 2 * i + 2] for i in range(ns)]
    acc_ref, st_ref = scratch[4 * ns:4 * ns + 2]
    tk = FLASH_TK
    qs = _flash_queries(q_ref)

    def scores1(slot, r0, e):
        st = _dot(km_ref[pl.ds(r0, tk), e * LANES:(e + 1) * LANES], qs[e])
        s_refs[slot][e][...] = st
        mx_refs[slot][e][...] = jnp.max(st, axis=0, keepdims=True)

    def absorb1(slot, r0, e):
        m = st_ref[e]
        m_new = jnp.maximum(m, mx_refs[slot][e][...])
        st_ref[e] = m_new
        alpha = jnp.exp2(m - m_new)
        p = jnp.exp2(s_refs[slot][e][...] - m_new).astype(BF16)
        v = vm_ref[e * HEAD_DIM:(e + 1) * HEAD_DIM, pl.ds(r0, tk)]
        acc_ref[e] = alpha * acc_ref[e] + _dot(_flash_values(v), p)

    def scores(slot, r0):
        for e in range(2):
            scores1(slot, r0, e)

    def absorb(slot, r0):
        for e in range(2):
            absorb1(slot, r0, e)

    ctx_scores = _flash_ctx_scores(qs, kc_ref)
    scores(0, 0)
    _flash_ctx_init(ctx_scores, vc_ref, acc_ref, st_ref)

    def advance(c, r0):
        scores1((c + 1) % ns, r0 + tk, 0)
        absorb1(c % ns, r0, 1)
        scores1((c + 1) % ns, r0 + tk, 1)
        absorb1(c % ns, r0, 0)

    nu = FLASH_UNROLL

    def body(i, carry):
        r0 = pl.multiple_of(i * (nu * tk), nu * tk)
        for c in range(nu):
            advance(c, r0 + c * tk)
        return carry

    n_loop = (n_main - 1) // nu
    lax.fori_loop(0, n_loop, body, 0)
    for c in range(n_loop * nu, n_main - 1):
        advance(c, c * tk)
    absorb((n_main - 1) % ns, (n_main - 1) * tk)
    _flash_finish(acc_ref, o_ref)


def _flash_ctx_kernel(o_in_ref, q_ref, kc_ref, vc_ref, o_ref, acc_ref, st_ref):
    del o_in_ref
    _flash_ctx_init(_flash_ctx_scores(_flash_queries(q_ref), kc_ref), vc_ref, acc_ref, st_ref)
    _flash_finish(acc_ref, o_ref)


def _flash(mq, mk, mvt, dims, with_ctx):
    B, S, CTX = dims
    T = mq.shape[0] if with_ctx else B * S
    tq = FLASH_TQ
    nq = S // tq
    ctx_blk0 = B * S // CTX
    npair = MLA_HEADS // 2
    acc = lambda n: [pltpu.VMEM((2, HEAD_DIM + DEN_ROWS, n), F32), pltpu.VMEM((2, 1, n), F32)]
    out_shape = jax.ShapeDtypeStruct((T, MLA_HEADS * MLA_V), BF16)
    kc_spec = pl.BlockSpec((CTX, 2 * LANES), lambda b, p, *_: (ctx_blk0 + b, p))
    vc_spec = pl.BlockSpec((LANES, CTX), lambda b, p, *_: (p, ctx_blk0 + b))

    o = pl.pallas_call(
        functools.partial(_flash_kernel, n_main=S // FLASH_TK),
        grid=(B, npair, nq),
        in_specs=[
            pl.BlockSpec((tq, 2 * LANES), lambda b, p, j: (b * nq + j, p)),
            pl.BlockSpec((S, 2 * LANES), lambda b, p, j: (b, p)),
            pl.BlockSpec((LANES, S), lambda b, p, j: (p, b)),
            kc_spec, vc_spec,
        ],
        out_specs=pl.BlockSpec((tq, LANES), lambda b, p, j: (b * nq + j, p)),
        out_shape=out_shape,
        scratch_shapes=[pltpu.VMEM((FLASH_TK, tq), F32)] * (2 * FLASH_SLOTS)
        + [pltpu.VMEM((1, tq), F32)] * (2 * FLASH_SLOTS) + acc(tq),
        compiler_params=_cparams(("arbitrary", "arbitrary", "arbitrary")),
        name="flash",
    )(mq, mk, mvt, mk, mvt)
    if not with_ctx:
        return o
    return pl.pallas_call(
        _flash_ctx_kernel,
        grid=(B, npair),
        in_specs=[
            pl.BlockSpec(memory_space=pl.ANY),
            pl.BlockSpec((CTX, 2 * LANES), lambda b, p: (ctx_blk0 + b, p)),
            kc_spec, vc_spec,
        ],
        out_specs=pl.BlockSpec((CTX, LANES), lambda b, p: (ctx_blk0 + b, p)),
        out_shape=out_shape,
        scratch_shapes=acc(CTX),
        input_output_aliases={0: 0},
        compiler_params=_cparams(("arbitrary", "arbitrary")),
        name="flash_ctx",
    )(o, mq, mk, mvt)


def _mlstm_gates(d, gates, bgate, tri):
    gb = gates + bgate
    ls = jnp.minimum(gb, 0.0) - jnp.log1p(jnp.exp(-jnp.abs(gb)))
    lane = lax.broadcasted_iota(jnp.int32, gb.shape, 1)
    is_f = ((lane >= 4) & (lane < 8)) | ((lane >= 12) & (lane < 16))
    xg = jnp.where(lane < 16, jnp.where(is_f, ls, gb), 0.0)
    x1 = xg.astype(BF16)
    r1 = xg - x1.astype(F32)
    x2 = r1.astype(BF16)
    x3 = (r1 - x2.astype(F32)).astype(BF16)
    cum = _dot(tri, x1) + _dot(tri, x2) + _dot(tri, x3)
    return xg, cum, xg.T, cum.T, jnp.sum(xg, axis=0, keepdims=True)


def _mlstm_kernel(xf_ref, xpf_ref, xnf_ref, vf_ref, gf_ref, xb_ref, xpb_ref, xnb_ref, vb_ref, gb_ref,
                  cw_ref, cb_ref, bgate_ref, tril_ref, triu_ref, hf_ref, hb_ref, c_ref, m_ref,
                  *, nb_ctx, nb_lat, nchk):
    @pl.when(pl.program_id(1) == 0)
    def _():
        c_ref[...] = jnp.zeros_like(c_ref)
        m_ref[...] = jnp.zeros_like(m_ref)

    L = ML_CHUNK
    bgate = bgate_ref[...]
    step = pl.program_id(1)
    in_ctx = step < nb_ctx
    pos = jnp.where(in_ctx, step, step - nb_ctx)
    starts = pos == 0
    ends = pos == jnp.where(in_ctx, nb_ctx, nb_lat) - 1
    qk_f = _short_conv(xf_ref[...], xpf_ref[...], xnf_ref[...], starts, ends, cw_ref, cb_ref)
    qk_b = _short_conv(xb_ref[...], xpb_ref[...], xnb_ref[...], ends, starts, cw_ref, cb_ref)
    q = (qk_f[0], qk_b[0])
    k = (qk_f[1], qk_b[1])
    v = (vf_ref[...], vb_ref[...])
    g_refs = (gf_ref, gb_ref)
    tris = (tril_ref[...], triu_ref[...])
    order = (list(range(nchk)), list(range(nchk - 1, -1, -1)))
    si = lax.broadcasted_iota(jnp.int32, (L, L), 0)
    ti = lax.broadcasted_iota(jnp.int32, (L, L), 1)
    valid = (si <= ti, si >= ti)
    lane_l = lax.broadcasted_iota(jnp.int32, (L, LANES), 1)
    row_l = lax.broadcasted_iota(jnp.int32, (LANES, L), 0)
    rows_of = lambda j: slice(j * L, (j + 1) * L)

    items = {}
    for d in range(2):
        for j in range(nchk):
            xg, cum, xg_t, cum_t, tot = _mlstm_gates(d, g_refs[d][rows_of(j), :], bgate, tris[d])
            for p in range(ML_HEADS // 2):
                cols = slice(p * LANES, (p + 1) * LANES)
                qp = q[d][rows_of(j), cols]
                kp = k[d][rows_of(j), cols]
                vt = v[d][rows_of(j), cols].astype(F32).T
                for e in range(2):
                    hd = 2 * p + e
                    ci = 8 * d + hd
                    cf = 8 * d + 4 + hd
                    in_head = (lane_l < HEAD_DIM) if e == 0 else (lane_l >= HEAD_DIM)
                    in_rows = (row_l < HEAD_DIM) if e == 0 else (row_l >= HEAD_DIM)
                    ones_row = HEAD_DIM if e == 0 else 0
                    b_row = cum_t[cf:cf + 1, :]
                    c_col = xg[:, ci:ci + 1] - cum[:, cf:cf + 1]
                    dm = jnp.where(valid[d], b_row + c_col, NEG)
                    b_tot = tot[:, cf:cf + 1]
                    a_row = b_tot - b_row + xg_t[ci:ci + 1, :]
                    items[d, j, hd] = dict(
                        qp=qp, km=jnp.where(in_head, kp, jnp.zeros_like(kp)),
                        vat=jnp.where(in_rows, vt, jnp.where(row_l == ones_row, 1.0, 0.0)),
                        ones_row=ones_row, b_row=b_row, b_tot=b_tot, dm=dm, a_row=a_row,
                        a_t=jnp.max(dm, axis=0, keepdims=True), a_max=jnp.max(a_row, axis=1, keepdims=True))

    for c in items.values():
        c["qk"] = (_dot_nt(c["km"], c["qp"]) * jnp.exp(c["dm"] - c["a_t"])).astype(BF16)

    for c in items.values():
        c["intra"] = _dot(c["vat"].astype(BF16), c["qk"])
        c["upd"] = _dot((c["vat"] * jnp.exp(c["a_row"] - c["a_max"])).astype(BF16), c["km"])

    state = {(d, hd): (c_ref[d, hd], m_ref[d * ML_HEADS + hd:d * ML_HEADS + hd + 1, 0:1])
             for d in range(2) for hd in range(ML_HEADS)}
    hs = {}
    for n in range(nchk):
        for d in range(2):
            j = order[d][n]
            for hd in range(ML_HEADS):
                c = items[d, j, hd]
                cst, m_prev = state[d, hd]
                g = c["b_row"] + m_prev
                mt = jnp.maximum(g, c["a_t"])
                haug = (jnp.exp(g - mt) * _dot_nt(cst.astype(BF16), c["qp"])
                        + jnp.exp(c["a_t"] - mt) * c["intra"])
                den = haug[c["ones_row"]:c["ones_row"] + 1, :]
                hs[d, j, hd] = haug / jnp.maximum(jnp.abs(den), jnp.exp(-mt))
                m_new = jnp.maximum(c["b_tot"] + m_prev, c["a_max"])
                cst = (jnp.exp(c["b_tot"] + m_prev - m_new) * cst
                       + jnp.exp(c["a_max"] - m_new) * c["upd"])
                state[d, hd] = (cst, m_new)
    for (d, hd), (cst, m_new) in state.items():
        c_ref[d, hd] = cst
        r = d * ML_HEADS + hd
        m_ref[r:r + 1, :] = jnp.broadcast_to(m_new, (1, LANES))

    for d, h_ref in enumerate((hf_ref, hb_ref)):
        for j in range(nchk):
            for p in range(ML_HEADS // 2):
                pair = jnp.where(row_l < HEAD_DIM, hs[d, j, 2 * p], hs[d, j, 2 * p + 1])
                h_ref[rows_of(j), p * LANES:(p + 1) * LANES] = pair.T


def _mlstm(l, mlqk, mlv, misc, prm, tabs, dims):
    T, W = mlqk.shape
    B, S, CTX = dims
    nchk = ML_STEP_CHUNKS
    blk = nchk * ML_CHUNK
    nb_ctx = CTX // blk
    nb_lat = S // blk
    ctx0 = B * S // blk

    def fwd(b, c):
        return jnp.where(c < nb_ctx, ctx0 + b * nb_ctx + c, b * nb_lat + (c - nb_ctx))

    def bwd(b, c):
        return jnp.where(c < nb_ctx, ctx0 + b * nb_ctx + (nb_ctx - 1 - c),
                         b * nb_lat + (nb_lat - 1 - (c - nb_ctx)))

    sub = blk // 8
    last8 = T // 8 - 1

    def specs(fn):
        return [pl.BlockSpec((blk, W), lambda b, c: (fn(b, c), 0)),
                pl.BlockSpec((8, W), lambda b, c: (jnp.maximum(fn(b, c) * sub - 1, 0), 0)),
                pl.BlockSpec((8, W), lambda b, c: (jnp.minimum((fn(b, c) + 1) * sub, last8), 0)),
                pl.BlockSpec((blk, ML_W), lambda b, c: (fn(b, c), 0)),
                pl.BlockSpec((blk, LANES), lambda b, c: (fn(b, c), 0))]

    return pl.pallas_call(
        functools.partial(_mlstm_kernel, nb_ctx=nb_ctx, nb_lat=nb_lat, nchk=nchk),
        grid=(B, nb_ctx + nb_lat),
        in_specs=specs(fwd) + specs(bwd) + [
            pl.BlockSpec((None, 8, W), lambda b, c: (l, 0, 0)),
            pl.BlockSpec((None, 1, W), lambda b, c: (l, 0, 0)),
            pl.BlockSpec((None, 1, LANES), lambda b, c: (l, 0, 0)),
            pl.BlockSpec((ML_CHUNK, ML_CHUNK), lambda b, c: (0, 0)),
            pl.BlockSpec((ML_CHUNK, ML_CHUNK), lambda b, c: (0, 0)),
        ],
        out_specs=[pl.BlockSpec((blk, ML_W), lambda b, c: (fwd(b, c), 0)),
                   pl.BlockSpec((blk, ML_W), lambda b, c: (bwd(b, c), 0))],
        out_shape=[jax.ShapeDtypeStruct((T, ML_W), F32)] * 2,
        scratch_shapes=[pltpu.VMEM((2, ML_HEADS, LANES, LANES), F32), pltpu.VMEM((8, LANES), F32)],
        compiler_params=_cparams(("arbitrary", "arbitrary")),
        name="mlstm",
    )(mlqk, mlqk, mlqk, mlv, misc, mlqk, mlqk, mlqk, mlv, misc,
      prm["conv_w"], prm["conv_b"], prm["b_gate"], tabs["tril"], tabs["triu"])


def _ffn_chunks(ff):
    step = 2 * MXU_DIM
    starts = list(range(0, ff - ff % step, step)) or [0]
    return [(c, (ff - c) if c == starts[-1] else step) for c in starts]


def _out_kernel(x_ref, mod_ref, ona_ref, omla_ref, hf_ref, hb_ref, mlo_ref, hg_ref, bd64_ref,
                w_out_ref, g_ref, wa_ref, wb_ref, wo_ref, o_ref):
    x = x_ref[...]
    gate1 = mod_ref[2:3, :]
    shift2 = mod_ref[3:4, :]
    scale2 = mod_ref[4:5, :]
    gate2 = mod_ref[5:6, :]

    h = hf_ref[...] + hb_ref[...]
    hn = h * lax.rsqrt(_group_sumsq(h, bd64_ref[...]) * (1.0 / HEAD_DIM) + EPS) * hg_ref[...]
    ob = (hn * _sigmoid(mlo_ref[...])).astype(BF16)
    mix = _dot(jnp.concatenate([ona_ref[...], ob, omla_ref[...]], axis=-1), w_out_ref[...])
    x1 = x + gate1 * mix

    h2 = ((_rms_rows(x1) * g_ref[...]) * (1.0 + scale2) + shift2).astype(BF16)
    acc = jnp.zeros_like(x1)
    for c0, w in _ffn_chunks(wa_ref.shape[1]):
        a = _dot(h2, wa_ref[:, c0:c0 + w])
        b = _dot(h2, wb_ref[:, c0:c0 + w])
        act = (a * _sigmoid(a) * b).astype(BF16)
        acc = acc + _dot(act, wo_ref[c0:c0 + w, :])
    o_ref[...] = x1 + gate2 * acc


def _out(l, xs, mods, ona, omla, hf, hb, mlo, prm, tabs, dims, with_ctx):
    D = xs.shape[1]
    B, S, CTX = dims
    T = xs.shape[0] if with_ctx else B * S
    nt = T // ROW_TILE
    lat_tiles = S // ROW_TILE
    ff = prm["w_a"].shape[-1]

    def grp(i):
        return jnp.minimum(i // lat_tiles, B)

    row = lambda w: pl.BlockSpec((ROW_TILE, w), lambda i: (i, 0))
    lay = lambda shape: pl.BlockSpec((None,) + shape, lambda i: (l,) + (0,) * len(shape))
    big = lambda shape: pl.BlockSpec((None,) + shape, lambda i: (l,) + (0,) * len(shape),
                                     pipeline_mode=pl.Buffered(1))
    return pl.pallas_call(
        _out_kernel,
        grid=(nt,),
        in_specs=[
            row(D),
            pl.BlockSpec((None, None, 6, D), lambda i: (l, grp(i), 0, 0)),
            row(NA_W), row(MLA_HEADS * MLA_V), row(ML_W), row(ML_W), row(ML_W),
            lay((1, ML_W)),
            pl.BlockSpec((MXU_DIM, MXU_DIM), lambda i: (0, 0)),
            big((NA_W + ML_W + MLA_HEADS * MLA_V, D)),
            lay((1, D)),
            big((D, ff)), big((D, ff)), big((ff, D)),
        ],
        out_specs=row(D),
        out_shape=jax.ShapeDtypeStruct((T, D), F32),
        compiler_params=_cparams(("arbitrary",)),
        name="out_ffn",
    )(xs, mods, ona, omla, hf, hb, mlo, prm["head_gain"], tabs["bd64"],
      prm["w_out"], prm["g_ffn"], prm["w_a"], prm["w_b"], prm["w_fo"])


def _block_diag_ones(group):
    idx = np.arange(MXU_DIM) // group
    return jnp.asarray((idx[:, None] == idx[None, :]).astype(np.float32), dtype=BF16)


def _rope_tables(S, n_ctx_rows):
    n_freq = MLA_ROPE // 4
    inv = ROPE_BASE ** (-jnp.arange(n_freq, dtype=F32) / n_freq)
    t = jnp.arange(S, dtype=jnp.int32)
    row = (t // GRID_W).astype(F32)
    col = (t % GRID_W).astype(F32)
    ang = jnp.concatenate([row[:, None] * inv, col[:, None] * inv], axis=-1)
    ang = jnp.concatenate([ang, jnp.zeros((n_ctx_rows, MLA_ROPE // 2), F32)], axis=0)
    cos, sin = jnp.cos(ang), jnp.sin(ang)
    n = S + n_ctx_rows
    half = MLA_ROPE // 2
    ones = jnp.ones((n, MLA_NOPE), F32)
    z = lambda w: jnp.zeros((n, w), F32)
    tail = LANES - MLA_NOPE - MLA_ROPE
    del half
    cos_t = jnp.concatenate([ones, cos, cos, jnp.ones((n, tail), F32)], axis=-1)
    sin_t = jnp.concatenate([z(MLA_NOPE), -sin, sin, z(tail)], axis=-1)
    return cos_t, sin_t


def _na_bias(rpb, rows):
    depth, H = rpb.shape[:2]
    qr, kr_n = ATT_BLOCK // GRID_W, NA_KEY_ROWS // GRID_W
    nb = rows // qr
    cols = np.arange(GRID_W)
    cs = np.clip(cols - NA_WIN_C // 2, 0, GRID_W - NA_WIN_C)
    col_ok = (cols[None, :] >= cs[:, None]) & (cols[None, :] < cs[:, None] + NA_WIN_C)
    dc = np.clip(cols[None, :] - cols[:, None] + NA_WIN_C - 1, 0, 2 * NA_WIN_C - 2)
    dr_all, ok_all = [], []
    for j in (0, 1, nb - 1):
        ks = int(np.clip(j - 1, 0, nb - 3)) * qr
        r = j * qr + np.arange(qr)
        rs = np.clip(r - NA_WIN_R // 2, 0, rows - NA_WIN_R)
        kr = ks + np.arange(kr_n)
        ok_all.append((kr[None, :] >= rs[:, None]) & (kr[None, :] < rs[:, None] + NA_WIN_R))
        dr_all.append(np.clip(kr[None, :] - r[:, None] + NA_WIN_R - 1, 0, 2 * NA_WIN_R - 2))
    dr_all = np.stack(dr_all)
    ok_all = np.stack(ok_all)
    t1 = jnp.take(rpb, jnp.asarray(dc.reshape(-1)), axis=3).reshape(depth, H, 2 * NA_WIN_R - 1, GRID_W, GRID_W)
    t1 = jnp.where(jnp.asarray(col_ok), t1, NEG).astype(BF16)
    t2 = jnp.take(t1, jnp.asarray(dr_all.reshape(-1)), axis=2).reshape(depth, H, 3, qr, kr_n, GRID_W, GRID_W)
    t2 = jnp.where(jnp.asarray(ok_all)[None, None, :, :, :, None, None], t2, NEG)
    bias = t2.transpose(0, 2, 1, 4, 6, 3, 5).reshape(depth, 3, H, NA_KEY_ROWS, ATT_BLOCK)
    none = jnp.full((depth, 1, H, NA_KEY_ROWS, ATT_BLOCK), NEG, BF16)
    return jnp.concatenate([bias, none], axis=1)


def _prepare(w_in, w_uq, w_ukv, w_out, w_ffn_in, w_ffn_out, g_mix, g_ffn, b_gate, na_qk_gain,
             ml_conv_w, ml_conv_b, ml_head_gain, mla_gq, mla_gkv, mla_qk_gain):
    depth, D, _ = w_in.shape
    o_gate = C_MLO + ML_W
    o_ql = o_gate + 4 * ML_HEADS
    o_kr = o_ql + Q_LORA + KV_LORA
    half = MLA_ROPE // 2
    zc = lambda w: jnp.zeros((depth, D, w), w_in.dtype)
    w_in_p = jnp.concatenate([
        w_in[..., :o_gate], w_in[..., o_ql:o_kr],
        w_in[..., o_gate:o_ql], zc(MISC_ROPE_LANE - 4 * ML_HEADS),
        w_in[..., o_kr:], zc(LANES - MISC_ROPE_LANE - MLA_ROPE),
        zc(MISC_ROPE_LANE), w_in[..., o_kr + half:], w_in[..., o_kr:o_kr + half],
        zc(LANES - MISC_ROPE_LANE - MLA_ROPE)], axis=-1).astype(BF16)
    pad_h = lambda a, w: jnp.pad(a, [(0, 0)] * (a.ndim - 1) + [(0, LANES - w)])

    def partner(a):
        lo, hi = a[..., MLA_NOPE:MLA_NOPE + half], a[..., MLA_NOPE + half:MLA_QK]
        return jnp.concatenate([jnp.zeros_like(a[..., :MLA_NOPE]), hi, lo,
                                jnp.zeros_like(a[..., MLA_QK:])], axis=-1)

    w_uq_h = pad_h(w_uq.reshape(depth, Q_LORA, MLA_HEADS, MLA_QK), MLA_QK)
    w_uq_p = w_uq_h.reshape(depth, Q_LORA, MLA_HEADS * LANES).astype(BF16)
    w_uq_s = partner(w_uq_h).reshape(depth, Q_LORA, MLA_HEADS * LANES).astype(BF16)
    ukv = w_ukv.reshape(depth, KV_LORA, MLA_HEADS, MLA_NOPE + MLA_V)
    w_uk_p = pad_h(ukv[..., :MLA_NOPE], MLA_NOPE).reshape(depth, KV_LORA, MLA_HEADS * LANES).astype(BF16)
    w_uv = ukv[..., MLA_NOPE:].reshape(depth, KV_LORA, MLA_HEADS * MLA_V).astype(BF16)
    ff = w_ffn_out.shape[1]
    mla_g = pad_h(mla_qk_gain, MLA_QK)
    return {
        "w_in": w_in_p, "w_uq": w_uq_p, "w_uqs": w_uq_s, "w_uk": w_uk_p, "w_uv": w_uv,
        "w_out": w_out.astype(BF16),
        "w_a": w_ffn_in[..., :ff].astype(BF16), "w_b": w_ffn_in[..., ff:].astype(BF16),
        "w_fo": w_ffn_out.astype(BF16),
        "g_mix": g_mix[:, None, :], "g_ffn": g_ffn[:, None, :],
        "gq": mla_gq[:, None, :], "gkv": mla_gkv[:, None, :],
        "na_gq": jnp.tile(na_qk_gain[:, 0:1, :], (1, 1, NA_HEADS)) * NA_SCALE,
        "na_gk": jnp.tile(na_qk_gain[:, 1:2, :], (1, 1, NA_HEADS)),
        "mla_gq": jnp.tile(mla_g[:, 0:1, :], (1, 1, MLA_HEADS)) * (MLA_SCALE * LOG2E),
        "mla_gk": jnp.tile(mla_g[:, 1:2, :], (1, 1, MLA_HEADS)),
        "mla_gqs": jnp.tile(partner(mla_g[:, 0:1, :]), (1, 1, MLA_HEADS)) * (MLA_SCALE * LOG2E),
        "mla_gks": jnp.tile(partner(mla_g[:, 1:2, :]), (1, 1, MLA_HEADS)),
        "head_gain": ml_head_gain.reshape(depth, 1, ML_W),
        "b_gate": pad_h(b_gate, 4 * ML_HEADS)[:, None, :],
        "conv_w": jnp.pad(ml_conv_w, ((0, 0), (0, 8 - ml_conv_w.shape[1]), (0, 0))),
        "conv_b": ml_conv_b[:, None, :],
    }


def kernel(x, c, ctx, c_ctx, w_mod, b_mod, g_mix, g_ffn, w_in, b_gate, na_qk_gain, na_rpb,
           ml_conv_w, ml_conv_b, ml_head_gain, mla_gq, mla_gkv, w_uq, w_ukv, mla_qk_gain,
           w_out, w_ffn_in, w_ffn_out):
    B, S, D = x.shape
    CTX = ctx.shape[1]
    depth = w_in.shape[0]
    assert CTX == ATT_BLOCK and S % ROW_TILE == 0 and (B * CTX) % ROW_TILE == 0
    assert CTX % (ML_STEP_CHUNKS * ML_CHUNK) == 0 and S % (ML_STEP_CHUNKS * ML_CHUNK) == 0
    assert S % (2 * FLASH_TK) == 0 and S % FLASH_TQ == 0 and S // ATT_BLOCK >= 3 and B + 1 <= 8
    dims = (B, S, CTX)

    prm = _prepare(w_in, w_uq, w_ukv, w_out, w_ffn_in, w_ffn_out, g_mix, g_ffn, b_gate, na_qk_gain,
                   ml_conv_w, ml_conv_b, ml_head_gain, mla_gq, mla_gkv, mla_qk_gain)
    cos_t, sin_t = _rope_tables(S, B * CTX)
    tri = np.tril(np.ones((ML_CHUNK, ML_CHUNK), np.float32))
    tabs = {
        "bd64": _block_diag_ones(HEAD_DIM), "bd128": _block_diag_ones(LANES),
        "cos": cos_t, "sin": sin_t,
        "tril": jnp.asarray(tri, dtype=BF16), "triu": jnp.asarray(tri.T, dtype=BF16),
    }
    bias = _na_bias(na_rpb, S // GRID_W)

    cond8 = jnp.concatenate([c, c_ctx[None, :], jnp.zeros((8 - B - 1, D), F32)], axis=0)
    mods = _modulation(cond8, w_mod, b_mod)
    mods = mods.reshape(depth, 8, 6, D)

    xs = jnp.concatenate([x.reshape(B * S, D), ctx.reshape(B * CTX, D)], axis=0)
    for l in range(depth):
        with_ctx = l < depth - 1
        naq, nak, mlqk, mlv, mlo, misc, mq, mk, navt, mvt = _proj(l, xs, mods, prm, tabs, dims)
        ona = _na(l, naq, nak, navt, bias, dims, with_ctx)
        omla = _flash(mq, mk, mvt, dims, with_ctx)
        hf, hb = _mlstm(l, mlqk, mlv, misc, prm, tabs, dims)
        xs = _out(l, xs, mods, ona, omla, hf, hb, mlo, prm, tabs, dims, with_ctx)
    return xs.reshape(B, S, D)
```

```python
import functools

import numpy as np
import jax
import jax.numpy as jnp
from jax import lax
from jax.experimental import pallas as pl
from jax.experimental.pallas import tpu as pltpu

GRID_W = 64
HEAD_DIM = 64
NA_HEADS = 4
NA_WIN_R = 8
NA_WIN_C = 16
ML_HEADS = 4
ML_CHUNK = 128
ML_STEP_CHUNKS = 2
MLA_HEADS = 8
MLA_NOPE = 64
MLA_ROPE = 32
MLA_V = 64
Q_LORA = 384
KV_LORA = 256
ROPE_BASE = 10000.0
EPS = 1e-6
NA_W = NA_HEADS * HEAD_DIM
ML_W = ML_HEADS * HEAD_DIM
MLA_QK = MLA_NOPE + MLA_ROPE
NA_SCALE = HEAD_DIM ** -0.5
MLA_SCALE = MLA_QK ** -0.5
LOG2E = 1.4426950408889634

LANES = 128
MXU_DIM = 256
VMEM_LIMIT = 56 * 1024 * 1024

ROW_TILE = 512
ATT_BLOCK = 256
NA_KEY_ROWS = 768
FLASH_TK = 512
FLASH_TQ = 1024
FLASH_SLOTS = 2
FLASH_UNROLL = 4
NEG = -1e30
DEN_ROWS = 16

C_NAQ, C_NAK, C_NAV = 0, 256, 512
C_MLQK, C_MLV, C_MLO = 768, 1280, 1536
C_QL = 1792
C_KVL = C_QL + Q_LORA
C_MISC = C_KVL + KV_LORA
C_MISC2 = C_MISC + LANES
D_IN_PAD = C_MISC2 + LANES
MISC_ROPE_LANE = 64

BF16 = jnp.bfloat16
F32 = jnp.float32


def _cparams(sem):
    return pltpu.CompilerParams(dimension_semantics=sem, vmem_limit_bytes=VMEM_LIMIT)


def _dot(a, b):
    return jnp.dot(a, b, preferred_element_type=F32)


def _dot_nt(a, b):
    return lax.dot_general(a, b, (((1,), (1,)), ((), ())), preferred_element_type=F32)


def _split2(x):
    hi = x.astype(BF16)
    lo = (x - hi.astype(F32)).astype(BF16)
    return hi, lo


def _group_sumsq(x, bd):
    x2 = x * x
    hi, lo = _split2(x2)
    outs = []
    for c in range(x.shape[1] // MXU_DIM):
        sl = slice(c * MXU_DIM, (c + 1) * MXU_DIM)
        outs.append(_dot(hi[:, sl], bd) + _dot(lo[:, sl], bd))
    return outs[0] if len(outs) == 1 else jnp.concatenate(outs, axis=-1)


def _rms_rows(x):
    return x * lax.rsqrt(jnp.mean(x * x, axis=-1, keepdims=True) + EPS)


def _sigmoid(x):
    return 1.0 / (1.0 + jnp.exp(-x))


def _mod_kernel(c_ref, w_ref, b_ref, o_ref):
    c = c_ref[...]
    a = c * _sigmoid(c)
    o_ref[...] = jnp.dot(a, w_ref[...], preferred_element_type=F32,
                         precision=lax.Precision.HIGHEST) + b_ref[...]


def _modulation(cond8, w_mod, b_mod):
    depth, d, d6 = w_mod.shape
    bw = 2 * d
    return pl.pallas_call(
        _mod_kernel,
        grid=(depth, d6 // bw),
        in_specs=[
            pl.BlockSpec((8, d), lambda l, j: (0, 0)),
            pl.BlockSpec((None, d, bw), lambda l, j: (l, 0, j)),
            pl.BlockSpec((None, 1, bw), lambda l, j: (l, 0, j)),
        ],
        out_specs=pl.BlockSpec((None, 8, bw), lambda l, j: (l, 0, j)),
        out_shape=jax.ShapeDtypeStruct((depth, 8, d6), F32),
        compiler_params=_cparams(("arbitrary", "arbitrary")),
        name="modulation",
    )(cond8, w_mod, b_mod.reshape(depth, 1, d6))


def _proj_kernel(x_ref, mod_ref, g_ref, w_in_ref, w_uq_ref, w_uqs_ref, w_uk_ref, w_uv_ref,
                 gq_ref, gkv_ref, na_gq_ref, na_gk_ref, mla_gq_ref, mla_gk_ref, mla_gqs_ref, mla_gks_ref,
                 bd64_ref, bd128_ref, cos_ref, sin_ref,
                 naq_ref, nak_ref, mlqk_ref, mlv_ref, mlo_ref, misc_ref,
                 mq_ref, mk_ref, nav_ref, mv_ref):
    x = x_ref[...]
    shift = mod_ref[0:1, :]
    scale = mod_ref[1:2, :]
    h = (_rms_rows(x) * g_ref[...]) * (1.0 + scale) + shift
    hb = h.astype(BF16)

    def proj(c0, width):
        return _dot(hb, w_in_ref[:, c0:c0 + width])

    bd64 = bd64_ref[...]
    bd128 = bd128_ref[...]

    lat = proj(C_QL, D_IN_PAD - C_QL)
    ql = lat[:, :Q_LORA]
    kvl = lat[:, C_KVL - C_QL:C_MISC - C_QL]
    misc = lat[:, C_MISC - C_QL:C_MISC2 - C_QL]
    kx = lat[:, C_MISC2 - C_QL:]
    misc_ref[...] = misc
    qn = (_rms_rows(ql) * gq_ref[...]).astype(BF16)
    kvn = (_rms_rows(kvl) * gkv_ref[...]).astype(BF16)

    na = proj(C_NAQ, 3 * NA_W)
    pq = na[:, :NA_W]
    pk = na[:, NA_W:2 * NA_W]
    nav_ref[...] = na[:, 2 * NA_W:].T.astype(BF16)

    qr = _dot(qn, w_uq_ref[...])
    qx = _dot(qn, w_uqs_ref[...])
    lane = lax.broadcasted_iota(jnp.int32, misc.shape, 1)
    krope = jnp.where((lane >= MISC_ROPE_LANE) & (lane < MISC_ROPE_LANE + MLA_ROPE), misc, 0.0)
    kr = _dot(kvn, w_uk_ref[...]) + jnp.concatenate([krope] * MLA_HEADS, axis=-1)
    mv_ref[...] = _dot(kvn, w_uv_ref[...]).T.astype(BF16)

    ml = proj(C_MLQK, 4 * ML_W)
    mlqk_ref[...] = ml[:, :2 * ML_W]
    mlv_ref[...] = ml[:, 2 * ML_W:3 * ML_W].astype(BF16)
    mlo_ref[...] = ml[:, 3 * ML_W:]

    naq_ref[...] = (pq * lax.rsqrt(_group_sumsq(pq, bd64) * (1.0 / HEAD_DIM) + EPS)
                    * na_gq_ref[...]).astype(BF16)
    nak_ref[...] = (pk * lax.rsqrt(_group_sumsq(pk, bd64) * (1.0 / HEAD_DIM) + EPS)
                    * na_gk_ref[...]).astype(BF16)

    cos = cos_ref[...]
    sin = sin_ref[...]
    rq = lax.rsqrt(_group_sumsq(qr, bd128) * (1.0 / MLA_QK) + EPS)
    rk = lax.rsqrt(_group_sumsq(kr, bd128) * (1.0 / MLA_QK) + EPS)
    kxs = kx * sin
    for g in range(MLA_HEADS):
        sl = slice(g * LANES, (g + 1) * LANES)
        mq_ref[:, sl] = (rq[:, sl] * (qr[:, sl] * mla_gq_ref[:, sl] * cos
                                      + qx[:, sl] * mla_gqs_ref[:, sl] * sin)).astype(BF16)
        mk_ref[:, sl] = (rk[:, sl] * (kr[:, sl] * mla_gk_ref[:, sl] * cos
                                      + kxs * mla_gks_ref[:, sl])).astype(BF16)


def _proj(l, xs, mods, prm, tabs, dims):
    T, D = xs.shape
    B, S, CTX = dims
    nt = T // ROW_TILE
    lat_tiles = S // ROW_TILE

    def grp(i):
        return jnp.minimum(i // lat_tiles, B)

    def tab_row(i):
        return jnp.where(i < B * lat_tiles, i % lat_tiles, lat_tiles + (i - B * lat_tiles))

    row = lambda w: pl.BlockSpec((ROW_TILE, w), lambda i: (i, 0))
    lay = lambda shape: pl.BlockSpec((None,) + shape, lambda i: (l,) + (0,) * len(shape))
    const = lambda shape: pl.BlockSpec(shape, lambda i: (0,) * len(shape))
    tab = pl.BlockSpec((ROW_TILE, LANES), lambda i: (tab_row(i), 0))
    out_w = [(NA_W, BF16), (NA_W, BF16), (2 * ML_W, F32), (ML_W, BF16), (ML_W, F32),
             (LANES, F32), (MLA_HEADS * LANES, BF16), (MLA_HEADS * LANES, BF16)]
    out_t = [NA_W, MLA_HEADS * MLA_V]
    return pl.pallas_call(
        _proj_kernel,
        grid=(nt,),
        in_specs=[
            row(D),
            pl.BlockSpec((None, None, 6, D), lambda i: (l, grp(i), 0, 0)),
            lay((1, D)),
            lay((D, D_IN_PAD)), lay((Q_LORA, MLA_HEADS * LANES)), lay((Q_LORA, MLA_HEADS * LANES)),
            lay((KV_LORA, MLA_HEADS * LANES)), lay((KV_LORA, MLA_HEADS * MLA_V)),
            lay((1, Q_LORA)), lay((1, KV_LORA)), lay((1, NA_W)), lay((1, NA_W)),
            lay((1, MLA_HEADS * LANES)), lay((1, MLA_HEADS * LANES)),
            lay((1, MLA_HEADS * LANES)), lay((1, MLA_HEADS * LANES)),
            const((MXU_DIM, MXU_DIM)), const((MXU_DIM, MXU_DIM)),
            tab, tab,
        ],
        out_specs=[row(w) for w, _ in out_w]
        + [pl.BlockSpec((w, ROW_TILE), lambda i: (0, i)) for w in out_t],
        out_shape=[jax.ShapeDtypeStruct((T, w), dt) for w, dt in out_w]
        + [jax.ShapeDtypeStruct((w, T), BF16) for w in out_t],
        compiler_params=_cparams(("arbitrary",)),
        name="proj",
    )(xs, mods, prm["g_mix"], prm["w_in"], prm["w_uq"], prm["w_uqs"], prm["w_uk"], prm["w_uv"],
      prm["gq"], prm["gkv"], prm["na_gq"], prm["na_gk"],
      prm["mla_gq"], prm["mla_gk"], prm["mla_gqs"], prm["mla_gks"],
      tabs["bd64"], tabs["bd128"], tabs["cos"], tabs["sin"])


def _short_conv(x, prev_blk, next_blk, first, last, w_ref, b_ref):
    n = x.shape[0]
    prev_row = jnp.where(first, 0.0, prev_blk[7:8, :])
    next_row = jnp.where(last, 0.0, next_blk[0:1, :])
    ridx = lax.broadcasted_iota(jnp.int32, x.shape, 0)
    xm1 = jnp.where(ridx == 0, prev_row, pltpu.roll(x, 1, 0))
    xp1 = jnp.where(ridx == n - 1, next_row, pltpu.roll(x, n - 1, 0))
    y = b_ref[...] + xm1 * w_ref[0:1, :] + x * w_ref[1:2, :] + xp1 * w_ref[2:3, :]
    y = y * _sigmoid(y)
    return y[:, :ML_W].astype(BF16), (y[:, ML_W:] * (HEAD_DIM ** -0.5)).astype(BF16)


def _na_kernel(q_ref, k_ref, vt_ref, kc_ref, vct_ref, bias_ref, o_ref, *, nb):
    j = pl.program_id(1)
    start = pl.multiple_of(jnp.clip(j - 1, 0, nb - 3) * ATT_BLOCK, ATT_BLOCK)
    q = q_ref[...]
    lane = lax.broadcasted_iota(jnp.int32, (ATT_BLOCK, LANES), 1)
    scores = []
    for h in range(NA_HEADS):
        p, e = divmod(h, 2)
        cols = slice(p * LANES, (p + 1) * LANES)
        qp = q[:, cols]
        keep = (lane < HEAD_DIM) if e == 0 else (lane >= HEAD_DIM)
        qt = jnp.where(keep, qp, jnp.zeros_like(qp)).astype(F32).T.astype(BF16)
        s_loc = _dot(k_ref[pl.ds(start, NA_KEY_ROWS), cols], qt) + bias_ref[h].astype(F32)
        s_ctx = _dot(kc_ref[:, cols], qt)
        scores.append((s_loc, s_ctx))
    probs = []
    for s_loc, s_ctx in scores:
        m = jnp.maximum(jnp.max(s_loc, axis=0, keepdims=True), jnp.max(s_ctx, axis=0, keepdims=True))
        probs.append((jnp.exp(s_loc - m).astype(BF16), jnp.exp(s_ctx - m).astype(BF16)))
    outs = []
    for h, (p_loc, p_ctx) in enumerate(probs):
        rows = slice(h * HEAD_DIM, (h + 1) * HEAD_DIM)
        ot = (_dot(_flash_values(vt_ref[rows, pl.ds(start, NA_KEY_ROWS)]), p_loc)
              + _dot(_flash_values(vct_ref[rows, :]), p_ctx))
        outs.append(ot[:HEAD_DIM, :] / ot[HEAD_DIM:HEAD_DIM + 1, :])
    o_ref[...] = jnp.concatenate(outs, axis=0).T.astype(o_ref.dtype)


def _na(l, naq, nak, navt, bias, dims, with_ctx):
    B, S, CTX = dims
    nb = S // ATT_BLOCK
    ctx_blk0 = B * S // CTX
    n_steps = nb + 1 if with_ctx else nb
    rows_out = naq.shape[0] if with_ctx else B * S

    def qrow(b, j):
        return jnp.where(j < nb, b * nb + j, ctx_blk0 + b)

    def variant(j):
        return jnp.where(j == 0, 0, jnp.where(j == nb - 1, 2, jnp.where(j == nb, 3, 1)))

    kern = functools.partial(_na_kernel, nb=nb)
    return pl.pallas_call(
        kern,
        grid=(B, n_steps),
        in_specs=[
            pl.BlockSpec((ATT_BLOCK, NA_W), lambda b, j: (qrow(b, j), 0)),
            pl.BlockSpec((S, NA_W), lambda b, j: (b, 0)),
            pl.BlockSpec((NA_W, S), lambda b, j: (0, b)),
            pl.BlockSpec((CTX, NA_W), lambda b, j: (ctx_blk0 + b, 0)),
            pl.BlockSpec((NA_W, CTX), lambda b, j: (0, ctx_blk0 + b)),
            pl.BlockSpec((None, None, NA_HEADS, NA_KEY_ROWS, ATT_BLOCK), lambda b, j: (l, variant(j), 0, 0, 0)),
        ],
        out_specs=pl.BlockSpec((ATT_BLOCK, NA_W), lambda b, j: (qrow(b, j), 0)),
        out_shape=jax.ShapeDtypeStruct((rows_out, NA_W), BF16),
        compiler_params=_cparams(("arbitrary", "arbitrary")),
        name="na",
    )(naq, nak, navt, nak, navt, bias)


def _flash_values(v):
    return jnp.concatenate([v, jnp.ones((DEN_ROWS, v.shape[1]), BF16)], axis=0)


def _flash_queries(q_ref):
    q = q_ref[...]
    return [q[:, e * LANES:(e + 1) * LANES].astype(F32).T.astype(BF16) for e in range(2)]


def _flash_ctx_scores(qs, kc_ref):
    return [_dot(kc_ref[:, e * LANES:(e + 1) * LANES], qs[e]) for e in range(2)]


def _flash_ctx_init(sts, vc_ref, acc_ref, st_ref):
    ps = []
    for e, st in enumerate(sts):
        m = jnp.max(st, axis=0, keepdims=True)
        st_ref[e] = m
        ps.append(jnp.exp2(st - m).astype(BF16))
    for e, p in enumerate(ps):
        acc_ref[e] = _dot(_flash_values(vc_ref[e * HEAD_DIM:(e + 1) * HEAD_DIM, :]), p)


def _flash_finish(acc_ref, o_ref):
    ot = jnp.concatenate([acc_ref[e, :HEAD_DIM, :] / acc_ref[e, HEAD_DIM:HEAD_DIM + 1, :]
                          for e in range(2)], axis=0)
    o_ref[...] = ot.T.astype(o_ref.dtype)


def _flash_kernel(q_ref, km_ref, vm_ref, kc_ref, vc_ref, o_ref, *scratch, n_main):
    ns = FLASH_SLOTS
    s_refs = [scratch[2 * i:2 * i + 2] for i in range(ns)]
    mx_refs = [scratch[2 * ns + 2 * i:2 * ns + 2 * i + 2] for i in range(ns)]
    acc_ref, st_ref = scratch[4 * ns:4 * ns + 2]
    tk = FLASH_TK
    qs = _flash_queries(q_ref)

    def scores1(slot, r0, e):
        st = _dot(km_ref[pl.ds(r0, tk), e * LANES:(e + 1) * LANES], qs[e])
        s_refs[slot][e][...] = st
        mx_refs[slot][e][...] = jnp.max(st, axis=0, keepdims=True)

    def absorb1(slot, r0, e):
        m = st_ref[e]
        m_new = jnp.maximum(m, mx_refs[slot][e][...])
        st_ref[e] = m_new
        alpha = jnp.exp2(m - m_new)
        p = jnp.exp2(s_refs[slot][e][...] - m_new).astype(BF16)
        v = vm_ref[e * HEAD_DIM:(e + 1) * HEAD_DIM, pl.ds(r0, tk)]
        acc_ref[e] = alpha * acc_ref[e] + _dot(_flash_values(v), p)

    def scores(slot, r0):
        for e in range(2):
            scores1(slot, r0, e)

    def absorb(slot, r0):
        for e in range(2):
            absorb1(slot, r0, e)

    ctx_scores = _flash_ctx_scores(qs, kc_ref)
    scores(0, 0)
    _flash_ctx_init(ctx_scores, vc_ref, acc_ref, st_ref)

    def advance(c, r0):
        scores1((c + 1) % ns, r0 + tk, 0)
        absorb1(c % ns, r0, 1)
        scores1((c + 1) % ns, r0 + tk, 1)
        absorb1(c % ns, r0, 0)

    nu = FLASH_UNROLL

    def body(i, carry):
        r0 = pl.multiple_of(i * (nu * tk), nu * tk)
        for c in range(nu):
            advance(c, r0 + c * tk)
        return carry

    n_loop = (n_main - 1) // nu
    lax.fori_loop(0, n_loop, body, 0)
    for c in range(n_loop * nu, n_main - 1):
        advance(c, c * tk)
    absorb((n_main - 1) % ns, (n_main - 1) * tk)
    _flash_finish(acc_ref, o_ref)


def _flash_ctx_kernel(o_in_ref, q_ref, kc_ref, vc_ref, o_ref, acc_ref, st_ref):
    del o_in_ref
    _flash_ctx_init(_flash_ctx_scores(_flash_queries(q_ref), kc_ref), vc_ref, acc_ref, st_ref)
    _flash_finish(acc_ref, o_ref)


def _flash(mq, mk, mvt, dims, with_ctx):
    B, S, CTX = dims
    T = mq.shape[0] if with_ctx else B * S
    tq = FLASH_TQ
    nq = S // tq
    ctx_blk0 = B * S // CTX
    npair = MLA_HEADS // 2
    acc = lambda n: [pltpu.VMEM((2, HEAD_DIM + DEN_ROWS, n), F32), pltpu.VMEM((2, 1, n), F32)]
    out_shape = jax.ShapeDtypeStruct((T, MLA_HEADS * MLA_V), BF16)
    kc_spec = pl.BlockSpec((CTX, 2 * LANES), lambda b, p, *_: (ctx_blk0 + b, p))
    vc_spec = pl.BlockSpec((LANES, CTX), lambda b, p, *_: (p, ctx_blk0 + b))

    o = pl.pallas_call(
        functools.partial(_flash_kernel, n_main=S // FLASH_TK),
        grid=(B, npair, nq),
        in_specs=[
            pl.BlockSpec((tq, 2 * LANES), lambda b, p, j: (b * nq + j, p)),
            pl.BlockSpec((S, 2 * LANES), lambda b, p, j: (b, p)),
            pl.BlockSpec((LANES, S), lambda b, p, j: (p, b)),
            kc_spec, vc_spec,
        ],
        out_specs=pl.BlockSpec((tq, LANES), lambda b, p, j: (b * nq + j, p)),
        out_shape=out_shape,
        scratch_shapes=[pltpu.VMEM((FLASH_TK, tq), F32)] * (2 * FLASH_SLOTS)
        + [pltpu.VMEM((1, tq), F32)] * (2 * FLASH_SLOTS) + acc(tq),
        compiler_params=_cparams(("arbitrary", "arbitrary", "arbitrary")),
        name="flash",
    )(mq, mk, mvt, mk, mvt)
    if not with_ctx:
        return o
    return pl.pallas_call(
        _flash_ctx_kernel,
        grid=(B, npair),
        in_specs=[
            pl.BlockSpec(memory_space=pl.ANY),
            pl.BlockSpec((CTX, 2 * LANES), lambda b, p: (ctx_blk0 + b, p)),
            kc_spec, vc_spec,
        ],
        out_specs=pl.BlockSpec((CTX, LANES), lambda b, p: (ctx_blk0 + b, p)),
        out_shape=out_shape,
        scratch_shapes=acc(CTX),
        input_output_aliases={0: 0},
        compiler_params=_cparams(("arbitrary", "arbitrary")),
        name="flash_ctx",
    )(o, mq, mk, mvt)


def _mlstm_gates(d, gates, bgate, tri):
    gb = gates + bgate
    ls = jnp.minimum(gb, 0.0) - jnp.log1p(jnp.exp(-jnp.abs(gb)))
    lane = lax.broadcasted_iota(jnp.int32, gb.shape, 1)
    is_f = ((lane >= 4) & (lane < 8)) | ((lane >= 12) & (lane < 16))
    xg = jnp.where(lane < 16, jnp.where(is_f, ls, gb), 0.0)
    x1 = xg.astype(BF16)
    r1 = xg - x1.astype(F32)
    x2 = r1.astype(BF16)
    x3 = (r1 - x2.astype(F32)).astype(BF16)
    cum = _dot(tri, x1) + _dot(tri, x2) + _dot(tri, x3)
    return xg, cum, xg.T, cum.T, jnp.sum(xg, axis=0, keepdims=True)


def _mlstm_kernel(xf_ref, xpf_ref, xnf_ref, vf_ref, gf_ref, xb_ref, xpb_ref, xnb_ref, vb_ref, gb_ref,
                  cw_ref, cb_ref, bgate_ref, tril_ref, triu_ref, hf_ref, hb_ref, c_ref, m_ref,
                  *, nb_ctx, nb_lat, nchk):
    @pl.when(pl.program_id(1) == 0)
    def _():
        c_ref[...] = jnp.zeros_like(c_ref)
        m_ref[...] = jnp.zeros_like(m_ref)

    L = ML_CHUNK
    bgate = bgate_ref[...]
    step = pl.program_id(1)
    in_ctx = step < nb_ctx
    pos = jnp.where(in_ctx, step, step - nb_ctx)
    starts = pos == 0
    ends = pos == jnp.where(in_ctx, nb_ctx, nb_lat) - 1
    qk_f = _short_conv(xf_ref[...], xpf_ref[...], xnf_ref[...], starts, ends, cw_ref, cb_ref)
    qk_b = _short_conv(xb_ref[...], xpb_ref[...], xnb_ref[...], ends, starts, cw_ref, cb_ref)
    q = (qk_f[0], qk_b[0])
    k = (qk_f[1], qk_b[1])
    v = (vf_ref[...], vb_ref[...])
    g_refs = (gf_ref, gb_ref)
    tris = (tril_ref[...], triu_ref[...])
    order = (list(range(nchk)), list(range(nchk - 1, -1, -1)))
    si = lax.broadcasted_iota(jnp.int32, (L, L), 0)
    ti = lax.broadcasted_iota(jnp.int32, (L, L), 1)
    valid = (si <= ti, si >= ti)
    lane_l = lax.broadcasted_iota(jnp.int32, (L, LANES), 1)
    row_l = lax.broadcasted_iota(jnp.int32, (LANES, L), 0)
    rows_of = lambda j: slice(j * L, (j + 1) * L)

    items = {}
    for d in range(2):
        for j in range(nchk):
            xg, cum, xg_t, cum_t, tot = _mlstm_gates(d, g_refs[d][rows_of(j), :], bgate, tris[d])
            for p in range(ML_HEADS // 2):
                cols = slice(p * LANES, (p + 1) * LANES)
                qp = q[d][rows_of(j), cols]
                kp = k[d][rows_of(j), cols]
                vt = v[d][rows_of(j), cols].astype(F32).T
                for e in range(2):
                    hd = 2 * p + e
                    ci = 8 * d + hd
                    cf = 8 * d + 4 + hd
                    in_head = (lane_l < HEAD_DIM) if e == 0 else (lane_l >= HEAD_DIM)
                    in_rows = (row_l < HEAD_DIM) if e == 0 else (row_l >= HEAD_DIM)
                    ones_row = HEAD_DIM if e == 0 else 0
                    b_row = cum_t[cf:cf + 1, :]
                    c_col = xg[:, ci:ci + 1] - cum[:, cf:cf + 1]
                    dm = jnp.where(valid[d], b_row + c_col, NEG)
                    b_tot = tot[:, cf:cf + 1]
                    a_row = b_tot - b_row + xg_t[ci:ci + 1, :]
                    items[d, j, hd] = dict(
                        qp=qp, km=jnp.where(in_head, kp, jnp.zeros_like(kp)),
                        vat=jnp.where(in_rows, vt, jnp.where(row_l == ones_row, 1.0, 0.0)),
                        ones_row=ones_row, b_row=b_row, b_tot=b_tot, dm=dm, a_row=a_row,
                        a_t=jnp.max(dm, axis=0, keepdims=True), a_max=jnp.max(a_row, axis=1, keepdims=True))

    for c in items.values():
        c["qk"] = (_dot_nt(c["km"], c["qp"]) * jnp.exp(c["dm"] - c["a_t"])).astype(BF16)

    for c in items.values():
        c["intra"] = _dot(c["vat"].astype(BF16), c["qk"])
        c["upd"] = _dot((c["vat"] * jnp.exp(c["a_row"] - c["a_max"])).astype(BF16), c["km"])

    state = {(d, hd): (c_ref[d, hd], m_ref[d * ML_HEADS + hd:d * ML_HEADS + hd + 1, 0:1])
             for d in range(2) for hd in range(ML_HEADS)}
    hs = {}
    for n in range(nchk):
        for d in range(2):
            j = order[d][n]
            for hd in range(ML_HEADS):
                c = items[d, j, hd]
                cst, m_prev = state[d, hd]
                g = c["b_row"] + m_prev
                mt = jnp.maximum(g, c["a_t"])
                haug = (jnp.exp(g - mt) * _dot_nt(cst.astype(BF16), c["qp"])
                        + jnp.exp(c["a_t"] - mt) * c["intra"])
                den = haug[c["ones_row"]:c["ones_row"] + 1, :]
                hs[d, j, hd] = haug / jnp.maximum(jnp.abs(den), jnp.exp(-mt))
                m_new = jnp.maximum(c["b_tot"] + m_prev, c["a_max"])
                cst = (jnp.exp(c["b_tot"] + m_prev - m_new) * cst
                       + jnp.exp(c["a_max"] - m_new) * c["upd"])
                state[d, hd] = (cst, m_new)
    for (d, hd), (cst, m_new) in state.items():
        c_ref[d, hd] = cst
        r = d * ML_HEADS + hd
        m_ref[r:r + 1, :] = jnp.broadcast_to(m_new, (1, LANES))

    for d, h_ref in enumerate((hf_ref, hb_ref)):
        for j in range(nchk):
            for p in range(ML_HEADS // 2):
                pair = jnp.where(row_l < HEAD_DIM, hs[d, j, 2 * p], hs[d, j, 2 * p + 1])
                h_ref[rows_of(j), p * LANES:(p + 1) * LANES] = pair.T


def _mlstm(l, mlqk, mlv, misc, prm, tabs, dims):
    T, W = mlqk.shape
    B, S, CTX = dims
    nchk = ML_STEP_CHUNKS
    blk = nchk * ML_CHUNK
    nb_ctx = CTX // blk
    nb_lat = S // blk
    ctx0 = B * S // blk

    def fwd(b, c):
        return jnp.where(c < nb_ctx, ctx0 + b * nb_ctx + c, b * nb_lat + (c - nb_ctx))

    def bwd(b, c):
        return jnp.where(c < nb_ctx, ctx0 + b * nb_ctx + (nb_ctx - 1 - c),
                         b * nb_lat + (nb_lat - 1 - (c - nb_ctx)))

    sub = blk // 8
    last8 = T // 8 - 1

    def specs(fn):
        return [pl.BlockSpec((blk, W), lambda b, c: (fn(b, c), 0)),
                pl.BlockSpec((8, W), lambda b, c: (jnp.maximum(fn(b, c) * sub - 1, 0), 0)),
                pl.BlockSpec((8, W), lambda b, c: (jnp.minimum((fn(b, c) + 1) * sub, last8), 0)),
                pl.BlockSpec((blk, ML_W), lambda b, c: (fn(b, c), 0)),
                pl.BlockSpec((blk, LANES), lambda b, c: (fn(b, c), 0))]

    return pl.pallas_call(
        functools.partial(_mlstm_kernel, nb_ctx=nb_ctx, nb_lat=nb_lat, nchk=nchk),
        grid=(B, nb_ctx + nb_lat),
        in_specs=specs(fwd) + specs(bwd) + [
            pl.BlockSpec((None, 8, W), lambda b, c: (l, 0, 0)),
            pl.BlockSpec((None, 1, W), lambda b, c: (l, 0, 0)),
            pl.BlockSpec((None, 1, LANES), lambda b, c: (l, 0, 0)),
            pl.BlockSpec((ML_CHUNK, ML_CHUNK), lambda b, c: (0, 0)),
            pl.BlockSpec((ML_CHUNK, ML_CHUNK), lambda b, c: (0, 0)),
        ],
        out_specs=[pl.BlockSpec((blk, ML_W), lambda b, c: (fwd(b, c), 0)),
                   pl.BlockSpec((blk, ML_W), lambda b, c: (bwd(b, c), 0))],
        out_shape=[jax.ShapeDtypeStruct((T, ML_W), F32)] * 2,
        scratch_shapes=[pltpu.VMEM((2, ML_HEADS, LANES, LANES), F32), pltpu.VMEM((8, LANES), F32)],
        compiler_params=_cparams(("arbitrary", "arbitrary")),
        name="mlstm",
    )(mlqk, mlqk, mlqk, mlv, misc, mlqk, mlqk, mlqk, mlv, misc,
      prm["conv_w"], prm["conv_b"], prm["b_gate"], tabs["tril"], tabs["triu"])


def _ffn_chunks(ff):
    step = 2 * MXU_DIM
    starts = list(range(0, ff - ff % step, step)) or [0]
    return [(c, (ff - c) if c == starts[-1] else step) for c in starts]


def _out_kernel(x_ref, mod_ref, ona_ref, omla_ref, hf_ref, hb_ref, mlo_ref, hg_ref, bd64_ref,
                w_out_ref, g_ref, wab_ref, wo_ref, o_ref):
    x = x_ref[...]
    gate1 = mod_ref[2:3, :]
    shift2 = mod_ref[3:4, :]
    scale2 = mod_ref[4:5, :]
    gate2 = mod_ref[5:6, :]

    h = hf_ref[...] + hb_ref[...]
    hn = h * lax.rsqrt(_group_sumsq(h, bd64_ref[...]) * (1.0 / HEAD_DIM) + EPS) * hg_ref[...]
    ob = (hn * _sigmoid(mlo_ref[...])).astype(BF16)
    mix = _dot(jnp.concatenate([ona_ref[...], ob, omla_ref[...]], axis=-1), w_out_ref[...])
    x1 = x + gate1 * mix

    h2 = ((_rms_rows(x1) * g_ref[...]) * (1.0 + scale2) + shift2).astype(BF16)
    acc = jnp.zeros_like(x1)
    ff = wo_ref.shape[0]
    for c0, w in _ffn_chunks(ff):
        a = _dot(h2, wab_ref[:, c0:c0 + w])
        b = _dot(h2, wab_ref[:, ff + c0:ff + c0 + w])
        act = (a * _sigmoid(a) * b).astype(BF16)
        acc = acc + _dot(act, wo_ref[c0:c0 + w, :])
    o_ref[...] = x1 + gate2 * acc


def _out(l, xs, mods, ona, omla, hf, hb, mlo, prm, tabs, dims, with_ctx):
    D = xs.shape[1]
    B, S, CTX = dims
    T = xs.shape[0] if with_ctx else B * S
    nt = T // ROW_TILE
    lat_tiles = S // ROW_TILE
    ff = prm["w_fo"].shape[1]

    def grp(i):
        return jnp.minimum(i // lat_tiles, B)

    row = lambda w: pl.BlockSpec((ROW_TILE, w), lambda i: (i, 0))
    lay = lambda shape: pl.BlockSpec((None,) + shape, lambda i: (l,) + (0,) * len(shape))
    big = lambda shape: pl.BlockSpec((None,) + shape, lambda i: (l,) + (0,) * len(shape),
                                     pipeline_mode=pl.Buffered(1))
    return pl.pallas_call(
        _out_kernel,
        grid=(nt,),
        in_specs=[
            row(D),
            pl.BlockSpec((None, None, 6, D), lambda i: (l, grp(i), 0, 0)),
            row(NA_W), row(MLA_HEADS * MLA_V), row(ML_W), row(ML_W), row(ML_W),
            lay((1, ML_W)),
            pl.BlockSpec((MXU_DIM, MXU_DIM), lambda i: (0, 0)),
            big((NA_W + ML_W + MLA_HEADS * MLA_V, D)),
            lay((1, D)),
            big((D, 2 * ff)), big((ff, D)),
        ],
        out_specs=row(D),
        out_shape=jax.ShapeDtypeStruct((T, D), F32),
        compiler_params=_cparams(("arbitrary",)),
        name="out_ffn",
    )(xs, mods, ona, omla, hf, hb, mlo, prm["head_gain"], tabs["bd64"],
      prm["w_out"], prm["g_ffn"], prm["w_ab"], prm["w_fo"])


def _block_diag_ones(group):
    idx = np.arange(MXU_DIM) // group
    return jnp.asarray((idx[:, None] == idx[None, :]).astype(np.float32), dtype=BF16)


def _rope_tables(S, n_ctx_rows):
    n_freq = MLA_ROPE // 4
    inv = ROPE_BASE ** (-jnp.arange(n_freq, dtype=F32) / n_freq)
    t = jnp.arange(S, dtype=jnp.int32)
    row = (t // GRID_W).astype(F32)
    col = (t % GRID_W).astype(F32)
    ang = jnp.concatenate([row[:, None] * inv, col[:, None] * inv], axis=-1)
    ang = jnp.concatenate([ang, jnp.zeros((n_ctx_rows, MLA_ROPE // 2), F32)], axis=0)
    cos, sin = jnp.cos(ang), jnp.sin(ang)
    n = S + n_ctx_rows
    half = MLA_ROPE // 2
    ones = jnp.ones((n, MLA_NOPE), F32)
    z = lambda w: jnp.zeros((n, w), F32)
    tail = LANES - MLA_NOPE - MLA_ROPE
    del half
    cos_t = jnp.concatenate([ones, cos, cos, jnp.ones((n, tail), F32)], axis=-1)
    sin_t = jnp.concatenate([z(MLA_NOPE), -sin, sin, z(tail)], axis=-1)
    return cos_t, sin_t


def _na_bias(rpb, rows):
    depth, H = rpb.shape[:2]
    qr, kr_n = ATT_BLOCK // GRID_W, NA_KEY_ROWS // GRID_W
    nb = rows // qr
    cols = np.arange(GRID_W)
    cs = np.clip(cols - NA_WIN_C // 2, 0, GRID_W - NA_WIN_C)
    col_ok = (cols[None, :] >= cs[:, None]) & (cols[None, :] < cs[:, None] + NA_WIN_C)
    dc = np.clip(cols[None, :] - cols[:, None] + NA_WIN_C - 1, 0, 2 * NA_WIN_C - 2)
    dr_all, ok_all = [], []
    for j in (0, 1, nb - 1):
        ks = int(np.clip(j - 1, 0, nb - 3)) * qr
        r = j * qr + np.arange(qr)
        rs = np.clip(r - NA_WIN_R // 2, 0, rows - NA_WIN_R)
        kr = ks + np.arange(kr_n)
        ok_all.append((kr[None, :] >= rs[:, None]) & (kr[None, :] < rs[:, None] + NA_WIN_R))
        dr_all.append(np.clip(kr[None, :] - r[:, None] + NA_WIN_R - 1, 0, 2 * NA_WIN_R - 2))
    dr_all = np.stack(dr_all)
    ok_all = np.stack(ok_all)
    t1 = jnp.take(rpb, jnp.asarray(dc.reshape(-1)), axis=3).reshape(depth, H, 2 * NA_WIN_R - 1, GRID_W, GRID_W)
    t1 = jnp.where(jnp.asarray(col_ok), t1, NEG).astype(BF16)
    t2 = jnp.take(t1, jnp.asarray(dr_all.reshape(-1)), axis=2).reshape(depth, H, 3, qr, kr_n, GRID_W, GRID_W)
    t2 = jnp.where(jnp.asarray(ok_all)[None, None, :, :, :, None, None], t2, NEG)
    bias = t2.transpose(0, 2, 1, 4, 6, 3, 5).reshape(depth, 3, H, NA_KEY_ROWS, ATT_BLOCK)
    none = jnp.full((depth, 1, H, NA_KEY_ROWS, ATT_BLOCK), NEG, BF16)
    return jnp.concatenate([bias, none], axis=1)


def _prepare(w_in, w_uq, w_ukv, w_out, w_ffn_in, w_ffn_out, g_mix, g_ffn, b_gate, na_qk_gain,
             ml_conv_w, ml_conv_b, ml_head_gain, mla_gq, mla_gkv, mla_qk_gain):
    depth, D, _ = w_in.shape
    o_gate = C_MLO + ML_W
    o_ql = o_gate + 4 * ML_HEADS
    o_kr = o_ql + Q_LORA + KV_LORA
    half = MLA_ROPE // 2
    zc = lambda w: jnp.zeros((depth, D, w), w_in.dtype)
    w_in_p = jnp.concatenate([
        w_in[..., :o_gate], w_in[..., o_ql:o_kr],
        w_in[..., o_gate:o_ql], zc(MISC_ROPE_LANE - 4 * ML_HEADS),
        w_in[..., o_kr:], zc(LANES - MISC_ROPE_LANE - MLA_ROPE),
        zc(MISC_ROPE_LANE), w_in[..., o_kr + half:], w_in[..., o_kr:o_kr + half],
        zc(LANES - MISC_ROPE_LANE - MLA_ROPE)], axis=-1).astype(BF16)
    pad_h = lambda a, w: jnp.pad(a, [(0, 0)] * (a.ndim - 1) + [(0, LANES - w)])

    def partner(a):
        lo, hi = a[..., MLA_NOPE:MLA_NOPE + half], a[..., MLA_NOPE + half:MLA_QK]
        return jnp.concatenate([jnp.zeros_like(a[..., :MLA_NOPE]), hi, lo,
                                jnp.zeros_like(a[..., MLA_QK:])], axis=-1)

    w_uq_h = pad_h(w_uq.reshape(depth, Q_LORA, MLA_HEADS, MLA_QK), MLA_QK)
    w_uq_p = w_uq_h.reshape(depth, Q_LORA, MLA_HEADS * LANES).astype(BF16)
    w_uq_s = partner(w_uq_h).reshape(depth, Q_LORA, MLA_HEADS * LANES).astype(BF16)
    ukv = w_ukv.reshape(depth, KV_LORA, MLA_HEADS, MLA_NOPE + MLA_V)
    w_uk_p = pad_h(ukv[..., :MLA_NOPE], MLA_NOPE).reshape(depth, KV_LORA, MLA_HEADS * LANES).astype(BF16)
    w_uv = ukv[..., MLA_NOPE:].reshape(depth, KV_LORA, MLA_HEADS * MLA_V).astype(BF16)
    mla_g = pad_h(mla_qk_gain, MLA_QK)
    return {
        "w_in": w_in_p, "w_uq": w_uq_p, "w_uqs": w_uq_s, "w_uk": w_uk_p, "w_uv": w_uv,
        "w_out": w_out.astype(BF16),
        "w_ab": w_ffn_in.astype(BF16),
        "w_fo": w_ffn_out.astype(BF16),
        "g_mix": g_mix[:, None, :], "g_ffn": g_ffn[:, None, :],
        "gq": mla_gq[:, None, :], "gkv": mla_gkv[:, None, :],
        "na_gq": jnp.tile(na_qk_gain[:, 0:1, :], (1, 1, NA_HEADS)) * NA_SCALE,
        "na_gk": jnp.tile(na_qk_gain[:, 1:2, :], (1, 1, NA_HEADS)),
        "mla_gq": jnp.tile(mla_g[:, 0:1, :], (1, 1, MLA_HEADS)) * (MLA_SCALE * LOG2E),
        "mla_gk": jnp.tile(mla_g[:, 1:2, :], (1, 1, MLA_HEADS)),
        "mla_gqs": jnp.tile(partner(mla_g[:, 0:1, :]), (1, 1, MLA_HEADS)) * (MLA_SCALE * LOG2E),
        "mla_gks": jnp.tile(partner(mla_g[:, 1:2, :]), (1, 1, MLA_HEADS)),
        "head_gain": ml_head_gain.reshape(depth, 1, ML_W),
        "b_gate": pad_h(b_gate, 4 * ML_HEADS)[:, None, :],
        "conv_w": jnp.pad(ml_conv_w, ((0, 0), (0, 8 - ml_conv_w.shape[1]), (0, 0))),
        "conv_b": ml_conv_b[:, None, :],
    }


def kernel(x, c, ctx, c_ctx, w_mod, b_mod, g_mix, g_ffn, w_in, b_gate, na_qk_gain, na_rpb,
           ml_conv_w, ml_conv_b, ml_head_gain, mla_gq, mla_gkv, w_uq, w_ukv, mla_qk_gain,
           w_out, w_ffn_in, w_ffn_out):
    B, S, D = x.shape
    CTX = ctx.shape[1]
    depth = w_in.shape[0]
    assert CTX == ATT_BLOCK and S % ROW_TILE == 0 and (B * CTX) % ROW_TILE == 0
    assert CTX % (ML_STEP_CHUNKS * ML_CHUNK) == 0 and S % (ML_STEP_CHUNKS * ML_CHUNK) == 0
    assert S % (2 * FLASH_TK) == 0 and S % FLASH_TQ == 0 and S // ATT_BLOCK >= 3 and B + 1 <= 8
    dims = (B, S, CTX)

    prm = _prepare(w_in, w_uq, w_ukv, w_out, w_ffn_in, w_ffn_out, g_mix, g_ffn, b_gate, na_qk_gain,
                   ml_conv_w, ml_conv_b, ml_head_gain, mla_gq, mla_gkv, mla_qk_gain)
    cos_t, sin_t = _rope_tables(S, B * CTX)
    tri = np.tril(np.ones((ML_CHUNK, ML_CHUNK), np.float32))
    tabs = {
        "bd64": _block_diag_ones(HEAD_DIM), "bd128": _block_diag_ones(LANES),
        "cos": cos_t, "sin": sin_t,
        "tril": jnp.asarray(tri, dtype=BF16), "triu": jnp.asarray(tri.T, dtype=BF16),
    }
    bias = _na_bias(na_rpb, S // GRID_W)

    cond8 = jnp.concatenate([c, c_ctx[None, :], jnp.zeros((8 - B - 1, D), F32)], axis=0)
    mods = _modulation(cond8, w_mod, b_mod)
    mods = mods.reshape(depth, 8, 6, D)

    xs = jnp.concatenate([x.reshape(B * S, D), ctx.reshape(B * CTX, D)], axis=0)
    for l in range(depth):
        with_ctx = l < depth - 1
        naq, nak, mlqk, mlv, mlo, misc, mq, mk, navt, mvt = _proj(l, xs, mods, prm, tabs, dims)
        ona = _na(l, naq, nak, navt, bias, dims, with_ctx)
        omla = _flash(mq, mk, mvt, dims, with_ctx)
        hf, hb = _mlstm(l, mlqk, mlv, misc, prm, tabs, dims)
        xs = _out(l, xs, mods, ona, omla, hf, hb, mlo, prm, tabs, dims, with_ctx)
    return xs.reshape(B, S, D)
```

```python
import functools

import numpy as np
import jax
import jax.numpy as jnp
from jax import lax
from jax.experimental import pallas as pl
from jax.experimental.pallas import tpu as pltpu

GRID_W = 64
HEAD_DIM = 64
NA_HEADS = 4
NA_WIN_R = 8
NA_WIN_C = 16
ML_HEADS = 4
ML_CHUNK = 128
ML_STEP_CHUNKS = 2
MLA_HEADS = 8
MLA_NOPE = 64
MLA_ROPE = 32
MLA_V = 64
Q_LORA = 384
KV_LORA = 256
ROPE_BASE = 10000.0
EPS = 1e-6
NA_W = NA_HEADS * HEAD_DIM
ML_W = ML_HEADS * HEAD_DIM
MLA_QK = MLA_NOPE + MLA_ROPE
NA_SCALE = HEAD_DIM ** -0.5
MLA_SCALE = MLA_QK ** -0.5
LOG2E = 1.4426950408889634

LANES = 128
MXU_DIM = 256
VMEM_LIMIT = 56 * 1024 * 1024

ROW_TILE = 512
ATT_BLOCK = 256
NA_KEY_ROWS = 768
FLASH_TK = 512
FLASH_TQ = 1024
FLASH_SLOTS = 2
FLASH_UNROLL = 4
NEG = -1e30
DEN_ROWS = 16

C_NAQ, C_NAK, C_NAV = 0, 256, 512
C_MLQK, C_MLV, C_MLO = 768, 1280, 1536
C_QL = 1792
C_KVL = C_QL + Q_LORA
C_MISC = C_KVL + KV_LORA
C_MISC2 = C_MISC + LANES
D_IN_PAD = C_MISC2 + LANES
MISC_ROPE_LANE = 64

BF16 = jnp.bfloat16
F32 = jnp.float32


def _cparams(sem):
    return pltpu.CompilerParams(dimension_semantics=sem, vmem_limit_bytes=VMEM_LIMIT)


def _dot(a, b):
    return jnp.dot(a, b, preferred_element_type=F32)


def _dot_nt(a, b):
    return lax.dot_general(a, b, (((1,), (1,)), ((), ())), preferred_element_type=F32)


def _split2(x):
    hi = x.astype(BF16)
    lo = (x - hi.astype(F32)).astype(BF16)
    return hi, lo


def _group_sumsq(x, bd):
    x2 = x * x
    hi, lo = _split2(x2)
    outs = []
    for c in range(x.shape[1] // MXU_DIM):
        sl = slice(c * MXU_DIM, (c + 1) * MXU_DIM)
        outs.append(_dot(hi[:, sl], bd) + _dot(lo[:, sl], bd))
    return outs[0] if len(outs) == 1 else jnp.concatenate(outs, axis=-1)


def _rms_rows(x):
    return x * lax.rsqrt(jnp.mean(x * x, axis=-1, keepdims=True) + EPS)


def _sigmoid(x):
    return 1.0 / (1.0 + jnp.exp(-x))


def _mod_kernel(c_ref, w_ref, b_ref, o_ref):
    c = c_ref[...]
    a = c * _sigmoid(c)
    o_ref[...] = jnp.dot(a, w_ref[...], preferred_element_type=F32,
                         precision=lax.Precision.HIGHEST) + b_ref[...]


def _modulation(cond8, w_mod, b_mod):
    depth, d, d6 = w_mod.shape
    bw = 2 * d
    return pl.pallas_call(
        _mod_kernel,
        grid=(depth, d6 // bw),
        in_specs=[
            pl.BlockSpec((8, d), lambda l, j: (0, 0)),
            pl.BlockSpec((None, d, bw), lambda l, j: (l, 0, j)),
            pl.BlockSpec((None, 1, bw), lambda l, j: (l, 0, j)),
        ],
        out_specs=pl.BlockSpec((None, 8, bw), lambda l, j: (l, 0, j)),
        out_shape=jax.ShapeDtypeStruct((depth, 8, d6), F32),
        compiler_params=_cparams(("arbitrary", "arbitrary")),
        name="modulation",
    )(cond8, w_mod, b_mod.reshape(depth, 1, d6))


def _proj_kernel(x_ref, mod_ref, g_ref, w_in_ref, w_uq_ref, w_uqs_ref, w_uk_ref, w_uv_ref,
                 gq_ref, gkv_ref, na_gq_ref, na_gk_ref, mla_gq_ref, mla_gk_ref, mla_gqs_ref, mla_gks_ref,
                 bd64_ref, bd128_ref, cos_ref, sin_ref,
                 naq_ref, nak_ref, mlqk_ref, mlv_ref, mlo_ref, misc_ref,
                 mq_ref, mk_ref, nav_ref, mv_ref):
    x = x_ref[...]
    shift = mod_ref[0:1, :]
    scale = mod_ref[1:2, :]
    h = (_rms_rows(x) * g_ref[...]) * (1.0 + scale) + shift
    hb = h.astype(BF16)

    def proj(c0, width):
        return _dot(hb, w_in_ref[:, c0:c0 + width])

    bd64 = bd64_ref[...]
    bd128 = bd128_ref[...]

    lat = proj(C_QL, D_IN_PAD - C_QL)
    ql = lat[:, :Q_LORA]
    kvl = lat[:, C_KVL - C_QL:C_MISC - C_QL]
    misc = lat[:, C_MISC - C_QL:C_MISC2 - C_QL]
    kx = lat[:, C_MISC2 - C_QL:]
    misc_ref[...] = misc
    qn = (_rms_rows(ql) * gq_ref[...]).astype(BF16)
    kvn = (_rms_rows(kvl) * gkv_ref[...]).astype(BF16)

    na = proj(C_NAQ, 3 * NA_W)
    pq = na[:, :NA_W]
    pk = na[:, NA_W:2 * NA_W]
    nav_ref[...] = na[:, 2 * NA_W:].T.astype(BF16)

    qr = _dot(qn, w_uq_ref[...])
    qx = _dot(qn, w_uqs_ref[...])
    lane = lax.broadcasted_iota(jnp.int32, misc.shape, 1)
    krope = jnp.where((lane >= MISC_ROPE_LANE) & (lane < MISC_ROPE_LANE + MLA_ROPE), misc, 0.0)
    kr = _dot(kvn, w_uk_ref[...]) + jnp.concatenate([krope] * MLA_HEADS, axis=-1)
    mv_ref[...] = _dot(kvn, w_uv_ref[...]).T.astype(BF16)

    ml = proj(C_MLQK, 4 * ML_W)
    mlqk_ref[...] = ml[:, :2 * ML_W]
    mlv_ref[...] = ml[:, 2 * ML_W:3 * ML_W].astype(BF16)
    mlo_ref[...] = ml[:, 3 * ML_W:]

    naq_ref[...] = (pq * lax.rsqrt(_group_sumsq(pq, bd64) * (1.0 / HEAD_DIM) + EPS)
                    * na_gq_ref[...]).astype(BF16)
    nak_ref[...] = (pk * lax.rsqrt(_group_sumsq(pk, bd64) * (1.0 / HEAD_DIM) + EPS)
                    * na_gk_ref[...]).astype(BF16)

    cos = cos_ref[...]
    sin = sin_ref[...]
    rq = lax.rsqrt(_group_sumsq(qr, bd128) * (1.0 / MLA_QK) + EPS)
    rk = lax.rsqrt(_group_sumsq(kr, bd128) * (1.0 / MLA_QK) + EPS)
    kxs = kx * sin
    for g in range(MLA_HEADS):
        sl = slice(g * LANES, (g + 1) * LANES)
        mq_ref[:, sl] = (rq[:, sl] * (qr[:, sl] * mla_gq_ref[:, sl] * cos
                                      + qx[:, sl] * mla_gqs_ref[:, sl] * sin)).astype(BF16)
        mk_ref[:, sl] = (rk[:, sl] * (kr[:, sl] * mla_gk_ref[:, sl] * cos
                                      + kxs * mla_gks_ref[:, sl])).astype(BF16)


def _proj(l, xs, mods, prm, tabs, dims):
    T, D = xs.shape
    B, S, CTX = dims
    nt = T // ROW_TILE
    lat_tiles = S // ROW_TILE

    def grp(i):
        return jnp.minimum(i // lat_tiles, B)

    def tab_row(i):
        return jnp.where(i < B * lat_tiles, i % lat_tiles, lat_tiles + (i - B * lat_tiles))

    row = lambda w: pl.BlockSpec((ROW_TILE, w), lambda i: (i, 0))
    lay = lambda shape: pl.BlockSpec((None,) + shape, lambda i: (l,) + (0,) * len(shape))
    const = lambda shape: pl.BlockSpec(shape, lambda i: (0,) * len(shape))
    tab = pl.BlockSpec((ROW_TILE, LANES), lambda i: (tab_row(i), 0))
    out_w = [(NA_W, BF16), (NA_W, BF16), (2 * ML_W, F32), (ML_W, BF16), (ML_W, F32),
             (LANES, F32), (MLA_HEADS * LANES, BF16), (MLA_HEADS * LANES, BF16)]
    out_t = [NA_W, MLA_HEADS * MLA_V]
    return pl.pallas_call(
        _proj_kernel,
        grid=(nt,),
        in_specs=[
            row(D),
            pl.BlockSpec((None, None, 6, D), lambda i: (l, grp(i), 0, 0)),
            lay((1, D)),
            lay((D, D_IN_PAD)), lay((Q_LORA, MLA_HEADS * LANES)), lay((Q_LORA, MLA_HEADS * LANES)),
            lay((KV_LORA, MLA_HEADS * LANES)), lay((KV_LORA, MLA_HEADS * MLA_V)),
            lay((1, Q_LORA)), lay((1, KV_LORA)), lay((1, NA_W)), lay((1, NA_W)),
            lay((1, MLA_HEADS * LANES)), lay((1, MLA_HEADS * LANES)),
            lay((1, MLA_HEADS * LANES)), lay((1, MLA_HEADS * LANES)),
            const((MXU_DIM, MXU_DIM)), const((MXU_DIM, MXU_DIM)),
            tab, tab,
        ],
        out_specs=[row(w) for w, _ in out_w]
        + [pl.BlockSpec((w, ROW_TILE), lambda i: (0, i)) for w in out_t],
        out_shape=[jax.ShapeDtypeStruct((T, w), dt) for w, dt in out_w]
        + [jax.ShapeDtypeStruct((w, T), BF16) for w in out_t],
        compiler_params=_cparams(("arbitrary",)),
        name="proj",
    )(xs, mods, prm["g_mix"], prm["w_in"], prm["w_uq"], prm["w_uqs"], prm["w_uk"], prm["w_uv"],
      prm["gq"], prm["gkv"], prm["na_gq"], prm["na_gk"],
      prm["mla_gq"], prm["mla_gk"], prm["mla_gqs"], prm["mla_gks"],
      tabs["bd64"], tabs["bd128"], tabs["cos"], tabs["sin"])


def _short_conv(x, prev_blk, next_blk, first, last, w_ref, b_ref):
    n = x.shape[0]
    prev_row = jnp.where(first, 0.0, prev_blk[7:8, :])
    next_row = jnp.where(last, 0.0, next_blk[0:1, :])
    ridx = lax.broadcasted_iota(jnp.int32, x.shape, 0)
    xm1 = jnp.where(ridx == 0, prev_row, pltpu.roll(x, 1, 0))
    xp1 = jnp.where(ridx == n - 1, next_row, pltpu.roll(x, n - 1, 0))
    y = b_ref[...] + xm1 * w_ref[0:1, :] + x * w_ref[1:2, :] + xp1 * w_ref[2:3, :]
    y = y * _sigmoid(y)
    return y[:, :ML_W].astype(BF16), (y[:, ML_W:] * (HEAD_DIM ** -0.5)).astype(BF16)


def _na_kernel(q_ref, k_ref, vt_ref, kc_ref, vct_ref, bias_ref, o_ref, *, nb):
    j = pl.program_id(1)
    start = pl.multiple_of(jnp.clip(j - 1, 0, nb - 3) * ATT_BLOCK, ATT_BLOCK)
    q = q_ref[...]
    lane = lax.broadcasted_iota(jnp.int32, (ATT_BLOCK, LANES), 1)
    scores = []
    for h in range(NA_HEADS):
        p, e = divmod(h, 2)
        cols = slice(p * LANES, (p + 1) * LANES)
        qp = q[:, cols]
        keep = (lane < HEAD_DIM) if e == 0 else (lane >= HEAD_DIM)
        qt = jnp.where(keep, qp, jnp.zeros_like(qp)).astype(F32).T.astype(BF16)
        s_loc = _dot(k_ref[pl.ds(start, NA_KEY_ROWS), cols], qt) + bias_ref[h].astype(F32)
        s_ctx = _dot(kc_ref[:, cols], qt)
        scores.append((s_loc, s_ctx))
    probs = []
    for s_loc, s_ctx in scores:
        m = jnp.maximum(jnp.max(s_loc, axis=0, keepdims=True), jnp.max(s_ctx, axis=0, keepdims=True))
        probs.append((jnp.exp(s_loc - m).astype(BF16), jnp.exp(s_ctx - m).astype(BF16)))
    outs = []
    for h, (p_loc, p_ctx) in enumerate(probs):
        rows = slice(h * HEAD_DIM, (h + 1) * HEAD_DIM)
        ot = (_dot(_flash_values(vt_ref[rows, pl.ds(start, NA_KEY_ROWS)]), p_loc)
              + _dot(_flash_values(vct_ref[rows, :]), p_ctx))
        outs.append(ot[:HEAD_DIM, :] / ot[HEAD_DIM:HEAD_DIM + 1, :])
    o_ref[...] = jnp.concatenate(outs, axis=0).T.astype(o_ref.dtype)


def _na(l, naq, nak, navt, bias, dims, with_ctx):
    B, S, CTX = dims
    nb = S // ATT_BLOCK
    ctx_blk0 = B * S // CTX
    n_steps = nb + 1 if with_ctx else nb
    rows_out = naq.shape[0] if with_ctx else B * S

    def qrow(b, j):
        return jnp.where(j < nb, b * nb + j, ctx_blk0 + b)

    def variant(j):
        return jnp.where(j == 0, 0, jnp.where(j == nb - 1, 2, jnp.where(j == nb, 3, 1)))

    kern = functools.partial(_na_kernel, nb=nb)
    return pl.pallas_call(
        kern,
        grid=(B, n_steps),
        in_specs=[
            pl.BlockSpec((ATT_BLOCK, NA_W), lambda b, j: (qrow(b, j), 0)),
            pl.BlockSpec((S, NA_W), lambda b, j: (b, 0)),
            pl.BlockSpec((NA_W, S), lambda b, j: (0, b)),
            pl.BlockSpec((CTX, NA_W), lambda b, j: (ctx_blk0 + b, 0)),
            pl.BlockSpec((NA_W, CTX), lambda b, j: (0, ctx_blk0 + b)),
            pl.BlockSpec((None, None, NA_HEADS, NA_KEY_ROWS, ATT_BLOCK), lambda b, j: (l, variant(j), 0, 0, 0)),
        ],
        out_specs=pl.BlockSpec((ATT_BLOCK, NA_W), lambda b, j: (qrow(b, j), 0)),
        out_shape=jax.ShapeDtypeStruct((rows_out, NA_W), BF16),
        compiler_params=_cparams(("arbitrary", "arbitrary")),
        name="na",
    )(naq, nak, navt, nak, navt, bias)


def _flash_values(v):
    return jnp.concatenate([v, jnp.ones((DEN_ROWS, v.shape[1]), BF16)], axis=0)


def _flash_queries(q_ref):
    q = q_ref[...]
    return [q[:, e * LANES:(e + 1) * LANES].astype(F32).T.astype(BF16) for e in range(2)]


def _flash_ctx_scores(qs, kc_ref):
    return [_dot(kc_ref[:, e * LANES:(e + 1) * LANES], qs[e]) for e in range(2)]


def _flash_ctx_init(sts, vc_ref, acc_ref, st_ref):
    ps = []
    for e, st in enumerate(sts):
        m = jnp.max(st, axis=0, keepdims=True)
        st_ref[e] = m
        ps.append(jnp.exp2(st - m).astype(BF16))
    for e, p in enumerate(ps):
        acc_ref[e] = _dot(_flash_values(vc_ref[e * HEAD_DIM:(e + 1) * HEAD_DIM, :]), p)


def _flash_finish(acc_ref, o_ref):
    ot = jnp.concatenate([acc_ref[e, :HEAD_DIM, :] / acc_ref[e, HEAD_DIM:HEAD_DIM + 1, :]
                          for e in range(2)], axis=0)
    o_ref[...] = ot.T.astype(o_ref.dtype)


def _flash_kernel(q_ref, km_ref, vm_ref, kc_ref, vc_ref, o_ref, *scratch, n_main):
    ns = FLASH_SLOTS
    s_refs = [scratch[2 * i:2 * i + 2] for i in range(ns)]
    mx_refs = [scratch[2 * ns + 2 * i:2 * ns + 2 * i + 2] for i in range(ns)]
    acc_ref, st_ref = scratch[4 * ns:4 * ns + 2]
    tk = FLASH_TK
    qs = _flash_queries(q_ref)

    def scores1(slot, r0, e):
        st = _dot(km_ref[pl.ds(r0, tk), e * LANES:(e + 1) * LANES], qs[e])
        s_refs[slot][e][...] = st
        mx_refs[slot][e][...] = jnp.max(st, axis=0, keepdims=True)

    def absorb1(slot, r0, e):
        m = st_ref[e]
        m_new = jnp.maximum(m, mx_refs[slot][e][...])
        st_ref[e] = m_new
        alpha = jnp.exp2(m - m_new)
        p = jnp.exp2(s_refs[slot][e][...] - m_new).astype(BF16)
        v = vm_ref[e * HEAD_DIM:(e + 1) * HEAD_DIM, pl.ds(r0, tk)]
        acc_ref[e] = alpha * acc_ref[e] + _dot(_flash_values(v), p)

    def scores(slot, r0):
        for e in range(2):
            scores1(slot, r0, e)

    def absorb(slot, r0):
        for e in range(2):
            absorb1(slot, r0, e)

    ctx_scores = _flash_ctx_scores(qs, kc_ref)
    scores(0, 0)
    _flash_ctx_init(ctx_scores, vc_ref, acc_ref, st_ref)

    def advance(c, r0):
        scores1((c + 1) % ns, r0 + tk, 0)
        absorb1(c % ns, r0, 1)
        scores1((c + 1) % ns, r0 + tk, 1)
        absorb1(c % ns, r0, 0)

    nu = FLASH_UNROLL

    def body(i, carry):
        r0 = pl.multiple_of(i * (nu * tk), nu * tk)
        for c in range(nu):
            advance(c, r0 + c * tk)
        return carry

    n_loop = (n_main - 1) // nu
    lax.fori_loop(0, n_loop, body, 0)
    for c in range(n_loop * nu, n_main - 1):
        advance(c, c * tk)
    absorb((n_main - 1) % ns, (n_main - 1) * tk)
    _flash_finish(acc_ref, o_ref)


def _flash_ctx_kernel(o_in_ref, q_ref, kc_ref, vc_ref, o_ref, acc_ref, st_ref):
    del o_in_ref
    _flash_ctx_init(_flash_ctx_scores(_flash_queries(q_ref), kc_ref), vc_ref, acc_ref, st_ref)
    _flash_finish(acc_ref, o_ref)


def _flash(mq, mk, mvt, dims, with_ctx):
    B, S, CTX = dims
    T = mq.shape[0] if with_ctx else B * S
    tq = FLASH_TQ
    nq = S // tq
    ctx_blk0 = B * S // CTX
    npair = MLA_HEADS // 2
    acc = lambda n: [pltpu.VMEM((2, HEAD_DIM + DEN_ROWS, n), F32), pltpu.VMEM((2, 1, n), F32)]
    out_shape = jax.ShapeDtypeStruct((T, MLA_HEADS * MLA_V), BF16)
    kc_spec = pl.BlockSpec((CTX, 2 * LANES), lambda b, p, *_: (ctx_blk0 + b, p))
    vc_spec = pl.BlockSpec((LANES, CTX), lambda b, p, *_: (p, ctx_blk0 + b))

    o = pl.pallas_call(
        functools.partial(_flash_kernel, n_main=S // FLASH_TK),
        grid=(B, npair, nq),
        in_specs=[
            pl.BlockSpec((tq, 2 * LANES), lambda b, p, j: (b * nq + j, p)),
            pl.BlockSpec((S, 2 * LANES), lambda b, p, j: (b, p)),
            pl.BlockSpec((LANES, S), lambda b, p, j: (p, b)),
            kc_spec, vc_spec,
        ],
        out_specs=pl.BlockSpec((tq, LANES), lambda b, p, j: (b * nq + j, p)),
        out_shape=out_shape,
        scratch_shapes=[pltpu.VMEM((FLASH_TK, tq), F32)] * (2 * FLASH_SLOTS)
        + [pltpu.VMEM((1, tq), F32)] * (2 * FLASH_SLOTS) + acc(tq),
        compiler_params=_cparams(("arbitrary", "arbitrary", "arbitrary")),
        name="flash",
    )(mq, mk, mvt, mk, mvt)
    if not with_ctx:
        return o
    return pl.pallas_call(
        _flash_ctx_kernel,
        grid=(B, npair),
        in_specs=[
            pl.BlockSpec(memory_space=pl.ANY),
            pl.BlockSpec((CTX, 2 * LANES), lambda b, p: (ctx_blk0 + b, p)),
            kc_spec, vc_spec,
        ],
        out_specs=pl.BlockSpec((CTX, LANES), lambda b, p: (ctx_blk0 + b, p)),
        out_shape=out_shape,
        scratch_shapes=acc(CTX),
        input_output_aliases={0: 0},
        compiler_params=_cparams(("arbitrary", "arbitrary")),
        name="flash_ctx",
    )(o, mq, mk, mvt)


def _mlstm_gates(d, gates, bgate, tri):
    gb = gates + bgate
    ls = jnp.minimum(gb, 0.0) - jnp.log1p(jnp.exp(-jnp.abs(gb)))
    lane = lax.broadcasted_iota(jnp.int32, gb.shape, 1)
    is_f = ((lane >= 4) & (lane < 8)) | ((lane >= 12) & (lane < 16))
    xg = jnp.where(lane < 16, jnp.where(is_f, ls, gb), 0.0)
    x1 = xg.astype(BF16)
    r1 = xg - x1.astype(F32)
    x2 = r1.astype(BF16)
    x3 = (r1 - x2.astype(F32)).astype(BF16)
    cum = _dot(tri, x1) + _dot(tri, x2) + _dot(tri, x3)
    return xg, cum, xg.T, cum.T, jnp.sum(xg, axis=0, keepdims=True)


def _mlstm_kernel(xf_ref, xpf_ref, xnf_ref, vf_ref, gf_ref, xb_ref, xpb_ref, xnb_ref, vb_ref, gb_ref,
                  cw_ref, cb_ref, bgate_ref, tril_ref, triu_ref, hf_ref, hb_ref, c_ref, m_ref,
                  *, nb_ctx, nb_lat, nchk):
    @pl.when(pl.program_id(1) == 0)
    def _():
        c_ref[...] = jnp.zeros_like(c_ref)
        m_ref[...] = jnp.zeros_like(m_ref)

    L = ML_CHUNK
    bgate = bgate_ref[...]
    step = pl.program_id(1)
    in_ctx = step < nb_ctx
    pos = jnp.where(in_ctx, step, step - nb_ctx)
    starts = pos == 0
    ends = pos == jnp.where(in_ctx, nb_ctx, nb_lat) - 1
    qk_f = _short_conv(xf_ref[...], xpf_ref[...], xnf_ref[...], starts, ends, cw_ref, cb_ref)
    qk_b = _short_conv(xb_ref[...], xpb_ref[...], xnb_ref[...], ends, starts, cw_ref, cb_ref)
    q = (qk_f[0], qk_b[0])
    k = (qk_f[1], qk_b[1])
    v = (vf_ref[...], vb_ref[...])
    g_refs = (gf_ref, gb_ref)
    tris = (tril_ref[...], triu_ref[...])
    order = (list(range(nchk)), list(range(nchk - 1, -1, -1)))
    si = lax.broadcasted_iota(jnp.int32, (L, L), 0)
    ti = lax.broadcasted_iota(jnp.int32, (L, L), 1)
    valid = (si <= ti, si >= ti)
    lane_l = lax.broadcasted_iota(jnp.int32, (L, LANES), 1)
    row_l = lax.broadcasted_iota(jnp.int32, (LANES, L), 0)
    rows_of = lambda j: slice(j * L, (j + 1) * L)

    items = {}
    for d in range(2):
        for j in range(nchk):
            xg, cum, xg_t, cum_t, tot = _mlstm_gates(d, g_refs[d][rows_of(j), :], bgate, tris[d])
            for p in range(ML_HEADS // 2):
                cols = slice(p * LANES, (p + 1) * LANES)
                qp = q[d][rows_of(j), cols]
                kp = k[d][rows_of(j), cols]
                vt = v[d][rows_of(j), cols].astype(F32).T
                for e in range(2):
                    hd = 2 * p + e
                    ci = 8 * d + hd
                    cf = 8 * d + 4 + hd
                    in_head = (lane_l < HEAD_DIM) if e == 0 else (lane_l >= HEAD_DIM)
                    in_rows = (row_l < HEAD_DIM) if e == 0 else (row_l >= HEAD_DIM)
                    ones_row = HEAD_DIM if e == 0 else 0
                    b_row = cum_t[cf:cf + 1, :]
                    c_col = xg[:, ci:ci + 1] - cum[:, cf:cf + 1]
                    dm = jnp.where(valid[d], b_row + c_col, NEG)
                    b_tot = tot[:, cf:cf + 1]
                    a_row = b_tot - b_row + xg_t[ci:ci + 1, :]
                    items[d, j, hd] = dict(
                        qp=qp, km=jnp.where(in_head, kp, jnp.zeros_like(kp)),
                        vat=jnp.where(in_rows, vt, jnp.where(row_l == ones_row, 1.0, 0.0)),
                        ones_row=ones_row, b_row=b_row, b_tot=b_tot, dm=dm, a_row=a_row,
                        a_t=jnp.max(dm, axis=0, keepdims=True), a_max=jnp.max(a_row, axis=1, keepdims=True))

    for c in items.values():
        c["qk"] = (_dot_nt(c["km"], c["qp"]) * jnp.exp(c["dm"] - c["a_t"])).astype(BF16)

    for c in items.values():
        c["intra"] = _dot(c["vat"].astype(BF16), c["qk"])
        c["upd"] = _dot((c["vat"] * jnp.exp(c["a_row"] - c["a_max"])).astype(BF16), c["km"])

    state = {(d, hd): (c_ref[d, hd], m_ref[d * ML_HEADS + hd:d * ML_HEADS + hd + 1, 0:1])
             for d in range(2) for hd in range(ML_HEADS)}
    hs = {}
    for n in range(nchk):
        for d in range(2):
            j = order[d][n]
            for hd in range(ML_HEADS):
                c = items[d, j, hd]
                cst, m_prev = state[d, hd]
                g = c["b_row"] + m_prev
                mt = jnp.maximum(g, c["a_t"])
                haug = (jnp.exp(g - mt) * _dot_nt(cst.astype(BF16), c["qp"])
                        + jnp.exp(c["a_t"] - mt) * c["intra"])
                den = haug[c["ones_row"]:c["ones_row"] + 1, :]
                hs[d, j, hd] = haug / jnp.maximum(jnp.abs(den), jnp.exp(-mt))
                m_new = jnp.maximum(c["b_tot"] + m_prev, c["a_max"])
                cst = (jnp.exp(c["b_tot"] + m_prev - m_new) * cst
                       + jnp.exp(c["a_max"] - m_new) * c["upd"])
                state[d, hd] = (cst, m_new)
    for (d, hd), (cst, m_new) in state.items():
        c_ref[d, hd] = cst
        r = d * ML_HEADS + hd
        m_ref[r:r + 1, :] = jnp.broadcast_to(m_new, (1, LANES))

    for d, h_ref in enumerate((hf_ref, hb_ref)):
        for j in range(nchk):
            for p in range(ML_HEADS // 2):
                pair = jnp.where(row_l < HEAD_DIM, hs[d, j, 2 * p], hs[d, j, 2 * p + 1])
                h_ref[rows_of(j), p * LANES:(p + 1) * LANES] = pair.T


def _mlstm(l, mlqk, mlv, misc, prm, tabs, dims):
    T, W = mlqk.shape
    B, S, CTX = dims
    nchk = ML_STEP_CHUNKS
    blk = nchk * ML_CHUNK
    nb_ctx = CTX // blk
    nb_lat = S // blk
    ctx0 = B * S // blk

    def fwd(b, c):
        return jnp.where(c < nb_ctx, ctx0 + b * nb_ctx + c, b * nb_lat + (c - nb_ctx))

    def bwd(b, c):
        return jnp.where(c < nb_ctx, ctx0 + b * nb_ctx + (nb_ctx - 1 - c),
                         b * nb_lat + (nb_lat - 1 - (c - nb_ctx)))

    sub = blk // 8
    last8 = T // 8 - 1

    def specs(fn):
        return [pl.BlockSpec((blk, W), lambda b, c: (fn(b, c), 0)),
                pl.BlockSpec((8, W), lambda b, c: (jnp.maximum(fn(b, c) * sub - 1, 0), 0)),
                pl.BlockSpec((8, W), lambda b, c: (jnp.minimum((fn(b, c) + 1) * sub, last8), 0)),
                pl.BlockSpec((blk, ML_W), lambda b, c: (fn(b, c), 0)),
                pl.BlockSpec((blk, LANES), lambda b, c: (fn(b, c), 0))]

    return pl.pallas_call(
        functools.partial(_mlstm_kernel, nb_ctx=nb_ctx, nb_lat=nb_lat, nchk=nchk),
        grid=(B, nb_ctx + nb_lat),
        in_specs=specs(fwd) + specs(bwd) + [
            pl.BlockSpec((None, 8, W), lambda b, c: (l, 0, 0)),
            pl.BlockSpec((None, 1, W), lambda b, c: (l, 0, 0)),
            pl.BlockSpec((None, 1, LANES), lambda b, c: (l, 0, 0)),
            pl.BlockSpec((ML_CHUNK, ML_CHUNK), lambda b, c: (0, 0)),
            pl.BlockSpec((ML_CHUNK, ML_CHUNK), lambda b, c: (0, 0)),
        ],
        out_specs=[pl.BlockSpec((blk, ML_W), lambda b, c: (fwd(b, c), 0)),
                   pl.BlockSpec((blk, ML_W), lambda b, c: (bwd(b, c), 0))],
        out_shape=[jax.ShapeDtypeStruct((T, ML_W), F32)] * 2,
        scratch_shapes=[pltpu.VMEM((2, ML_HEADS, LANES, LANES), F32), pltpu.VMEM((8, LANES), F32)],
        compiler_params=_cparams(("arbitrary", "arbitrary")),
        name="mlstm",
    )(mlqk, mlqk, mlqk, mlv, misc, mlqk, mlqk, mlqk, mlv, misc,
      prm["conv_w"], prm["conv_b"], prm["b_gate"], tabs["tril"], tabs["triu"])


def _ffn_chunks(ff):
    step = 2 * MXU_DIM
    starts = list(range(0, ff - ff % step, step)) or [0]
    return [(c, (ff - c) if c == starts[-1] else step) for c in starts]


def _out_kernel(x_ref, mod_ref, ona_ref, omla_ref, hf_ref, hb_ref, mlo_ref, hg_ref, bd64_ref,
                w_out_ref, g_ref, wab_ref, wo_ref, o_ref):
    x = x_ref[...]
    gate1 = mod_ref[2:3, :]
    shift2 = mod_ref[3:4, :]
    scale2 = mod_ref[4:5, :]
    gate2 = mod_ref[5:6, :]

    h = hf_ref[...] + hb_ref[...]
    hn = h * lax.rsqrt(_group_sumsq(h, bd64_ref[...]) * (1.0 / HEAD_DIM) + EPS) * hg_ref[...]
    ob = (hn * _sigmoid(mlo_ref[...])).astype(BF16)
    mix = _dot(jnp.concatenate([ona_ref[...], ob, omla_ref[...]], axis=-1), w_out_ref[...])
    x1 = x + gate1 * mix

    h2 = ((_rms_rows(x1) * g_ref[...]) * (1.0 + scale2) + shift2).astype(BF16)
    acc = jnp.zeros_like(x1)
    ff = wo_ref.shape[0]
    for c0, w in _ffn_chunks(ff):
        a = _dot(h2, wab_ref[:, c0:c0 + w])
        b = _dot(h2, wab_ref[:, ff + c0:ff + c0 + w])
        act = (a * _sigmoid(a) * b).astype(BF16)
        acc = acc + _dot(act, wo_ref[c0:c0 + w, :])
    o_ref[...] = x1 + gate2 * acc


def _out(l, xs, mods, ona, omla, hf, hb, mlo, prm, tabs, dims, with_ctx):
    D = xs.shape[1]
    B, S, CTX = dims
    T = xs.shape[0] if with_ctx else B * S
    nt = T // ROW_TILE
    lat_tiles = S // ROW_TILE
    ff = prm["w_fo"].shape[1]

    def grp(i):
        return jnp.minimum(i // lat_tiles, B)

    row = lambda w: pl.BlockSpec((ROW_TILE, w), lambda i: (i, 0))
    lay = lambda shape: pl.BlockSpec((None,) + shape, lambda i: (l,) + (0,) * len(shape))
    big = lambda shape: pl.BlockSpec((None,) + shape, lambda i: (l,) + (0,) * len(shape),
                                     pipeline_mode=pl.Buffered(1))
    return pl.pallas_call(
        _out_kernel,
        grid=(nt,),
        in_specs=[
            row(D),
            pl.BlockSpec((None, None, 6, D), lambda i: (l, grp(i), 0, 0)),
            row(NA_W), row(MLA_HEADS * MLA_V), row(ML_W), row(ML_W), row(ML_W),
            lay((1, ML_W)),
            pl.BlockSpec((MXU_DIM, MXU_DIM), lambda i: (0, 0)),
            big((NA_W + ML_W + MLA_HEADS * MLA_V, D)),
            lay((1, D)),
            big((D, 2 * ff)), big((ff, D)),
        ],
        out_specs=row(D),
        out_shape=jax.ShapeDtypeStruct((T, D), F32),
        compiler_params=_cparams(("arbitrary",)),
        name="out_ffn",
    )(xs, mods, ona, omla, hf, hb, mlo, prm["head_gain"], tabs["bd64"],
      prm["w_out"], prm["g_ffn"], prm["w_ab"], prm["w_fo"])


def _block_diag_ones(group):
    idx = np.arange(MXU_DIM) // group
    return jnp.asarray((idx[:, None] == idx[None, :]).astype(np.float32), dtype=BF16)


def _rope_tables(S, n_ctx_rows):
    n_freq = MLA_ROPE // 4
    inv = ROPE_BASE ** (-jnp.arange(n_freq, dtype=F32) / n_freq)
    t = jnp.arange(S, dtype=jnp.int32)
    row = (t // GRID_W).astype(F32)
    col = (t % GRID_W).astype(F32)
    ang = jnp.concatenate([row[:, None] * inv, col[:, None] * inv], axis=-1)
    ang = jnp.concatenate([ang, jnp.zeros((n_ctx_rows, MLA_ROPE // 2), F32)], axis=0)
    cos, sin = jnp.cos(ang), jnp.sin(ang)
    n = S + n_ctx_rows
    half = MLA_ROPE // 2
    ones = jnp.ones((n, MLA_NOPE), F32)
    z = lambda w: jnp.zeros((n, w), F32)
    tail = LANES - MLA_NOPE - MLA_ROPE
    del half
    cos_t = jnp.concatenate([ones, cos, cos, jnp.ones((n, tail), F32)], axis=-1)
    sin_t = jnp.concatenate([z(MLA_NOPE), -sin, sin, z(tail)], axis=-1)
    return cos_t, sin_t


def _na_bias_kernel(rpb_ref, o_ref, *, plan):
    li = pl.program_id(0)
    h = pl.program_id(1)
    n_dc = 2 * NA_WIN_C - 1
    ck = lax.broadcasted_iota(jnp.int32, (GRID_W, LANES), 0)
    lane = lax.broadcasted_iota(jnp.int32, (GRID_W, LANES), 1)
    cq = lane & (GRID_W - 1)
    dc = ck - cq + (NA_WIN_C - 1)
    cs = jnp.clip(cq - NA_WIN_C // 2, 0, GRID_W - NA_WIN_C)
    col_ok = (ck >= cs) & (ck < cs + NA_WIN_C)
    hits = [dc == j for j in range(n_dc)]
    neg = jnp.full((GRID_W, LANES), NEG, F32)
    tiles = []
    for dr in range(2 * NA_WIN_R - 1):
        t = neg
        for j in range(n_dc):
            t = jnp.where(hits[j], rpb_ref[li, h, dr * n_dc + j], t)
        tiles.append(jnp.where(col_ok, t, neg))
    low = lane < GRID_W
    for v, rows in enumerate(plan):
        for rk, drs in enumerate(rows):
            pick = [neg if d is None else tiles[d] for d in drs]
            groups = [jnp.where(low, pick[2 * g], pick[2 * g + 1]) for g in range(len(drs) // 2)]
            o_ref[v, rk * GRID_W:(rk + 1) * GRID_W, :] = jnp.concatenate(groups, axis=-1).astype(o_ref.dtype)
    o_ref[len(plan)] = jnp.full(o_ref.shape[1:], NEG, o_ref.dtype)


def _na_bias(rpb, rows):
    depth, H = rpb.shape[:2]
    qr, kr_n = ATT_BLOCK // GRID_W, NA_KEY_ROWS // GRID_W
    nb = rows // qr
    plan = []
    for j in (0, 1, nb - 1):
        ks = int(np.clip(j - 1, 0, nb - 3)) * qr
        per_key_row = []
        for kr in range(ks, ks + kr_n):
            drs = []
            for r in range(j * qr, (j + 1) * qr):
                rs = int(np.clip(r - NA_WIN_R // 2, 0, rows - NA_WIN_R))
                drs.append(kr - r + NA_WIN_R - 1 if rs <= kr < rs + NA_WIN_R else None)
            per_key_row.append(tuple(drs))
        plan.append(tuple(per_key_row))
    n_rel = (2 * NA_WIN_R - 1) * (2 * NA_WIN_C - 1)
    return pl.pallas_call(
        functools.partial(_na_bias_kernel, plan=tuple(plan)),
        grid=(depth, H),
        in_specs=[pl.BlockSpec(memory_space=pltpu.SMEM)],
        out_specs=pl.BlockSpec((None, len(plan) + 1, None, NA_KEY_ROWS, ATT_BLOCK),
                               lambda li, h: (li, 0, h, 0, 0)),
        out_shape=jax.ShapeDtypeStruct((depth, len(plan) + 1, H, NA_KEY_ROWS, ATT_BLOCK), BF16),
        compiler_params=_cparams(("arbitrary", "arbitrary")),
        name="na_bias",
    )(rpb.reshape(depth, H, n_rel))


def _prepare(w_in, w_uq, w_ukv, w_out, w_ffn_in, w_ffn_out, g_mix, g_ffn, b_gate, na_qk_gain,
             ml_conv_w, ml_conv_b, ml_head_gain, mla_gq, mla_gkv, mla_qk_gain):
    depth, D, _ = w_in.shape
    o_gate = C_MLO + ML_W
    o_ql = o_gate + 4 * ML_HEADS
    o_kr = o_ql + Q_LORA + KV_LORA
    half = MLA_ROPE // 2

    def layout_kernel(w_ref, o_ref):
        w = w_ref[...]
        zc = lambda n: jnp.zeros((w.shape[0], n), w.dtype)
        o_ref[...] = jnp.concatenate([
            w[:, :o_gate], w[:, o_ql:o_kr],
            w[:, o_gate:o_ql], zc(MISC_ROPE_LANE - 4 * ML_HEADS),
            w[:, o_kr:], zc(LANES - MISC_ROPE_LANE - MLA_ROPE),
            zc(MISC_ROPE_LANE), w[:, o_kr + half:], w[:, o_kr:o_kr + half],
            zc(LANES - MISC_ROPE_LANE - MLA_ROPE)], axis=-1).astype(BF16)

    rows = MXU_DIM
    w_in_p = pl.pallas_call(
        layout_kernel,
        grid=(depth, D // rows),
        in_specs=[pl.BlockSpec((None, rows, w_in.shape[2]), lambda li, i: (li, i, 0))],
        out_specs=pl.BlockSpec((None, rows, D_IN_PAD), lambda li, i: (li, i, 0)),
        out_shape=jax.ShapeDtypeStruct((depth, D, D_IN_PAD), BF16),
        compiler_params=_cparams(("arbitrary", "arbitrary")),
        name="w_in_layout",
    )(w_in)
    pad_h = lambda a, w: jnp.pad(a, [(0, 0)] * (a.ndim - 1) + [(0, LANES - w)])

    def partner(a):
        lo, hi = a[..., MLA_NOPE:MLA_NOPE + half], a[..., MLA_NOPE + half:MLA_QK]
        return jnp.concatenate([jnp.zeros_like(a[..., :MLA_NOPE]), hi, lo,
                                jnp.zeros_like(a[..., MLA_QK:])], axis=-1)

    w_uq_h = pad_h(w_uq.reshape(depth, Q_LORA, MLA_HEADS, MLA_QK), MLA_QK)
    w_uq_p = w_uq_h.reshape(depth, Q_LORA, MLA_HEADS * LANES).astype(BF16)
    w_uq_s = partner(w_uq_h).reshape(depth, Q_LORA, MLA_HEADS * LANES).astype(BF16)
    ukv = w_ukv.reshape(depth, KV_LORA, MLA_HEADS, MLA_NOPE + MLA_V)
    w_uk_p = pad_h(ukv[..., :MLA_NOPE], MLA_NOPE).reshape(depth, KV_LORA, MLA_HEADS * LANES).astype(BF16)
    w_uv = ukv[..., MLA_NOPE:].reshape(depth, KV_LORA, MLA_HEADS * MLA_V).astype(BF16)
    mla_g = pad_h(mla_qk_gain, MLA_QK)
    return {
        "w_in": w_in_p, "w_uq": w_uq_p, "w_uqs": w_uq_s, "w_uk": w_uk_p, "w_uv": w_uv,
        "w_out": w_out.astype(BF16),
        "w_ab": w_ffn_in.astype(BF16),
        "w_fo": w_ffn_out.astype(BF16),
        "g_mix": g_mix[:, None, :], "g_ffn": g_ffn[:, None, :],
        "gq": mla_gq[:, None, :], "gkv": mla_gkv[:, None, :],
        "na_gq": jnp.tile(na_qk_gain[:, 0:1, :], (1, 1, NA_HEADS)) * NA_SCALE,
        "na_gk": jnp.tile(na_qk_gain[:, 1:2, :], (1, 1, NA_HEADS)),
        "mla_gq": jnp.tile(mla_g[:, 0:1, :], (1, 1, MLA_HEADS)) * (MLA_SCALE * LOG2E),
        "mla_gk": jnp.tile(mla_g[:, 1:2, :], (1, 1, MLA_HEADS)),
        "mla_gqs": jnp.tile(partner(mla_g[:, 0:1, :]), (1, 1, MLA_HEADS)) * (MLA_SCALE * LOG2E),
        "mla_gks": jnp.tile(partner(mla_g[:, 1:2, :]), (1, 1, MLA_HEADS)),
        "head_gain": ml_head_gain.reshape(depth, 1, ML_W),
        "b_gate": pad_h(b_gate, 4 * ML_HEADS)[:, None, :],
        "conv_w": jnp.pad(ml_conv_w, ((0, 0), (0, 8 - ml_conv_w.shape[1]), (0, 0))),
        "conv_b": ml_conv_b[:, None, :],
    }


def kernel(x, c, ctx, c_ctx, w_mod, b_mod, g_mix, g_ffn, w_in, b_gate, na_qk_gain, na_rpb,
           ml_conv_w, ml_conv_b, ml_head_gain, mla_gq, mla_gkv, w_uq, w_ukv, mla_qk_gain,
           w_out, w_ffn_in, w_ffn_out):
    B, S, D = x.shape
    CTX = ctx.shape[1]
    depth = w_in.shape[0]
    assert CTX == ATT_BLOCK and S % ROW_TILE == 0 and (B * CTX) % ROW_TILE == 0
    assert CTX % (ML_STEP_CHUNKS * ML_CHUNK) == 0 and S % (ML_STEP_CHUNKS * ML_CHUNK) == 0
    assert S % (2 * FLASH_TK) == 0 and S % FLASH_TQ == 0 and S // ATT_BLOCK >= 3 and B + 1 <= 8
    dims = (B, S, CTX)

    prm = _prepare(w_in, w_uq, w_ukv, w_out, w_ffn_in, w_ffn_out, g_mix, g_ffn, b_gate, na_qk_gain,
                   ml_conv_w, ml_conv_b, ml_head_gain, mla_gq, mla_gkv, mla_qk_gain)
    cos_t, sin_t = _rope_tables(S, B * CTX)
    tri = np.tril(np.ones((ML_CHUNK, ML_CHUNK), np.float32))
    tabs = {
        "bd64": _block_diag_ones(HEAD_DIM), "bd128": _block_diag_ones(LANES),
        "cos": cos_t, "sin": sin_t,
        "tril": jnp.asarray(tri, dtype=BF16), "triu": jnp.asarray(tri.T, dtype=BF16),
    }
    bias = _na_bias(na_rpb, S // GRID_W)

    cond8 = jnp.concatenate([c, c_ctx[None, :], jnp.zeros((8 - B - 1, D), F32)], axis=0)
    mods = _modulation(cond8, w_mod, b_mod)
    mods = mods.reshape(depth, 8, 6, D)

    xs = jnp.concatenate([x.reshape(B * S, D), ctx.reshape(B * CTX, D)], axis=0)
    for l in range(depth):
        with_ctx = l < depth - 1
        naq, nak, mlqk, mlv, mlo, misc, mq, mk, navt, mvt = _proj(l, xs, mods, prm, tabs, dims)
        ona = _na(l, naq, nak, navt, bias, dims, with_ctx)
        omla = _flash(mq, mk, mvt, dims, with_ctx)
        hf, hb = _mlstm(l, mlqk, mlv, misc, prm, tabs, dims)
        xs = _out(l, xs, mods, ona, omla, hf, hb, mlo, prm, tabs, dims, with_ctx)
    return xs.reshape(B, S, D)
```

```python
import functools

import numpy as np
import jax
import jax.numpy as jnp
from jax import lax
from jax.experimental import pallas as pl
from jax.experimental.pallas import tpu as pltpu

GRID_W = 64
HEAD_DIM = 64
NA_HEADS = 4
NA_WIN_R = 8
NA_WIN_C = 16
ML_HEADS = 4
ML_CHUNK = 128
ML_STEP_CHUNKS = 2
MLA_HEADS = 8
MLA_NOPE = 64
MLA_ROPE = 32
MLA_V = 64
Q_LORA = 384
KV_LORA = 256
ROPE_BASE = 10000.0
EPS = 1e-6
NA_W = NA_HEADS * HEAD_DIM
ML_W = ML_HEADS * HEAD_DIM
MLA_QK = MLA_NOPE + MLA_ROPE
NA_SCALE = HEAD_DIM ** -0.5
MLA_SCALE = MLA_QK ** -0.5
LOG2E = 1.4426950408889634

LANES = 128
MXU_DIM = 256
VMEM_LIMIT = 56 * 1024 * 1024

ROW_TILE = 512
ATT_BLOCK = 256
NA_KEY_ROWS = 768
FLASH_TK = 512
FLASH_TQ = 1024
FLASH_TILES = 2
FLASH_SLOTS = 2
FLASH_UNROLL = 4
NEG = -1e30
DEN_ROWS = 16

C_NAQ, C_NAK, C_NAV = 0, 256, 512
C_MLQK, C_MLV, C_MLO = 768, 1280, 1536
C_QL = 1792
C_KVL = C_QL + Q_LORA
C_MISC = C_KVL + KV_LORA
C_MISC2 = C_MISC + LANES
D_IN_PAD = C_MISC2 + LANES
MISC_ROPE_LANE = 64

BF16 = jnp.bfloat16
F32 = jnp.float32


def _cparams(sem):
    return pltpu.CompilerParams(dimension_semantics=sem, vmem_limit_bytes=VMEM_LIMIT)


def _dot(a, b):
    return jnp.dot(a, b, preferred_element_type=F32)


def _dot_nt(a, b):
    return lax.dot_general(a, b, (((1,), (1,)), ((), ())), preferred_element_type=F32)


def _split2(x):
    hi = x.astype(BF16)
    lo = (x - hi.astype(F32)).astype(BF16)
    return hi, lo


def _group_sumsq(x, bd):
    x2 = x * x
    hi, lo = _split2(x2)
    outs = []
    for c in range(x.shape[1] // MXU_DIM):
        sl = slice(c * MXU_DIM, (c + 1) * MXU_DIM)
        outs.append(_dot(hi[:, sl], bd) + _dot(lo[:, sl], bd))
    return outs[0] if len(outs) == 1 else jnp.concatenate(outs, axis=-1)


def _rms_rows(x):
    return x * lax.rsqrt(jnp.mean(x * x, axis=-1, keepdims=True) + EPS)


def _sigmoid(x):
    return 1.0 / (1.0 + jnp.exp(-x))


def _mod_kernel(c_ref, w_ref, b_ref, o_ref):
    c = c_ref[...]
    a = c * _sigmoid(c)
    o_ref[...] = jnp.dot(a, w_ref[...], preferred_element_type=F32,
                         precision=lax.Precision.HIGHEST) + b_ref[...]


def _modulation(cond8, w_mod, b_mod):
    depth, d, d6 = w_mod.shape
    bw = 2 * d
    return pl.pallas_call(
        _mod_kernel,
        grid=(depth, d6 // bw),
        in_specs=[
            pl.BlockSpec((8, d), lambda l, j: (0, 0)),
            pl.BlockSpec((None, d, bw), lambda l, j: (l, 0, j)),
            pl.BlockSpec((None, 1, bw), lambda l, j: (l, 0, j)),
        ],
        out_specs=pl.BlockSpec((None, 8, bw), lambda l, j: (l, 0, j)),
        out_shape=jax.ShapeDtypeStruct((depth, 8, d6), F32),
        compiler_params=_cparams(("arbitrary", "arbitrary")),
        name="modulation",
    )(cond8, w_mod, b_mod.reshape(depth, 1, d6))


def _proj_kernel(x_ref, mod_ref, g_ref, w_in_ref, w_uq_ref, w_uqs_ref, w_uk_ref, w_uv_ref,
                 gq_ref, gkv_ref, na_gq_ref, na_gk_ref, mla_gq_ref, mla_gk_ref, mla_gqs_ref, mla_gks_ref,
                 bd64_ref, bd128_ref, cos_ref, sin_ref,
                 naq_ref, nak_ref, mlqk_ref, mlv_ref, mlo_ref, misc_ref,
                 mq_ref, mk_ref, nav_ref, mv_ref):
    x = x_ref[...]
    shift = mod_ref[0:1, :]
    scale = mod_ref[1:2, :]
    h = (_rms_rows(x) * g_ref[...]) * (1.0 + scale) + shift
    hb = h.astype(BF16)

    def proj(c0, width):
        return _dot(hb, w_in_ref[:, c0:c0 + width])

    bd64 = bd64_ref[...]
    bd128 = bd128_ref[...]

    lat = proj(C_QL, D_IN_PAD - C_QL)
    ql = lat[:, :Q_LORA]
    kvl = lat[:, C_KVL - C_QL:C_MISC - C_QL]
    misc = lat[:, C_MISC - C_QL:C_MISC2 - C_QL]
    kx = lat[:, C_MISC2 - C_QL:]
    misc_ref[...] = misc
    qn = (_rms_rows(ql) * gq_ref[...]).astype(BF16)
    kvn = (_rms_rows(kvl) * gkv_ref[...]).astype(BF16)

    na = proj(C_NAQ, 3 * NA_W)
    pq = na[:, :NA_W]
    pk = na[:, NA_W:2 * NA_W]
    nav_ref[...] = na[:, 2 * NA_W:].T.astype(BF16)

    qr = _dot(qn, w_uq_ref[...])
    qx = _dot(qn, w_uqs_ref[...])
    lane = lax.broadcasted_iota(jnp.int32, misc.shape, 1)
    krope = jnp.where((lane >= MISC_ROPE_LANE) & (lane < MISC_ROPE_LANE + MLA_ROPE), misc, 0.0)
    kr = _dot(kvn, w_uk_ref[...]) + jnp.concatenate([krope] * MLA_HEADS, axis=-1)
    mv_ref[...] = _dot(kvn, w_uv_ref[...]).T.astype(BF16)

    ml = proj(C_MLQK, 4 * ML_W)
    mlqk_ref[...] = ml[:, :2 * ML_W]
    mlv_ref[...] = ml[:, 2 * ML_W:3 * ML_W].astype(BF16)
    mlo_ref[...] = ml[:, 3 * ML_W:]

    naq_ref[...] = (pq * lax.rsqrt(_group_sumsq(pq, bd64) * (1.0 / HEAD_DIM) + EPS)
                    * na_gq_ref[...]).astype(BF16)
    nak_ref[...] = (pk * lax.rsqrt(_group_sumsq(pk, bd64) * (1.0 / HEAD_DIM) + EPS)
                    * na_gk_ref[...]).astype(BF16)

    cos = cos_ref[...]
    sin = sin_ref[...]
    rq = lax.rsqrt(_group_sumsq(qr, bd128) * (1.0 / MLA_QK) + EPS)
    rk = lax.rsqrt(_group_sumsq(kr, bd128) * (1.0 / MLA_QK) + EPS)
    kxs = kx * sin
    for g in range(MLA_HEADS):
        sl = slice(g * LANES, (g + 1) * LANES)
        mq_ref[:, sl] = (rq[:, sl] * (qr[:, sl] * mla_gq_ref[:, sl] * cos
                                      + qx[:, sl] * mla_gqs_ref[:, sl] * sin)).astype(BF16)
        mk_ref[:, sl] = (rk[:, sl] * (kr[:, sl] * mla_gk_ref[:, sl] * cos
                                      + kxs * mla_gks_ref[:, sl])).astype(BF16)


def _proj(l, xs, mods, prm, tabs, dims):
    T, D = xs.shape
    B, S, CTX = dims
    nt = T // ROW_TILE
    lat_tiles = S // ROW_TILE

    def grp(i):
        return jnp.minimum(i // lat_tiles, B)

    def tab_row(i):
        return jnp.where(i < B * lat_tiles, i % lat_tiles, lat_tiles + (i - B * lat_tiles))

    row = lambda w: pl.BlockSpec((ROW_TILE, w), lambda i: (i, 0))
    lay = lambda shape: pl.BlockSpec((None,) + shape, lambda i: (l,) + (0,) * len(shape))
    const = lambda shape: pl.BlockSpec(shape, lambda i: (0,) * len(shape))
    tab = pl.BlockSpec((ROW_TILE, LANES), lambda i: (tab_row(i), 0))
    out_w = [(NA_W, BF16), (NA_W, BF16), (2 * ML_W, F32), (ML_W, BF16), (ML_W, F32),
             (LANES, F32), (MLA_HEADS * LANES, BF16), (MLA_HEADS * LANES, BF16)]
    out_t = [NA_W, MLA_HEADS * MLA_V]
    return pl.pallas_call(
        _proj_kernel,
        grid=(nt,),
        in_specs=[
            row(D),
            pl.BlockSpec((None, None, 6, D), lambda i: (l, grp(i), 0, 0)),
            lay((1, D)),
            lay((D, D_IN_PAD)), lay((Q_LORA, MLA_HEADS * LANES)), lay((Q_LORA, MLA_HEADS * LANES)),
            lay((KV_LORA, MLA_HEADS * LANES)), lay((KV_LORA, MLA_HEADS * MLA_V)),
            lay((1, Q_LORA)), lay((1, KV_LORA)), lay((1, NA_W)), lay((1, NA_W)),
            lay((1, MLA_HEADS * LANES)), lay((1, MLA_HEADS * LANES)),
            lay((1, MLA_HEADS * LANES)), lay((1, MLA_HEADS * LANES)),
            const((MXU_DIM, MXU_DIM)), const((MXU_DIM, MXU_DIM)),
            tab, tab,
        ],
        out_specs=[row(w) for w, _ in out_w]
        + [pl.BlockSpec((w, ROW_TILE), lambda i: (0, i)) for w in out_t],
        out_shape=[jax.ShapeDtypeStruct((T, w), dt) for w, dt in out_w]
        + [jax.ShapeDtypeStruct((w, T), BF16) for w in out_t],
        compiler_params=_cparams(("arbitrary",)),
        name="proj",
    )(xs, mods, prm["g_mix"], prm["w_in"], prm["w_uq"], prm["w_uqs"], prm["w_uk"], prm["w_uv"],
      prm["gq"], prm["gkv"], prm["na_gq"], prm["na_gk"],
      prm["mla_gq"], prm["mla_gk"], prm["mla_gqs"], prm["mla_gks"],
      tabs["bd64"], tabs["bd128"], tabs["cos"], tabs["sin"])


def _short_conv(x, prev_blk, next_blk, first, last, w_ref, b_ref):
    n = x.shape[0]
    prev_row = jnp.where(first, 0.0, prev_blk[7:8, :])
    next_row = jnp.where(last, 0.0, next_blk[0:1, :])
    ridx = lax.broadcasted_iota(jnp.int32, x.shape, 0)
    xm1 = jnp.where(ridx == 0, prev_row, pltpu.roll(x, 1, 0))
    xp1 = jnp.where(ridx == n - 1, next_row, pltpu.roll(x, n - 1, 0))
    y = b_ref[...] + xm1 * w_ref[0:1, :] + x * w_ref[1:2, :] + xp1 * w_ref[2:3, :]
    y = y * _sigmoid(y)
    return y[:, :ML_W].astype(BF16), (y[:, ML_W:] * (HEAD_DIM ** -0.5)).astype(BF16)


def _na_kernel(q_ref, k_ref, vt_ref, kc_ref, vct_ref, bias_ref, o_ref, *, nb):
    j = pl.program_id(1)
    start = pl.multiple_of(jnp.clip(j - 1, 0, nb - 3) * ATT_BLOCK, ATT_BLOCK)
    q = q_ref[...]
    lane = lax.broadcasted_iota(jnp.int32, (ATT_BLOCK, LANES), 1)
    scores = []
    for h in range(NA_HEADS):
        p, e = divmod(h, 2)
        cols = slice(p * LANES, (p + 1) * LANES)
        qp = q[:, cols]
        keep = (lane < HEAD_DIM) if e == 0 else (lane >= HEAD_DIM)
        qt = jnp.where(keep, qp, jnp.zeros_like(qp)).astype(F32).T.astype(BF16)
        s_loc = _dot(k_ref[pl.ds(start, NA_KEY_ROWS), cols], qt) + bias_ref[h].astype(F32)
        s_ctx = _dot(kc_ref[:, cols], qt)
        scores.append((s_loc, s_ctx))
    probs = []
    for s_loc, s_ctx in scores:
        m = jnp.maximum(jnp.max(s_loc, axis=0, keepdims=True), jnp.max(s_ctx, axis=0, keepdims=True))
        probs.append((jnp.exp2(s_loc - m).astype(BF16), jnp.exp2(s_ctx - m).astype(BF16)))
    outs = []
    for h, (p_loc, p_ctx) in enumerate(probs):
        rows = slice(h * HEAD_DIM, (h + 1) * HEAD_DIM)
        ot = (_dot(_flash_values(vt_ref[rows, pl.ds(start, NA_KEY_ROWS)]), p_loc)
              + _dot(_flash_values(vct_ref[rows, :]), p_ctx))
        outs.append(ot[:HEAD_DIM, :] / ot[HEAD_DIM:HEAD_DIM + 1, :])
    o_ref[...] = jnp.concatenate(outs, axis=0).T.astype(o_ref.dtype)


def _na(l, naq, nak, navt, bias, dims, with_ctx):
    B, S, CTX = dims
    nb = S // ATT_BLOCK
    ctx_blk0 = B * S // CTX
    n_steps = nb + 1 if with_ctx else nb
    rows_out = naq.shape[0] if with_ctx else B * S

    def qrow(b, j):
        return jnp.where(j < nb, b * nb + j, ctx_blk0 + b)

    def variant(j):
        return jnp.where(j == 0, 0, jnp.where(j == nb - 1, 2, jnp.where(j == nb, 3, 1)))

    kern = functools.partial(_na_kernel, nb=nb)
    return pl.pallas_call(
        kern,
        grid=(B, n_steps),
        in_specs=[
            pl.BlockSpec((ATT_BLOCK, NA_W), lambda b, j: (qrow(b, j), 0)),
            pl.BlockSpec((S, NA_W), lambda b, j: (b, 0)),
            pl.BlockSpec((NA_W, S), lambda b, j: (0, b)),
            pl.BlockSpec((CTX, NA_W), lambda b, j: (ctx_blk0 + b, 0)),
            pl.BlockSpec((NA_W, CTX), lambda b, j: (0, ctx_blk0 + b)),
            pl.BlockSpec((None, None, NA_HEADS, NA_KEY_ROWS, ATT_BLOCK), lambda b, j: (l, variant(j), 0, 0, 0)),
        ],
        out_specs=pl.BlockSpec((ATT_BLOCK, NA_W), lambda b, j: (qrow(b, j), 0)),
        out_shape=jax.ShapeDtypeStruct((rows_out, NA_W), BF16),
        compiler_params=_cparams(("arbitrary", "arbitrary")),
        name="na",
    )(naq, nak, navt, nak, navt, bias)


def _flash_values(v):
    return jnp.concatenate([v, jnp.ones((DEN_ROWS, v.shape[1]), BF16)], axis=0)


def _flash_queries(q):
    return [q[:, e * LANES:(e + 1) * LANES].astype(F32).T.astype(BF16) for e in range(2)]


def _flash_ctx_scores(qs, kc_ref):
    return [_dot(kc_ref[:, e * LANES:(e + 1) * LANES], qs[e]) for e in range(2)]


def _flash_ctx_init(sts, vc_ref, acc_ref, st_ref):
    ps = []
    for e, st in enumerate(sts):
        m = jnp.max(st, axis=0, keepdims=True)
        st_ref[e] = m
        ps.append(jnp.exp2(st - m).astype(BF16))
    for e, p in enumerate(ps):
        acc_ref[e] = _dot(_flash_values(vc_ref[e * HEAD_DIM:(e + 1) * HEAD_DIM, :]), p)


def _flash_finish(acc_ref, o_ref, rows=slice(None)):
    ot = jnp.concatenate([acc_ref[e, :HEAD_DIM, :] / acc_ref[e, HEAD_DIM:HEAD_DIM + 1, :]
                          for e in range(2)], axis=0)
    o_ref[rows, :] = ot.T.astype(o_ref.dtype)


def _flash_kernel(q_ref, km_ref, vm_ref, kc_ref, vc_ref, o_ref, *scratch, n_main):
    ns = FLASH_SLOTS
    s_refs = [scratch[2 * i:2 * i + 2] for i in range(ns)]
    mx_refs = [scratch[2 * ns + 2 * i:2 * ns + 2 * i + 2] for i in range(ns)]
    tk = FLASH_TK
    tq = FLASH_TQ
    for t in range(FLASH_TILES):
        acc_ref, st_ref = scratch[4 * ns + 2 * t:4 * ns + 2 * t + 2]
        rows = slice(t * tq, (t + 1) * tq)
        _flash_tile(_flash_queries(q_ref[rows, :]), km_ref, vm_ref, kc_ref, vc_ref, o_ref, rows,
                    s_refs, mx_refs, acc_ref, st_ref, n_main)


def _flash_tile(qs, km_ref, vm_ref, kc_ref, vc_ref, o_ref, rows, s_refs, mx_refs, acc_ref, st_ref, n_main):
    ns = FLASH_SLOTS
    tk = FLASH_TK

    def scores1(slot, r0, e):
        st = _dot(km_ref[pl.ds(r0, tk), e * LANES:(e + 1) * LANES], qs[e])
        s_refs[slot][e][...] = st
        mx_refs[slot][e][...] = jnp.max(st, axis=0, keepdims=True)

    def absorb1(slot, r0, e):
        m = st_ref[e]
        m_new = jnp.maximum(m, mx_refs[slot][e][...])
        st_ref[e] = m_new
        alpha = jnp.exp2(m - m_new)
        p = jnp.exp2(s_refs[slot][e][...] - m_new).astype(BF16)
        v = vm_ref[e * HEAD_DIM:(e + 1) * HEAD_DIM, pl.ds(r0, tk)]
        acc_ref[e] = alpha * acc_ref[e] + _dot(_flash_values(v), p)

    def scores(slot, r0):
        for e in range(2):
            scores1(slot, r0, e)

    def absorb(slot, r0):
        for e in range(2):
            absorb1(slot, r0, e)

    ctx_scores = _flash_ctx_scores(qs, kc_ref)
    scores(0, 0)
    _flash_ctx_init(ctx_scores, vc_ref, acc_ref, st_ref)

    def advance(c, r0):
        scores1((c + 1) % ns, r0 + tk, 0)
        absorb1(c % ns, r0, 1)
        scores1((c + 1) % ns, r0 + tk, 1)
        absorb1(c % ns, r0, 0)

    nu = FLASH_UNROLL

    def body(i, carry):
        r0 = pl.multiple_of(i * (nu * tk), nu * tk)
        for c in range(nu):
            advance(c, r0 + c * tk)
        return carry

    n_loop = (n_main - 1) // nu
    lax.fori_loop(0, n_loop, body, 0)
    for c in range(n_loop * nu, n_main - 1):
        advance(c, c * tk)
    absorb((n_main - 1) % ns, (n_main - 1) * tk)
    _flash_finish(acc_ref, o_ref, rows)


def _flash_ctx_kernel(o_in_ref, q_ref, kc_ref, vc_ref, o_ref, acc_ref, st_ref):
    del o_in_ref
    _flash_ctx_init(_flash_ctx_scores(_flash_queries(q_ref[...]), kc_ref), vc_ref, acc_ref, st_ref)
    _flash_finish(acc_ref, o_ref)


def _flash(mq, mk, mvt, dims, with_ctx):
    B, S, CTX = dims
    T = mq.shape[0] if with_ctx else B * S
    tq = FLASH_TQ
    blk = FLASH_TILES * tq
    nq = S // blk
    ctx_blk0 = B * S // CTX
    npair = MLA_HEADS // 2
    acc = lambda n: [pltpu.VMEM((2, HEAD_DIM + DEN_ROWS, n), F32), pltpu.VMEM((2, 1, n), F32)]
    out_shape = jax.ShapeDtypeStruct((T, MLA_HEADS * MLA_V), BF16)
    kc_spec = pl.BlockSpec((CTX, 2 * LANES), lambda b, p, *_: (ctx_blk0 + b, p))
    vc_spec = pl.BlockSpec((LANES, CTX), lambda b, p, *_: (p, ctx_blk0 + b))

    o = pl.pallas_call(
        functools.partial(_flash_kernel, n_main=S // FLASH_TK),
        grid=(B, npair, nq),
        in_specs=[
            pl.BlockSpec((blk, 2 * LANES), lambda b, p, j: (b * nq + j, p)),
            pl.BlockSpec((S, 2 * LANES), lambda b, p, j: (b, p)),
            pl.BlockSpec((LANES, S), lambda b, p, j: (p, b)),
            kc_spec, vc_spec,
        ],
        out_specs=pl.BlockSpec((blk, LANES), lambda b, p, j: (b * nq + j, p)),
        out_shape=out_shape,
        scratch_shapes=[pltpu.VMEM((FLASH_TK, tq), F32)] * (2 * FLASH_SLOTS)
        + [pltpu.VMEM((1, tq), F32)] * (2 * FLASH_SLOTS) + acc(tq) * FLASH_TILES,
        compiler_params=_cparams(("arbitrary", "arbitrary", "arbitrary")),
        name="flash",
    )(mq, mk, mvt, mk, mvt)
    if not with_ctx:
        return o
    return pl.pallas_call(
        _flash_ctx_kernel,
        grid=(B, npair),
        in_specs=[
            pl.BlockSpec(memory_space=pl.ANY),
            pl.BlockSpec((CTX, 2 * LANES), lambda b, p: (ctx_blk0 + b, p)),
            kc_spec, vc_spec,
        ],
        out_specs=pl.BlockSpec((CTX, LANES), lambda b, p: (ctx_blk0 + b, p)),
        out_shape=out_shape,
        scratch_shapes=acc(CTX),
        input_output_aliases={0: 0},
        compiler_params=_cparams(("arbitrary", "arbitrary")),
        name="flash_ctx",
    )(o, mq, mk, mvt)


def _mlstm_gates(d, gates, bgate, tri):
    gb = gates + bgate
    ls = jnp.minimum(gb, 0.0) - jnp.log1p(jnp.exp(-jnp.abs(gb)))
    lane = lax.broadcasted_iota(jnp.int32, gb.shape, 1)
    is_f = ((lane >= 4) & (lane < 8)) | ((lane >= 12) & (lane < 16))
    xg = jnp.where(lane < 16, jnp.where(is_f, ls, gb), 0.0)
    x1 = xg.astype(BF16)
    r1 = xg - x1.astype(F32)
    x2 = r1.astype(BF16)
    x3 = (r1 - x2.astype(F32)).astype(BF16)
    cum = _dot(tri, x1) + _dot(tri, x2) + _dot(tri, x3)
    return xg, cum, xg.T, cum.T, jnp.sum(xg, axis=0, keepdims=True)


def _mlstm_kernel(xf_ref, xpf_ref, xnf_ref, vf_ref, gf_ref, xb_ref, xpb_ref, xnb_ref, vb_ref, gb_ref,
                  cw_ref, cb_ref, bgate_ref, tril_ref, triu_ref, hf_ref, hb_ref, c_ref, m_ref,
                  *, nb_ctx, nb_lat, nchk):
    @pl.when(pl.program_id(1) == 0)
    def _():
        c_ref[...] = jnp.zeros_like(c_ref)
        m_ref[...] = jnp.zeros_like(m_ref)

    L = ML_CHUNK
    bgate = bgate_ref[...]
    step = pl.program_id(1)
    in_ctx = step < nb_ctx
    pos = jnp.where(in_ctx, step, step - nb_ctx)
    starts = pos == 0
    ends = pos == jnp.where(in_ctx, nb_ctx, nb_lat) - 1
    qk_f = _short_conv(xf_ref[...], xpf_ref[...], xnf_ref[...], starts, ends, cw_ref, cb_ref)
    qk_b = _short_conv(xb_ref[...], xpb_ref[...], xnb_ref[...], ends, starts, cw_ref, cb_ref)
    q = (qk_f[0], qk_b[0])
    k = (qk_f[1], qk_b[1])
    v = (vf_ref[...], vb_ref[...])
    g_refs = (gf_ref, gb_ref)
    tris = (tril_ref[...], triu_ref[...])
    order = (list(range(nchk)), list(range(nchk - 1, -1, -1)))
    si = lax.broadcasted_iota(jnp.int32, (L, L), 0)
    ti = lax.broadcasted_iota(jnp.int32, (L, L), 1)
    valid = (si <= ti, si >= ti)
    lane_l = lax.broadcasted_iota(jnp.int32, (L, LANES), 1)
    row_l = lax.broadcasted_iota(jnp.int32, (LANES, L), 0)
    rows_of = lambda j: slice(j * L, (j + 1) * L)

    items = {}
    for d in range(2):
        for j in range(nchk):
            xg, cum, xg_t, cum_t, tot = _mlstm_gates(d, g_refs[d][rows_of(j), :], bgate, tris[d])
            for p in range(ML_HEADS // 2):
                cols = slice(p * LANES, (p + 1) * LANES)
                qp = q[d][rows_of(j), cols]
                kp = k[d][rows_of(j), cols]
                vt = v[d][rows_of(j), cols].astype(F32).T
                for e in range(2):
                    hd = 2 * p + e
                    ci = 8 * d + hd
                    cf = 8 * d + 4 + hd
                    in_head = (lane_l < HEAD_DIM) if e == 0 else (lane_l >= HEAD_DIM)
                    in_rows = (row_l < HEAD_DIM) if e == 0 else (row_l >= HEAD_DIM)
                    ones_row = HEAD_DIM if e == 0 else 0
                    b_row = cum_t[cf:cf + 1, :]
                    c_col = xg[:, ci:ci + 1] - cum[:, cf:cf + 1]
                    dm = jnp.where(valid[d], b_row + c_col, NEG)
                    b_tot = tot[:, cf:cf + 1]
                    a_row = b_tot - b_row + xg_t[ci:ci + 1, :]
                    items[d, j, hd] = dict(
                        qp=qp, km=jnp.where(in_head, kp, jnp.zeros_like(kp)),
                        vat=jnp.where(in_rows, vt, jnp.where(row_l == ones_row, 1.0, 0.0)),
                        ones_row=ones_row, b_row=b_row, b_tot=b_tot, dm=dm, a_row=a_row,
                        a_t=jnp.max(dm, axis=0, keepdims=True), a_max=jnp.max(a_row, axis=1, keepdims=True))

    for c in items.values():
        c["qk"] = (_dot_nt(c["km"], c["qp"]) * jnp.exp(c["dm"] - c["a_t"])).astype(BF16)

    for c in items.values():
        c["intra"] = _dot(c["vat"].astype(BF16), c["qk"])
        c["upd"] = _dot((c["vat"] * jnp.exp(c["a_row"] - c["a_max"])).astype(BF16), c["km"])

    state = {(d, hd): (c_ref[d, hd], m_ref[d * ML_HEADS + hd:d * ML_HEADS + hd + 1, 0:1])
             for d in range(2) for hd in range(ML_HEADS)}
    hs = {}
    for n in range(nchk):
        for d in range(2):
            j = order[d][n]
            for hd in range(ML_HEADS):
                c = items[d, j, hd]
                cst, m_prev = state[d, hd]
                g = c["b_row"] + m_prev
                mt = jnp.maximum(g, c["a_t"])
                haug = (jnp.exp(g - mt) * _dot_nt(cst.astype(BF16), c["qp"])
                        + jnp.exp(c["a_t"] - mt) * c["intra"])
                den = haug[c["ones_row"]:c["ones_row"] + 1, :]
                hs[d, j, hd] = haug / jnp.maximum(jnp.abs(den), jnp.exp(-mt))
                m_new = jnp.maximum(c["b_tot"] + m_prev, c["a_max"])
                cst = (jnp.exp(c["b_tot"] + m_prev - m_new) * cst
                       + jnp.exp(c["a_max"] - m_new) * c["upd"])
                state[d, hd] = (cst, m_new)
    for (d, hd), (cst, m_new) in state.items():
        c_ref[d, hd] = cst
        r = d * ML_HEADS + hd
        m_ref[r:r + 1, :] = jnp.broadcast_to(m_new, (1, LANES))

    for d, h_ref in enumerate((hf_ref, hb_ref)):
        for j in range(nchk):
            for p in range(ML_HEADS // 2):
                pair = jnp.where(row_l < HEAD_DIM, hs[d, j, 2 * p], hs[d, j, 2 * p + 1])
                h_ref[rows_of(j), p * LANES:(p + 1) * LANES] = pair.T


def _mlstm(l, mlqk, mlv, misc, prm, tabs, dims):
    T, W = mlqk.shape
    B, S, CTX = dims
    nchk = ML_STEP_CHUNKS
    blk = nchk * ML_CHUNK
    nb_ctx = CTX // blk
    nb_lat = S // blk
    ctx0 = B * S // blk

    def fwd(b, c):
        return jnp.where(c < nb_ctx, ctx0 + b * nb_ctx + c, b * nb_lat + (c - nb_ctx))

    def bwd(b, c):
        return jnp.where(c < nb_ctx, ctx0 + b * nb_ctx + (nb_ctx - 1 - c),
                         b * nb_lat + (nb_lat - 1 - (c - nb_ctx)))

    sub = blk // 8
    last8 = T // 8 - 1

    def specs(fn):
        return [pl.BlockSpec((blk, W), lambda b, c: (fn(b, c), 0)),
                pl.BlockSpec((8, W), lambda b, c: (jnp.maximum(fn(b, c) * sub - 1, 0), 0)),
                pl.BlockSpec((8, W), lambda b, c: (jnp.minimum((fn(b, c) + 1) * sub, last8), 0)),
                pl.BlockSpec((blk, ML_W), lambda b, c: (fn(b, c), 0)),
                pl.BlockSpec((blk, LANES), lambda b, c: (fn(b, c), 0))]

    return pl.pallas_call(
        functools.partial(_mlstm_kernel, nb_ctx=nb_ctx, nb_lat=nb_lat, nchk=nchk),
        grid=(B, nb_ctx + nb_lat),
        in_specs=specs(fwd) + specs(bwd) + [
            pl.BlockSpec((None, 8, W), lambda b, c: (l, 0, 0)),
            pl.BlockSpec((None, 1, W), lambda b, c: (l, 0, 0)),
            pl.BlockSpec((None, 1, LANES), lambda b, c: (l, 0, 0)),
            pl.BlockSpec((ML_CHUNK, ML_CHUNK), lambda b, c: (0, 0)),
            pl.BlockSpec((ML_CHUNK, ML_CHUNK), lambda b, c: (0, 0)),
        ],
        out_specs=[pl.BlockSpec((blk, ML_W), lambda b, c: (fwd(b, c), 0)),
                   pl.BlockSpec((blk, ML_W), lambda b, c: (bwd(b, c), 0))],
        out_shape=[jax.ShapeDtypeStruct((T, ML_W), F32)] * 2,
        scratch_shapes=[pltpu.VMEM((2, ML_HEADS, LANES, LANES), F32), pltpu.VMEM((8, LANES), F32)],
        compiler_params=_cparams(("arbitrary", "arbitrary")),
        name="mlstm",
    )(mlqk, mlqk, mlqk, mlv, misc, mlqk, mlqk, mlqk, mlv, misc,
      prm["conv_w"], prm["conv_b"], prm["b_gate"], tabs["tril"], tabs["triu"])


def _ffn_chunks(ff):
    step = 2 * MXU_DIM
    starts = list(range(0, ff - ff % step, step)) or [0]
    return [(c, (ff - c) if c == starts[-1] else step) for c in starts]


def _out_kernel(x_ref, mod_ref, ona_ref, omla_ref, hf_ref, hb_ref, mlo_ref, hg_ref, bd64_ref,
                w_out_ref, g_ref, wab_ref, wo_ref, o_ref):
    x = x_ref[...]
    gate1 = mod_ref[2:3, :]
    shift2 = mod_ref[3:4, :]
    scale2 = mod_ref[4:5, :]
    gate2 = mod_ref[5:6, :]

    h = hf_ref[...] + hb_ref[...]
    hn = h * lax.rsqrt(_group_sumsq(h, bd64_ref[...]) * (1.0 / HEAD_DIM) + EPS) * hg_ref[...]
    ob = (hn * _sigmoid(mlo_ref[...])).astype(BF16)
    mix = _dot(jnp.concatenate([ona_ref[...], ob, omla_ref[...]], axis=-1), w_out_ref[...])
    x1 = x + gate1 * mix

    h2 = ((_rms_rows(x1) * g_ref[...]) * (1.0 + scale2) + shift2).astype(BF16)
    acc = jnp.zeros_like(x1)
    ff = wo_ref.shape[0]
    for c0, w in _ffn_chunks(ff):
        a = _dot(h2, wab_ref[:, c0:c0 + w])
        b = _dot(h2, wab_ref[:, ff + c0:ff + c0 + w])
        act = (a * _sigmoid(a) * b).astype(BF16)
        acc = acc + _dot(act, wo_ref[c0:c0 + w, :])
    o_ref[...] = x1 + gate2 * acc


def _out(l, xs, mods, ona, omla, hf, hb, mlo, prm, tabs, dims, with_ctx):
    D = xs.shape[1]
    B, S, CTX = dims
    T = xs.shape[0] if with_ctx else B * S
    nt = T // ROW_TILE
    lat_tiles = S // ROW_TILE
    ff = prm["w_fo"].shape[1]

    def grp(i):
        return jnp.minimum(i // lat_tiles, B)

    row = lambda w: pl.BlockSpec((ROW_TILE, w), lambda i: (i, 0))
    lay = lambda shape: pl.BlockSpec((None,) + shape, lambda i: (l,) + (0,) * len(shape))
    big = lambda shape: pl.BlockSpec((None,) + shape, lambda i: (l,) + (0,) * len(shape),
                                     pipeline_mode=pl.Buffered(1))
    return pl.pallas_call(
        _out_kernel,
        grid=(nt,),
        in_specs=[
            row(D),
            pl.BlockSpec((None, None, 6, D), lambda i: (l, grp(i), 0, 0)),
            row(NA_W), row(MLA_HEADS * MLA_V), row(ML_W), row(ML_W), row(ML_W),
            lay((1, ML_W)),
            pl.BlockSpec((MXU_DIM, MXU_DIM), lambda i: (0, 0)),
            big((NA_W + ML_W + MLA_HEADS * MLA_V, D)),
            lay((1, D)),
            big((D, 2 * ff)), big((ff, D)),
        ],
        out_specs=row(D),
        out_shape=jax.ShapeDtypeStruct((T, D), F32),
        compiler_params=_cparams(("arbitrary",)),
        name="out_ffn",
    )(xs, mods, ona, omla, hf, hb, mlo, prm["head_gain"], tabs["bd64"],
      prm["w_out"], prm["g_ffn"], prm["w_ab"], prm["w_fo"])


def _block_diag_ones(group):
    idx = np.arange(MXU_DIM) // group
    return jnp.asarray((idx[:, None] == idx[None, :]).astype(np.float32), dtype=BF16)


def _rope_tables(S, n_ctx_rows):
    n_freq = MLA_ROPE // 4
    inv = ROPE_BASE ** (-jnp.arange(n_freq, dtype=F32) / n_freq)
    t = jnp.arange(S, dtype=jnp.int32)
    row = (t // GRID_W).astype(F32)
    col = (t % GRID_W).astype(F32)
    ang = jnp.concatenate([row[:, None] * inv, col[:, None] * inv], axis=-1)
    ang = jnp.concatenate([ang, jnp.zeros((n_ctx_rows, MLA_ROPE // 2), F32)], axis=0)
    cos, sin = jnp.cos(ang), jnp.sin(ang)
    n = S + n_ctx_rows
    half = MLA_ROPE // 2
    ones = jnp.ones((n, MLA_NOPE), F32)
    z = lambda w: jnp.zeros((n, w), F32)
    tail = LANES - MLA_NOPE - MLA_ROPE
    del half
    cos_t = jnp.concatenate([ones, cos, cos, jnp.ones((n, tail), F32)], axis=-1)
    sin_t = jnp.concatenate([z(MLA_NOPE), -sin, sin, z(tail)], axis=-1)
    return cos_t, sin_t


def _na_bias_kernel(rpb_ref, o_ref, *, plan):
    li = pl.program_id(0)
    h = pl.program_id(1)
    n_dc = 2 * NA_WIN_C - 1
    ck = lax.broadcasted_iota(jnp.int32, (GRID_W, LANES), 0)
    lane = lax.broadcasted_iota(jnp.int32, (GRID_W, LANES), 1)
    cq = lane & (GRID_W - 1)
    dc = ck - cq + (NA_WIN_C - 1)
    cs = jnp.clip(cq - NA_WIN_C // 2, 0, GRID_W - NA_WIN_C)
    col_ok = (ck >= cs) & (ck < cs + NA_WIN_C)
    hits = [dc == j for j in range(n_dc)]
    neg = jnp.full((GRID_W, LANES), NEG, F32)
    tiles = []
    for dr in range(2 * NA_WIN_R - 1):
        t = neg
        for j in range(n_dc):
            t = jnp.where(hits[j], rpb_ref[li, h, dr * n_dc + j] * LOG2E, t)
        tiles.append(jnp.where(col_ok, t, neg))
    low = lane < GRID_W
    for v, rows in enumerate(plan):
        for rk, drs in enumerate(rows):
            pick = [neg if d is None else tiles[d] for d in drs]
            groups = [jnp.where(low, pick[2 * g], pick[2 * g + 1]) for g in range(len(drs) // 2)]
            o_ref[v, rk * GRID_W:(rk + 1) * GRID_W, :] = jnp.concatenate(groups, axis=-1).astype(o_ref.dtype)
    o_ref[len(plan)] = jnp.full(o_ref.shape[1:], NEG, o_ref.dtype)


def _na_bias(rpb, rows):
    depth, H = rpb.shape[:2]
    qr, kr_n = ATT_BLOCK // GRID_W, NA_KEY_ROWS // GRID_W
    nb = rows // qr
    plan = []
    for j in (0, 1, nb - 1):
        ks = int(np.clip(j - 1, 0, nb - 3)) * qr
        per_key_row = []
        for kr in range(ks, ks + kr_n):
            drs = []
            for r in range(j * qr, (j + 1) * qr):
                rs = int(np.clip(r - NA_WIN_R // 2, 0, rows - NA_WIN_R))
                drs.append(kr - r + NA_WIN_R - 1 if rs <= kr < rs + NA_WIN_R else None)
            per_key_row.append(tuple(drs))
        plan.append(tuple(per_key_row))
    n_rel = (2 * NA_WIN_R - 1) * (2 * NA_WIN_C - 1)
    return pl.pallas_call(
        functools.partial(_na_bias_kernel, plan=tuple(plan)),
        grid=(depth, H),
        in_specs=[pl.BlockSpec(memory_space=pltpu.SMEM)],
        out_specs=pl.BlockSpec((None, len(plan) + 1, None, NA_KEY_ROWS, ATT_BLOCK),
                               lambda li, h: (li, 0, h, 0, 0)),
        out_shape=jax.ShapeDtypeStruct((depth, len(plan) + 1, H, NA_KEY_ROWS, ATT_BLOCK), BF16),
        compiler_params=_cparams(("arbitrary", "arbitrary")),
        name="na_bias",
    )(rpb.reshape(depth, H, n_rel))


def _prepare(w_in, w_uq, w_ukv, w_out, w_ffn_in, w_ffn_out, g_mix, g_ffn, b_gate, na_qk_gain,
             ml_conv_w, ml_conv_b, ml_head_gain, mla_gq, mla_gkv, mla_qk_gain):
    depth, D, _ = w_in.shape
    o_gate = C_MLO + ML_W
    o_ql = o_gate + 4 * ML_HEADS
    o_kr = o_ql + Q_LORA + KV_LORA
    half = MLA_ROPE // 2

    def layout_kernel(w_ref, o_ref):
        w = w_ref[...]
        zc = lambda n: jnp.zeros((w.shape[0], n), w.dtype)
        o_ref[...] = jnp.concatenate([
            w[:, :o_gate], w[:, o_ql:o_kr],
            w[:, o_gate:o_ql], zc(MISC_ROPE_LANE - 4 * ML_HEADS),
            w[:, o_kr:], zc(LANES - MISC_ROPE_LANE - MLA_ROPE),
            zc(MISC_ROPE_LANE), w[:, o_kr + half:], w[:, o_kr:o_kr + half],
            zc(LANES - MISC_ROPE_LANE - MLA_ROPE)], axis=-1).astype(BF16)

    rows = MXU_DIM
    w_in_p = pl.pallas_call(
        layout_kernel,
        grid=(depth, D // rows),
        in_specs=[pl.BlockSpec((None, rows, w_in.shape[2]), lambda li, i: (li, i, 0))],
        out_specs=pl.BlockSpec((None, rows, D_IN_PAD), lambda li, i: (li, i, 0)),
        out_shape=jax.ShapeDtypeStruct((depth, D, D_IN_PAD), BF16),
        compiler_params=_cparams(("arbitrary", "arbitrary")),
        name="w_in_layout",
    )(w_in)
    pad_h = lambda a, w: jnp.pad(a, [(0, 0)] * (a.ndim - 1) + [(0, LANES - w)])

    def partner(a):
        lo, hi = a[..., MLA_NOPE:MLA_NOPE + half], a[..., MLA_NOPE + half:MLA_QK]
        return jnp.concatenate([jnp.zeros_like(a[..., :MLA_NOPE]), hi, lo,
                                jnp.zeros_like(a[..., MLA_QK:])], axis=-1)

    w_uq_h = pad_h(w_uq.reshape(depth, Q_LORA, MLA_HEADS, MLA_QK), MLA_QK)
    w_uq_p = w_uq_h.reshape(depth, Q_LORA, MLA_HEADS * LANES).astype(BF16)
    w_uq_s = partner(w_uq_h).reshape(depth, Q_LORA, MLA_HEADS * LANES).astype(BF16)
    ukv = w_ukv.reshape(depth, KV_LORA, MLA_HEADS, MLA_NOPE + MLA_V)
    w_uk_p = pad_h(ukv[..., :MLA_NOPE], MLA_NOPE).reshape(depth, KV_LORA, MLA_HEADS * LANES).astype(BF16)
    w_uv = ukv[..., MLA_NOPE:].reshape(depth, KV_LORA, MLA_HEADS * MLA_V).astype(BF16)
    mla_g = pad_h(mla_qk_gain, MLA_QK)
    return {
        "w_in": w_in_p, "w_uq": w_uq_p, "w_uqs": w_uq_s, "w_uk": w_uk_p, "w_uv": w_uv,
        "w_out": w_out.astype(BF16),
        "w_ab": w_ffn_in.astype(BF16),
        "w_fo": w_ffn_out.astype(BF16),
        "g_mix": g_mix[:, None, :], "g_ffn": g_ffn[:, None, :],
        "gq": mla_gq[:, None, :], "gkv": mla_gkv[:, None, :],
        "na_gq": jnp.tile(na_qk_gain[:, 0:1, :], (1, 1, NA_HEADS)) * (NA_SCALE * LOG2E),
        "na_gk": jnp.tile(na_qk_gain[:, 1:2, :], (1, 1, NA_HEADS)),
        "mla_gq": jnp.tile(mla_g[:, 0:1, :], (1, 1, MLA_HEADS)) * (MLA_SCALE * LOG2E),
        "mla_gk": jnp.tile(mla_g[:, 1:2, :], (1, 1, MLA_HEADS)),
        "mla_gqs": jnp.tile(partner(mla_g[:, 0:1, :]), (1, 1, MLA_HEADS)) * (MLA_SCALE * LOG2E),
        "mla_gks": jnp.tile(partner(mla_g[:, 1:2, :]), (1, 1, MLA_HEADS)),
        "head_gain": ml_head_gain.reshape(depth, 1, ML_W),
        "b_gate": pad_h(b_gate, 4 * ML_HEADS)[:, None, :],
        "conv_w": jnp.pad(ml_conv_w, ((0, 0), (0, 8 - ml_conv_w.shape[1]), (0, 0))),
        "conv_b": ml_conv_b[:, None, :],
    }


def kernel(x, c, ctx, c_ctx, w_mod, b_mod, g_mix, g_ffn, w_in, b_gate, na_qk_gain, na_rpb,
           ml_conv_w, ml_conv_b, ml_head_gain, mla_gq, mla_gkv, w_uq, w_ukv, mla_qk_gain,
           w_out, w_ffn_in, w_ffn_out):
    B, S, D = x.shape
    CTX = ctx.shape[1]
    depth = w_in.shape[0]
    assert CTX == ATT_BLOCK and S % ROW_TILE == 0 and (B * CTX) % ROW_TILE == 0
    assert CTX % (ML_STEP_CHUNKS * ML_CHUNK) == 0 and S % (ML_STEP_CHUNKS * ML_CHUNK) == 0
    assert S % (2 * FLASH_TK) == 0 and S % (FLASH_TILES * FLASH_TQ) == 0 and S // ATT_BLOCK >= 3 and B + 1 <= 8
    dims = (B, S, CTX)

    prm = _prepare(w_in, w_uq, w_ukv, w_out, w_ffn_in, w_ffn_out, g_mix, g_ffn, b_gate, na_qk_gain,
                   ml_conv_w, ml_conv_b, ml_head_gain, mla_gq, mla_gkv, mla_qk_gain)
    cos_t, sin_t = _rope_tables(S, B * CTX)
    tri = np.tril(np.ones((ML_CHUNK, ML_CHUNK), np.float32))
    tabs = {
        "bd64": _block_diag_ones(HEAD_DIM), "bd128": _block_diag_ones(LANES),
        "cos": cos_t, "sin": sin_t,
        "tril": jnp.asarray(tri, dtype=BF16), "triu": jnp.asarray(tri.T, dtype=BF16),
    }
    bias = _na_bias(na_rpb, S // GRID_W)

    cond8 = jnp.concatenate([c, c_ctx[None, :], jnp.zeros((8 - B - 1, D), F32)], axis=0)
    mods = _modulation(cond8, w_mod, b_mod)
    mods = mods.reshape(depth, 8, 6, D)

    xs = jnp.concatenate([x.reshape(B * S, D), ctx.reshape(B * CTX, D)], axis=0)
    for l in range(depth):
        with_ctx = l < depth - 1
        naq, nak, mlqk, mlv, mlo, misc, mq, mk, navt, mvt = _proj(l, xs, mods, prm, tabs, dims)
        ona = _na(l, naq, nak, navt, bias, dims, with_ctx)
        omla = _flash(mq, mk, mvt, dims, with_ctx)
        hf, hb = _mlstm(l, mlqk, mlv, misc, prm, tabs, dims)
        xs = _out(l, xs, mods, ona, omla, hf, hb, mlo, prm, tabs, dims, with_ctx)
    return xs.reshape(B, S, D)
```

```python
import functools

import numpy as np
import jax
import jax.numpy as jnp
from jax import lax
from jax.experimental import pallas as pl
from jax.experimental.pallas import tpu as pltpu

GRID_W = 64
HEAD_DIM = 64
NA_HEADS = 4
NA_WIN_R = 8
NA_WIN_C = 16
ML_HEADS = 4
ML_CHUNK = 128
ML_STEP_CHUNKS = 2
MLA_HEADS = 8
MLA_NOPE = 64
MLA_ROPE = 32
MLA_V = 64
Q_LORA = 384
KV_LORA = 256
ROPE_BASE = 10000.0
EPS = 1e-6
NA_W = NA_HEADS * HEAD_DIM
ML_W = ML_HEADS * HEAD_DIM
MLA_QK = MLA_NOPE + MLA_ROPE
NA_SCALE = HEAD_DIM ** -0.5
MLA_SCALE = MLA_QK ** -0.5
LOG2E = 1.4426950408889634

LANES = 128
MXU_DIM = 256
VMEM_LIMIT = 56 * 1024 * 1024

ROW_TILE = 512
ATT_BLOCK = 256
NA_KEY_ROWS = 768
FLASH_TK = 512
FLASH_TQ = 1024
FLASH_TILES = 2
FLASH_SLOTS = 2
FLASH_UNROLL = 4
NEG = -1e30
DEN_ROWS = 16

C_NAQ, C_NAK, C_NAV = 0, 256, 512
C_MLQK, C_MLV, C_MLO = 768, 1280, 1536
C_QL = 1792
C_KVL = C_QL + Q_LORA
C_MISC = C_KVL + KV_LORA
C_MISC2 = C_MISC + LANES
D_IN_PAD = C_MISC2 + LANES
MISC_ROPE_LANE = 64

BF16 = jnp.bfloat16
F32 = jnp.float32


def _cparams(sem):
    return pltpu.CompilerParams(dimension_semantics=sem, vmem_limit_bytes=VMEM_LIMIT)


def _dot(a, b):
    return jnp.dot(a, b, preferred_element_type=F32)


def _dot_nt(a, b):
    return lax.dot_general(a, b, (((1,), (1,)), ((), ())), preferred_element_type=F32)


def _split2(x):
    hi = x.astype(BF16)
    lo = (x - hi.astype(F32)).astype(BF16)
    return hi, lo


def _group_sumsq(x, bd):
    x2 = x * x
    hi, lo = _split2(x2)
    outs = []
    for c in range(x.shape[1] // MXU_DIM):
        sl = slice(c * MXU_DIM, (c + 1) * MXU_DIM)
        outs.append(_dot(hi[:, sl], bd) + _dot(lo[:, sl], bd))
    return outs[0] if len(outs) == 1 else jnp.concatenate(outs, axis=-1)


def _rms_rows(x):
    return x * lax.rsqrt(jnp.mean(x * x, axis=-1, keepdims=True) + EPS)


def _sigmoid(x):
    return 1.0 / (1.0 + jnp.exp(-x))


def _mod_kernel(c_ref, w_ref, b_ref, o_ref):
    c = c_ref[...]
    a = c * _sigmoid(c)
    o_ref[...] = jnp.dot(a, w_ref[...], preferred_element_type=F32,
                         precision=lax.Precision.HIGHEST) + b_ref[...]


def _modulation(cond8, w_mod, b_mod):
    depth, d, d6 = w_mod.shape
    bw = 2 * d
    return pl.pallas_call(
        _mod_kernel,
        grid=(depth, d6 // bw),
        in_specs=[
            pl.BlockSpec((8, d), lambda l, j: (0, 0)),
            pl.BlockSpec((None, d, bw), lambda l, j: (l, 0, j)),
            pl.BlockSpec((None, 1, bw), lambda l, j: (l, 0, j)),
        ],
        out_specs=pl.BlockSpec((None, 8, bw), lambda l, j: (l, 0, j)),
        out_shape=jax.ShapeDtypeStruct((depth, 8, d6), F32),
        compiler_params=_cparams(("arbitrary", "arbitrary")),
        name="modulation",
    )(cond8, w_mod, b_mod.reshape(depth, 1, d6))


def _proj_kernel(x_ref, mod_ref, g_ref, w_in_ref, w_uq_ref, w_uqs_ref, w_uk_ref, w_uv_ref,
                 gq_ref, gkv_ref, na_gq_ref, na_gk_ref, mla_gq_ref, mla_gk_ref, mla_gqs_ref, mla_gks_ref,
                 bd64_ref, bd128_ref, cos_ref, sin_ref,
                 naq_ref, nak_ref, mlqk_ref, mlv_ref, mlo_ref, misc_ref,
                 mq_ref, mk_ref, nav_ref, mv_ref):
    x = x_ref[...]
    shift = mod_ref[0:1, :]
    scale = mod_ref[1:2, :]
    h = (_rms_rows(x) * g_ref[...]) * (1.0 + scale) + shift
    hb = h.astype(BF16)

    def proj(c0, width):
        return _dot(hb, w_in_ref[:, c0:c0 + width])

    bd64 = bd64_ref[...]
    bd128 = bd128_ref[...]

    lat = proj(C_QL, D_IN_PAD - C_QL)
    ql = lat[:, :Q_LORA]
    kvl = lat[:, C_KVL - C_QL:C_MISC - C_QL]
    misc = lat[:, C_MISC - C_QL:C_MISC2 - C_QL]
    kx = lat[:, C_MISC2 - C_QL:]
    misc_ref[...] = misc
    qn = (_rms_rows(ql) * gq_ref[...]).astype(BF16)
    kvn = (_rms_rows(kvl) * gkv_ref[...]).astype(BF16)

    na = proj(C_NAQ, 3 * NA_W)
    pq = na[:, :NA_W]
    pk = na[:, NA_W:2 * NA_W]
    nav_ref[...] = na[:, 2 * NA_W:].T.astype(BF16)

    qr = _dot(qn, w_uq_ref[...])
    qx = _dot(qn, w_uqs_ref[...])
    lane = lax.broadcasted_iota(jnp.int32, misc.shape, 1)
    krope = jnp.where((lane >= MISC_ROPE_LANE) & (lane < MISC_ROPE_LANE + MLA_ROPE), misc, 0.0)
    kr = _dot(kvn, w_uk_ref[...]) + jnp.concatenate([krope] * MLA_HEADS, axis=-1)
    mv_ref[...] = _dot(kvn, w_uv_ref[...]).T.astype(BF16)

    ml = proj(C_MLQK, 4 * ML_W)
    mlqk_ref[...] = ml[:, :2 * ML_W]
    mlv_ref[...] = ml[:, 2 * ML_W:3 * ML_W].astype(BF16)
    mlo_ref[...] = ml[:, 3 * ML_W:]

    naq_ref[...] = (pq * lax.rsqrt(_group_sumsq(pq, bd64) * (1.0 / HEAD_DIM) + EPS)
                    * na_gq_ref[...]).astype(BF16)
    nak_ref[...] = (pk * lax.rsqrt(_group_sumsq(pk, bd64) * (1.0 / HEAD_DIM) + EPS)
                    * na_gk_ref[...]).astype(BF16)

    cos = cos_ref[...]
    sin = sin_ref[...]
    rq = lax.rsqrt(_group_sumsq(qr, bd128) * (1.0 / MLA_QK) + EPS)
    rk = lax.rsqrt(_group_sumsq(kr, bd128) * (1.0 / MLA_QK) + EPS)
    kxs = kx * sin
    for g in range(MLA_HEADS):
        sl = slice(g * LANES, (g + 1) * LANES)
        mq_ref[:, sl] = (rq[:, sl] * (qr[:, sl] * mla_gq_ref[:, sl] * cos
                                      + qx[:, sl] * mla_gqs_ref[:, sl] * sin)).astype(BF16)
        mk_ref[:, sl] = (rk[:, sl] * (kr[:, sl] * mla_gk_ref[:, sl] * cos
                                      + kxs * mla_gks_ref[:, sl])).astype(BF16)


def _proj(l, xs, mods, prm, tabs, dims):
    T, D = xs.shape
    B, S, CTX = dims
    nt = T // ROW_TILE
    lat_tiles = S // ROW_TILE

    def grp(i):
        return jnp.minimum(i // lat_tiles, B)

    def tab_row(i):
        return jnp.where(i < B * lat_tiles, i % lat_tiles, lat_tiles + (i - B * lat_tiles))

    row = lambda w: pl.BlockSpec((ROW_TILE, w), lambda i: (i, 0))
    lay = lambda shape: pl.BlockSpec((None,) + shape, lambda i: (l,) + (0,) * len(shape))
    const = lambda shape: pl.BlockSpec(shape, lambda i: (0,) * len(shape))
    tab = pl.BlockSpec((ROW_TILE, LANES), lambda i: (tab_row(i), 0))
    out_w = [(NA_W, BF16), (NA_W, BF16), (2 * ML_W, F32), (ML_W, BF16), (ML_W, F32),
             (LANES, F32), (MLA_HEADS * LANES, BF16), (MLA_HEADS * LANES, BF16)]
    out_t = [NA_W, MLA_HEADS * MLA_V]
    return pl.pallas_call(
        _proj_kernel,
        grid=(nt,),
        in_specs=[
            row(D),
            pl.BlockSpec((None, None, 6, D), lambda i: (l, grp(i), 0, 0)),
            lay((1, D)),
            lay((D, D_IN_PAD)), lay((Q_LORA, MLA_HEADS * LANES)), lay((Q_LORA, MLA_HEADS * LANES)),
            lay((KV_LORA, MLA_HEADS * LANES)), lay((KV_LORA, MLA_HEADS * MLA_V)),
            lay((1, Q_LORA)), lay((1, KV_LORA)), lay((1, NA_W)), lay((1, NA_W)),
            lay((1, MLA_HEADS * LANES)), lay((1, MLA_HEADS * LANES)),
            lay((1, MLA_HEADS * LANES)), lay((1, MLA_HEADS * LANES)),
            const((MXU_DIM, MXU_DIM)), const((MXU_DIM, MXU_DIM)),
            tab, tab,
        ],
        out_specs=[row(w) for w, _ in out_w]
        + [pl.BlockSpec((w, ROW_TILE), lambda i: (0, i)) for w in out_t],
        out_shape=[jax.ShapeDtypeStruct((T, w), dt) for w, dt in out_w]
        + [jax.ShapeDtypeStruct((w, T), BF16) for w in out_t],
        compiler_params=_cparams(("arbitrary",)),
        name="proj",
    )(xs, mods, prm["g_mix"], prm["w_in"], prm["w_uq"], prm["w_uqs"], prm["w_uk"], prm["w_uv"],
      prm["gq"], prm["gkv"], prm["na_gq"], prm["na_gk"],
      prm["mla_gq"], prm["mla_gk"], prm["mla_gqs"], prm["mla_gks"],
      tabs["bd64"], tabs["bd128"], tabs["cos"], tabs["sin"])


def _short_conv(x, prev_blk, next_blk, first, last, w_ref, b_ref):
    n = x.shape[0]
    prev_row = jnp.where(first, 0.0, prev_blk[7:8, :])
    next_row = jnp.where(last, 0.0, next_blk[0:1, :])
    ridx = lax.broadcasted_iota(jnp.int32, x.shape, 0)
    xm1 = jnp.where(ridx == 0, prev_row, pltpu.roll(x, 1, 0))
    xp1 = jnp.where(ridx == n - 1, next_row, pltpu.roll(x, n - 1, 0))
    y = b_ref[...] + xm1 * w_ref[0:1, :] + x * w_ref[1:2, :] + xp1 * w_ref[2:3, :]
    y = y * _sigmoid(y)
    return y[:, :ML_W].astype(BF16), (y[:, ML_W:] * (HEAD_DIM ** -0.5)).astype(BF16)


def _na_kernel(q_ref, k_ref, vt_ref, kc_ref, vct_ref, bias_ref, o_ref, *, nb):
    j = pl.program_id(1)
    start = pl.multiple_of(jnp.clip(j - 1, 0, nb - 3) * ATT_BLOCK, ATT_BLOCK)
    q = q_ref[...]
    lane = lax.broadcasted_iota(jnp.int32, (ATT_BLOCK, LANES), 1)
    scores = []
    for h in range(NA_HEADS):
        p, e = divmod(h, 2)
        cols = slice(p * LANES, (p + 1) * LANES)
        qp = q[:, cols]
        keep = (lane < HEAD_DIM) if e == 0 else (lane >= HEAD_DIM)
        qt = jnp.where(keep, qp, jnp.zeros_like(qp)).astype(F32).T.astype(BF16)
        s_loc = _dot(k_ref[pl.ds(start, NA_KEY_ROWS), cols], qt) + bias_ref[h].astype(F32)
        s_ctx = _dot(kc_ref[:, cols], qt)
        scores.append((s_loc, s_ctx))
    probs = []
    for s_loc, s_ctx in scores:
        m = jnp.maximum(jnp.max(s_loc, axis=0, keepdims=True), jnp.max(s_ctx, axis=0, keepdims=True))
        probs.append((jnp.exp2(s_loc - m).astype(BF16), jnp.exp2(s_ctx - m).astype(BF16)))
    outs = []
    for h, (p_loc, p_ctx) in enumerate(probs):
        rows = slice(h * HEAD_DIM, (h + 1) * HEAD_DIM)
        ot = (_dot(_flash_values(vt_ref[rows, pl.ds(start, NA_KEY_ROWS)]), p_loc)
              + _dot(_flash_values(vct_ref[rows, :]), p_ctx))
        outs.append(ot[:HEAD_DIM, :] / ot[HEAD_DIM:HEAD_DIM + 1, :])
    o_ref[...] = jnp.concatenate(outs, axis=0).T.astype(o_ref.dtype)


def _na(l, naq, nak, navt, bias, dims, with_ctx):
    B, S, CTX = dims
    nb = S // ATT_BLOCK
    ctx_blk0 = B * S // CTX
    n_steps = nb + 1 if with_ctx else nb
    rows_out = naq.shape[0] if with_ctx else B * S

    def qrow(b, j):
        return jnp.where(j < nb, b * nb + j, ctx_blk0 + b)

    def variant(j):
        return jnp.where(j == 0, 0, jnp.where(j == nb - 1, 2, jnp.where(j == nb, 3, 1)))

    kern = functools.partial(_na_kernel, nb=nb)
    return pl.pallas_call(
        kern,
        grid=(B, n_steps),
        in_specs=[
            pl.BlockSpec((ATT_BLOCK, NA_W), lambda b, j: (qrow(b, j), 0)),
            pl.BlockSpec((S, NA_W), lambda b, j: (b, 0)),
            pl.BlockSpec((NA_W, S), lambda b, j: (0, b)),
            pl.BlockSpec((CTX, NA_W), lambda b, j: (ctx_blk0 + b, 0)),
            pl.BlockSpec((NA_W, CTX), lambda b, j: (0, ctx_blk0 + b)),
            pl.BlockSpec((None, None, NA_HEADS, NA_KEY_ROWS, ATT_BLOCK), lambda b, j: (l, variant(j), 0, 0, 0)),
        ],
        out_specs=pl.BlockSpec((ATT_BLOCK, NA_W), lambda b, j: (qrow(b, j), 0)),
        out_shape=jax.ShapeDtypeStruct((rows_out, NA_W), BF16),
        compiler_params=_cparams(("arbitrary", "arbitrary")),
        name="na",
    )(naq, nak, navt, nak, navt, bias)


def _flash_values(v):
    return jnp.concatenate([v, jnp.ones((DEN_ROWS, v.shape[1]), BF16)], axis=0)


def _flash_queries(q):
    return [q[:, e * LANES:(e + 1) * LANES].astype(F32).T.astype(BF16) for e in range(2)]


def _flash_ctx_scores(qs, kc_ref):
    return [_dot(kc_ref[:, e * LANES:(e + 1) * LANES], qs[e]) for e in range(2)]


def _flash_ctx_init(sts, vc_ref, acc_ref, st_ref):
    ps = []
    for e, st in enumerate(sts):
        m = jnp.max(st, axis=0, keepdims=True)
        st_ref[e] = m
        ps.append(jnp.exp2(st - m).astype(BF16))
    for e, p in enumerate(ps):
        acc_ref[e] = _dot(_flash_values(vc_ref[e * HEAD_DIM:(e + 1) * HEAD_DIM, :]), p)


def _flash_finish(acc_ref, o_ref, rows=slice(None)):
    ot = jnp.concatenate([acc_ref[e, :HEAD_DIM, :] / acc_ref[e, HEAD_DIM:HEAD_DIM + 1, :]
                          for e in range(2)], axis=0)
    o_ref[rows, :] = ot.T.astype(o_ref.dtype)


def _flash_kernel(q_ref, km_ref, vm_ref, kc_ref, vc_ref, o_ref, *scratch, n_main):
    ns = FLASH_SLOTS
    s_refs = [scratch[2 * i:2 * i + 2] for i in range(ns)]
    mx_refs = [scratch[2 * ns + 2 * i:2 * ns + 2 * i + 2] for i in range(ns)]
    tk = FLASH_TK
    tq = FLASH_TQ
    for t in range(FLASH_TILES):
        acc_ref, st_ref = scratch[4 * ns + 2 * t:4 * ns + 2 * t + 2]
        rows = slice(t * tq, (t + 1) * tq)
        _flash_tile(_flash_queries(q_ref[rows, :]), km_ref, vm_ref, kc_ref, vc_ref, o_ref, rows,
                    s_refs, mx_refs, acc_ref, st_ref, n_main)


def _flash_tile(qs, km_ref, vm_ref, kc_ref, vc_ref, o_ref, rows, s_refs, mx_refs, acc_ref, st_ref, n_main):
    ns = FLASH_SLOTS
    tk = FLASH_TK

    def scores1(slot, r0, e):
        st = _dot(km_ref[pl.ds(r0, tk), e * LANES:(e + 1) * LANES], qs[e])
        s_refs[slot][e][...] = st
        mx_refs[slot][e][...] = jnp.max(st, axis=0, keepdims=True)

    def absorb1(slot, r0, e):
        m = st_ref[e]
        m_new = jnp.maximum(m, mx_refs[slot][e][...])
        st_ref[e] = m_new
        alpha = jnp.exp2(m - m_new)
        p = jnp.exp2(s_refs[slot][e][...] - m_new).astype(BF16)
        v = vm_ref[e * HEAD_DIM:(e + 1) * HEAD_DIM, pl.ds(r0, tk)]
        acc_ref[e] = alpha * acc_ref[e] + _dot(_flash_values(v), p)

    def scores(slot, r0):
        for e in range(2):
            scores1(slot, r0, e)

    def absorb(slot, r0):
        for e in range(2):
            absorb1(slot, r0, e)

    ctx_scores = _flash_ctx_scores(qs, kc_ref)
    scores(0, 0)
    _flash_ctx_init(ctx_scores, vc_ref, acc_ref, st_ref)

    def advance(c, r0):
        scores1((c + 1) % ns, r0 + tk, 0)
        absorb1(c % ns, r0, 1)
        scores1((c + 1) % ns, r0 + tk, 1)
        absorb1(c % ns, r0, 0)

    nu = FLASH_UNROLL

    def body(i, carry):
        r0 = pl.multiple_of(i * (nu * tk), nu * tk)
        for c in range(nu):
            advance(c, r0 + c * tk)
        return carry

    n_loop = (n_main - 1) // nu
    lax.fori_loop(0, n_loop, body, 0)
    for c in range(n_loop * nu, n_main - 1):
        advance(c, c * tk)
    absorb((n_main - 1) % ns, (n_main - 1) * tk)
    _flash_finish(acc_ref, o_ref, rows)


def _flash_ctx_kernel(q_ref, kc_ref, vc_ref, o_ref, acc_ref, st_ref):
    _flash_ctx_init(_flash_ctx_scores(_flash_queries(q_ref[...]), kc_ref), vc_ref, acc_ref, st_ref)
    _flash_finish(acc_ref, o_ref)


def _flash(mq, mk, mvt, dims, with_ctx):
    B, S, CTX = dims
    tq = FLASH_TQ
    blk = FLASH_TILES * tq
    nq = S // blk
    ctx_blk0 = B * S // CTX
    npair = MLA_HEADS // 2
    acc = lambda n: [pltpu.VMEM((2, HEAD_DIM + DEN_ROWS, n), F32), pltpu.VMEM((2, 1, n), F32)]
    kc_spec = pl.BlockSpec((CTX, 2 * LANES), lambda b, p, *_: (ctx_blk0 + b, p))
    vc_spec = pl.BlockSpec((LANES, CTX), lambda b, p, *_: (p, ctx_blk0 + b))

    o = pl.pallas_call(
        functools.partial(_flash_kernel, n_main=S // FLASH_TK),
        grid=(B, npair, nq),
        in_specs=[
            pl.BlockSpec((blk, 2 * LANES), lambda b, p, j: (b * nq + j, p)),
            pl.BlockSpec((S, 2 * LANES), lambda b, p, j: (b, p)),
            pl.BlockSpec((LANES, S), lambda b, p, j: (p, b)),
            kc_spec, vc_spec,
        ],
        out_specs=pl.BlockSpec((blk, LANES), lambda b, p, j: (b * nq + j, p)),
        out_shape=jax.ShapeDtypeStruct((B * S, MLA_HEADS * MLA_V), BF16),
        scratch_shapes=[pltpu.VMEM((FLASH_TK, tq), F32)] * (2 * FLASH_SLOTS)
        + [pltpu.VMEM((1, tq), F32)] * (2 * FLASH_SLOTS) + acc(tq) * FLASH_TILES,
        compiler_params=_cparams(("arbitrary", "arbitrary", "arbitrary")),
        name="flash",
    )(mq, mk, mvt, mk, mvt)
    if not with_ctx:
        return o, None
    o_ctx = pl.pallas_call(
        _flash_ctx_kernel,
        grid=(B, npair),
        in_specs=[
            pl.BlockSpec((CTX, 2 * LANES), lambda b, p: (ctx_blk0 + b, p)),
            kc_spec, vc_spec,
        ],
        out_specs=pl.BlockSpec((CTX, LANES), lambda b, p: (b, p)),
        out_shape=jax.ShapeDtypeStruct((B * CTX, MLA_HEADS * MLA_V), BF16),
        scratch_shapes=acc(CTX),
        compiler_params=_cparams(("arbitrary", "arbitrary")),
        name="flash_ctx",
    )(mq, mk, mvt)
    return o, o_ctx


def _mlstm_gates(d, gates, bgate, tri):
    gb = gates + bgate
    ls = jnp.minimum(gb, 0.0) - jnp.log1p(jnp.exp(-jnp.abs(gb)))
    lane = lax.broadcasted_iota(jnp.int32, gb.shape, 1)
    is_f = ((lane >= 4) & (lane < 8)) | ((lane >= 12) & (lane < 16))
    xg = jnp.where(lane < 16, jnp.where(is_f, ls, gb), 0.0)
    x1 = xg.astype(BF16)
    r1 = xg - x1.astype(F32)
    x2 = r1.astype(BF16)
    x3 = (r1 - x2.astype(F32)).astype(BF16)
    cum = _dot(tri, x1) + _dot(tri, x2) + _dot(tri, x3)
    return xg, cum, xg.T, cum.T, jnp.sum(xg, axis=0, keepdims=True)


def _mlstm_kernel(xf_ref, xpf_ref, xnf_ref, vf_ref, gf_ref, xb_ref, xpb_ref, xnb_ref, vb_ref, gb_ref,
                  cw_ref, cb_ref, bgate_ref, tril_ref, triu_ref, hf_ref, hb_ref, c_ref, m_ref,
                  *, nb_ctx, nb_lat, nchk):
    @pl.when(pl.program_id(1) == 0)
    def _():
        c_ref[...] = jnp.zeros_like(c_ref)
        m_ref[...] = jnp.zeros_like(m_ref)

    L = ML_CHUNK
    bgate = bgate_ref[...]
    step = pl.program_id(1)
    in_ctx = step < nb_ctx
    pos = jnp.where(in_ctx, step, step - nb_ctx)
    starts = pos == 0
    ends = pos == jnp.where(in_ctx, nb_ctx, nb_lat) - 1
    qk_f = _short_conv(xf_ref[...], xpf_ref[...], xnf_ref[...], starts, ends, cw_ref, cb_ref)
    qk_b = _short_conv(xb_ref[...], xpb_ref[...], xnb_ref[...], ends, starts, cw_ref, cb_ref)
    q = (qk_f[0], qk_b[0])
    k = (qk_f[1], qk_b[1])
    v = (vf_ref[...], vb_ref[...])
    g_refs = (gf_ref, gb_ref)
    tris = (tril_ref[...], triu_ref[...])
    order = (list(range(nchk)), list(range(nchk - 1, -1, -1)))
    si = lax.broadcasted_iota(jnp.int32, (L, L), 0)
    ti = lax.broadcasted_iota(jnp.int32, (L, L), 1)
    valid = (si <= ti, si >= ti)
    lane_l = lax.broadcasted_iota(jnp.int32, (L, LANES), 1)
    row_l = lax.broadcasted_iota(jnp.int32, (LANES, L), 0)
    rows_of = lambda j: slice(j * L, (j + 1) * L)

    items = {}
    for d in range(2):
        for j in range(nchk):
            xg, cum, xg_t, cum_t, tot = _mlstm_gates(d, g_refs[d][rows_of(j), :], bgate, tris[d])
            for p in range(ML_HEADS // 2):
                cols = slice(p * LANES, (p + 1) * LANES)
                qp = q[d][rows_of(j), cols]
                kp = k[d][rows_of(j), cols]
                vt = v[d][rows_of(j), cols].astype(F32).T
                for e in range(2):
                    hd = 2 * p + e
                    ci = 8 * d + hd
                    cf = 8 * d + 4 + hd
                    in_head = (lane_l < HEAD_DIM) if e == 0 else (lane_l >= HEAD_DIM)
                    in_rows = (row_l < HEAD_DIM) if e == 0 else (row_l >= HEAD_DIM)
                    ones_row = HEAD_DIM if e == 0 else 0
                    b_row = cum_t[cf:cf + 1, :]
                    c_col = xg[:, ci:ci + 1] - cum[:, cf:cf + 1]
                    dm = jnp.where(valid[d], b_row + c_col, NEG)
                    b_tot = tot[:, cf:cf + 1]
                    a_row = b_tot - b_row + xg_t[ci:ci + 1, :]
                    items[d, j, hd] = dict(
                        qp=qp, km=jnp.where(in_head, kp, jnp.zeros_like(kp)),
                        vat=jnp.where(in_rows, vt, jnp.where(row_l == ones_row, 1.0, 0.0)),
                        ones_row=ones_row, b_row=b_row, b_tot=b_tot, dm=dm, a_row=a_row,
                        a_t=jnp.max(dm, axis=0, keepdims=True), a_max=jnp.max(a_row, axis=1, keepdims=True))

    for c in items.values():
        c["qk"] = (_dot_nt(c["km"], c["qp"]) * jnp.exp(c["dm"] - c["a_t"])).astype(BF16)

    for c in items.values():
        c["intra"] = _dot(c["vat"].astype(BF16), c["qk"])
        c["upd"] = _dot((c["vat"] * jnp.exp(c["a_row"] - c["a_max"])).astype(BF16), c["km"])

    state = {(d, hd): (c_ref[d, hd], m_ref[d * ML_HEADS + hd:d * ML_HEADS + hd + 1, 0:1])
             for d in range(2) for hd in range(ML_HEADS)}
    hs = {}
    for n in range(nchk):
        for d in range(2):
            j = order[d][n]
            for hd in range(ML_HEADS):
                c = items[d, j, hd]
                cst, m_prev = state[d, hd]
                g = c["b_row"] + m_prev
                mt = jnp.maximum(g, c["a_t"])
                haug = (jnp.exp(g - mt) * _dot_nt(cst.astype(BF16), c["qp"])
                        + jnp.exp(c["a_t"] - mt) * c["intra"])
                den = haug[c["ones_row"]:c["ones_row"] + 1, :]
                hs[d, j, hd] = haug / jnp.maximum(jnp.abs(den), jnp.exp(-mt))
                m_new = jnp.maximum(c["b_tot"] + m_prev, c["a_max"])
                cst = (jnp.exp(c["b_tot"] + m_prev - m_new) * cst
                       + jnp.exp(c["a_max"] - m_new) * c["upd"])
                state[d, hd] = (cst, m_new)
    for (d, hd), (cst, m_new) in state.items():
        c_ref[d, hd] = cst
        r = d * ML_HEADS + hd
        m_ref[r:r + 1, :] = jnp.broadcast_to(m_new, (1, LANES))

    for d, h_ref in enumerate((hf_ref, hb_ref)):
        for j in range(nchk):
            for p in range(ML_HEADS // 2):
                pair = jnp.where(row_l < HEAD_DIM, hs[d, j, 2 * p], hs[d, j, 2 * p + 1])
                h_ref[rows_of(j), p * LANES:(p + 1) * LANES] = pair.T


def _mlstm(l, mlqk, mlv, misc, prm, tabs, dims):
    T, W = mlqk.shape
    B, S, CTX = dims
    nchk = ML_STEP_CHUNKS
    blk = nchk * ML_CHUNK
    nb_ctx = CTX // blk
    nb_lat = S // blk
    ctx0 = B * S // blk

    def fwd(b, c):
        return jnp.where(c < nb_ctx, ctx0 + b * nb_ctx + c, b * nb_lat + (c - nb_ctx))

    def bwd(b, c):
        return jnp.where(c < nb_ctx, ctx0 + b * nb_ctx + (nb_ctx - 1 - c),
                         b * nb_lat + (nb_lat - 1 - (c - nb_ctx)))

    sub = blk // 8
    last8 = T // 8 - 1

    def specs(fn):
        return [pl.BlockSpec((blk, W), lambda b, c: (fn(b, c), 0)),
                pl.BlockSpec((8, W), lambda b, c: (jnp.maximum(fn(b, c) * sub - 1, 0), 0)),
                pl.BlockSpec((8, W), lambda b, c: (jnp.minimum((fn(b, c) + 1) * sub, last8), 0)),
                pl.BlockSpec((blk, ML_W), lambda b, c: (fn(b, c), 0)),
                pl.BlockSpec((blk, LANES), lambda b, c: (fn(b, c), 0))]

    return pl.pallas_call(
        functools.partial(_mlstm_kernel, nb_ctx=nb_ctx, nb_lat=nb_lat, nchk=nchk),
        grid=(B, nb_ctx + nb_lat),
        in_specs=specs(fwd) + specs(bwd) + [
            pl.BlockSpec((None, 8, W), lambda b, c: (l, 0, 0)),
            pl.BlockSpec((None, 1, W), lambda b, c: (l, 0, 0)),
            pl.BlockSpec((None, 1, LANES), lambda b, c: (l, 0, 0)),
            pl.BlockSpec((ML_CHUNK, ML_CHUNK), lambda b, c: (0, 0)),
            pl.BlockSpec((ML_CHUNK, ML_CHUNK), lambda b, c: (0, 0)),
        ],
        out_specs=[pl.BlockSpec((blk, ML_W), lambda b, c: (fwd(b, c), 0)),
                   pl.BlockSpec((blk, ML_W), lambda b, c: (bwd(b, c), 0))],
        out_shape=[jax.ShapeDtypeStruct((T, ML_W), F32)] * 2,
        scratch_shapes=[pltpu.VMEM((2, ML_HEADS, LANES, LANES), F32), pltpu.VMEM((8, LANES), F32)],
        compiler_params=_cparams(("arbitrary", "arbitrary")),
        name="mlstm",
    )(mlqk, mlqk, mlqk, mlv, misc, mlqk, mlqk, mlqk, mlv, misc,
      prm["conv_w"], prm["conv_b"], prm["b_gate"], tabs["tril"], tabs["triu"])


def _ffn_chunks(ff):
    step = 2 * MXU_DIM
    starts = list(range(0, ff - ff % step, step)) or [0]
    return [(c, (ff - c) if c == starts[-1] else step) for c in starts]


def _out_kernel(x_ref, mod_ref, ona_ref, omla_ref, omlac_ref, hf_ref, hb_ref, mlo_ref, hg_ref, bd64_ref,
                w_out_ref, g_ref, wab_ref, wo_ref, o_ref, *, n_lat):
    x = x_ref[...]
    omla = jnp.where(pl.program_id(0) < n_lat, omla_ref[...], omlac_ref[...])
    gate1 = mod_ref[2:3, :]
    shift2 = mod_ref[3:4, :]
    scale2 = mod_ref[4:5, :]
    gate2 = mod_ref[5:6, :]

    h = hf_ref[...] + hb_ref[...]
    hn = h * lax.rsqrt(_group_sumsq(h, bd64_ref[...]) * (1.0 / HEAD_DIM) + EPS) * hg_ref[...]
    ob = (hn * _sigmoid(mlo_ref[...])).astype(BF16)
    mix = _dot(jnp.concatenate([ona_ref[...], ob, omla], axis=-1), w_out_ref[...])
    x1 = x + gate1 * mix

    h2 = ((_rms_rows(x1) * g_ref[...]) * (1.0 + scale2) + shift2).astype(BF16)
    acc = jnp.zeros_like(x1)
    ff = wo_ref.shape[0]
    for c0, w in _ffn_chunks(ff):
        a = _dot(h2, wab_ref[:, c0:c0 + w])
        b = _dot(h2, wab_ref[:, ff + c0:ff + c0 + w])
        act = (a * _sigmoid(a) * b).astype(BF16)
        acc = acc + _dot(act, wo_ref[c0:c0 + w, :])
    o_ref[...] = x1 + gate2 * acc


def _out(l, xs, mods, ona, omla, omla_ctx, hf, hb, mlo, prm, tabs, dims, with_ctx):
    D = xs.shape[1]
    B, S, CTX = dims
    T = xs.shape[0] if with_ctx else B * S
    nt = T // ROW_TILE
    lat_tiles = S // ROW_TILE
    n_lat = B * lat_tiles
    ff = prm["w_fo"].shape[1]

    def grp(i):
        return jnp.minimum(i // lat_tiles, B)

    row = lambda w: pl.BlockSpec((ROW_TILE, w), lambda i: (i, 0))
    lay = lambda shape: pl.BlockSpec((None,) + shape, lambda i: (l,) + (0,) * len(shape))
    big = lambda shape: pl.BlockSpec((None,) + shape, lambda i: (l,) + (0,) * len(shape),
                                     pipeline_mode=pl.Buffered(1))
    return pl.pallas_call(
        functools.partial(_out_kernel, n_lat=n_lat),
        grid=(nt,),
        in_specs=[
            row(D),
            pl.BlockSpec((None, None, 6, D), lambda i: (l, grp(i), 0, 0)),
            row(NA_W),
            pl.BlockSpec((ROW_TILE, MLA_HEADS * MLA_V), lambda i: (jnp.minimum(i, n_lat - 1), 0)),
            pl.BlockSpec((ROW_TILE, MLA_HEADS * MLA_V), lambda i: (jnp.maximum(i - n_lat, 0), 0)),
            row(ML_W), row(ML_W), row(ML_W),
            lay((1, ML_W)),
            pl.BlockSpec((MXU_DIM, MXU_DIM), lambda i: (0, 0)),
            big((NA_W + ML_W + MLA_HEADS * MLA_V, D)),
            lay((1, D)),
            big((D, 2 * ff)), big((ff, D)),
        ],
        out_specs=row(D),
        out_shape=jax.ShapeDtypeStruct((T, D), F32),
        compiler_params=_cparams(("arbitrary",)),
        name="out_ffn",
    )(xs, mods, ona, omla, omla if omla_ctx is None else omla_ctx, hf, hb, mlo, prm["head_gain"], tabs["bd64"],
      prm["w_out"], prm["g_ffn"], prm["w_ab"], prm["w_fo"])


def _block_diag_ones(group):
    idx = np.arange(MXU_DIM) // group
    return jnp.asarray((idx[:, None] == idx[None, :]).astype(np.float32), dtype=BF16)


def _rope_tables(S, n_ctx_rows):
    n_freq = MLA_ROPE // 4
    inv = ROPE_BASE ** (-jnp.arange(n_freq, dtype=F32) / n_freq)
    t = jnp.arange(S, dtype=jnp.int32)
    row = (t // GRID_W).astype(F32)
    col = (t % GRID_W).astype(F32)
    ang = jnp.concatenate([row[:, None] * inv, col[:, None] * inv], axis=-1)
    ang = jnp.concatenate([ang, jnp.zeros((n_ctx_rows, MLA_ROPE // 2), F32)], axis=0)
    cos, sin = jnp.cos(ang), jnp.sin(ang)
    n = S + n_ctx_rows
    half = MLA_ROPE // 2
    ones = jnp.ones((n, MLA_NOPE), F32)
    z = lambda w: jnp.zeros((n, w), F32)
    tail = LANES - MLA_NOPE - MLA_ROPE
    del half
    cos_t = jnp.concatenate([ones, cos, cos, jnp.ones((n, tail), F32)], axis=-1)
    sin_t = jnp.concatenate([z(MLA_NOPE), -sin, sin, z(tail)], axis=-1)
    return cos_t, sin_t


def _na_bias_kernel(rpb_ref, o_ref, *, plan):
    li = pl.program_id(0)
    h = pl.program_id(1)
    n_dc = 2 * NA_WIN_C - 1
    ck = lax.broadcasted_iota(jnp.int32, (GRID_W, LANES), 0)
    lane = lax.broadcasted_iota(jnp.int32, (GRID_W, LANES), 1)
    cq = lane & (GRID_W - 1)
    dc = ck - cq + (NA_WIN_C - 1)
    cs = jnp.clip(cq - NA_WIN_C // 2, 0, GRID_W - NA_WIN_C)
    col_ok = (ck >= cs) & (ck < cs + NA_WIN_C)
    hits = [dc == j for j in range(n_dc)]
    neg = jnp.full((GRID_W, LANES), NEG, F32)
    tiles = []
    for dr in range(2 * NA_WIN_R - 1):
        t = neg
        for j in range(n_dc):
            t = jnp.where(hits[j], rpb_ref[li, h, dr * n_dc + j] * LOG2E, t)
        tiles.append(jnp.where(col_ok, t, neg))
    low = lane < GRID_W
    for v, rows in enumerate(plan):
        for rk, drs in enumerate(rows):
            pick = [neg if d is None else tiles[d] for d in drs]
            groups = [jnp.where(low, pick[2 * g], pick[2 * g + 1]) for g in range(len(drs) // 2)]
            o_ref[v, rk * GRID_W:(rk + 1) * GRID_W, :] = jnp.concatenate(groups, axis=-1).astype(o_ref.dtype)
    o_ref[len(plan)] = jnp.full(o_ref.shape[1:], NEG, o_ref.dtype)


def _na_bias(rpb, rows):
    depth, H = rpb.shape[:2]
    qr, kr_n = ATT_BLOCK // GRID_W, NA_KEY_ROWS // GRID_W
    nb = rows // qr
    plan = []
    for j in (0, 1, nb - 1):
        ks = int(np.clip(j - 1, 0, nb - 3)) * qr
        per_key_row = []
        for kr in range(ks, ks + kr_n):
            drs = []
            for r in range(j * qr, (j + 1) * qr):
                rs = int(np.clip(r - NA_WIN_R // 2, 0, rows - NA_WIN_R))
                drs.append(kr - r + NA_WIN_R - 1 if rs <= kr < rs + NA_WIN_R else None)
            per_key_row.append(tuple(drs))
        plan.append(tuple(per_key_row))
    n_rel = (2 * NA_WIN_R - 1) * (2 * NA_WIN_C - 1)
    return pl.pallas_call(
        functools.partial(_na_bias_kernel, plan=tuple(plan)),
        grid=(depth, H),
        in_specs=[pl.BlockSpec(memory_space=pltpu.SMEM)],
        out_specs=pl.BlockSpec((None, len(plan) + 1, None, NA_KEY_ROWS, ATT_BLOCK),
                               lambda li, h: (li, 0, h, 0, 0)),
        out_shape=jax.ShapeDtypeStruct((depth, len(plan) + 1, H, NA_KEY_ROWS, ATT_BLOCK), BF16),
        compiler_params=_cparams(("arbitrary", "arbitrary")),
        name="na_bias",
    )(rpb.reshape(depth, H, n_rel))


def _prepare(w_in, w_uq, w_ukv, w_out, w_ffn_in, w_ffn_out, g_mix, g_ffn, b_gate, na_qk_gain,
             ml_conv_w, ml_conv_b, ml_head_gain, mla_gq, mla_gkv, mla_qk_gain):
    depth, D, _ = w_in.shape
    o_gate = C_MLO + ML_W
    o_ql = o_gate + 4 * ML_HEADS
    o_kr = o_ql + Q_LORA + KV_LORA
    half = MLA_ROPE // 2

    def layout_kernel(w_ref, o_ref):
        w = w_ref[...]
        zc = lambda n: jnp.zeros((w.shape[0], n), w.dtype)
        o_ref[...] = jnp.concatenate([
            w[:, :o_gate], w[:, o_ql:o_kr],
            w[:, o_gate:o_ql], zc(MISC_ROPE_LANE - 4 * ML_HEADS),
            w[:, o_kr:], zc(LANES - MISC_ROPE_LANE - MLA_ROPE),
            zc(MISC_ROPE_LANE), w[:, o_kr + half:], w[:, o_kr:o_kr + half],
            zc(LANES - MISC_ROPE_LANE - MLA_ROPE)], axis=-1).astype(BF16)

    rows = MXU_DIM
    w_in_p = pl.pallas_call(
        layout_kernel,
        grid=(depth, D // rows),
        in_specs=[pl.BlockSpec((None, rows, w_in.shape[2]), lambda li, i: (li, i, 0))],
        out_specs=pl.BlockSpec((None, rows, D_IN_PAD), lambda li, i: (li, i, 0)),
        out_shape=jax.ShapeDtypeStruct((depth, D, D_IN_PAD), BF16),
        compiler_params=_cparams(("arbitrary", "arbitrary")),
        name="w_in_layout",
    )(w_in)
    pad_h = lambda a, w: jnp.pad(a, [(0, 0)] * (a.ndim - 1) + [(0, LANES - w)])

    def partner(a):
        lo, hi = a[..., MLA_NOPE:MLA_NOPE + half], a[..., MLA_NOPE + half:MLA_QK]
        return jnp.concatenate([jnp.zeros_like(a[..., :MLA_NOPE]), hi, lo,
                                jnp.zeros_like(a[..., MLA_QK:])], axis=-1)

    w_uq_h = pad_h(w_uq.reshape(depth, Q_LORA, MLA_HEADS, MLA_QK), MLA_QK)
    w_uq_p = w_uq_h.reshape(depth, Q_LORA, MLA_HEADS * LANES).astype(BF16)
    w_uq_s = partner(w_uq_h).reshape(depth, Q_LORA, MLA_HEADS * LANES).astype(BF16)
    ukv = w_ukv.reshape(depth, KV_LORA, MLA_HEADS, MLA_NOPE + MLA_V)
    w_uk_p = pad_h(ukv[..., :MLA_NOPE], MLA_NOPE).reshape(depth, KV_LORA, MLA_HEADS * LANES).astype(BF16)
    w_uv = ukv[..., MLA_NOPE:].reshape(depth, KV_LORA, MLA_HEADS * MLA_V).astype(BF16)
    mla_g = pad_h(mla_qk_gain, MLA_QK)
    return {
        "w_in": w_in_p, "w_uq": w_uq_p, "w_uqs": w_uq_s, "w_uk": w_uk_p, "w_uv": w_uv,
        "w_out": w_out.astype(BF16),
        "w_ab": w_ffn_in.astype(BF16),
        "w_fo": w_ffn_out.astype(BF16),
        "g_mix": g_mix[:, None, :], "g_ffn": g_ffn[:, None, :],
        "gq": mla_gq[:, None, :], "gkv": mla_gkv[:, None, :],
        "na_gq": jnp.tile(na_qk_gain[:, 0:1, :], (1, 1, NA_HEADS)) * (NA_SCALE * LOG2E),
        "na_gk": jnp.tile(na_qk_gain[:, 1:2, :], (1, 1, NA_HEADS)),
        "mla_gq": jnp.tile(mla_g[:, 0:1, :], (1, 1, MLA_HEADS)) * (MLA_SCALE * LOG2E),
        "mla_gk": jnp.tile(mla_g[:, 1:2, :], (1, 1, MLA_HEADS)),
        "mla_gqs": jnp.tile(partner(mla_g[:, 0:1, :]), (1, 1, MLA_HEADS)) * (MLA_SCALE * LOG2E),
        "mla_gks": jnp.tile(partner(mla_g[:, 1:2, :]), (1, 1, MLA_HEADS)),
        "head_gain": ml_head_gain.reshape(depth, 1, ML_W),
        "b_gate": pad_h(b_gate, 4 * ML_HEADS)[:, None, :],
        "conv_w": jnp.pad(ml_conv_w, ((0, 0), (0, 8 - ml_conv_w.shape[1]), (0, 0))),
        "conv_b": ml_conv_b[:, None, :],
    }


def kernel(x, c, ctx, c_ctx, w_mod, b_mod, g_mix, g_ffn, w_in, b_gate, na_qk_gain, na_rpb,
           ml_conv_w, ml_conv_b, ml_head_gain, mla_gq, mla_gkv, w_uq, w_ukv, mla_qk_gain,
           w_out, w_ffn_in, w_ffn_out):
    B, S, D = x.shape
    CTX = ctx.shape[1]
    depth = w_in.shape[0]
    assert CTX == ATT_BLOCK and S % ROW_TILE == 0 and (B * CTX) % ROW_TILE == 0
    assert CTX % (ML_STEP_CHUNKS * ML_CHUNK) == 0 and S % (ML_STEP_CHUNKS * ML_CHUNK) == 0
    assert S % (2 * FLASH_TK) == 0 and S % (FLASH_TILES * FLASH_TQ) == 0 and S // ATT_BLOCK >= 3 and B + 1 <= 8
    dims = (B, S, CTX)

    prm = _prepare(w_in, w_uq, w_ukv, w_out, w_ffn_in, w_ffn_out, g_mix, g_ffn, b_gate, na_qk_gain,
                   ml_conv_w, ml_conv_b, ml_head_gain, mla_gq, mla_gkv, mla_qk_gain)
    cos_t, sin_t = _rope_tables(S, B * CTX)
    tri = np.tril(np.ones((ML_CHUNK, ML_CHUNK), np.float32))
    tabs = {
        "bd64": _block_diag_ones(HEAD_DIM), "bd128": _block_diag_ones(LANES),
        "cos": cos_t, "sin": sin_t,
        "tril": jnp.asarray(tri, dtype=BF16), "triu": jnp.asarray(tri.T, dtype=BF16),
    }
    bias = _na_bias(na_rpb, S // GRID_W)

    cond8 = jnp.concatenate([c, c_ctx[None, :], jnp.zeros((8 - B - 1, D), F32)], axis=0)
    mods = _modulation(cond8, w_mod, b_mod)
    mods = mods.reshape(depth, 8, 6, D)

    xs = jnp.concatenate([x.reshape(B * S, D), ctx.reshape(B * CTX, D)], axis=0)
    for l in range(depth):
        with_ctx = l < depth - 1
        naq, nak, mlqk, mlv, mlo, misc, mq, mk, navt, mvt = _proj(l, xs, mods, prm, tabs, dims)
        ona = _na(l, naq, nak, navt, bias, dims, with_ctx)
        omla, omla_ctx = _flash(mq, mk, mvt, dims, with_ctx)
        hf, hb = _mlstm(l, mlqk, mlv, misc, prm, tabs, dims)
        xs = _out(l, xs, mods, ona, omla, omla_ctx, hf, hb, mlo, prm, tabs, dims, with_ctx)
    return xs.reshape(B, S, D)
```

```python
import functools

import numpy as np
import jax
import jax.numpy as jnp
from jax import lax
from jax.experimental import pallas as pl
from jax.experimental.pallas import tpu as pltpu

GRID_W = 64
HEAD_DIM = 64
NA_HEADS = 4
NA_WIN_R = 8
NA_WIN_C = 16
ML_HEADS = 4
ML_CHUNK = 128
ML_STEP_CHUNKS = 2
MLA_HEADS = 8
MLA_NOPE = 64
MLA_ROPE = 32
MLA_V = 64
Q_LORA = 384
KV_LORA = 256
ROPE_BASE = 10000.0
EPS = 1e-6
NA_W = NA_HEADS * HEAD_DIM
ML_W = ML_HEADS * HEAD_DIM
MLA_QK = MLA_NOPE + MLA_ROPE
NA_SCALE = HEAD_DIM ** -0.5
MLA_SCALE = MLA_QK ** -0.5
LOG2E = 1.4426950408889634

LANES = 128
MXU_DIM = 256
VMEM_LIMIT = 56 * 1024 * 1024

ROW_TILE = 512
ATT_BLOCK = 256
NA_KEY_ROWS = 768
FLASH_TK = 512
FLASH_TQ = 1024
FLASH_TILES = 2
FLASH_SLOTS = 2
FLASH_UNROLL = 4
NEG = -1e30
DEN_ROWS = 16

C_NAQ, C_NAK, C_NAV = 0, 256, 512
C_MLQK, C_MLV, C_MLO = 768, 1280, 1536
C_QL = 1792
C_KVL = C_QL + Q_LORA
C_MISC = C_KVL + KV_LORA
C_MISC2 = C_MISC + LANES
D_IN_PAD = C_MISC2 + LANES
MISC_ROPE_LANE = 64

BF16 = jnp.bfloat16
F32 = jnp.float32


def _cparams(sem):
    return pltpu.CompilerParams(dimension_semantics=sem, vmem_limit_bytes=VMEM_LIMIT)


def _dot(a, b):
    return jnp.dot(a, b, preferred_element_type=F32)


def _dot_nt(a, b):
    return lax.dot_general(a, b, (((1,), (1,)), ((), ())), preferred_element_type=F32)


def _split2(x):
    hi = x.astype(BF16)
    lo = (x - hi.astype(F32)).astype(BF16)
    return hi, lo


def _group_sumsq(x, bd):
    x2 = x * x
    hi, lo = _split2(x2)
    outs = []
    for c in range(x.shape[1] // MXU_DIM):
        sl = slice(c * MXU_DIM, (c + 1) * MXU_DIM)
        outs.append(_dot(hi[:, sl], bd) + _dot(lo[:, sl], bd))
    return outs[0] if len(outs) == 1 else jnp.concatenate(outs, axis=-1)


def _rms_rows(x):
    return x * lax.rsqrt(jnp.mean(x * x, axis=-1, keepdims=True) + EPS)


def _sigmoid(x):
    return 1.0 / (1.0 + jnp.exp(-x))


def _mod_kernel(c_ref, w_ref, b_ref, o_ref):
    c = c_ref[...]
    a = c * _sigmoid(c)
    o_ref[...] = jnp.dot(a, w_ref[...], preferred_element_type=F32,
                         precision=lax.Precision.HIGHEST) + b_ref[...]


def _modulation(cond8, w_mod, b_mod):
    depth, d, d6 = w_mod.shape
    bw = 2 * d
    return pl.pallas_call(
        _mod_kernel,
        grid=(depth, d6 // bw),
        in_specs=[
            pl.BlockSpec((8, d), lambda l, j: (0, 0)),
            pl.BlockSpec((None, d, bw), lambda l, j: (l, 0, j)),
            pl.BlockSpec((None, 1, bw), lambda l, j: (l, 0, j)),
        ],
        out_specs=pl.BlockSpec((None, 8, bw), lambda l, j: (l, 0, j)),
        out_shape=jax.ShapeDtypeStruct((depth, 8, d6), F32),
        compiler_params=_cparams(("arbitrary", "arbitrary")),
        name="modulation",
    )(cond8, w_mod, b_mod.reshape(depth, 1, d6))


def _proj_kernel(x_ref, xc_ref, mod_ref, g_ref, w_in_ref, w_uq_ref, w_uqs_ref, w_uk_ref, w_uv_ref,
                 gq_ref, gkv_ref, na_gq_ref, na_gk_ref, mla_gq_ref, mla_gk_ref, mla_gqs_ref, mla_gks_ref,
                 bd64_ref, bd128_ref, cos_ref, sin_ref,
                 naq_ref, nak_ref, mlqk_ref, mlv_ref, mlo_ref, misc_ref,
                 mq_ref, mk_ref, nav_ref, mv_ref, *, n_lat_split):
    x = _stream_tile(x_ref, xc_ref, n_lat_split)
    shift = mod_ref[0:1, :]
    scale = mod_ref[1:2, :]
    h = (_rms_rows(x) * g_ref[...]) * (1.0 + scale) + shift
    hb = h.astype(BF16)

    def proj(c0, width):
        return _dot(hb, w_in_ref[:, c0:c0 + width])

    bd64 = bd64_ref[...]
    bd128 = bd128_ref[...]

    lat = proj(C_QL, D_IN_PAD - C_QL)
    ql = lat[:, :Q_LORA]
    kvl = lat[:, C_KVL - C_QL:C_MISC - C_QL]
    misc = lat[:, C_MISC - C_QL:C_MISC2 - C_QL]
    kx = lat[:, C_MISC2 - C_QL:]
    misc_ref[...] = misc
    qn = (_rms_rows(ql) * gq_ref[...]).astype(BF16)
    kvn = (_rms_rows(kvl) * gkv_ref[...]).astype(BF16)

    na = proj(C_NAQ, 3 * NA_W)
    pq = na[:, :NA_W]
    pk = na[:, NA_W:2 * NA_W]
    nav_ref[...] = na[:, 2 * NA_W:].T.astype(BF16)

    qr = _dot(qn, w_uq_ref[...])
    qx = _dot(qn, w_uqs_ref[...])
    lane = lax.broadcasted_iota(jnp.int32, misc.shape, 1)
    krope = jnp.where((lane >= MISC_ROPE_LANE) & (lane < MISC_ROPE_LANE + MLA_ROPE), misc, 0.0)
    kr = _dot(kvn, w_uk_ref[...]) + jnp.concatenate([krope] * MLA_HEADS, axis=-1)
    mv_ref[...] = _dot(kvn, w_uv_ref[...]).T.astype(BF16)

    ml = proj(C_MLQK, 4 * ML_W)
    mlqk_ref[...] = ml[:, :2 * ML_W]
    mlv_ref[...] = ml[:, 2 * ML_W:3 * ML_W].astype(BF16)
    mlo_ref[...] = ml[:, 3 * ML_W:]

    naq_ref[...] = (pq * lax.rsqrt(_group_sumsq(pq, bd64) * (1.0 / HEAD_DIM) + EPS)
                    * na_gq_ref[...]).astype(BF16)
    nak_ref[...] = (pk * lax.rsqrt(_group_sumsq(pk, bd64) * (1.0 / HEAD_DIM) + EPS)
                    * na_gk_ref[...]).astype(BF16)

    cos = cos_ref[...]
    sin = sin_ref[...]
    rq = lax.rsqrt(_group_sumsq(qr, bd128) * (1.0 / MLA_QK) + EPS)
    rk = lax.rsqrt(_group_sumsq(kr, bd128) * (1.0 / MLA_QK) + EPS)
    kxs = kx * sin
    for g in range(MLA_HEADS):
        sl = slice(g * LANES, (g + 1) * LANES)
        mq_ref[:, sl] = (rq[:, sl] * (qr[:, sl] * mla_gq_ref[:, sl] * cos
                                      + qx[:, sl] * mla_gqs_ref[:, sl] * sin)).astype(BF16)
        mk_ref[:, sl] = (rk[:, sl] * (kr[:, sl] * mla_gk_ref[:, sl] * cos
                                      + kxs * mla_gks_ref[:, sl])).astype(BF16)


def _stream_tile(x_ref, xc_ref, n_lat_split):
    if n_lat_split is None:
        return x_ref[...]
    return jnp.where(pl.program_id(0) < n_lat_split, x_ref[...], xc_ref[...])


def _stream_specs(xs, n_lat, D):
    if isinstance(xs, tuple):
        specs = [pl.BlockSpec((ROW_TILE, D), lambda i: (jnp.minimum(i, n_lat - 1), 0)),
                 pl.BlockSpec((ROW_TILE, D), lambda i: (jnp.maximum(i - n_lat, 0), 0))]
        return xs, specs, n_lat
    specs = [pl.BlockSpec((ROW_TILE, D), lambda i: (i, 0)), pl.BlockSpec((ROW_TILE, D), lambda i: (0, 0))]
    return (xs, xs), specs, None


def _proj(l, xs, mods, prm, tabs, dims):
    B, S, CTX = dims
    T, D = B * (S + CTX), mods.shape[-1]
    nt = T // ROW_TILE
    lat_tiles = S // ROW_TILE
    x_in, x_specs, n_lat_split = _stream_specs(xs, B * lat_tiles, D)

    def grp(i):
        return jnp.minimum(i // lat_tiles, B)

    def tab_row(i):
        return jnp.where(i < B * lat_tiles, i % lat_tiles, lat_tiles + (i - B * lat_tiles))

    row = lambda w: pl.BlockSpec((ROW_TILE, w), lambda i: (i, 0))
    lay = lambda shape: pl.BlockSpec((None,) + shape, lambda i: (l,) + (0,) * len(shape))
    const = lambda shape: pl.BlockSpec(shape, lambda i: (0,) * len(shape))
    tab = pl.BlockSpec((ROW_TILE, LANES), lambda i: (tab_row(i), 0))
    out_w = [(NA_W, BF16), (NA_W, BF16), (2 * ML_W, F32), (ML_W, BF16), (ML_W, F32),
             (LANES, F32), (MLA_HEADS * LANES, BF16), (MLA_HEADS * LANES, BF16)]
    out_t = [NA_W, MLA_HEADS * MLA_V]
    return pl.pallas_call(
        functools.partial(_proj_kernel, n_lat_split=n_lat_split),
        grid=(nt,),
        in_specs=x_specs + [
            pl.BlockSpec((None, None, 6, D), lambda i: (l, grp(i), 0, 0)),
            lay((1, D)),
            lay((D, D_IN_PAD)), lay((Q_LORA, MLA_HEADS * LANES)), lay((Q_LORA, MLA_HEADS * LANES)),
            lay((KV_LORA, MLA_HEADS * LANES)), lay((KV_LORA, MLA_HEADS * MLA_V)),
            lay((1, Q_LORA)), lay((1, KV_LORA)), lay((1, NA_W)), lay((1, NA_W)),
            lay((1, MLA_HEADS * LANES)), lay((1, MLA_HEADS * LANES)),
            lay((1, MLA_HEADS * LANES)), lay((1, MLA_HEADS * LANES)),
            const((MXU_DIM, MXU_DIM)), const((MXU_DIM, MXU_DIM)),
            tab, tab,
        ],
        out_specs=[row(w) for w, _ in out_w]
        + [pl.BlockSpec((w, ROW_TILE), lambda i: (0, i)) for w in out_t],
        out_shape=[jax.ShapeDtypeStruct((T, w), dt) for w, dt in out_w]
        + [jax.ShapeDtypeStruct((w, T), BF16) for w in out_t],
        compiler_params=_cparams(("arbitrary",)),
        name="proj",
    )(*x_in, mods, prm["g_mix"], prm["w_in"], prm["w_uq"], prm["w_uqs"], prm["w_uk"], prm["w_uv"],
      prm["gq"], prm["gkv"], prm["na_gq"], prm["na_gk"],
      prm["mla_gq"], prm["mla_gk"], prm["mla_gqs"], prm["mla_gks"],
      tabs["bd64"], tabs["bd128"], tabs["cos"], tabs["sin"])


def _short_conv(x, prev_blk, next_blk, first, last, w_ref, b_ref):
    n = x.shape[0]
    prev_row = jnp.where(first, 0.0, prev_blk[7:8, :])
    next_row = jnp.where(last, 0.0, next_blk[0:1, :])
    ridx = lax.broadcasted_iota(jnp.int32, x.shape, 0)
    xm1 = jnp.where(ridx == 0, prev_row, pltpu.roll(x, 1, 0))
    xp1 = jnp.where(ridx == n - 1, next_row, pltpu.roll(x, n - 1, 0))
    y = b_ref[...] + xm1 * w_ref[0:1, :] + x * w_ref[1:2, :] + xp1 * w_ref[2:3, :]
    y = y * _sigmoid(y)
    return y[:, :ML_W].astype(BF16), (y[:, ML_W:] * (HEAD_DIM ** -0.5)).astype(BF16)


def _na_kernel(q_ref, k_ref, vt_ref, kc_ref, vct_ref, bias_ref, o_ref, *, nb):
    j = pl.program_id(1)
    start = pl.multiple_of(jnp.clip(j - 1, 0, nb - 3) * ATT_BLOCK, ATT_BLOCK)
    q = q_ref[...]
    lane = lax.broadcasted_iota(jnp.int32, (ATT_BLOCK, LANES), 1)
    scores = []
    for h in range(NA_HEADS):
        p, e = divmod(h, 2)
        cols = slice(p * LANES, (p + 1) * LANES)
        qp = q[:, cols]
        keep = (lane < HEAD_DIM) if e == 0 else (lane >= HEAD_DIM)
        qt = jnp.where(keep, qp, jnp.zeros_like(qp)).astype(F32).T.astype(BF16)
        s_loc = _dot(k_ref[pl.ds(start, NA_KEY_ROWS), cols], qt) + bias_ref[h].astype(F32)
        s_ctx = _dot(kc_ref[:, cols], qt)
        scores.append((s_loc, s_ctx))
    probs = []
    for s_loc, s_ctx in scores:
        m = jnp.maximum(jnp.max(s_loc, axis=0, keepdims=True), jnp.max(s_ctx, axis=0, keepdims=True))
        probs.append((jnp.exp2(s_loc - m).astype(BF16), jnp.exp2(s_ctx - m).astype(BF16)))
    outs = []
    for h, (p_loc, p_ctx) in enumerate(probs):
        rows = slice(h * HEAD_DIM, (h + 1) * HEAD_DIM)
        ot = (_dot(_flash_values(vt_ref[rows, pl.ds(start, NA_KEY_ROWS)]), p_loc)
              + _dot(_flash_values(vct_ref[rows, :]), p_ctx))
        outs.append(ot[:HEAD_DIM, :] / ot[HEAD_DIM:HEAD_DIM + 1, :])
    o_ref[...] = jnp.concatenate(outs, axis=0).T.astype(o_ref.dtype)


def _na(l, naq, nak, navt, bias, dims, with_ctx):
    B, S, CTX = dims
    nb = S // ATT_BLOCK
    ctx_blk0 = B * S // CTX
    n_steps = nb + 1 if with_ctx else nb
    rows_out = naq.shape[0] if with_ctx else B * S

    def qrow(b, j):
        return jnp.where(j < nb, b * nb + j, ctx_blk0 + b)

    def variant(j):
        return jnp.where(j == 0, 0, jnp.where(j == nb - 1, 2, jnp.where(j == nb, 3, 1)))

    kern = functools.partial(_na_kernel, nb=nb)
    return pl.pallas_call(
        kern,
        grid=(B, n_steps),
        in_specs=[
            pl.BlockSpec((ATT_BLOCK, NA_W), lambda b, j: (qrow(b, j), 0)),
            pl.BlockSpec((S, NA_W), lambda b, j: (b, 0)),
            pl.BlockSpec((NA_W, S), lambda b, j: (0, b)),
            pl.BlockSpec((CTX, NA_W), lambda b, j: (ctx_blk0 + b, 0)),
            pl.BlockSpec((NA_W, CTX), lambda b, j: (0, ctx_blk0 + b)),
            pl.BlockSpec((None, None, NA_HEADS, NA_KEY_ROWS, ATT_BLOCK), lambda b, j: (l, variant(j), 0, 0, 0)),
        ],
        out_specs=pl.BlockSpec((ATT_BLOCK, NA_W), lambda b, j: (qrow(b, j), 0)),
        out_shape=jax.ShapeDtypeStruct((rows_out, NA_W), BF16),
        compiler_params=_cparams(("arbitrary", "arbitrary")),
        name="na",
    )(naq, nak, navt, nak, navt, bias)


def _flash_values(v):
    return jnp.concatenate([v, jnp.ones((DEN_ROWS, v.shape[1]), BF16)], axis=0)


def _flash_queries(q):
    return [q[:, e * LANES:(e + 1) * LANES].astype(F32).T.astype(BF16) for e in range(2)]


def _flash_ctx_scores(qs, kc_ref):
    return [_dot(kc_ref[:, e * LANES:(e + 1) * LANES], qs[e]) for e in range(2)]


def _flash_ctx_init(sts, vc_ref, acc_ref, st_ref):
    ps = []
    for e, st in enumerate(sts):
        m = jnp.max(st, axis=0, keepdims=True)
        st_ref[e] = m
        ps.append(jnp.exp2(st - m).astype(BF16))
    for e, p in enumerate(ps):
        acc_ref[e] = _dot(_flash_values(vc_ref[e * HEAD_DIM:(e + 1) * HEAD_DIM, :]), p)


def _flash_finish(acc_ref, o_ref, rows=slice(None)):
    ot = jnp.concatenate([acc_ref[e, :HEAD_DIM, :] / acc_ref[e, HEAD_DIM:HEAD_DIM + 1, :]
                          for e in range(2)], axis=0)
    o_ref[rows, :] = ot.T.astype(o_ref.dtype)


def _flash_kernel(q_ref, km_ref, vm_ref, kc_ref, vc_ref, o_ref, *scratch, n_main):
    ns = FLASH_SLOTS
    s_refs = [scratch[2 * i:2 * i + 2] for i in range(ns)]
    mx_refs = [scratch[2 * ns + 2 * i:2 * ns + 2 * i + 2] for i in range(ns)]
    tk = FLASH_TK
    tq = FLASH_TQ
    for t in range(FLASH_TILES):
        acc_ref, st_ref = scratch[4 * ns + 2 * t:4 * ns + 2 * t + 2]
        rows = slice(t * tq, (t + 1) * tq)
        _flash_tile(_flash_queries(q_ref[rows, :]), km_ref, vm_ref, kc_ref, vc_ref, o_ref, rows,
                    s_refs, mx_refs, acc_ref, st_ref, n_main)


def _flash_tile(qs, km_ref, vm_ref, kc_ref, vc_ref, o_ref, rows, s_refs, mx_refs, acc_ref, st_ref, n_main):
    ns = FLASH_SLOTS
    tk = FLASH_TK

    def scores1(slot, r0, e):
        st = _dot(km_ref[pl.ds(r0, tk), e * LANES:(e + 1) * LANES], qs[e])
        s_refs[slot][e][...] = st
        mx_refs[slot][e][...] = jnp.max(st, axis=0, keepdims=True)

    def absorb1(slot, r0, e):
        m = st_ref[e]
        m_new = jnp.maximum(m, mx_refs[slot][e][...])
        st_ref[e] = m_new
        alpha = jnp.exp2(m - m_new)
        p = jnp.exp2(s_refs[slot][e][...] - m_new).astype(BF16)
        v = vm_ref[e * HEAD_DIM:(e + 1) * HEAD_DIM, pl.ds(r0, tk)]
        acc_ref[e] = alpha * acc_ref[e] + _dot(_flash_values(v), p)

    def scores(slot, r0):
        for e in range(2):
            scores1(slot, r0, e)

    def absorb(slot, r0):
        for e in range(2):
            absorb1(slot, r0, e)

    ctx_scores = _flash_ctx_scores(qs, kc_ref)
    scores(0, 0)
    _flash_ctx_init(ctx_scores, vc_ref, acc_ref, st_ref)

    def advance(c, r0):
        scores1((c + 1) % ns, r0 + tk, 0)
        absorb1(c % ns, r0, 1)
        scores1((c + 1) % ns, r0 + tk, 1)
        absorb1(c % ns, r0, 0)

    nu = FLASH_UNROLL

    def body(i, carry):
        r0 = pl.multiple_of(i * (nu * tk), nu * tk)
        for c in range(nu):
            advance(c, r0 + c * tk)
        return carry

    n_loop = (n_main - 1) // nu
    lax.fori_loop(0, n_loop, body, 0)
    for c in range(n_loop * nu, n_main - 1):
        advance(c, c * tk)
    absorb((n_main - 1) % ns, (n_main - 1) * tk)
    _flash_finish(acc_ref, o_ref, rows)


def _flash_ctx_kernel(q_ref, kc_ref, vc_ref, o_ref, acc_ref, st_ref):
    _flash_ctx_init(_flash_ctx_scores(_flash_queries(q_ref[...]), kc_ref), vc_ref, acc_ref, st_ref)
    _flash_finish(acc_ref, o_ref)


def _flash(mq, mk, mvt, dims, with_ctx):
    B, S, CTX = dims
    tq = FLASH_TQ
    blk = FLASH_TILES * tq
    nq = S // blk
    ctx_blk0 = B * S // CTX
    npair = MLA_HEADS // 2
    acc = lambda n: [pltpu.VMEM((2, HEAD_DIM + DEN_ROWS, n), F32), pltpu.VMEM((2, 1, n), F32)]
    kc_spec = pl.BlockSpec((CTX, 2 * LANES), lambda b, p, *_: (ctx_blk0 + b, p))
    vc_spec = pl.BlockSpec((LANES, CTX), lambda b, p, *_: (p, ctx_blk0 + b))

    o = pl.pallas_call(
        functools.partial(_flash_kernel, n_main=S // FLASH_TK),
        grid=(B, npair, nq),
        in_specs=[
            pl.BlockSpec((blk, 2 * LANES), lambda b, p, j: (b * nq + j, p)),
            pl.BlockSpec((S, 2 * LANES), lambda b, p, j: (b, p)),
            pl.BlockSpec((LANES, S), lambda b, p, j: (p, b)),
            kc_spec, vc_spec,
        ],
        out_specs=pl.BlockSpec((blk, LANES), lambda b, p, j: (b * nq + j, p)),
        out_shape=jax.ShapeDtypeStruct((B * S, MLA_HEADS * MLA_V), BF16),
        scratch_shapes=[pltpu.VMEM((FLASH_TK, tq), F32)] * (2 * FLASH_SLOTS)
        + [pltpu.VMEM((1, tq), F32)] * (2 * FLASH_SLOTS) + acc(tq) * FLASH_TILES,
        compiler_params=_cparams(("arbitrary", "arbitrary", "arbitrary")),
        name="flash",
    )(mq, mk, mvt, mk, mvt)
    if not with_ctx:
        return o, None
    o_ctx = pl.pallas_call(
        _flash_ctx_kernel,
        grid=(B, npair),
        in_specs=[
            pl.BlockSpec((CTX, 2 * LANES), lambda b, p: (ctx_blk0 + b, p)),
            kc_spec, vc_spec,
        ],
        out_specs=pl.BlockSpec((CTX, LANES), lambda b, p: (b, p)),
        out_shape=jax.ShapeDtypeStruct((B * CTX, MLA_HEADS * MLA_V), BF16),
        scratch_shapes=acc(CTX),
        compiler_params=_cparams(("arbitrary", "arbitrary")),
        name="flash_ctx",
    )(mq, mk, mvt)
    return o, o_ctx


def _mlstm_gates(d, gates, bgate, tri):
    gb = gates + bgate
    ls = jnp.minimum(gb, 0.0) - jnp.log1p(jnp.exp(-jnp.abs(gb)))
    lane = lax.broadcasted_iota(jnp.int32, gb.shape, 1)
    is_f = ((lane >= 4) & (lane < 8)) | ((lane >= 12) & (lane < 16))
    xg = jnp.where(lane < 16, jnp.where(is_f, ls, gb), 0.0)
    x1 = xg.astype(BF16)
    r1 = xg - x1.astype(F32)
    x2 = r1.astype(BF16)
    x3 = (r1 - x2.astype(F32)).astype(BF16)
    cum = _dot(tri, x1) + _dot(tri, x2) + _dot(tri, x3)
    return xg, cum, xg.T, cum.T, jnp.sum(xg, axis=0, keepdims=True)


def _mlstm_kernel(xf_ref, xpf_ref, xnf_ref, vf_ref, gf_ref, xb_ref, xpb_ref, xnb_ref, vb_ref, gb_ref,
                  cw_ref, cb_ref, bgate_ref, tril_ref, triu_ref, hf_ref, hb_ref, c_ref, m_ref,
                  *, nb_ctx, nb_lat, nchk):
    @pl.when(pl.program_id(1) == 0)
    def _():
        c_ref[...] = jnp.zeros_like(c_ref)
        m_ref[...] = jnp.zeros_like(m_ref)

    L = ML_CHUNK
    bgate = bgate_ref[...]
    step = pl.program_id(1)
    in_ctx = step < nb_ctx
    pos = jnp.where(in_ctx, step, step - nb_ctx)
    starts = pos == 0
    ends = pos == jnp.where(in_ctx, nb_ctx, nb_lat) - 1
    qk_f = _short_conv(xf_ref[...], xpf_ref[...], xnf_ref[...], starts, ends, cw_ref, cb_ref)
    qk_b = _short_conv(xb_ref[...], xpb_ref[...], xnb_ref[...], ends, starts, cw_ref, cb_ref)
    q = (qk_f[0], qk_b[0])
    k = (qk_f[1], qk_b[1])
    v = (vf_ref[...], vb_ref[...])
    g_refs = (gf_ref, gb_ref)
    tris = (tril_ref[...], triu_ref[...])
    order = (list(range(nchk)), list(range(nchk - 1, -1, -1)))
    si = lax.broadcasted_iota(jnp.int32, (L, L), 0)
    ti = lax.broadcasted_iota(jnp.int32, (L, L), 1)
    valid = (si <= ti, si >= ti)
    lane_l = lax.broadcasted_iota(jnp.int32, (L, LANES), 1)
    row_l = lax.broadcasted_iota(jnp.int32, (LANES, L), 0)
    rows_of = lambda j: slice(j * L, (j + 1) * L)

    items = {}
    for d in range(2):
        for j in range(nchk):
            xg, cum, xg_t, cum_t, tot = _mlstm_gates(d, g_refs[d][rows_of(j), :], bgate, tris[d])
            for p in range(ML_HEADS // 2):
                cols = slice(p * LANES, (p + 1) * LANES)
                qp = q[d][rows_of(j), cols]
                kp = k[d][rows_of(j), cols]
                vt = v[d][rows_of(j), cols].astype(F32).T
                for e in range(2):
                    hd = 2 * p + e
                    ci = 8 * d + hd
                    cf = 8 * d + 4 + hd
                    in_head = (lane_l < HEAD_DIM) if e == 0 else (lane_l >= HEAD_DIM)
                    in_rows = (row_l < HEAD_DIM) if e == 0 else (row_l >= HEAD_DIM)
                    ones_row = HEAD_DIM if e == 0 else 0
                    b_row = cum_t[cf:cf + 1, :]
                    c_col = xg[:, ci:ci + 1] - cum[:, cf:cf + 1]
                    dm = jnp.where(valid[d], b_row + c_col, NEG)
                    b_tot = tot[:, cf:cf + 1]
                    a_row = b_tot - b_row + xg_t[ci:ci + 1, :]
                    items[d, j, hd] = dict(
                        qp=qp, km=jnp.where(in_head, kp, jnp.zeros_like(kp)),
                        vat=jnp.where(in_rows, vt, jnp.where(row_l == ones_row, 1.0, 0.0)),
                        ones_row=ones_row, b_row=b_row, b_tot=b_tot, dm=dm, a_row=a_row,
                        a_t=jnp.max(dm, axis=0, keepdims=True), a_max=jnp.max(a_row, axis=1, keepdims=True))

    for c in items.values():
        c["qk"] = (_dot_nt(c["km"], c["qp"]) * jnp.exp(c["dm"] - c["a_t"])).astype(BF16)

    for c in items.values():
        c["intra"] = _dot(c["vat"].astype(BF16), c["qk"])
        c["upd"] = _dot((c["vat"] * jnp.exp(c["a_row"] - c["a_max"])).astype(BF16), c["km"])

    state = {(d, hd): (c_ref[d, hd], m_ref[d * ML_HEADS + hd:d * ML_HEADS + hd + 1, 0:1])
             for d in range(2) for hd in range(ML_HEADS)}
    hs = {}
    for n in range(nchk):
        for d in range(2):
            j = order[d][n]
            for hd in range(ML_HEADS):
                c = items[d, j, hd]
                cst, m_prev = state[d, hd]
                g = c["b_row"] + m_prev
                mt = jnp.maximum(g, c["a_t"])
                haug = (jnp.exp(g - mt) * _dot_nt(cst.astype(BF16), c["qp"])
                        + jnp.exp(c["a_t"] - mt) * c["intra"])
                den = haug[c["ones_row"]:c["ones_row"] + 1, :]
                hs[d, j, hd] = haug / jnp.maximum(jnp.abs(den), jnp.exp(-mt))
                m_new = jnp.maximum(c["b_tot"] + m_prev, c["a_max"])
                cst = (jnp.exp(c["b_tot"] + m_prev - m_new) * cst
                       + jnp.exp(c["a_max"] - m_new) * c["upd"])
                state[d, hd] = (cst, m_new)
    for (d, hd), (cst, m_new) in state.items():
        c_ref[d, hd] = cst
        r = d * ML_HEADS + hd
        m_ref[r:r + 1, :] = jnp.broadcast_to(m_new, (1, LANES))

    for d, h_ref in enumerate((hf_ref, hb_ref)):
        for j in range(nchk):
            for p in range(ML_HEADS // 2):
                pair = jnp.where(row_l < HEAD_DIM, hs[d, j, 2 * p], hs[d, j, 2 * p + 1])
                h_ref[rows_of(j), p * LANES:(p + 1) * LANES] = pair.T


def _mlstm(l, mlqk, mlv, misc, prm, tabs, dims):
    T, W = mlqk.shape
    B, S, CTX = dims
    nchk = ML_STEP_CHUNKS
    blk = nchk * ML_CHUNK
    nb_ctx = CTX // blk
    nb_lat = S // blk
    ctx0 = B * S // blk

    def fwd(b, c):
        return jnp.where(c < nb_ctx, ctx0 + b * nb_ctx + c, b * nb_lat + (c - nb_ctx))

    def bwd(b, c):
        return jnp.where(c < nb_ctx, ctx0 + b * nb_ctx + (nb_ctx - 1 - c),
                         b * nb_lat + (nb_lat - 1 - (c - nb_ctx)))

    sub = blk // 8
    last8 = T // 8 - 1

    def specs(fn):
        return [pl.BlockSpec((blk, W), lambda b, c: (fn(b, c), 0)),
                pl.BlockSpec((8, W), lambda b, c: (jnp.maximum(fn(b, c) * sub - 1, 0), 0)),
                pl.BlockSpec((8, W), lambda b, c: (jnp.minimum((fn(b, c) + 1) * sub, last8), 0)),
                pl.BlockSpec((blk, ML_W), lambda b, c: (fn(b, c), 0)),
                pl.BlockSpec((blk, LANES), lambda b, c: (fn(b, c), 0))]

    return pl.pallas_call(
        functools.partial(_mlstm_kernel, nb_ctx=nb_ctx, nb_lat=nb_lat, nchk=nchk),
        grid=(B, nb_ctx + nb_lat),
        in_specs=specs(fwd) + specs(bwd) + [
            pl.BlockSpec((None, 8, W), lambda b, c: (l, 0, 0)),
            pl.BlockSpec((None, 1, W), lambda b, c: (l, 0, 0)),
            pl.BlockSpec((None, 1, LANES), lambda b, c: (l, 0, 0)),
            pl.BlockSpec((ML_CHUNK, ML_CHUNK), lambda b, c: (0, 0)),
            pl.BlockSpec((ML_CHUNK, ML_CHUNK), lambda b, c: (0, 0)),
        ],
        out_specs=[pl.BlockSpec((blk, ML_W), lambda b, c: (fwd(b, c), 0)),
                   pl.BlockSpec((blk, ML_W), lambda b, c: (bwd(b, c), 0))],
        out_shape=[jax.ShapeDtypeStruct((T, ML_W), F32)] * 2,
        scratch_shapes=[pltpu.VMEM((2, ML_HEADS, LANES, LANES), F32), pltpu.VMEM((8, LANES), F32)],
        compiler_params=_cparams(("arbitrary", "arbitrary")),
        name="mlstm",
    )(mlqk, mlqk, mlqk, mlv, misc, mlqk, mlqk, mlqk, mlv, misc,
      prm["conv_w"], prm["conv_b"], prm["b_gate"], tabs["tril"], tabs["triu"])


def _ffn_chunks(ff):
    step = 2 * MXU_DIM
    starts = list(range(0, ff - ff % step, step)) or [0]
    return [(c, (ff - c) if c == starts[-1] else step) for c in starts]


def _out_kernel(x_ref, xc_ref, mod_ref, ona_ref, omla_ref, omlac_ref, hf_ref, hb_ref, mlo_ref, hg_ref,
                bd64_ref, w_out_ref, g_ref, wab_ref, wo_ref, o_ref, *, n_lat, n_lat_split):
    x = _stream_tile(x_ref, xc_ref, n_lat_split)
    omla = jnp.where(pl.program_id(0) < n_lat, omla_ref[...], omlac_ref[...])
    gate1 = mod_ref[2:3, :]
    shift2 = mod_ref[3:4, :]
    scale2 = mod_ref[4:5, :]
    gate2 = mod_ref[5:6, :]

    h = hf_ref[...] + hb_ref[...]
    hn = h * lax.rsqrt(_group_sumsq(h, bd64_ref[...]) * (1.0 / HEAD_DIM) + EPS) * hg_ref[...]
    ob = (hn * _sigmoid(mlo_ref[...])).astype(BF16)
    mix = _dot(jnp.concatenate([ona_ref[...], ob, omla], axis=-1), w_out_ref[...])
    x1 = x + gate1 * mix

    h2 = ((_rms_rows(x1) * g_ref[...]) * (1.0 + scale2) + shift2).astype(BF16)
    acc = jnp.zeros_like(x1)
    ff = wo_ref.shape[0]
    for c0, w in _ffn_chunks(ff):
        a = _dot(h2, wab_ref[:, c0:c0 + w])
        b = _dot(h2, wab_ref[:, ff + c0:ff + c0 + w])
        act = (a * _sigmoid(a) * b).astype(BF16)
        acc = acc + _dot(act, wo_ref[c0:c0 + w, :])
    o_ref[...] = x1 + gate2 * acc


def _out(l, xs, mods, ona, omla, omla_ctx, hf, hb, mlo, prm, tabs, dims, with_ctx):
    D = mods.shape[-1]
    B, S, CTX = dims
    T = B * (S + CTX) if with_ctx else B * S
    nt = T // ROW_TILE
    lat_tiles = S // ROW_TILE
    n_lat = B * lat_tiles
    x_in, x_specs, n_lat_split = _stream_specs(xs, n_lat, D)
    ff = prm["w_fo"].shape[1]

    def grp(i):
        return jnp.minimum(i // lat_tiles, B)

    row = lambda w: pl.BlockSpec((ROW_TILE, w), lambda i: (i, 0))
    lay = lambda shape: pl.BlockSpec((None,) + shape, lambda i: (l,) + (0,) * len(shape))
    big = lambda shape: pl.BlockSpec((None,) + shape, lambda i: (l,) + (0,) * len(shape),
                                     pipeline_mode=pl.Buffered(1))
    return pl.pallas_call(
        functools.partial(_out_kernel, n_lat=n_lat, n_lat_split=n_lat_split),
        grid=(nt,),
        in_specs=x_specs + [
            pl.BlockSpec((None, None, 6, D), lambda i: (l, grp(i), 0, 0)),
            row(NA_W),
            pl.BlockSpec((ROW_TILE, MLA_HEADS * MLA_V), lambda i: (jnp.minimum(i, n_lat - 1), 0)),
            pl.BlockSpec((ROW_TILE, MLA_HEADS * MLA_V), lambda i: (jnp.maximum(i - n_lat, 0), 0)),
            row(ML_W), row(ML_W), row(ML_W),
            lay((1, ML_W)),
            pl.BlockSpec((MXU_DIM, MXU_DIM), lambda i: (0, 0)),
            big((NA_W + ML_W + MLA_HEADS * MLA_V, D)),
            lay((1, D)),
            big((D, 2 * ff)), big((ff, D)),
        ],
        out_specs=row(D),
        out_shape=jax.ShapeDtypeStruct((T, D), F32),
        compiler_params=_cparams(("arbitrary",)),
        name="out_ffn",
    )(*x_in, mods, ona, omla, omla if omla_ctx is None else omla_ctx, hf, hb, mlo, prm["head_gain"], tabs["bd64"],
      prm["w_out"], prm["g_ffn"], prm["w_ab"], prm["w_fo"])


def _block_diag_ones(group):
    idx = np.arange(MXU_DIM) // group
    return jnp.asarray((idx[:, None] == idx[None, :]).astype(np.float32), dtype=BF16)


def _rope_tables(S, n_ctx_rows):
    n_freq = MLA_ROPE // 4
    inv = ROPE_BASE ** (-jnp.arange(n_freq, dtype=F32) / n_freq)
    t = jnp.arange(S, dtype=jnp.int32)
    row = (t // GRID_W).astype(F32)
    col = (t % GRID_W).astype(F32)
    ang = jnp.concatenate([row[:, None] * inv, col[:, None] * inv], axis=-1)
    ang = jnp.concatenate([ang, jnp.zeros((n_ctx_rows, MLA_ROPE // 2), F32)], axis=0)
    cos, sin = jnp.cos(ang), jnp.sin(ang)
    n = S + n_ctx_rows
    half = MLA_ROPE // 2
    ones = jnp.ones((n, MLA_NOPE), F32)
    z = lambda w: jnp.zeros((n, w), F32)
    tail = LANES - MLA_NOPE - MLA_ROPE
    del half
    cos_t = jnp.concatenate([ones, cos, cos, jnp.ones((n, tail), F32)], axis=-1)
    sin_t = jnp.concatenate([z(MLA_NOPE), -sin, sin, z(tail)], axis=-1)
    return cos_t, sin_t


def _na_bias_kernel(rpb_ref, o_ref, *, plan):
    li = pl.program_id(0)
    h = pl.program_id(1)
    n_dc = 2 * NA_WIN_C - 1
    ck = lax.broadcasted_iota(jnp.int32, (GRID_W, LANES), 0)
    lane = lax.broadcasted_iota(jnp.int32, (GRID_W, LANES), 1)
    cq = lane & (GRID_W - 1)
    dc = ck - cq + (NA_WIN_C - 1)
    cs = jnp.clip(cq - NA_WIN_C // 2, 0, GRID_W - NA_WIN_C)
    col_ok = (ck >= cs) & (ck < cs + NA_WIN_C)
    hits = [dc == j for j in range(n_dc)]
    neg = jnp.full((GRID_W, LANES), NEG, F32)
    tiles = []
    for dr in range(2 * NA_WIN_R - 1):
        t = neg
        for j in range(n_dc):
            t = jnp.where(hits[j], rpb_ref[li, h, dr * n_dc + j] * LOG2E, t)
        tiles.append(jnp.where(col_ok, t, neg))
    low = lane < GRID_W
    for v, rows in enumerate(plan):
        for rk, drs in enumerate(rows):
            pick = [neg if d is None else tiles[d] for d in drs]
            groups = [jnp.where(low, pick[2 * g], pick[2 * g + 1]) for g in range(len(drs) // 2)]
            o_ref[v, rk * GRID_W:(rk + 1) * GRID_W, :] = jnp.concatenate(groups, axis=-1).astype(o_ref.dtype)
    o_ref[len(plan)] = jnp.full(o_ref.shape[1:], NEG, o_ref.dtype)


def _na_bias(rpb, rows):
    depth, H = rpb.shape[:2]
    qr, kr_n = ATT_BLOCK // GRID_W, NA_KEY_ROWS // GRID_W
    nb = rows // qr
    plan = []
    for j in (0, 1, nb - 1):
        ks = int(np.clip(j - 1, 0, nb - 3)) * qr
        per_key_row = []
        for kr in range(ks, ks + kr_n):
            drs = []
            for r in range(j * qr, (j + 1) * qr):
                rs = int(np.clip(r - NA_WIN_R // 2, 0, rows - NA_WIN_R))
                drs.append(kr - r + NA_WIN_R - 1 if rs <= kr < rs + NA_WIN_R else None)
            per_key_row.append(tuple(drs))
        plan.append(tuple(per_key_row))
    n_rel = (2 * NA_WIN_R - 1) * (2 * NA_WIN_C - 1)
    return pl.pallas_call(
        functools.partial(_na_bias_kernel, plan=tuple(plan)),
        grid=(depth, H),
        in_specs=[pl.BlockSpec(memory_space=pltpu.SMEM)],
        out_specs=pl.BlockSpec((None, len(plan) + 1, None, NA_KEY_ROWS, ATT_BLOCK),
                               lambda li, h: (li, 0, h, 0, 0)),
        out_shape=jax.ShapeDtypeStruct((depth, len(plan) + 1, H, NA_KEY_ROWS, ATT_BLOCK), BF16),
        compiler_params=_cparams(("arbitrary", "arbitrary")),
        name="na_bias",
    )(rpb.reshape(depth, H, n_rel))


def _prepare(w_in, w_uq, w_ukv, w_out, w_ffn_in, w_ffn_out, g_mix, g_ffn, b_gate, na_qk_gain,
             ml_conv_w, ml_conv_b, ml_head_gain, mla_gq, mla_gkv, mla_qk_gain):
    depth, D, _ = w_in.shape
    o_gate = C_MLO + ML_W
    o_ql = o_gate + 4 * ML_HEADS
    o_kr = o_ql + Q_LORA + KV_LORA
    half = MLA_ROPE // 2

    def layout_kernel(w_ref, o_ref):
        w = w_ref[...]
        zc = lambda n: jnp.zeros((w.shape[0], n), w.dtype)
        o_ref[...] = jnp.concatenate([
            w[:, :o_gate], w[:, o_ql:o_kr],
            w[:, o_gate:o_ql], zc(MISC_ROPE_LANE - 4 * ML_HEADS),
            w[:, o_kr:], zc(LANES - MISC_ROPE_LANE - MLA_ROPE),
            zc(MISC_ROPE_LANE), w[:, o_kr + half:], w[:, o_kr:o_kr + half],
            zc(LANES - MISC_ROPE_LANE - MLA_ROPE)], axis=-1).astype(BF16)

    rows = MXU_DIM
    w_in_p = pl.pallas_call(
        layout_kernel,
        grid=(depth, D // rows),
        in_specs=[pl.BlockSpec((None, rows, w_in.shape[2]), lambda li, i: (li, i, 0))],
        out_specs=pl.BlockSpec((None, rows, D_IN_PAD), lambda li, i: (li, i, 0)),
        out_shape=jax.ShapeDtypeStruct((depth, D, D_IN_PAD), BF16),
        compiler_params=_cparams(("arbitrary", "arbitrary")),
        name="w_in_layout",
    )(w_in)
    pad_h = lambda a, w: jnp.pad(a, [(0, 0)] * (a.ndim - 1) + [(0, LANES - w)])

    def partner(a):
        lo, hi = a[..., MLA_NOPE:MLA_NOPE + half], a[..., MLA_NOPE + half:MLA_QK]
        return jnp.concatenate([jnp.zeros_like(a[..., :MLA_NOPE]), hi, lo,
                                jnp.zeros_like(a[..., MLA_QK:])], axis=-1)

    w_uq_h = pad_h(w_uq.reshape(depth, Q_LORA, MLA_HEADS, MLA_QK), MLA_QK)
    w_uq_p = w_uq_h.reshape(depth, Q_LORA, MLA_HEADS * LANES).astype(BF16)
    w_uq_s = partner(w_uq_h).reshape(depth, Q_LORA, MLA_HEADS * LANES).astype(BF16)
    ukv = w_ukv.reshape(depth, KV_LORA, MLA_HEADS, MLA_NOPE + MLA_V)
    w_uk_p = pad_h(ukv[..., :MLA_NOPE], MLA_NOPE).reshape(depth, KV_LORA, MLA_HEADS * LANES).astype(BF16)
    w_uv = ukv[..., MLA_NOPE:].reshape(depth, KV_LORA, MLA_HEADS * MLA_V).astype(BF16)
    mla_g = pad_h(mla_qk_gain, MLA_QK)
    return {
        "w_in": w_in_p, "w_uq": w_uq_p, "w_uqs": w_uq_s, "w_uk": w_uk_p, "w_uv": w_uv,
        "w_out": w_out.astype(BF16),
        "w_ab": w_ffn_in.astype(BF16),
        "w_fo": w_ffn_out.astype(BF16),
        "g_mix": g_mix[:, None, :], "g_ffn": g_ffn[:, None, :],
        "gq": mla_gq[:, None, :], "gkv": mla_gkv[:, None, :],
        "na_gq": jnp.tile(na_qk_gain[:, 0:1, :], (1, 1, NA_HEADS)) * (NA_SCALE * LOG2E),
        "na_gk": jnp.tile(na_qk_gain[:, 1:2, :], (1, 1, NA_HEADS)),
        "mla_gq": jnp.tile(mla_g[:, 0:1, :], (1, 1, MLA_HEADS)) * (MLA_SCALE * LOG2E),
        "mla_gk": jnp.tile(mla_g[:, 1:2, :], (1, 1, MLA_HEADS)),
        "mla_gqs": jnp.tile(partner(mla_g[:, 0:1, :]), (1, 1, MLA_HEADS)) * (MLA_SCALE * LOG2E),
        "mla_gks": jnp.tile(partner(mla_g[:, 1:2, :]), (1, 1, MLA_HEADS)),
        "head_gain": ml_head_gain.reshape(depth, 1, ML_W),
        "b_gate": pad_h(b_gate, 4 * ML_HEADS)[:, None, :],
        "conv_w": jnp.pad(ml_conv_w, ((0, 0), (0, 8 - ml_conv_w.shape[1]), (0, 0))),
        "conv_b": ml_conv_b[:, None, :],
    }


def kernel(x, c, ctx, c_ctx, w_mod, b_mod, g_mix, g_ffn, w_in, b_gate, na_qk_gain, na_rpb,
           ml_conv_w, ml_conv_b, ml_head_gain, mla_gq, mla_gkv, w_uq, w_ukv, mla_qk_gain,
           w_out, w_ffn_in, w_ffn_out):
    B, S, D = x.shape
    CTX = ctx.shape[1]
    depth = w_in.shape[0]
    assert CTX == ATT_BLOCK and S % ROW_TILE == 0 and (B * CTX) % ROW_TILE == 0
    assert CTX % (ML_STEP_CHUNKS * ML_CHUNK) == 0 and S % (ML_STEP_CHUNKS * ML_CHUNK) == 0
    assert S % (2 * FLASH_TK) == 0 and S % (FLASH_TILES * FLASH_TQ) == 0 and S // ATT_BLOCK >= 3 and B + 1 <= 8
    dims = (B, S, CTX)

    prm = _prepare(w_in, w_uq, w_ukv, w_out, w_ffn_in, w_ffn_out, g_mix, g_ffn, b_gate, na_qk_gain,
                   ml_conv_w, ml_conv_b, ml_head_gain, mla_gq, mla_gkv, mla_qk_gain)
    cos_t, sin_t = _rope_tables(S, B * CTX)
    tri = np.tril(np.ones((ML_CHUNK, ML_CHUNK), np.float32))
    tabs = {
        "bd64": _block_diag_ones(HEAD_DIM), "bd128": _block_diag_ones(LANES),
        "cos": cos_t, "sin": sin_t,
        "tril": jnp.asarray(tri, dtype=BF16), "triu": jnp.asarray(tri.T, dtype=BF16),
    }
    bias = _na_bias(na_rpb, S // GRID_W)

    cond8 = jnp.concatenate([c, c_ctx[None, :], jnp.zeros((8 - B - 1, D), F32)], axis=0)
    mods = _modulation(cond8, w_mod, b_mod)
    mods = mods.reshape(depth, 8, 6, D)

    xs = (x.reshape(B * S, D), ctx.reshape(B * CTX, D))
    for l in range(depth):
        with_ctx = l < depth - 1
        naq, nak, mlqk, mlv, mlo, misc, mq, mk, navt, mvt = _proj(l, xs, mods, prm, tabs, dims)
        ona = _na(l, naq, nak, navt, bias, dims, with_ctx)
        omla, omla_ctx = _flash(mq, mk, mvt, dims, with_ctx)
        hf, hb = _mlstm(l, mlqk, mlv, misc, prm, tabs, dims)
        xs = _out(l, xs, mods, ona, omla, omla_ctx, hf, hb, mlo, prm, tabs, dims, with_ctx)
    return xs.reshape(B, S, D)
```

```python
import functools

import numpy as np
import jax
import jax.numpy as jnp
from jax import lax
from jax.experimental import pallas as pl
from jax.experimental.pallas import tpu as pltpu

GRID_W = 64
HEAD_DIM = 64
NA_HEADS = 4
NA_WIN_R = 8
NA_WIN_C = 16
ML_HEADS = 4
ML_CHUNK = 128
ML_STEP_CHUNKS = 2
MLA_HEADS = 8
MLA_NOPE = 64
MLA_ROPE = 32
MLA_V = 64
Q_LORA = 384
KV_LORA = 256
ROPE_BASE = 10000.0
EPS = 1e-6
NA_W = NA_HEADS * HEAD_DIM
ML_W = ML_HEADS * HEAD_DIM
MLA_QK = MLA_NOPE + MLA_ROPE
NA_SCALE = HEAD_DIM ** -0.5
MLA_SCALE = MLA_QK ** -0.5
LOG2E = 1.4426950408889634

LANES = 128
MXU_DIM = 256
VMEM_LIMIT = 56 * 1024 * 1024

ROW_TILE = 512
ATT_BLOCK = 256
NA_KEY_ROWS = 768
FLASH_TK = 512
FLASH_TQ = 1024
FLASH_TILES = 2
FLASH_SLOTS = 2
FLASH_UNROLL = 4
NEG = -1e30
DEN_ROWS = 16

C_NAQ, C_NAK, C_NAV = 0, 256, 512
C_MLQK, C_MLV, C_MLO = 768, 1280, 1536
C_QL = 1792
C_KVL = C_QL + Q_LORA
C_MISC = C_KVL + KV_LORA
C_MISC2 = C_MISC + LANES
D_IN_PAD = C_MISC2 + LANES
MISC_ROPE_LANE = 64

BF16 = jnp.bfloat16
F32 = jnp.float32


def _cparams(sem):
    return pltpu.CompilerParams(dimension_semantics=sem, vmem_limit_bytes=VMEM_LIMIT)


def _dot(a, b):
    return jnp.dot(a, b, preferred_element_type=F32)


def _dot_nt(a, b):
    return lax.dot_general(a, b, (((1,), (1,)), ((), ())), preferred_element_type=F32)


def _split2(x):
    hi = x.astype(BF16)
    lo = (x - hi.astype(F32)).astype(BF16)
    return hi, lo


def _group_sumsq(x, bd):
    x2 = x * x
    hi, lo = _split2(x2)
    outs = []
    for c in range(x.shape[1] // MXU_DIM):
        sl = slice(c * MXU_DIM, (c + 1) * MXU_DIM)
        outs.append(_dot(hi[:, sl], bd) + _dot(lo[:, sl], bd))
    return outs[0] if len(outs) == 1 else jnp.concatenate(outs, axis=-1)


def _rms_rows(x):
    return x * lax.rsqrt(jnp.mean(x * x, axis=-1, keepdims=True) + EPS)


def _sigmoid(x):
    return 1.0 / (1.0 + jnp.exp(-x))


def _mod_kernel(c_ref, w_ref, b_ref, o_ref):
    c = c_ref[...]
    a = c * _sigmoid(c)
    o_ref[...] = jnp.dot(a, w_ref[...], preferred_element_type=F32,
                         precision=lax.Precision.HIGHEST) + b_ref[...]


def _modulation(cond8, w_mod, b_mod):
    depth, d, d6 = w_mod.shape
    bw = 2 * d
    return pl.pallas_call(
        _mod_kernel,
        grid=(depth, d6 // bw),
        in_specs=[
            pl.BlockSpec((8, d), lambda l, j: (0, 0)),
            pl.BlockSpec((None, d, bw), lambda l, j: (l, 0, j)),
            pl.BlockSpec((None, 1, bw), lambda l, j: (l, 0, j)),
        ],
        out_specs=pl.BlockSpec((None, 8, bw), lambda l, j: (l, 0, j)),
        out_shape=jax.ShapeDtypeStruct((depth, 8, d6), F32),
        compiler_params=_cparams(("arbitrary", "arbitrary")),
        name="modulation",
    )(cond8, w_mod, b_mod.reshape(depth, 1, d6))


def _proj_kernel(x_ref, xc_ref, mod_ref, g_ref, w_in_ref, w_uq_ref, w_uqs_ref, w_uk_ref, w_uv_ref,
                 gq_ref, gkv_ref, na_gq_ref, na_gk_ref, mla_gq_ref, mla_gk_ref, mla_gqs_ref, mla_gks_ref,
                 bd64_ref, bd128_ref, cos_ref, sin_ref,
                 naq_ref, nak_ref, mlqk_ref, mlv_ref, mlo_ref, misc_ref,
                 mq_ref, mk_ref, nav_ref, mv_ref, *, n_lat_split):
    x = _stream_tile(x_ref, xc_ref, n_lat_split)
    shift = mod_ref[0:1, :]
    scale = mod_ref[1:2, :]
    h = (_rms_rows(x) * g_ref[...]) * (1.0 + scale) + shift
    hb = h.astype(BF16)

    def proj(c0, width):
        return _dot(hb, w_in_ref[:, c0:c0 + width])

    bd64 = bd64_ref[...]
    bd128 = bd128_ref[...]

    lat = proj(C_QL, D_IN_PAD - C_QL)
    ql = lat[:, :Q_LORA]
    kvl = lat[:, C_KVL - C_QL:C_MISC - C_QL]
    misc = lat[:, C_MISC - C_QL:C_MISC2 - C_QL]
    kx = lat[:, C_MISC2 - C_QL:]
    misc_ref[...] = misc
    qn = (_rms_rows(ql) * gq_ref[...]).astype(BF16)
    kvn = (_rms_rows(kvl) * gkv_ref[...]).astype(BF16)

    na = proj(C_NAQ, 3 * NA_W)
    pq = na[:, :NA_W]
    pk = na[:, NA_W:2 * NA_W]
    nav_ref[...] = na[:, 2 * NA_W:].T.astype(BF16)

    qr = _dot(qn, w_uq_ref[...])
    qx = _dot(qn, w_uqs_ref[...])
    lane = lax.broadcasted_iota(jnp.int32, misc.shape, 1)
    krope = jnp.where((lane >= MISC_ROPE_LANE) & (lane < MISC_ROPE_LANE + MLA_ROPE), misc, 0.0)
    kr = _dot(kvn, w_uk_ref[...]) + jnp.concatenate([krope] * MLA_HEADS, axis=-1)
    mv_ref[...] = _dot(kvn, w_uv_ref[...]).T.astype(BF16)

    ml = proj(C_MLQK, 4 * ML_W)
    mlqk_ref[...] = ml[:, :2 * ML_W]
    mlv_ref[...] = ml[:, 2 * ML_W:3 * ML_W].astype(BF16)
    mlo_ref[...] = ml[:, 3 * ML_W:]

    naq_ref[...] = (pq * lax.rsqrt(_group_sumsq(pq, bd64) * (1.0 / HEAD_DIM) + EPS)
                    * na_gq_ref[...]).astype(BF16)
    nak_ref[...] = (pk * lax.rsqrt(_group_sumsq(pk, bd64) * (1.0 / HEAD_DIM) + EPS)
                    * na_gk_ref[...]).astype(BF16)

    cos = cos_ref[...]
    sin = sin_ref[...]
    rq = lax.rsqrt(_group_sumsq(qr, bd128) * (1.0 / MLA_QK) + EPS)
    rk = lax.rsqrt(_group_sumsq(kr, bd128) * (1.0 / MLA_QK) + EPS)
    kxs = kx * sin
    for g in range(MLA_HEADS):
        sl = slice(g * LANES, (g + 1) * LANES)
        mq_ref[:, sl] = (rq[:, sl] * (qr[:, sl] * mla_gq_ref[:, sl] * cos
                                      + qx[:, sl] * mla_gqs_ref[:, sl] * sin)).astype(BF16)
        mk_ref[:, sl] = (rk[:, sl] * (kr[:, sl] * mla_gk_ref[:, sl] * cos
                                      + kxs * mla_gks_ref[:, sl])).astype(BF16)


def _stream_tile(x_ref, xc_ref, n_lat_split):
    if n_lat_split is None:
        return x_ref[...]
    return jnp.where(pl.program_id(0) < n_lat_split, x_ref[...], xc_ref[...])


def _stream_specs(xs, n_lat, D):
    if isinstance(xs, tuple):
        specs = [pl.BlockSpec((ROW_TILE, D), lambda i: (jnp.minimum(i, n_lat - 1), 0)),
                 pl.BlockSpec((ROW_TILE, D), lambda i: (jnp.maximum(i - n_lat, 0), 0))]
        return xs, specs, n_lat
    specs = [pl.BlockSpec((ROW_TILE, D), lambda i: (i, 0)), pl.BlockSpec((ROW_TILE, D), lambda i: (0, 0))]
    return (xs, xs), specs, None


def _proj(l, xs, mods, prm, tabs, dims):
    B, S, CTX = dims
    T, D = B * (S + CTX), mods.shape[-1]
    nt = T // ROW_TILE
    lat_tiles = S // ROW_TILE
    x_in, x_specs, n_lat_split = _stream_specs(xs, B * lat_tiles, D)

    def grp(i):
        return jnp.minimum(i // lat_tiles, B)

    def tab_row(i):
        return jnp.where(i < B * lat_tiles, i % lat_tiles, lat_tiles + (i - B * lat_tiles))

    row = lambda w: pl.BlockSpec((ROW_TILE, w), lambda i: (i, 0))
    lay = lambda shape: pl.BlockSpec((None,) + shape, lambda i: (l,) + (0,) * len(shape))
    const = lambda shape: pl.BlockSpec(shape, lambda i: (0,) * len(shape))
    tab = pl.BlockSpec((ROW_TILE, LANES), lambda i: (tab_row(i), 0))
    out_w = [(NA_W, BF16), (NA_W, BF16), (2 * ML_W, F32), (ML_W, BF16), (ML_W, F32),
             (LANES, F32), (MLA_HEADS * LANES, BF16), (MLA_HEADS * LANES, BF16)]
    out_t = [NA_W, MLA_HEADS * MLA_V]
    return pl.pallas_call(
        functools.partial(_proj_kernel, n_lat_split=n_lat_split),
        grid=(nt,),
        in_specs=x_specs + [
            pl.BlockSpec((None, None, 6, D), lambda i: (l, grp(i), 0, 0)),
            lay((1, D)),
            lay((D, D_IN_PAD)), lay((Q_LORA, MLA_HEADS * LANES)), lay((Q_LORA, MLA_HEADS * LANES)),
            lay((KV_LORA, MLA_HEADS * LANES)), lay((KV_LORA, MLA_HEADS * MLA_V)),
            lay((1, Q_LORA)), lay((1, KV_LORA)), lay((1, NA_W)), lay((1, NA_W)),
            lay((1, MLA_HEADS * LANES)), lay((1, MLA_HEADS * LANES)),
            lay((1, MLA_HEADS * LANES)), lay((1, MLA_HEADS * LANES)),
            const((MXU_DIM, MXU_DIM)), const((MXU_DIM, MXU_DIM)),
            tab, tab,
        ],
        out_specs=[row(w) for w, _ in out_w]
        + [pl.BlockSpec((w, ROW_TILE), lambda i: (0, i)) for w in out_t],
        out_shape=[jax.ShapeDtypeStruct((T, w), dt) for w, dt in out_w]
        + [jax.ShapeDtypeStruct((w, T), BF16) for w in out_t],
        compiler_params=_cparams(("arbitrary",)),
        name="proj",
    )(*x_in, mods, prm["g_mix"], prm["w_in"], prm["w_uq"], prm["w_uqs"], prm["w_uk"], prm["w_uv"],
      prm["gq"], prm["gkv"], prm["na_gq"], prm["na_gk"],
      prm["mla_gq"], prm["mla_gk"], prm["mla_gqs"], prm["mla_gks"],
      tabs["bd64"], tabs["bd128"], tabs["cos"], tabs["sin"])


def _short_conv(x, prev_blk, next_blk, first, last, w_ref, b_ref):
    n = x.shape[0]
    prev_row = jnp.where(first, 0.0, prev_blk[7:8, :])
    next_row = jnp.where(last, 0.0, next_blk[0:1, :])
    ridx = lax.broadcasted_iota(jnp.int32, x.shape, 0)
    xm1 = jnp.where(ridx == 0, prev_row, pltpu.roll(x, 1, 0))
    xp1 = jnp.where(ridx == n - 1, next_row, pltpu.roll(x, n - 1, 0))
    y = b_ref[...] + xm1 * w_ref[0:1, :] + x * w_ref[1:2, :] + xp1 * w_ref[2:3, :]
    y = y * _sigmoid(y)
    return y[:, :ML_W].astype(BF16), (y[:, ML_W:] * (HEAD_DIM ** -0.5)).astype(BF16)


def _na_kernel(q_ref, k_ref, vt_ref, kc_ref, vct_ref, bias_ref, o_ref, *, nb):
    j = pl.program_id(1)
    start = pl.multiple_of(jnp.clip(j - 1, 0, nb - 3) * ATT_BLOCK, ATT_BLOCK)
    q = q_ref[...]
    lane = lax.broadcasted_iota(jnp.int32, (ATT_BLOCK, LANES), 1)
    scores = []
    for h in range(NA_HEADS):
        p, e = divmod(h, 2)
        cols = slice(p * LANES, (p + 1) * LANES)
        qp = q[:, cols]
        keep = (lane < HEAD_DIM) if e == 0 else (lane >= HEAD_DIM)
        qt = jnp.where(keep, qp, jnp.zeros_like(qp)).astype(F32).T.astype(BF16)
        s_loc = _dot(k_ref[pl.ds(start, NA_KEY_ROWS), cols], qt) + bias_ref[h].astype(F32)
        s_ctx = _dot(kc_ref[:, cols], qt)
        scores.append((s_loc, s_ctx))
    probs = []
    for s_loc, s_ctx in scores:
        m = jnp.maximum(jnp.max(s_loc, axis=0, keepdims=True), jnp.max(s_ctx, axis=0, keepdims=True))
        probs.append((jnp.exp2(s_loc - m).astype(BF16), jnp.exp2(s_ctx - m).astype(BF16)))
    outs = []
    for h, (p_loc, p_ctx) in enumerate(probs):
        rows = slice(h * HEAD_DIM, (h + 1) * HEAD_DIM)
        ot = (_dot(_flash_values(vt_ref[rows, pl.ds(start, NA_KEY_ROWS)]), p_loc)
              + _dot(_flash_values(vct_ref[rows, :]), p_ctx))
        outs.append(ot[:HEAD_DIM, :] / ot[HEAD_DIM:HEAD_DIM + 1, :])
    o_ref[...] = jnp.concatenate(outs, axis=0).T.astype(o_ref.dtype)


def _na(l, naq, nak, navt, bias, dims, with_ctx):
    B, S, CTX = dims
    nb = S // ATT_BLOCK
    ctx_blk0 = B * S // CTX
    n_steps = nb + 1 if with_ctx else nb
    rows_out = naq.shape[0] if with_ctx else B * S

    def qrow(b, j):
        return jnp.where(j < nb, b * nb + j, ctx_blk0 + b)

    def variant(j):
        return jnp.where(j == 0, 0, jnp.where(j == nb - 1, 2, jnp.where(j == nb, 3, 1)))

    kern = functools.partial(_na_kernel, nb=nb)
    return pl.pallas_call(
        kern,
        grid=(B, n_steps),
        in_specs=[
            pl.BlockSpec((ATT_BLOCK, NA_W), lambda b, j: (qrow(b, j), 0)),
            pl.BlockSpec((S, NA_W), lambda b, j: (b, 0)),
            pl.BlockSpec((NA_W, S), lambda b, j: (0, b)),
            pl.BlockSpec((CTX, NA_W), lambda b, j: (ctx_blk0 + b, 0)),
            pl.BlockSpec((NA_W, CTX), lambda b, j: (0, ctx_blk0 + b)),
            pl.BlockSpec((None, None, NA_HEADS, NA_KEY_ROWS, ATT_BLOCK), lambda b, j: (l, variant(j), 0, 0, 0)),
        ],
        out_specs=pl.BlockSpec((ATT_BLOCK, NA_W), lambda b, j: (qrow(b, j), 0)),
        out_shape=jax.ShapeDtypeStruct((rows_out, NA_W), BF16),
        compiler_params=_cparams(("arbitrary", "arbitrary")),
        name="na",
    )(naq, nak, navt, nak, navt, bias)


def _flash_values(v):
    return jnp.concatenate([v, jnp.ones((DEN_ROWS, v.shape[1]), BF16)], axis=0)


def _flash_queries(q):
    return [q[:, e * LANES:(e + 1) * LANES].astype(F32).T.astype(BF16) for e in range(2)]


def _flash_ctx_scores(qs, kc_ref):
    return [_dot(kc_ref[:, e * LANES:(e + 1) * LANES], qs[e]) for e in range(2)]


def _flash_ctx_init(sts, vc_ref, acc_ref, st_ref):
    ps = []
    for e, st in enumerate(sts):
        m = jnp.max(st, axis=0, keepdims=True)
        st_ref[e] = m
        ps.append(jnp.exp2(st - m).astype(BF16))
    for e, p in enumerate(ps):
        acc_ref[e] = _dot(_flash_values(vc_ref[e * HEAD_DIM:(e + 1) * HEAD_DIM, :]), p)


def _flash_finish(acc_ref, o_ref, rows=slice(None)):
    ot = jnp.concatenate([acc_ref[e, :HEAD_DIM, :] / acc_ref[e, HEAD_DIM:HEAD_DIM + 1, :]
                          for e in range(2)], axis=0)
    o_ref[rows, :] = ot.T.astype(o_ref.dtype)


def _flash_kernel(q_ref, km_ref, vm_ref, kc_ref, vc_ref, o_ref, *scratch, n_main):
    ns = FLASH_SLOTS
    s_refs = [scratch[2 * i:2 * i + 2] for i in range(ns)]
    mx_refs = [scratch[2 * ns + 2 * i:2 * ns + 2 * i + 2] for i in range(ns)]
    tk = FLASH_TK
    tq = FLASH_TQ
    for t in range(FLASH_TILES):
        acc_ref, st_ref = scratch[4 * ns + 2 * t:4 * ns + 2 * t + 2]
        rows = slice(t * tq, (t + 1) * tq)
        _flash_tile(_flash_queries(q_ref[rows, :]), km_ref, vm_ref, kc_ref, vc_ref, o_ref, rows,
                    s_refs, mx_refs, acc_ref, st_ref, n_main)


def _flash_tile(qs, km_ref, vm_ref, kc_ref, vc_ref, o_ref, rows, s_refs, mx_refs, acc_ref, st_ref, n_main):
    ns = FLASH_SLOTS
    tk = FLASH_TK

    def scores1(slot, r0, e):
        st = _dot(km_ref[pl.ds(r0, tk), e * LANES:(e + 1) * LANES], qs[e])
        s_refs[slot][e][...] = st
        mx_refs[slot][e][...] = jnp.max(st, axis=0, keepdims=True)

    def absorb1(slot, r0, e):
        m = st_ref[e]
        m_new = jnp.maximum(m, mx_refs[slot][e][...])
        st_ref[e] = m_new
        alpha = jnp.exp2(m - m_new)
        p = jnp.exp2(s_refs[slot][e][...] - m_new).astype(BF16)
        v = vm_ref[e * HEAD_DIM:(e + 1) * HEAD_DIM, pl.ds(r0, tk)]
        acc_ref[e] = alpha * acc_ref[e] + _dot(_flash_values(v), p)

    def scores(slot, r0):
        for e in range(2):
            scores1(slot, r0, e)

    def absorb(slot, r0):
        for e in range(2):
            absorb1(slot, r0, e)

    ctx_scores = _flash_ctx_scores(qs, kc_ref)
    scores(0, 0)
    _flash_ctx_init(ctx_scores, vc_ref, acc_ref, st_ref)

    def advance(c, r0):
        scores1((c + 1) % ns, r0 + tk, 0)
        absorb1(c % ns, r0, 1)
        scores1((c + 1) % ns, r0 + tk, 1)
        absorb1(c % ns, r0, 0)

    nu = FLASH_UNROLL

    def body(i, carry):
        r0 = pl.multiple_of(i * (nu * tk), nu * tk)
        for c in range(nu):
            advance(c, r0 + c * tk)
        return carry

    n_loop = (n_main - 1) // nu
    lax.fori_loop(0, n_loop, body, 0)
    for c in range(n_loop * nu, n_main - 1):
        advance(c, c * tk)
    absorb((n_main - 1) % ns, (n_main - 1) * tk)
    _flash_finish(acc_ref, o_ref, rows)


def _flash_ctx_kernel(q_ref, kc_ref, vc_ref, o_ref, acc_ref, st_ref):
    _flash_ctx_init(_flash_ctx_scores(_flash_queries(q_ref[...]), kc_ref), vc_ref, acc_ref, st_ref)
    _flash_finish(acc_ref, o_ref)


def _flash(mq, mk, mvt, dims, with_ctx):
    B, S, CTX = dims
    tq = FLASH_TQ
    blk = FLASH_TILES * tq
    nq = S // blk
    ctx_blk0 = B * S // CTX
    npair = MLA_HEADS // 2
    acc = lambda n: [pltpu.VMEM((2, HEAD_DIM + DEN_ROWS, n), F32), pltpu.VMEM((2, 1, n), F32)]
    kc_spec = pl.BlockSpec((CTX, 2 * LANES), lambda b, p, *_: (ctx_blk0 + b, p))
    vc_spec = pl.BlockSpec((LANES, CTX), lambda b, p, *_: (p, ctx_blk0 + b))

    o = pl.pallas_call(
        functools.partial(_flash_kernel, n_main=S // FLASH_TK),
        grid=(B, npair, nq),
        in_specs=[
            pl.BlockSpec((blk, 2 * LANES), lambda b, p, j: (b * nq + j, p)),
            pl.BlockSpec((S, 2 * LANES), lambda b, p, j: (b, p)),
            pl.BlockSpec((LANES, S), lambda b, p, j: (p, b)),
            kc_spec, vc_spec,
        ],
        out_specs=pl.BlockSpec((blk, LANES), lambda b, p, j: (b * nq + j, p)),
        out_shape=jax.ShapeDtypeStruct((B * S, MLA_HEADS * MLA_V), BF16),
        scratch_shapes=[pltpu.VMEM((FLASH_TK, tq), F32)] * (2 * FLASH_SLOTS)
        + [pltpu.VMEM((1, tq), F32)] * (2 * FLASH_SLOTS) + acc(tq) * FLASH_TILES,
        compiler_params=_cparams(("arbitrary", "arbitrary", "arbitrary")),
        name="flash",
    )(mq, mk, mvt, mk, mvt)
    if not with_ctx:
        return o, None
    o_ctx = pl.pallas_call(
        _flash_ctx_kernel,
        grid=(B, npair),
        in_specs=[
            pl.BlockSpec((CTX, 2 * LANES), lambda b, p: (ctx_blk0 + b, p)),
            kc_spec, vc_spec,
        ],
        out_specs=pl.BlockSpec((CTX, LANES), lambda b, p: (b, p)),
        out_shape=jax.ShapeDtypeStruct((B * CTX, MLA_HEADS * MLA_V), BF16),
        scratch_shapes=acc(CTX),
        compiler_params=_cparams(("arbitrary", "arbitrary")),
        name="flash_ctx",
    )(mq, mk, mvt)
    return o, o_ctx


def _mlstm_gates(d, gates, bgate, tri):
    gb = gates + bgate
    ls = jnp.minimum(gb, 0.0) - jnp.log1p(jnp.exp(-jnp.abs(gb)))
    lane = lax.broadcasted_iota(jnp.int32, gb.shape, 1)
    is_f = ((lane >= 4) & (lane < 8)) | ((lane >= 12) & (lane < 16))
    xg = jnp.where(lane < 16, jnp.where(is_f, ls, gb), 0.0)
    x1 = xg.astype(BF16)
    r1 = xg - x1.astype(F32)
    x2 = r1.astype(BF16)
    x3 = (r1 - x2.astype(F32)).astype(BF16)
    cum = _dot(tri, x1) + _dot(tri, x2) + _dot(tri, x3)
    return xg, cum, xg.T, cum.T, jnp.sum(xg, axis=0, keepdims=True)


def _mlstm_kernel(xf_ref, xpf_ref, xnf_ref, vf_ref, gf_ref, xb_ref, xpb_ref, xnb_ref, vb_ref, gb_ref,
                  cw_ref, cb_ref, bgate_ref, tril_ref, triu_ref, hf_ref, hb_ref, c_ref, m_ref,
                  *, nb_ctx, nb_lat, nchk):
    @pl.when(pl.program_id(1) == 0)
    def _():
        c_ref[...] = jnp.zeros_like(c_ref)
        m_ref[...] = jnp.zeros_like(m_ref)

    L = ML_CHUNK
    bgate = bgate_ref[...]
    step = pl.program_id(1)
    in_ctx = step < nb_ctx
    pos = jnp.where(in_ctx, step, step - nb_ctx)
    starts = pos == 0
    ends = pos == jnp.where(in_ctx, nb_ctx, nb_lat) - 1
    qk_f = _short_conv(xf_ref[...], xpf_ref[...], xnf_ref[...], starts, ends, cw_ref, cb_ref)
    qk_b = _short_conv(xb_ref[...], xpb_ref[...], xnb_ref[...], ends, starts, cw_ref, cb_ref)
    q = (qk_f[0], qk_b[0])
    k = (qk_f[1], qk_b[1])
    v = (vf_ref[...], vb_ref[...])
    g_refs = (gf_ref, gb_ref)
    tris = (tril_ref[...], triu_ref[...])
    order = (list(range(nchk)), list(range(nchk - 1, -1, -1)))
    si = lax.broadcasted_iota(jnp.int32, (L, L), 0)
    ti = lax.broadcasted_iota(jnp.int32, (L, L), 1)
    valid = (si <= ti, si >= ti)
    lane_l = lax.broadcasted_iota(jnp.int32, (L, LANES), 1)
    row_l = lax.broadcasted_iota(jnp.int32, (LANES, L), 0)
    rows_of = lambda j: slice(j * L, (j + 1) * L)

    items = {}
    for d in range(2):
        for j in range(nchk):
            xg, cum, xg_t, cum_t, tot = _mlstm_gates(d, g_refs[d][rows_of(j), :], bgate, tris[d])
            for p in range(ML_HEADS // 2):
                cols = slice(p * LANES, (p + 1) * LANES)
                qp = q[d][rows_of(j), cols]
                kp = k[d][rows_of(j), cols]
                vt = v[d][rows_of(j), cols].astype(F32).T
                for e in range(2):
                    hd = 2 * p + e
                    ci = 8 * d + hd
                    cf = 8 * d + 4 + hd
                    in_head = (lane_l < HEAD_DIM) if e == 0 else (lane_l >= HEAD_DIM)
                    in_rows = (row_l < HEAD_DIM) if e == 0 else (row_l >= HEAD_DIM)
                    ones_row = HEAD_DIM if e == 0 else 0
                    b_row = cum_t[cf:cf + 1, :]
                    c_col = xg[:, ci:ci + 1] - cum[:, cf:cf + 1]
                    dm = jnp.where(valid[d], b_row + c_col, NEG)
                    b_tot = tot[:, cf:cf + 1]
                    a_row = b_tot - b_row + xg_t[ci:ci + 1, :]
                    items[d, j, hd] = dict(
                        qp=qp, km=jnp.where(in_head, kp, jnp.zeros_like(kp)),
                        vat=jnp.where(in_rows, vt, jnp.where(row_l == ones_row, 1.0, 0.0)),
                        ones_row=ones_row, b_row=b_row, b_tot=b_tot, dm=dm, a_row=a_row,
                        a_t=jnp.max(dm, axis=0, keepdims=True), a_max=jnp.max(a_row, axis=1, keepdims=True))

    for c in items.values():
        c["qk"] = (_dot_nt(c["km"], c["qp"]) * jnp.exp(c["dm"] - c["a_t"])).astype(BF16)

    for c in items.values():
        c["intra"] = _dot(c["vat"].astype(BF16), c["qk"])
        c["upd"] = _dot((c["vat"] * jnp.exp(c["a_row"] - c["a_max"])).astype(BF16), c["km"])

    state = {(d, hd): (c_ref[d, hd], m_ref[d * ML_HEADS + hd:d * ML_HEADS + hd + 1, 0:1])
             for d in range(2) for hd in range(ML_HEADS)}
    hs = {}
    for n in range(nchk):
        for d in range(2):
            j = order[d][n]
            for hd in range(ML_HEADS):
                c = items[d, j, hd]
                cst, m_prev = state[d, hd]
                g = c["b_row"] + m_prev
                mt = jnp.maximum(g, c["a_t"])
                haug = (jnp.exp(g - mt) * _dot_nt(cst.astype(BF16), c["qp"])
                        + jnp.exp(c["a_t"] - mt) * c["intra"])
                den = haug[c["ones_row"]:c["ones_row"] + 1, :]
                hs[d, j, hd] = haug / jnp.maximum(jnp.abs(den), jnp.exp(-mt))
                m_new = jnp.maximum(c["b_tot"] + m_prev, c["a_max"])
                cst = (jnp.exp(c["b_tot"] + m_prev - m_new) * cst
                       + jnp.exp(c["a_max"] - m_new) * c["upd"])
                state[d, hd] = (cst, m_new)
    for (d, hd), (cst, m_new) in state.items():
        c_ref[d, hd] = cst
        r = d * ML_HEADS + hd
        m_ref[r:r + 1, :] = jnp.broadcast_to(m_new, (1, LANES))

    for d, h_ref in enumerate((hf_ref, hb_ref)):
        for j in range(nchk):
            for p in range(ML_HEADS // 2):
                pair = jnp.where(row_l < HEAD_DIM, hs[d, j, 2 * p], hs[d, j, 2 * p + 1])
                h_ref[rows_of(j), p * LANES:(p + 1) * LANES] = pair.T


def _mlstm(l, mlqk, mlv, misc, prm, tabs, dims):
    T, W = mlqk.shape
    B, S, CTX = dims
    nchk = ML_STEP_CHUNKS
    blk = nchk * ML_CHUNK
    nb_ctx = CTX // blk
    nb_lat = S // blk
    ctx0 = B * S // blk

    def fwd(b, c):
        return jnp.where(c < nb_ctx, ctx0 + b * nb_ctx + c, b * nb_lat + (c - nb_ctx))

    def bwd(b, c):
        return jnp.where(c < nb_ctx, ctx0 + b * nb_ctx + (nb_ctx - 1 - c),
                         b * nb_lat + (nb_lat - 1 - (c - nb_ctx)))

    sub = blk // 8
    last8 = T // 8 - 1

    def specs(fn):
        return [pl.BlockSpec((blk, W), lambda b, c: (fn(b, c), 0)),
                pl.BlockSpec((8, W), lambda b, c: (jnp.maximum(fn(b, c) * sub - 1, 0), 0)),
                pl.BlockSpec((8, W), lambda b, c: (jnp.minimum((fn(b, c) + 1) * sub, last8), 0)),
                pl.BlockSpec((blk, ML_W), lambda b, c: (fn(b, c), 0)),
                pl.BlockSpec((blk, LANES), lambda b, c: (fn(b, c), 0))]

    return pl.pallas_call(
        functools.partial(_mlstm_kernel, nb_ctx=nb_ctx, nb_lat=nb_lat, nchk=nchk),
        grid=(B, nb_ctx + nb_lat),
        in_specs=specs(fwd) + specs(bwd) + [
            pl.BlockSpec((None, 8, W), lambda b, c: (l, 0, 0)),
            pl.BlockSpec((None, 1, W), lambda b, c: (l, 0, 0)),
            pl.BlockSpec((None, 1, LANES), lambda b, c: (l, 0, 0)),
            pl.BlockSpec((ML_CHUNK, ML_CHUNK), lambda b, c: (0, 0)),
            pl.BlockSpec((ML_CHUNK, ML_CHUNK), lambda b, c: (0, 0)),
        ],
        out_specs=[pl.BlockSpec((blk, ML_W), lambda b, c: (fwd(b, c), 0)),
                   pl.BlockSpec((blk, ML_W), lambda b, c: (bwd(b, c), 0))],
        out_shape=[jax.ShapeDtypeStruct((T, ML_W), F32)] * 2,
        scratch_shapes=[pltpu.VMEM((2, ML_HEADS, LANES, LANES), F32), pltpu.VMEM((8, LANES), F32)],
        compiler_params=_cparams(("arbitrary", "arbitrary")),
        name="mlstm",
    )(mlqk, mlqk, mlqk, mlv, misc, mlqk, mlqk, mlqk, mlv, misc,
      prm["conv_w"], prm["conv_b"], prm["b_gate"], tabs["tril"], tabs["triu"])


def _ffn_chunks(ff):
    step = 2 * MXU_DIM
    starts = list(range(0, ff - ff % step, step)) or [0]
    return [(c, (ff - c) if c == starts[-1] else step) for c in starts]


def _out_kernel(x_ref, xc_ref, mod_ref, ona_ref, omla_ref, omlac_ref, hf_ref, hb_ref, mlo_ref, hg_ref,
                bd64_ref, w_out_ref, g_ref, wab_ref, wo_ref, o_ref, *, n_lat, n_lat_split):
    x = _stream_tile(x_ref, xc_ref, n_lat_split)
    omla = jnp.where(pl.program_id(0) < n_lat, omla_ref[...], omlac_ref[...])
    gate1 = mod_ref[2:3, :]
    shift2 = mod_ref[3:4, :]
    scale2 = mod_ref[4:5, :]
    gate2 = mod_ref[5:6, :]

    h = hf_ref[...] + hb_ref[...]
    hn = h * lax.rsqrt(_group_sumsq(h, bd64_ref[...]) * (1.0 / HEAD_DIM) + EPS) * hg_ref[...]
    ob = (hn * _sigmoid(mlo_ref[...])).astype(BF16)
    mix = _dot(jnp.concatenate([ona_ref[...], ob, omla], axis=-1), w_out_ref[...])
    x1 = x + gate1 * mix

    h2 = ((_rms_rows(x1) * g_ref[...]) * (1.0 + scale2) + shift2).astype(BF16)
    acc = jnp.zeros_like(x1)
    ff = wo_ref.shape[0]
    for c0, w in _ffn_chunks(ff):
        a = _dot(h2, wab_ref[:, c0:c0 + w])
        b = _dot(h2, wab_ref[:, ff + c0:ff + c0 + w])
        act = (a * _sigmoid(a) * b).astype(BF16)
        acc = acc + _dot(act, wo_ref[c0:c0 + w, :])
    o_ref[...] = x1 + gate2 * acc


def _out(l, xs, mods, ona, omla, omla_ctx, hf, hb, mlo, prm, tabs, dims, with_ctx):
    D = mods.shape[-1]
    B, S, CTX = dims
    T = B * (S + CTX) if with_ctx else B * S
    nt = T // ROW_TILE
    lat_tiles = S // ROW_TILE
    n_lat = B * lat_tiles
    x_in, x_specs, n_lat_split = _stream_specs(xs, n_lat, D)
    ff = prm["w_fo"].shape[1]

    def grp(i):
        return jnp.minimum(i // lat_tiles, B)

    row = lambda w: pl.BlockSpec((ROW_TILE, w), lambda i: (i, 0))
    lay = lambda shape: pl.BlockSpec((None,) + shape, lambda i: (l,) + (0,) * len(shape))
    big = lambda shape: pl.BlockSpec((None,) + shape, lambda i: (l,) + (0,) * len(shape),
                                     pipeline_mode=pl.Buffered(1))
    return pl.pallas_call(
        functools.partial(_out_kernel, n_lat=n_lat, n_lat_split=n_lat_split),
        grid=(nt,),
        in_specs=x_specs + [
            pl.BlockSpec((None, None, 6, D), lambda i: (l, grp(i), 0, 0)),
            row(NA_W),
            pl.BlockSpec((ROW_TILE, MLA_HEADS * MLA_V), lambda i: (jnp.minimum(i, n_lat - 1), 0)),
            pl.BlockSpec((ROW_TILE, MLA_HEADS * MLA_V), lambda i: (jnp.maximum(i - n_lat, 0), 0)),
            row(ML_W), row(ML_W), row(ML_W),
            lay((1, ML_W)),
            pl.BlockSpec((MXU_DIM, MXU_DIM), lambda i: (0, 0)),
            big((NA_W + ML_W + MLA_HEADS * MLA_V, D)),
            lay((1, D)),
            big((D, 2 * ff)), big((ff, D)),
        ],
        out_specs=row(D),
        out_shape=jax.ShapeDtypeStruct((T, D), F32),
        compiler_params=_cparams(("arbitrary",)),
        name="out_ffn",
    )(*x_in, mods, ona, omla, omla if omla_ctx is None else omla_ctx, hf, hb, mlo, prm["head_gain"], tabs["bd64"],
      prm["w_out"], prm["g_ffn"], prm["w_ab"], prm["w_fo"])


def _block_diag_ones(group):
    idx = np.arange(MXU_DIM) // group
    return jnp.asarray((idx[:, None] == idx[None, :]).astype(np.float32), dtype=BF16)


def _rope_tables(S, n_ctx_rows):
    n_freq = MLA_ROPE // 4
    inv = ROPE_BASE ** (-np.arange(n_freq, dtype=np.float64) / n_freq)
    t = np.arange(S)
    row = (t // GRID_W).astype(np.float64)
    col = (t % GRID_W).astype(np.float64)
    ang = np.concatenate([row[:, None] * inv, col[:, None] * inv], axis=-1)
    ang = np.concatenate([ang, np.zeros((n_ctx_rows, MLA_ROPE // 2))], axis=0)
    cos, sin = np.cos(ang), np.sin(ang)
    n = S + n_ctx_rows
    ones = np.ones((n, MLA_NOPE))
    z = lambda w: np.zeros((n, w))
    tail = LANES - MLA_NOPE - MLA_ROPE
    cos_t = np.concatenate([ones, cos, cos, np.ones((n, tail))], axis=-1)
    sin_t = np.concatenate([z(MLA_NOPE), -sin, sin, z(tail)], axis=-1)
    return jnp.asarray(cos_t, dtype=F32), jnp.asarray(sin_t, dtype=F32)


def _na_bias_kernel(rpb_ref, o_ref, *, plan):
    li = pl.program_id(0)
    h = pl.program_id(1)
    n_dc = 2 * NA_WIN_C - 1
    ck = lax.broadcasted_iota(jnp.int32, (GRID_W, LANES), 0)
    lane = lax.broadcasted_iota(jnp.int32, (GRID_W, LANES), 1)
    cq = lane & (GRID_W - 1)
    dc = ck - cq + (NA_WIN_C - 1)
    cs = jnp.clip(cq - NA_WIN_C // 2, 0, GRID_W - NA_WIN_C)
    col_ok = (ck >= cs) & (ck < cs + NA_WIN_C)
    hits = [dc == j for j in range(n_dc)]
    neg = jnp.full((GRID_W, LANES), NEG, F32)
    tiles = []
    for dr in range(2 * NA_WIN_R - 1):
        t = neg
        for j in range(n_dc):
            t = jnp.where(hits[j], rpb_ref[li, h, dr * n_dc + j] * LOG2E, t)
        tiles.append(jnp.where(col_ok, t, neg))
    low = lane < GRID_W
    for v, rows in enumerate(plan):
        for rk, drs in enumerate(rows):
            pick = [neg if d is None else tiles[d] for d in drs]
            groups = [jnp.where(low, pick[2 * g], pick[2 * g + 1]) for g in range(len(drs) // 2)]
            o_ref[v, rk * GRID_W:(rk + 1) * GRID_W, :] = jnp.concatenate(groups, axis=-1).astype(o_ref.dtype)
    o_ref[len(plan)] = jnp.full(o_ref.shape[1:], NEG, o_ref.dtype)


def _na_bias(rpb, rows):
    depth, H = rpb.shape[:2]
    qr, kr_n = ATT_BLOCK // GRID_W, NA_KEY_ROWS // GRID_W
    nb = rows // qr
    plan = []
    for j in (0, 1, nb - 1):
        ks = int(np.clip(j - 1, 0, nb - 3)) * qr
        per_key_row = []
        for kr in range(ks, ks + kr_n):
            drs = []
            for r in range(j * qr, (j + 1) * qr):
                rs = int(np.clip(r - NA_WIN_R // 2, 0, rows - NA_WIN_R))
                drs.append(kr - r + NA_WIN_R - 1 if rs <= kr < rs + NA_WIN_R else None)
            per_key_row.append(tuple(drs))
        plan.append(tuple(per_key_row))
    n_rel = (2 * NA_WIN_R - 1) * (2 * NA_WIN_C - 1)
    return pl.pallas_call(
        functools.partial(_na_bias_kernel, plan=tuple(plan)),
        grid=(depth, H),
        in_specs=[pl.BlockSpec(memory_space=pltpu.SMEM)],
        out_specs=pl.BlockSpec((None, len(plan) + 1, None, NA_KEY_ROWS, ATT_BLOCK),
                               lambda li, h: (li, 0, h, 0, 0)),
        out_shape=jax.ShapeDtypeStruct((depth, len(plan) + 1, H, NA_KEY_ROWS, ATT_BLOCK), BF16),
        compiler_params=_cparams(("arbitrary", "arbitrary")),
        name="na_bias",
    )(rpb.reshape(depth, H, n_rel))


def _prepare(w_in, w_uq, w_ukv, w_out, w_ffn_in, w_ffn_out, g_mix, g_ffn, b_gate, na_qk_gain,
             ml_conv_w, ml_conv_b, ml_head_gain, mla_gq, mla_gkv, mla_qk_gain):
    depth, D, _ = w_in.shape
    o_gate = C_MLO + ML_W
    o_ql = o_gate + 4 * ML_HEADS
    o_kr = o_ql + Q_LORA + KV_LORA
    half = MLA_ROPE // 2

    def layout_kernel(w_ref, o_ref):
        w = w_ref[...]
        zc = lambda n: jnp.zeros((w.shape[0], n), w.dtype)
        o_ref[...] = jnp.concatenate([
            w[:, :o_gate], w[:, o_ql:o_kr],
            w[:, o_gate:o_ql], zc(MISC_ROPE_LANE - 4 * ML_HEADS),
            w[:, o_kr:], zc(LANES - MISC_ROPE_LANE - MLA_ROPE),
            zc(MISC_ROPE_LANE), w[:, o_kr + half:], w[:, o_kr:o_kr + half],
            zc(LANES - MISC_ROPE_LANE - MLA_ROPE)], axis=-1).astype(BF16)

    rows = MXU_DIM
    w_in_p = pl.pallas_call(
        layout_kernel,
        grid=(depth, D // rows),
        in_specs=[pl.BlockSpec((None, rows, w_in.shape[2]), lambda li, i: (li, i, 0))],
        out_specs=pl.BlockSpec((None, rows, D_IN_PAD), lambda li, i: (li, i, 0)),
        out_shape=jax.ShapeDtypeStruct((depth, D, D_IN_PAD), BF16),
        compiler_params=_cparams(("arbitrary", "arbitrary")),
        name="w_in_layout",
    )(w_in)
    pad_h = lambda a, w: jnp.pad(a, [(0, 0)] * (a.ndim - 1) + [(0, LANES - w)])

    def partner(a):
        lo, hi = a[..., MLA_NOPE:MLA_NOPE + half], a[..., MLA_NOPE + half:MLA_QK]
        return jnp.concatenate([jnp.zeros_like(a[..., :MLA_NOPE]), hi, lo,
                                jnp.zeros_like(a[..., MLA_QK:])], axis=-1)

    w_uq_h = pad_h(w_uq.reshape(depth, Q_LORA, MLA_HEADS, MLA_QK), MLA_QK)
    w_uq_p = w_uq_h.reshape(depth, Q_LORA, MLA_HEADS * LANES).astype(BF16)
    w_uq_s = partner(w_uq_h).reshape(depth, Q_LORA, MLA_HEADS * LANES).astype(BF16)
    ukv = w_ukv.reshape(depth, KV_LORA, MLA_HEADS, MLA_NOPE + MLA_V)
    w_uk_p = pad_h(ukv[..., :MLA_NOPE], MLA_NOPE).reshape(depth, KV_LORA, MLA_HEADS * LANES).astype(BF16)
    w_uv = ukv[..., MLA_NOPE:].reshape(depth, KV_LORA, MLA_HEADS * MLA_V).astype(BF16)
    mla_g = pad_h(mla_qk_gain, MLA_QK)
    return {
        "w_in": w_in_p, "w_uq": w_uq_p, "w_uqs": w_uq_s, "w_uk": w_uk_p, "w_uv": w_uv,
        "w_out": w_out.astype(BF16),
        "w_ab": w_ffn_in.astype(BF16),
        "w_fo": w_ffn_out.astype(BF16),
        "g_mix": g_mix[:, None, :], "g_ffn": g_ffn[:, None, :],
        "gq": mla_gq[:, None, :], "gkv": mla_gkv[:, None, :],
        "na_gq": jnp.tile(na_qk_gain[:, 0:1, :], (1, 1, NA_HEADS)) * (NA_SCALE * LOG2E),
        "na_gk": jnp.tile(na_qk_gain[:, 1:2, :], (1, 1, NA_HEADS)),
        "mla_gq": jnp.tile(mla_g[:, 0:1, :], (1, 1, MLA_HEADS)) * (MLA_SCALE * LOG2E),
        "mla_gk": jnp.tile(mla_g[:, 1:2, :], (1, 1, MLA_HEADS)),
        "mla_gqs": jnp.tile(partner(mla_g[:, 0:1, :]), (1, 1, MLA_HEADS)) * (MLA_SCALE * LOG2E),
        "mla_gks": jnp.tile(partner(mla_g[:, 1:2, :]), (1, 1, MLA_HEADS)),
        "head_gain": ml_head_gain.reshape(depth, 1, ML_W),
        "b_gate": pad_h(b_gate, 4 * ML_HEADS)[:, None, :],
        "conv_w": jnp.pad(ml_conv_w, ((0, 0), (0, 8 - ml_conv_w.shape[1]), (0, 0))),
        "conv_b": ml_conv_b[:, None, :],
    }


def kernel(x, c, ctx, c_ctx, w_mod, b_mod, g_mix, g_ffn, w_in, b_gate, na_qk_gain, na_rpb,
           ml_conv_w, ml_conv_b, ml_head_gain, mla_gq, mla_gkv, w_uq, w_ukv, mla_qk_gain,
           w_out, w_ffn_in, w_ffn_out):
    B, S, D = x.shape
    CTX = ctx.shape[1]
    depth = w_in.shape[0]
    assert CTX == ATT_BLOCK and S % ROW_TILE == 0 and (B * CTX) % ROW_TILE == 0
    assert CTX % (ML_STEP_CHUNKS * ML_CHUNK) == 0 and S % (ML_STEP_CHUNKS * ML_CHUNK) == 0
    assert S % (2 * FLASH_TK) == 0 and S % (FLASH_TILES * FLASH_TQ) == 0 and S // ATT_BLOCK >= 3 and B + 1 <= 8
    dims = (B, S, CTX)

    prm = _prepare(w_in, w_uq, w_ukv, w_out, w_ffn_in, w_ffn_out, g_mix, g_ffn, b_gate, na_qk_gain,
                   ml_conv_w, ml_conv_b, ml_head_gain, mla_gq, mla_gkv, mla_qk_gain)
    cos_t, sin_t = _rope_tables(S, B * CTX)
    tri = np.tril(np.ones((ML_CHUNK, ML_CHUNK), np.float32))
    tabs = {
        "bd64": _block_diag_ones(HEAD_DIM), "bd128": _block_diag_ones(LANES),
        "cos": cos_t, "sin": sin_t,
        "tril": jnp.asarray(tri, dtype=BF16), "triu": jnp.asarray(tri.T, dtype=BF16),
    }
    bias = _na_bias(na_rpb, S // GRID_W)

    cond8 = jnp.concatenate([c, c_ctx[None, :], jnp.zeros((8 - B - 1, D), F32)], axis=0)
    mods = _modulation(cond8, w_mod, b_mod)
    mods = mods.reshape(depth, 8, 6, D)

    xs = (x.reshape(B * S, D), ctx.reshape(B * CTX, D))
    for l in range(depth):
        with_ctx = l < depth - 1
        naq, nak, mlqk, mlv, mlo, misc, mq, mk, navt, mvt = _proj(l, xs, mods, prm, tabs, dims)
        ona = _na(l, naq, nak, navt, bias, dims, with_ctx)
        omla, omla_ctx = _flash(mq, mk, mvt, dims, with_ctx)
        hf, hb = _mlstm(l, mlqk, mlv, misc, prm, tabs, dims)
        xs = _out(l, xs, mods, ona, omla, omla_ctx, hf, hb, mlo, prm, tabs, dims, with_ctx)
    return xs.reshape(B, S, D)
```
